```python
import jax, jax.numpy as jnp
from jax import lax
import numpy as np

D_MODEL = 1024
BATCH = 4
SEQ = 4096
DEPTH = 2

GRID_W = 64
CTX_LEN = 256
EPS = 1e-6
GLA_HEADS = 4
GLA_DK = D_MODEL // (2 * GLA_HEADS)
GLA_DV = D_MODEL // GLA_HEADS
GLA_QK = GLA_HEADS * GLA_DK
GLA_V = GLA_HEADS * GLA_DV
GLA_LOWRANK = 16
GLA_TAU = 16.0
GLA_CHUNK = 64
SC_WIDTH = D_MODEL
SC_CONV = 3
RG_WIDTH = 2 * D_MODEL
RG_BLOCKS = 16
RG_BLOCK_W = RG_WIDTH // RG_BLOCKS
RG_C = 8.0
RG_CONV = 4

EVEN_SIZES = (GLA_QK, GLA_QK, GLA_V, GLA_V, GLA_LOWRANK, GLA_LOWRANK,
              SC_WIDTH, SC_WIDTH, SC_WIDTH, SC_WIDTH)
EVEN_IN = 2 * GLA_QK + 2 * GLA_V + 2 * GLA_LOWRANK + 4 * SC_WIDTH
EVEN_OUT = GLA_V + SC_WIDTH

kernel_name = "hybrid_gla_shortconv_rglru_prefix_dit"


def _rmsnorm(x, g):
    xf = x.astype(jnp.float32)
    y = xf * lax.rsqrt(jnp.mean(xf * xf, axis=-1, keepdims=True) + EPS)
    return (y * g.astype(jnp.float32)).astype(x.dtype)


def _modulate(x, g, shift, scale):
    return _rmsnorm(x, g) * (1 + scale) + shift


def _split_cols(p, sizes):
    out, off = [], 0
    for s in sizes:
        out.append(p[..., off:off + s])
        off += s
    return out


def _conv3_centered(x, w):
    n = x.shape[-2]
    pad = [(0, 0)] * (x.ndim - 2) + [(1, 1), (0, 0)]
    xp = jnp.pad(x, pad)
    return w[0] * xp[..., 0:n, :] + w[1] * xp[..., 1:n + 1, :] + w[2] * xp[..., 2:n + 2, :]


def _conv4_directional(x, w, b, reverse):
    if reverse:
        x = jnp.flip(x, 1)
    length = x.shape[1]
    xp = jnp.pad(x, ((0, 0), (RG_CONV - 1, 0), (0, 0)))
    y = b
    for j in range(RG_CONV):
        y = y + w[j] * xp[:, j:j + length]
    return jnp.flip(y, 1) if reverse else y


def _gla_direction(q, k, v, log_g, s0):
    bsz, length, heads, _ = q.shape
    dv = v.shape[-1]
    n = length // GLA_CHUNK

    def chunks(t):
        return t.reshape(bsz, n, GLA_CHUNK, heads, t.shape[-1])

    q, k, v, log_g = chunks(q), chunks(k), chunks(v), chunks(log_g)
    b = jnp.cumsum(log_g, axis=2)
    b_last = b[:, :, -1:]
    q_in = q * jnp.exp(b)
    k_in = k * jnp.exp(-b)
    scores = jnp.einsum("bnthk,bnshk->bnhts", q_in, k_in)
    mask = jnp.tril(jnp.ones((GLA_CHUNK, GLA_CHUNK), dtype=bool))
    scores = jnp.where(mask, scores, 0.0)
    o_intra = jnp.einsum("bnhts,bnshv->bnthv", scores, v)
    k_end = k * jnp.exp(b_last - b)
    incr = jnp.einsum("bnshk,bnshv->bnhkv", k_end, v)
    decay = jnp.exp(b_last[:, :, 0])

    def step(state, inp):
        d, u = inp
        return d[..., None] * state + u, state

    s_final, s_prev = lax.scan(step, s0, (jnp.moveaxis(decay, 1, 0), jnp.moveaxis(incr, 1, 0)))
    o_inter = jnp.einsum("bnthk,nbhkv->bnthv", q_in, s_prev)
    return (o_intra + o_inter).reshape(bsz, length, heads, dv), s_final


def _even_branch(h, s0_f, s0_b, on_grid, need_out, w_in, w_a2, b_a2, gla_g, conv_w):
    f32 = jnp.float32
    bsz, length, _ = h.shape
    p = h @ w_in
    q, k, v, g_a, a_f, a_b, c_b, c_c, c_x, g_b = _split_cols(p, EVEN_SIZES)
    q = q.reshape(bsz, length, GLA_HEADS, GLA_DK).astype(f32) * (GLA_DK ** -0.5)
    k = k.reshape(bsz, length, GLA_HEADS, GLA_DK).astype(f32)
    v = v.reshape(bsz, length, GLA_HEADS, GLA_DV).astype(f32)

    def log_gate(a_lr, d):
        z = a_lr.astype(f32) @ w_a2[d].astype(f32) + b_a2[d].astype(f32)
        return (jax.nn.log_sigmoid(z) / GLA_TAU).reshape(bsz, length, GLA_HEADS, GLA_DK)

    o_f, s_f = _gla_direction(q, k, v, log_gate(a_f, 0), s0_f)
    fl = lambda t: jnp.flip(t, 1)
    o_b, s_b = _gla_direction(fl(q), fl(k), fl(v), fl(log_gate(a_b, 1)), s0_b)
    if not need_out:
        return None, s_f, s_b
    o = _rmsnorm(o_f + fl(o_b), gla_g).reshape(bsz, length, GLA_V).astype(h.dtype)
    o = o * jax.nn.silu(g_a)
    z = c_c * c_x
    if on_grid:
        rows = length // GRID_W
        zc = _conv3_centered(z.reshape(bsz, rows, GRID_W, SC_WIDTH), conv_w).reshape(bsz, length, SC_WIDTH)
    else:
        zc = _conv3_centered(z, conv_w)
    y = c_b * zc * jax.nn.silu(g_b)
    return jnp.concatenate([o, y], axis=-1), s_f, s_b


def _lin_combine(e1, e2):
    a1, b1 = e1
    a2, b2 = e2
    return a1 * a2, a2 * b1 + b2


def _rglru_direction(xr, conv_w, conv_b, w_a, b_a, w_x, b_x, lam, h0, reverse):
    f32 = jnp.float32
    xc = _conv4_directional(xr.astype(f32), conv_w.astype(f32), conv_b.astype(f32), reverse)
    bsz, length, _ = xc.shape
    blk = xc.reshape(bsz, length, RG_BLOCKS, RG_BLOCK_W)
    r = jax.nn.sigmoid(jnp.einsum("blni,nij->blnj", blk, w_a.astype(f32)) + b_a.astype(f32))
    i = jax.nn.sigmoid(jnp.einsum("blni,nij->blnj", blk, w_x.astype(f32)) + b_x.astype(f32))
    r = r.reshape(bsz, length, RG_WIDTH)
    i = i.reshape(bsz, length, RG_WIDTH)
    log_a = -RG_C * r * jax.nn.softplus(-lam.astype(f32))
    a = jnp.exp(log_a)
    u = jnp.sqrt(-jnp.expm1(2.0 * log_a)) * (i * xc)
    a_cum, hs = lax.associative_scan(_lin_combine, (a, u), reverse=reverse, axis=1)
    hs = hs + a_cum * h0[:, None, :]
    fin = hs[:, 0] if reverse else hs[:, -1]
    return hs, fin


def _odd_branch(h, h0_f, h0_b, need_out, w_in, conv_w, conv_b, w_a, b_a, w_x, b_x, lam):
    xr = h @ w_in[:, :RG_WIDTH]
    h_f, fin_f = _rglru_direction(xr, conv_w[0], conv_b[0], w_a[0], b_a[0], w_x[0], b_x[0], lam[0], h0_f, False)
    h_b, fin_b = _rglru_direction(xr, conv_w[1], conv_b[1], w_a[1], b_a[1], w_x[1], b_x[1], lam[1], h0_b, True)
    if not need_out:
        return None, fin_f, fin_b
    gate = h @ w_in[:, RG_WIDTH:]
    y = (h_f + h_b).astype(h.dtype) * jax.nn.silu(gate)
    return y, fin_f, fin_b


def setup_inputs(seed: int = 0) -> dict:
    key = jax.random.key(seed)
    ks = jax.random.split(key, 32)
    n_even = (DEPTH + 1) // 2
    n_odd = DEPTH // 2
    nrm = lambda k, shape, s: jax.random.normal(k, shape, jnp.float32) * s
    a8 = jax.random.uniform(ks[22], (n_odd, 2, RG_WIDTH), jnp.float32, minval=0.9, maxval=0.999)
    s = a8 ** (1.0 / RG_C)
    lam = jnp.log(s) - jnp.log1p(-s)
    return {
        "x": nrm(ks[0], (BATCH, SEQ, D_MODEL), 1.0),
        "c": nrm(ks[1], (BATCH, D_MODEL), 1.0),
        "ctx": nrm(ks[2], (BATCH, CTX_LEN, D_MODEL), 1.0),
        "c_ctx": nrm(ks[3], (D_MODEL,), 1.0),
        "norm_g": 1.0 + nrm(ks[4], (DEPTH, D_MODEL), 0.02),
        "w_mod": nrm(ks[5], (DEPTH, D_MODEL, 3 * D_MODEL), 0.5 * D_MODEL ** -0.5),
        "b_mod": nrm(ks[6], (DEPTH, 3 * D_MODEL), 0.02),
        "e_w_in": nrm(ks[7], (n_even, D_MODEL, EVEN_IN), D_MODEL ** -0.5),
        "e_w_a2": nrm(ks[8], (n_even, 2, GLA_LOWRANK, GLA_QK), GLA_LOWRANK ** -0.5),
        "e_b_a2": nrm(ks[9], (n_even, 2, GLA_QK), 0.1),
        "e_gla_g": 1.0 + nrm(ks[10], (n_even, GLA_DV), 0.02),
        "e_conv_w": nrm(ks[11], (n_even, SC_CONV, SC_WIDTH), SC_CONV ** -0.5),
        "e_w_out": nrm(ks[12], (n_even, EVEN_OUT, D_MODEL), EVEN_OUT ** -0.5),
        "o_w_in": nrm(ks[13], (n_odd, D_MODEL, 2 * RG_WIDTH), D_MODEL ** -0.5),
        "o_conv_w": nrm(ks[14], (n_odd, 2, RG_CONV, RG_WIDTH), RG_CONV ** -0.5),
        "o_conv_b": nrm(ks[15], (n_odd, 2, RG_WIDTH), 0.02),
        "o_w_a": nrm(ks[16], (n_odd, 2, RG_BLOCKS, RG_BLOCK_W, RG_BLOCK_W), RG_BLOCK_W ** -0.5),
        "o_b_a": nrm(ks[17], (n_odd, 2, RG_BLOCKS, RG_BLOCK_W), 0.02),
        "o_w_x": nrm(ks[18], (n_odd, 2, RG_BLOCKS, RG_BLOCK_W, RG_BLOCK_W), RG_BLOCK_W ** -0.5),
        "o_b_x": nrm(ks[19], (n_odd, 2, RG_BLOCKS, RG_BLOCK_W), 0.02),
        "o_lam": lam,
        "o_w_out": nrm(ks[20], (n_odd, RG_WIDTH, D_MODEL), RG_WIDTH ** -0.5),
        "final_g": 1.0 + nrm(ks[21], (D_MODEL,), 0.02),
    }


def reference(x, c, ctx, c_ctx, norm_g, w_mod, b_mod, e_w_in, e_w_a2, e_b_a2, e_gla_g,
              e_conv_w, e_w_out, o_w_in, o_conv_w, o_conv_b, o_w_a, o_b_a, o_w_x, o_b_x,
              o_lam, o_w_out, final_g):
    bsz = x.shape[0]
    s_c = jax.nn.silu(c)
    s_cc = jax.nn.silu(c_ctx)
    x_ctx = ctx
    for li in range(DEPTH):
        last = li == DEPTH - 1
        mod = s_c @ w_mod[li] + b_mod[li]
        shift, scale, gate = jnp.split(mod[:, None, :], 3, axis=-1)
        mod_c = s_cc @ w_mod[li] + b_mod[li]
        shift_c, scale_c, gate_c = jnp.split(mod_c, 3, axis=-1)
        h = _modulate(x, norm_g[li], shift, scale)
        h_c = _modulate(x_ctx, norm_g[li], shift_c, scale_c)
        if li % 2 == 0:
            j = li // 2
            prm = (e_w_in[j], e_w_a2[j], e_b_a2[j], e_gla_g[j], e_conv_w[j])
            s0 = jnp.zeros((bsz, GLA_HEADS, GLA_DK, GLA_DV), jnp.float32)
            y_c, s_f, s_b = _even_branch(h_c, s0, s0, False, not last, *prm)
            y, _, _ = _even_branch(h, s_f, s_b, True, True, *prm)
            w_out = e_w_out[j]
        else:
            j = li // 2
            prm = (o_w_in[j], o_conv_w[j], o_conv_b[j], o_w_a[j], o_b_a[j], o_w_x[j], o_b_x[j], o_lam[j])
            h0 = jnp.zeros((bsz, RG_WIDTH), jnp.float32)
            y_c, h_f, h_b = _odd_branch(h_c, h0, h0, not last, *prm)
            y, _, _ = _odd_branch(h, h_f, h_b, True, *prm)
            w_out = o_w_out[j]
        x = x + gate * (y @ w_out)
        if not last:
            x_ctx = x_ctx + gate_c * (y_c @ w_out)
    return _rmsnorm(x, final_g)
```

```python
import functools

import jax
import jax.numpy as jnp
from jax import lax
from jax.experimental import pallas as pl
from jax.experimental.pallas import tpu as pltpu

F32 = jnp.float32
BF16 = jnp.bfloat16

D_MODEL = 1024
EPS = 1e-6
GLA_HEADS = 4
GLA_DK = 128
GLA_DV = 256
GLA_QK = GLA_HEADS * GLA_DK
GLA_V = GLA_HEADS * GLA_DV
GLA_LOWRANK = 16
GLA_TAU = 16.0
GLA_CHUNK = 64
GRID_W = 64
SC_WIDTH = D_MODEL
RG_WIDTH = 2 * D_MODEL
RG_BLOCKS = 16
RG_BLOCK_W = 128
RG_C = 8.0
RG_CONV = 4

TM = 256
CHUNKS = TM // GLA_CHUNK
SUBLANES = 8
LANES = 128
VMEM_LIMIT = 56 * 1024 * 1024


def _silu(x):
    return x * jax.nn.sigmoid(x)


def _log_sigmoid(z):
    return jnp.minimum(z, 0.0) - jnp.log1p(jnp.exp(-jnp.abs(z)))


def _softplus(z):
    return jnp.maximum(z, 0.0) + jnp.log1p(jnp.exp(-jnp.abs(z)))


def _dot(a, b):
    return jnp.dot(a, b, preferred_element_type=F32)


def _dot_nt(a, b):
    return lax.dot_general(a, b, (((1,), (1,)), ((), ())), preferred_element_type=F32)


def _dot_tn(a, b):
    return lax.dot_general(a, b, (((0,), (0,)), ((), ())), preferred_element_type=F32)


def _modulated_norm(x, g, shift, scale):
    ms = jnp.mean(x * x, axis=-1, keepdims=True)
    return (x * lax.rsqrt(ms + EPS) * g) * (1.0 + scale) + shift


def _params(n_axes):
    return pltpu.CompilerParams(
        dimension_semantics=("arbitrary",) * n_axes,
        vmem_limit_bytes=VMEM_LIMIT)


def _const_spec(shape):
    zeros = (0,) * len(shape)
    return pl.BlockSpec(shape, lambda *_: zeros, pipeline_mode=pl.Buffered(1))


def _mods_kernel(s_ref, w_ref, b_ref, o_ref):
    s = _silu(s_ref[...])
    o_ref[0] = jnp.dot(s, w_ref[0], preferred_element_type=F32,
                       precision=lax.Precision.HIGHEST) + b_ref[0]


def _mods(rows, w_mod, b_mod):
    depth = w_mod.shape[0]
    nb = 3
    return pl.pallas_call(
        _mods_kernel,
        grid=(depth, nb),
        in_specs=[
            pl.BlockSpec((SUBLANES, D_MODEL), lambda l, j: (0, 0)),
            pl.BlockSpec((1, D_MODEL, D_MODEL), lambda l, j: (l, 0, j)),
            pl.BlockSpec((1, 1, D_MODEL), lambda l, j: (l, 0, j)),
        ],
        out_specs=pl.BlockSpec((1, SUBLANES, D_MODEL), lambda l, j: (l, 0, j)),
        out_shape=jax.ShapeDtypeStruct((depth, SUBLANES, 3 * D_MODEL), F32),
        compiler_params=_params(2),
        name="mods",
    )(rows, w_mod, b_mod.reshape(depth, 1, 3 * D_MODEL))


def _split_dot(mat, x):
    hi = x.astype(BF16)
    lo = (x - hi.astype(F32)).astype(BF16)
    return _dot(mat, hi) + _dot(mat, lo)


def _even_in_kernel(xx_ref, mod_ref, g_ref, wqk_ref, wv_ref, wga_ref, wa_ref, w2_ref, b2_ref,
                    wcb_ref, wcc_ref, wcx_ref, wgb_ref, cw_ref,
                    qif_ref, kif_ref, kef_ref, decf_ref, qib_ref, kib_ref, keb_ref, decb_ref,
                    v_ref, sga_ref, y_ref):
    t = pl.program_id(1)
    h = _modulated_norm(xx_ref[...], g_ref[...], mod_ref[0:1, :], mod_ref[1:2, :])
    hb = h.astype(BF16)

    a_lr = _dot(hb, wa_ref[...])
    z = _dot(a_lr.astype(BF16), w2_ref[...]) + b2_ref[...]
    lg = _log_sigmoid(z) * (1.0 / GLA_TAU)

    row = lax.broadcasted_iota(jnp.int32, (TM, TM), 0)
    col = lax.broadcasted_iota(jnp.int32, (TM, TM), 1)
    same = (row // GLA_CHUNK) == (col // GLA_CHUNK)
    lower = jnp.where(same & (col <= row), 1.0, 0.0).astype(BF16)
    upper = jnp.where(same & (col >= row), 1.0, 0.0).astype(BF16)
    ones = jnp.where(same, 1.0, 0.0).astype(BF16)

    lg_f = lg[:, :GLA_QK]
    lg_b = lg[:, GLA_QK:]
    cs_f = _split_dot(lower, lg_f)
    cs_b = _split_dot(upper, lg_b)
    tot_f = _split_dot(ones, lg_f)
    tot_b = _split_dot(ones, lg_b)

    qk = _dot(hb, wqk_ref[...])
    q = qk[:, :GLA_QK] * (GLA_DK ** -0.5)
    k = qk[:, GLA_QK:]
    qif_ref[...] = (q * jnp.exp(cs_f)).astype(BF16)
    kif_ref[...] = (k * jnp.exp(-cs_f)).astype(BF16)
    kef_ref[...] = (k * jnp.exp(tot_f - cs_f)).astype(BF16)
    qib_ref[...] = (q * jnp.exp(cs_b)).astype(BF16)
    kib_ref[...] = (k * jnp.exp(-cs_b)).astype(BF16)
    keb_ref[...] = (k * jnp.exp(tot_b - cs_b)).astype(BF16)
    for c in range(CHUNKS):
        r = c * GLA_CHUNK
        decf_ref[c:c + 1, :] = jnp.exp(tot_f[r:r + 1, :])
        decb_ref[c:c + 1, :] = jnp.exp(tot_b[r:r + 1, :])

    v_ref[...] = _dot(hb, wv_ref[...]).astype(BF16)
    sga_ref[...] = _silu(_dot(hb, wga_ref[...])).astype(BF16)

    zz = _dot(hb, wcc_ref[...]) * _dot(hb, wcx_ref[...])
    pos = lax.broadcasted_iota(jnp.int32, (TM, SC_WIDTH), 0)
    row_len = jnp.where(t == 0, TM, GRID_W)
    in_row = pos & (row_len - 1)
    z_prev = jnp.where(in_row != 0, pltpu.roll(zz, 1, 0), 0.0)
    z_next = jnp.where(in_row != row_len - 1, pltpu.roll(zz, TM - 1, 0), 0.0)
    zc = cw_ref[0:1, :] * z_prev + cw_ref[1:2, :] * zz + cw_ref[2:3, :] * z_next
    y = _dot(hb, wcb_ref[...]) * zc * _silu(_dot(hb, wgb_ref[...]))
    y_ref[...] = y.astype(BF16)


def _even_in(xx, sel, g, wts, nt):
    bsz = xx.shape[0]
    tok = lambda n: pl.BlockSpec((None, TM, n), lambda b, t: (b, t, 0))
    dec = pl.BlockSpec((None, None, CHUNKS, GLA_QK), lambda b, t: (b, t, 0, 0))
    tok_shape = lambda n: jax.ShapeDtypeStruct((bsz, nt * TM, n), BF16)
    dec_shape = jax.ShapeDtypeStruct((bsz, nt, CHUNKS, GLA_QK), F32)
    return pl.pallas_call(
        _even_in_kernel,
        grid=(bsz, nt),
        in_specs=[
            pl.BlockSpec((None, TM, D_MODEL), lambda b, t: (b, t, 0)),
            pl.BlockSpec((None, None, 3, D_MODEL), lambda b, t: (b, jnp.minimum(t, 1), 0, 0)),
        ] + [_const_spec(w.shape) for w in (g,) + tuple(wts)],
        out_specs=[tok(GLA_QK), tok(GLA_QK), tok(GLA_QK), dec,
                   tok(GLA_QK), tok(GLA_QK), tok(GLA_QK), dec,
                   tok(GLA_V), tok(GLA_V), tok(SC_WIDTH)],
        out_shape=[tok_shape(GLA_QK), tok_shape(GLA_QK), tok_shape(GLA_QK), dec_shape,
                   tok_shape(GLA_QK), tok_shape(GLA_QK), tok_shape(GLA_QK), dec_shape,
                   tok_shape(GLA_V), tok_shape(GLA_V), tok_shape(SC_WIDTH)],
        compiler_params=_params(2),
        name="even_in",
    )(xx, sel, g, *wts)


def _gla_tile(qi_ref, ki_ref, ke_ref, v_ref, dec_ref, st_ref, emit, reverse):
    r = lax.broadcasted_iota(jnp.int32, (GLA_CHUNK, GLA_CHUNK), 0)
    c = lax.broadcasted_iota(jnp.int32, (GLA_CHUNK, GLA_CHUNK), 1)
    mask = (c >= r) if reverse else (c <= r)
    order = range(CHUNKS - 1, -1, -1) if reverse else range(CHUNKS)
    for ch in order:
        rows = slice(ch * GLA_CHUNK, (ch + 1) * GLA_CHUNK)
        for hd in range(GLA_HEADS):
            kcols = slice(hd * GLA_DK, (hd + 1) * GLA_DK)
            vcols = slice(hd * GLA_DV, (hd + 1) * GLA_DV)
            qi = qi_ref[rows, kcols]
            vv = v_ref[rows, vcols]
            scores = jnp.where(mask, _dot_nt(qi, ki_ref[rows, kcols]), 0.0)
            state = st_ref[hd]
            o = _dot(scores.astype(BF16), vv) + _dot_nt(qi, state.astype(BF16))
            emit(rows, vcols, o)
            st_ref[hd] = dec_ref[ch:ch + 1, kcols] * state + _dot_tn(vv, ke_ref[rows, kcols])


def _gla_fwd_kernel(qi_ref, ki_ref, ke_ref, v_ref, dec_ref, o_ref, st_ref):
    @pl.when(pl.program_id(1) == 0)
    def _():
        st_ref[...] = jnp.zeros_like(st_ref)

    def emit(rows, vcols, o):
        o_ref[rows, vcols] = o

    _gla_tile(qi_ref, ki_ref, ke_ref, v_ref, dec_ref, st_ref, emit, reverse=False)


def _gla_fwd(qi, ki, ke, v, dec, nt):
    bsz = qi.shape[0]
    tok = lambda n: pl.BlockSpec((None, TM, n), lambda b, t: (b, t, 0))
    return pl.pallas_call(
        _gla_fwd_kernel,
        grid=(bsz, nt),
        in_specs=[tok(GLA_QK), tok(GLA_QK), tok(GLA_QK), tok(GLA_V),
                  pl.BlockSpec((None, None, CHUNKS, GLA_QK), lambda b, t: (b, t, 0, 0))],
        out_specs=tok(GLA_V),
        out_shape=jax.ShapeDtypeStruct((bsz, nt * TM, GLA_V), F32),
        scratch_shapes=[pltpu.VMEM((GLA_HEADS, GLA_DV, GLA_DK), F32)],
        compiler_params=_params(2),
        name="gla_fwd",
    )(qi, ki, ke, v, dec)


def _gla_bwd_kernel(qi_ref, ki_ref, ke_ref, v_ref, dec_ref, of_ref, sga_ref, y_ref, xx_ref,
                    mod_ref, gg_ref, wo_ref, wy_ref, out_ref, st_ref, ob_ref):
    @pl.when(pl.program_id(1) == 0)
    def _():
        st_ref[...] = jnp.zeros_like(st_ref)

    def emit(rows, vcols, o):
        ob_ref[rows, vcols] = o

    _gla_tile(qi_ref, ki_ref, ke_ref, v_ref, dec_ref, st_ref, emit, reverse=True)

    heads = []
    for hd in range(GLA_HEADS):
        vcols = slice(hd * GLA_DV, (hd + 1) * GLA_DV)
        o = of_ref[:, vcols] + ob_ref[:, vcols]
        ms = jnp.mean(o * o, axis=-1, keepdims=True)
        on = o * lax.rsqrt(ms + EPS) * gg_ref[...]
        heads.append((on * sga_ref[:, vcols].astype(F32)).astype(BF16))
    inner = jnp.concatenate(heads, axis=-1)
    proj = _dot(inner, wo_ref[...]) + _dot(y_ref[...], wy_ref[...])
    out_ref[...] = xx_ref[...] + mod_ref[2:3, :] * proj


def _bwd_tile(nt):
    return lambda j: jnp.where(j == 0, 0, nt - j)


def _gla_bwd(qi, ki, ke, v, dec, o_f, sga, y, xx, sel, gg, wo, wy, nt):
    bsz = qi.shape[0]
    tile = _bwd_tile(nt)
    tok = lambda n: pl.BlockSpec((None, TM, n), lambda b, j: (b, tile(j), 0))
    return pl.pallas_call(
        _gla_bwd_kernel,
        grid=(bsz, nt),
        in_specs=[tok(GLA_QK), tok(GLA_QK), tok(GLA_QK), tok(GLA_V),
                  pl.BlockSpec((None, None, CHUNKS, GLA_QK), lambda b, j: (b, tile(j), 0, 0)),
                  tok(GLA_V), tok(GLA_V), tok(SC_WIDTH), tok(D_MODEL),
                  pl.BlockSpec((None, None, 3, D_MODEL), lambda b, j: (b, jnp.minimum(j, 1), 0, 0)),
                  _const_spec(gg.shape), _const_spec(wo.shape), _const_spec(wy.shape)],
        out_specs=tok(D_MODEL),
        out_shape=jax.ShapeDtypeStruct((bsz, nt * TM, D_MODEL), F32),
        scratch_shapes=[pltpu.VMEM((GLA_HEADS, GLA_DV, GLA_DK), F32),
                        pltpu.VMEM((TM, GLA_V), F32)],
        compiler_params=_params(2),
        name="gla_bwd",
    )(qi, ki, ke, v, dec, o_f, sga, y, xx, sel, gg, wo, wy)


def _rg_gates(xc, wg_ref, ba_ref, bx_ref, sp_ref, a_ref, u_ref):
    for n in range(RG_BLOCKS):
        cols = slice(n * RG_BLOCK_W, (n + 1) * RG_BLOCK_W)
        xb = xc[:, cols]
        gates = _dot(xb.astype(BF16), wg_ref[n])
        r = jax.nn.sigmoid(gates[:, :RG_BLOCK_W] + ba_ref[:, cols])
        i = jax.nn.sigmoid(gates[:, RG_BLOCK_W:] + bx_ref[:, cols])
        log_a = (-RG_C) * r * sp_ref[:, cols]
        a = jnp.exp(log_a)
        a_ref[:, cols] = a
        u_ref[:, cols] = jnp.sqrt(-jnp.tanh(log_a) * (1.0 + a * a)) * (i * xb)


def _rg_scan(a_ref, u_ref, h_ref, carry_ref, reverse):
    row = lax.broadcasted_iota(jnp.int32, (SUBLANES, RG_WIDTH), 0)
    groups = TM // SUBLANES

    def body(g, carry):
        if reverse:
            g = groups - 1 - g
        r0 = pl.multiple_of(g * SUBLANES, SUBLANES)
        a = a_ref[pl.ds(r0, SUBLANES), :]
        u = u_ref[pl.ds(r0, SUBLANES), :]
        for s in (1, 2, 4):
            shift = SUBLANES - s if reverse else s
            valid = (row < SUBLANES - s) if reverse else (row >= s)
            a_sh = pltpu.roll(a, shift, 0)
            u_sh = pltpu.roll(u, shift, 0)
            u = u + a * jnp.where(valid, u_sh, 0.0)
            a = a * jnp.where(valid, a_sh, 1.0)
        hh = u + a * carry
        h_ref[pl.ds(r0, SUBLANES), :] = hh
        last = hh[0:1, :] if reverse else hh[SUBLANES - 1:SUBLANES, :]
        return jnp.broadcast_to(last, (SUBLANES, RG_WIDTH))

    carry_ref[...] = lax.fori_loop(0, groups, body, carry_ref[...])


def _odd_fwd_kernel(xx_ref, mod_ref, g_ref, wx_ref, cw_ref, cb_ref, wg_ref, ba_ref, bx_ref, lam_ref,
                    xr_ref, hf_ref, xbuf_ref, a_ref, u_ref, carry_ref, sp_ref):
    t = pl.program_id(1)

    @pl.when(t == 0)
    def _():
        carry_ref[...] = jnp.zeros_like(carry_ref)
        sp_ref[...] = _softplus(-lam_ref[...])

    @pl.when(t <= 1)
    def _():
        xbuf_ref[0:SUBLANES, :] = jnp.zeros((SUBLANES, RG_WIDTH), F32)

    h = _modulated_norm(xx_ref[...], g_ref[...], mod_ref[0:1, :], mod_ref[1:2, :])
    xr = _dot(h.astype(BF16), wx_ref[...])
    xr_ref[...] = xr
    xbuf_ref[SUBLANES:, :] = xr
    xc = cb_ref[...] + cw_ref[RG_CONV - 1:RG_CONV, :] * xr
    for j in range(RG_CONV - 1):
        back = RG_CONV - 1 - j
        xc = xc + cw_ref[j:j + 1, :] * xbuf_ref[SUBLANES - back:SUBLANES - back + TM, :]
    xbuf_ref[0:SUBLANES, :] = xr[TM - SUBLANES:, :]

    _rg_gates(xc, wg_ref, ba_ref, bx_ref, sp_ref, a_ref, u_ref)
    _rg_scan(a_ref, u_ref, hf_ref, carry_ref, reverse=False)


def _odd_fwd(xx, sel, g, wx, cw, cb, wg, ba, bx, lam, nt):
    bsz = xx.shape[0]
    tok = lambda n: pl.BlockSpec((None, TM, n), lambda b, t: (b, t, 0))
    consts = (g, wx, cw, cb, wg, ba, bx, lam)
    wide = jax.ShapeDtypeStruct((bsz, nt * TM, RG_WIDTH), F32)
    return pl.pallas_call(
        _odd_fwd_kernel,
        grid=(bsz, nt),
        in_specs=[tok(D_MODEL),
                  pl.BlockSpec((None, None, 3, D_MODEL), lambda b, t: (b, jnp.minimum(t, 1), 0, 0))]
        + [_const_spec(w.shape) for w in consts],
        out_specs=[tok(RG_WIDTH), tok(RG_WIDTH)],
        out_shape=[wide, wide],
        scratch_shapes=[pltpu.VMEM((TM + SUBLANES, RG_WIDTH), F32),
                        pltpu.VMEM((TM, RG_WIDTH), F32),
                        pltpu.VMEM((TM, RG_WIDTH), F32),
                        pltpu.VMEM((SUBLANES, RG_WIDTH), F32),
                        pltpu.VMEM((1, RG_WIDTH), F32)],
        compiler_params=_params(2),
        name="odd_fwd",
    )(xx, sel, *consts)


def _odd_bwd_kernel(xx_ref, mod_ref, g_ref, xr_ref, hf_ref, wgate_ref, cw_ref, cb_ref, wg_ref, ba_ref,
                    bx_ref, lam_ref, wout_ref, fg_ref, out_ref,
                    xbuf_ref, a_ref, u_ref, hb_ref, carry_ref, sp_ref):
    j = pl.program_id(1)

    @pl.when(j == 0)
    def _():
        carry_ref[...] = jnp.zeros_like(carry_ref)
        sp_ref[...] = _softplus(-lam_ref[...])

    @pl.when(j <= 1)
    def _():
        xbuf_ref[TM:, :] = jnp.zeros((SUBLANES, RG_WIDTH), F32)

    xr = xr_ref[...]
    xbuf_ref[0:TM, :] = xr
    xc = cb_ref[...] + cw_ref[RG_CONV - 1:RG_CONV, :] * xr
    for k in range(RG_CONV - 1):
        ahead = RG_CONV - 1 - k
        xc = xc + cw_ref[k:k + 1, :] * xbuf_ref[ahead:ahead + TM, :]
    xbuf_ref[TM:, :] = xr[0:SUBLANES, :]

    _rg_gates(xc, wg_ref, ba_ref, bx_ref, sp_ref, a_ref, u_ref)
    _rg_scan(a_ref, u_ref, hb_ref, carry_ref, reverse=True)

    @pl.when(j > 0)
    def _():
        x = xx_ref[...]
        h = _modulated_norm(x, g_ref[...], mod_ref[0:1, :], mod_ref[1:2, :])
        gate = _dot(h.astype(BF16), wgate_ref[...])
        yy = (hf_ref[...] + hb_ref[...]) * _silu(gate)
        xn = x + mod_ref[2:3, :] * _dot(yy.astype(BF16), wout_ref[...])
        ms = jnp.mean(xn * xn, axis=-1, keepdims=True)
        out_ref[...] = xn * lax.rsqrt(ms + EPS) * fg_ref[...]


def _odd_bwd(xx, sel, g, xr, hf, wgate, cw, cb, wg, ba, bx, lam, wout, fg, nt):
    bsz = xx.shape[0]
    tile = _bwd_tile(nt)
    tok = lambda n: pl.BlockSpec((None, TM, n), lambda b, j: (b, tile(j), 0))
    consts = (wgate, cw, cb, wg, ba, bx, lam, wout, fg)
    out_block = lambda b, j: (b, jnp.where(j == 0, nt - 2, nt - 1 - j), 0)
    return pl.pallas_call(
        _odd_bwd_kernel,
        grid=(bsz, nt),
        in_specs=[tok(D_MODEL),
                  pl.BlockSpec((None, None, 3, D_MODEL), lambda b, j: (b, jnp.minimum(j, 1), 0, 0)),
                  _const_spec(g.shape), tok(RG_WIDTH), tok(RG_WIDTH)]
        + [_const_spec(w.shape) for w in consts],
        out_specs=pl.BlockSpec((None, TM, D_MODEL), out_block),
        out_shape=jax.ShapeDtypeStruct((bsz, (nt - 1) * TM, D_MODEL), F32),
        scratch_shapes=[pltpu.VMEM((TM + SUBLANES, RG_WIDTH), F32),
                        pltpu.VMEM((TM, RG_WIDTH), F32),
                        pltpu.VMEM((TM, RG_WIDTH), F32),
                        pltpu.VMEM((TM, RG_WIDTH), F32),
                        pltpu.VMEM((SUBLANES, RG_WIDTH), F32),
                        pltpu.VMEM((1, RG_WIDTH), F32)],
        compiler_params=_params(2),
        name="odd_bwd",
    )(xx, sel, g, xr, hf, *consts)


def _select_mods(m, bsz):
    per = m[:bsz].reshape(bsz, 1, 3, D_MODEL)
    ctx = jnp.broadcast_to(m[bsz].reshape(1, 1, 3, D_MODEL), (bsz, 1, 3, D_MODEL))
    return jnp.concatenate([ctx, per], axis=1)


def _gate_weights(w_a, w_x):
    return jnp.concatenate([w_a, w_x], axis=-1).astype(BF16)


def kernel(x, c, ctx, c_ctx, norm_g, w_mod, b_mod, e_w_in, e_w_a2, e_b_a2, e_gla_g, e_conv_w, e_w_out,
           o_w_in, o_conv_w, o_conv_b, o_w_a, o_b_a, o_w_x, o_b_x, o_lam, o_w_out, final_g):
    bsz, seq, _ = x.shape
    assert ctx.shape[1] == TM and seq % TM == 0 and bsz < SUBLANES
    assert w_mod.shape[0] == 2
    nt = 1 + seq // TM

    rows = jnp.concatenate([c, c_ctx[None, :], jnp.zeros((SUBLANES - bsz - 1, D_MODEL), F32)], axis=0)
    mods = _mods(rows, w_mod, b_mod)
    sel0 = _select_mods(mods[0], bsz)
    sel1 = _select_mods(mods[1], bsz)

    xx = jnp.concatenate([ctx, x], axis=1)

    w_in = e_w_in[0]
    offs = {}
    off = 0
    for name, size in (("q", GLA_QK), ("k", GLA_QK), ("v", GLA_V), ("ga", GLA_V), ("af", GLA_LOWRANK),
                       ("ab", GLA_LOWRANK), ("cb", SC_WIDTH), ("cc", SC_WIDTH), ("cx", SC_WIDTH),
                       ("gb", SC_WIDTH)):
        offs[name] = (off, off + size)
        off += size
    cols = lambda lo, hi: w_in[:, lo:hi].astype(BF16)
    wqk = cols(offs["q"][0], offs["k"][1])
    wv = cols(*offs["v"])
    wga = cols(*offs["ga"])
    wa = jnp.pad(w_in[:, offs["af"][0]:offs["ab"][1]], ((0, 0), (0, LANES - 2 * GLA_LOWRANK))).astype(BF16)
    w2 = jnp.zeros((LANES, 2 * GLA_QK), F32)
    w2 = w2.at[:GLA_LOWRANK, :GLA_QK].set(e_w_a2[0, 0])
    w2 = w2.at[GLA_LOWRANK:2 * GLA_LOWRANK, GLA_QK:].set(e_w_a2[0, 1]).astype(BF16)
    b2 = e_b_a2[0].reshape(1, 2 * GLA_QK)
    wts = (wqk, wv, wga, wa, w2, b2, cols(*offs["cb"]), cols(*offs["cc"]), cols(*offs["cx"]),
           cols(*offs["gb"]), e_conv_w[0])
    (qi_f, ki_f, ke_f, dec_f, qi_b, ki_b, ke_b, dec_b, v, sga, y) = _even_in(
        xx, sel0, norm_g[0:1], wts, nt)
    o_f = _gla_fwd(qi_f, ki_f, ke_f, v, dec_f, nt)
    w_out = e_w_out[0].astype(BF16)
    xx = _gla_bwd(qi_b, ki_b, ke_b, v, dec_b, o_f, sga, y, xx, sel0, e_gla_g[0:1],
                  w_out[:GLA_V], w_out[GLA_V:], nt)

    w_in = o_w_in[0]
    wide = lambda a: a.reshape(1, RG_WIDTH)
    xr, h_f = _odd_fwd(xx, sel1, norm_g[1:2], w_in[:, :RG_WIDTH].astype(BF16), o_conv_w[0, 0],
                       wide(o_conv_b[0, 0]), _gate_weights(o_w_a[0, 0], o_w_x[0, 0]),
                       wide(o_b_a[0, 0]), wide(o_b_x[0, 0]), wide(o_lam[0, 0]), nt)
    return _odd_bwd(xx, sel1, norm_g[1:2], xr, h_f, w_in[:, RG_WIDTH:].astype(BF16), o_conv_w[0, 1],
                    wide(o_conv_b[0, 1]), _gate_weights(o_w_a[0, 1], o_w_x[0, 1]),
                    wide(o_b_a[0, 1]), wide(o_b_x[0, 1]), wide(o_lam[0, 1]), o_w_out[0].astype(BF16),
                    final_g.reshape(1, D_MODEL), nt)
```

```python
import functools

import jax
import jax.numpy as jnp
from jax import lax
from jax.experimental import pallas as pl
from jax.experimental.pallas import tpu as pltpu

F32 = jnp.float32
BF16 = jnp.bfloat16

D_MODEL = 1024
EPS = 1e-6
GLA_HEADS = 4
GLA_DK = 128
GLA_DV = 256
GLA_QK = GLA_HEADS * GLA_DK
GLA_V = GLA_HEADS * GLA_DV
GLA_LOWRANK = 16
GLA_TAU = 16.0
GLA_CHUNK = 64
GRID_W = 64
SC_WIDTH = D_MODEL
RG_WIDTH = 2 * D_MODEL
RG_BLOCKS = 16
RG_BLOCK_W = 128
RG_C = 8.0
RG_CONV = 4

TM = 256
CHUNKS = TM // GLA_CHUNK
SUBLANES = 8
LANES = 128
VMEM_LIMIT = 56 * 1024 * 1024
SEG_LEN = TM // SUBLANES
SEG_PITCH = SEG_LEN + SUBLANES
HALO_ROWS = (RG_CONV - 1) * SUBLANES
SCAN_LANES = 1024
SCAN_UNROLL = 4
LOG2_E = 1.4426950408889634
F32_TINY = 1.1754943508222875e-38


def _silu(x):
    return x * jax.nn.sigmoid(x)


def _log_sigmoid(z):
    return jnp.minimum(z, 0.0) - jnp.log1p(jnp.exp(-jnp.abs(z)))


def _softplus(z):
    return jnp.maximum(z, 0.0) + jnp.log1p(jnp.exp(-jnp.abs(z)))


def _dot(a, b):
    return jnp.dot(a, b, preferred_element_type=F32)


def _dot_nt(a, b):
    return lax.dot_general(a, b, (((1,), (1,)), ((), ())), preferred_element_type=F32)


def _dot_tn(a, b):
    return lax.dot_general(a, b, (((0,), (0,)), ((), ())), preferred_element_type=F32)


def _modulated_norm(x, g, shift, scale):
    ms = jnp.mean(x * x, axis=-1, keepdims=True)
    return (x * lax.rsqrt(ms + EPS) * g) * (1.0 + scale) + shift


def _params(n_axes):
    return pltpu.CompilerParams(
        dimension_semantics=("arbitrary",) * n_axes,
        vmem_limit_bytes=VMEM_LIMIT)


def _const_spec(shape):
    zeros = (0,) * len(shape)
    return pl.BlockSpec(shape, lambda *_: zeros, pipeline_mode=pl.Buffered(1))


def _mods_kernel(s_ref, w_ref, b_ref, o_ref):
    s = _silu(s_ref[...])
    o_ref[0] = jnp.dot(s, w_ref[0], preferred_element_type=F32,
                       precision=lax.Precision.HIGHEST) + b_ref[0]


def _mods(rows, w_mod, b_mod):
    depth = w_mod.shape[0]
    nb = 3
    return pl.pallas_call(
        _mods_kernel,
        grid=(depth, nb),
        in_specs=[
            pl.BlockSpec((SUBLANES, D_MODEL), lambda l, j: (0, 0)),
            pl.BlockSpec((1, D_MODEL, D_MODEL), lambda l, j: (l, 0, j)),
            pl.BlockSpec((1, 1, D_MODEL), lambda l, j: (l, 0, j)),
        ],
        out_specs=pl.BlockSpec((1, SUBLANES, D_MODEL), lambda l, j: (l, 0, j)),
        out_shape=jax.ShapeDtypeStruct((depth, SUBLANES, 3 * D_MODEL), F32),
        compiler_params=_params(2),
        name="mods",
    )(rows, w_mod, b_mod.reshape(depth, 1, 3 * D_MODEL))


def _split_dot(mat, x):
    hi = x.astype(BF16)
    lo = (x - hi.astype(F32)).astype(BF16)
    return _dot(mat, hi) + _dot(mat, lo)


def _even_in_kernel(xx_ref, mod_ref, g_ref, wqk_ref, wv_ref, wga_ref, wa_ref, w2_ref, b2_ref,
                    wcb_ref, wcc_ref, wcx_ref, wgb_ref, cw_ref,
                    qif_ref, kif_ref, kef_ref, decf_ref, qib_ref, kib_ref, keb_ref, decb_ref,
                    v_ref, sga_ref, y_ref):
    t = pl.program_id(1)
    h = _modulated_norm(xx_ref[...], g_ref[...], mod_ref[0:1, :], mod_ref[1:2, :])
    hb = h.astype(BF16)

    a_lr = _dot(hb, wa_ref[...])
    z = _dot(a_lr.astype(BF16), w2_ref[...]) + b2_ref[...]
    lg = _log_sigmoid(z) * (1.0 / GLA_TAU)

    row = lax.broadcasted_iota(jnp.int32, (TM, TM), 0)
    col = lax.broadcasted_iota(jnp.int32, (TM, TM), 1)
    same = (row // GLA_CHUNK) == (col // GLA_CHUNK)
    lower = jnp.where(same & (col <= row), 1.0, 0.0).astype(BF16)
    upper = jnp.where(same & (col >= row), 1.0, 0.0).astype(BF16)
    ones = jnp.where(same, 1.0, 0.0).astype(BF16)

    lg_f = lg[:, :GLA_QK]
    lg_b = lg[:, GLA_QK:]
    cs_f = _split_dot(lower, lg_f)
    cs_b = _split_dot(upper, lg_b)
    tot_f = _split_dot(ones, lg_f)
    tot_b = _split_dot(ones, lg_b)

    qk = _dot(hb, wqk_ref[...])
    q = qk[:, :GLA_QK] * (GLA_DK ** -0.5)
    k = qk[:, GLA_QK:]
    qif_ref[...] = (q * jnp.exp(cs_f)).astype(BF16)
    kif_ref[...] = (k * jnp.exp(-cs_f)).astype(BF16)
    kef_ref[...] = (k * jnp.exp(tot_f - cs_f)).astype(BF16)
    qib_ref[...] = (q * jnp.exp(cs_b)).astype(BF16)
    kib_ref[...] = (k * jnp.exp(-cs_b)).astype(BF16)
    keb_ref[...] = (k * jnp.exp(tot_b - cs_b)).astype(BF16)
    for c in range(CHUNKS):
        r = c * GLA_CHUNK
        decf_ref[c:c + 1, :] = jnp.exp(tot_f[r:r + 1, :])
        decb_ref[c:c + 1, :] = jnp.exp(tot_b[r:r + 1, :])

    v_ref[...] = _dot(hb, wv_ref[...]).astype(BF16)
    sga_ref[...] = _silu(_dot(hb, wga_ref[...])).astype(BF16)

    zz = _dot(hb, wcc_ref[...]) * _dot(hb, wcx_ref[...])
    pos = lax.broadcasted_iota(jnp.int32, (TM, SC_WIDTH), 0)
    row_len = jnp.where(t == 0, TM, GRID_W)
    in_row = pos & (row_len - 1)
    z_prev = jnp.where(in_row != 0, pltpu.roll(zz, 1, 0), 0.0)
    z_next = jnp.where(in_row != row_len - 1, pltpu.roll(zz, TM - 1, 0), 0.0)
    zc = cw_ref[0:1, :] * z_prev + cw_ref[1:2, :] * zz + cw_ref[2:3, :] * z_next
    y = _dot(hb, wcb_ref[...]) * zc * _silu(_dot(hb, wgb_ref[...]))
    y_ref[...] = y.astype(BF16)


def _even_in(xx, sel, g, wts, nt):
    bsz = xx.shape[0]
    tok = lambda n: pl.BlockSpec((None, TM, n), lambda b, t: (b, t, 0))
    dec = pl.BlockSpec((None, None, CHUNKS, GLA_QK), lambda b, t: (b, t, 0, 0))
    tok_shape = lambda n: jax.ShapeDtypeStruct((bsz, nt * TM, n), BF16)
    dec_shape = jax.ShapeDtypeStruct((bsz, nt, CHUNKS, GLA_QK), F32)
    return pl.pallas_call(
        _even_in_kernel,
        grid=(bsz, nt),
        in_specs=[
            pl.BlockSpec((None, TM, D_MODEL), lambda b, t: (b, t, 0)),
            pl.BlockSpec((None, None, 3, D_MODEL), lambda b, t: (b, jnp.minimum(t, 1), 0, 0)),
        ] + [_const_spec(w.shape) for w in (g,) + tuple(wts)],
        out_specs=[tok(GLA_QK), tok(GLA_QK), tok(GLA_QK), dec,
                   tok(GLA_QK), tok(GLA_QK), tok(GLA_QK), dec,
                   tok(GLA_V), tok(GLA_V), tok(SC_WIDTH)],
        out_shape=[tok_shape(GLA_QK), tok_shape(GLA_QK), tok_shape(GLA_QK), dec_shape,
                   tok_shape(GLA_QK), tok_shape(GLA_QK), tok_shape(GLA_QK), dec_shape,
                   tok_shape(GLA_V), tok_shape(GLA_V), tok_shape(SC_WIDTH)],
        compiler_params=_params(2),
        name="even_in",
    )(xx, sel, g, *wts)


def _gla_tile(qi_ref, ki_ref, ke_ref, v_ref, dec_ref, st_ref, emit, reverse):
    r = lax.broadcasted_iota(jnp.int32, (GLA_CHUNK, GLA_CHUNK), 0)
    c = lax.broadcasted_iota(jnp.int32, (GLA_CHUNK, GLA_CHUNK), 1)
    mask = (c >= r) if reverse else (c <= r)
    order = range(CHUNKS - 1, -1, -1) if reverse else range(CHUNKS)
    for ch in order:
        rows = slice(ch * GLA_CHUNK, (ch + 1) * GLA_CHUNK)
        for hd in range(GLA_HEADS):
            kcols = slice(hd * GLA_DK, (hd + 1) * GLA_DK)
            vcols = slice(hd * GLA_DV, (hd + 1) * GLA_DV)
            qi = qi_ref[rows, kcols]
            vv = v_ref[rows, vcols]
            scores = jnp.where(mask, _dot_nt(qi, ki_ref[rows, kcols]), 0.0)
            state = st_ref[hd]
            o = _dot(scores.astype(BF16), vv) + _dot_nt(qi, state.astype(BF16))
            emit(rows, vcols, o)
            st_ref[hd] = dec_ref[ch:ch + 1, kcols] * state + _dot_tn(vv, ke_ref[rows, kcols])


def _gla_fwd_kernel(qi_ref, ki_ref, ke_ref, v_ref, dec_ref, o_ref, st_ref):
    @pl.when(pl.program_id(1) == 0)
    def _():
        st_ref[...] = jnp.zeros_like(st_ref)

    def emit(rows, vcols, o):
        o_ref[rows, vcols] = o

    _gla_tile(qi_ref, ki_ref, ke_ref, v_ref, dec_ref, st_ref, emit, reverse=False)


def _gla_fwd(qi, ki, ke, v, dec, nt):
    bsz = qi.shape[0]
    tok = lambda n: pl.BlockSpec((None, TM, n), lambda b, t: (b, t, 0))
    return pl.pallas_call(
        _gla_fwd_kernel,
        grid=(bsz, nt),
        in_specs=[tok(GLA_QK), tok(GLA_QK), tok(GLA_QK), tok(GLA_V),
                  pl.BlockSpec((None, None, CHUNKS, GLA_QK), lambda b, t: (b, t, 0, 0))],
        out_specs=tok(GLA_V),
        out_shape=jax.ShapeDtypeStruct((bsz, nt * TM, GLA_V), F32),
        scratch_shapes=[pltpu.VMEM((GLA_HEADS, GLA_DV, GLA_DK), F32)],
        compiler_params=_params(2),
        name="gla_fwd",
    )(qi, ki, ke, v, dec)


def _gla_bwd_kernel(qi_ref, ki_ref, ke_ref, v_ref, dec_ref, of_ref, sga_ref, y_ref, xx_ref,
                    mod_ref, gg_ref, wo_ref, wy_ref, out_ref, st_ref, ob_ref):
    @pl.when(pl.program_id(1) == 0)
    def _():
        st_ref[...] = jnp.zeros_like(st_ref)

    def emit(rows, vcols, o):
        ob_ref[rows, vcols] = o

    _gla_tile(qi_ref, ki_ref, ke_ref, v_ref, dec_ref, st_ref, emit, reverse=True)

    heads = []
    for hd in range(GLA_HEADS):
        vcols = slice(hd * GLA_DV, (hd + 1) * GLA_DV)
        o = of_ref[:, vcols] + ob_ref[:, vcols]
        ms = jnp.mean(o * o, axis=-1, keepdims=True)
        on = o * lax.rsqrt(ms + EPS) * gg_ref[...]
        heads.append((on * sga_ref[:, vcols].astype(F32)).astype(BF16))
    inner = jnp.concatenate(heads, axis=-1)
    proj = _dot(inner, wo_ref[...]) + _dot(y_ref[...], wy_ref[...])
    out_ref[...] = xx_ref[...] + mod_ref[2:3, :] * proj


def _bwd_tile(nt):
    return lambda j: jnp.where(j == 0, 0, nt - j)


def _gla_bwd(qi, ki, ke, v, dec, o_f, sga, y, xx, sel, gg, wo, wy, nt):
    bsz = qi.shape[0]
    tile = _bwd_tile(nt)
    tok = lambda n: pl.BlockSpec((None, TM, n), lambda b, j: (b, tile(j), 0))
    return pl.pallas_call(
        _gla_bwd_kernel,
        grid=(bsz, nt),
        in_specs=[tok(GLA_QK), tok(GLA_QK), tok(GLA_QK), tok(GLA_V),
                  pl.BlockSpec((None, None, CHUNKS, GLA_QK), lambda b, j: (b, tile(j), 0, 0)),
                  tok(GLA_V), tok(GLA_V), tok(SC_WIDTH), tok(D_MODEL),
                  pl.BlockSpec((None, None, 3, D_MODEL), lambda b, j: (b, jnp.minimum(j, 1), 0, 0)),
                  _const_spec(gg.shape), _const_spec(wo.shape), _const_spec(wy.shape)],
        out_specs=tok(D_MODEL),
        out_shape=jax.ShapeDtypeStruct((bsz, nt * TM, D_MODEL), F32),
        scratch_shapes=[pltpu.VMEM((GLA_HEADS, GLA_DV, GLA_DK), F32),
                        pltpu.VMEM((TM, GLA_V), F32)],
        compiler_params=_params(2),
        name="gla_bwd",
    )(qi, ki, ke, v, dec, o_f, sga, y, xx, sel, gg, wo, wy)


def _to_segment_major(src_ref, stage_ref, width):
    slabs = width // LANES
    for n in range(slabs):
        for s in range(SUBLANES):
            stage_ref[n, s * SEG_PITCH:s * SEG_PITCH + SEG_LEN, :] = (
                src_ref[s * SEG_LEN:(s + 1) * SEG_LEN, n * LANES:(n + 1) * LANES])
    groups = [
        jnp.concatenate([stage_ref[n, pl.ds(j, SUBLANES, stride=SEG_PITCH), :] for n in range(slabs)], axis=1)
        for j in range(SEG_LEN)]
    return jnp.concatenate(groups, axis=0)


def _from_segment_major(val, stage_ref, dst_ref, width):
    slabs = width // LANES
    for j in range(SEG_LEN):
        for n in range(slabs):
            stage_ref[n, pl.ds(j, SUBLANES, stride=SEG_PITCH), :] = (
                val[j * SUBLANES:(j + 1) * SUBLANES, n * LANES:(n + 1) * LANES])
    for n in range(slabs):
        for s in range(SUBLANES):
            dst_ref[s * SEG_LEN:(s + 1) * SEG_LEN, n * LANES:(n + 1) * LANES] = (
                stage_ref[n, s * SEG_PITCH:s * SEG_PITCH + SEG_LEN, :])


def _rg_conv(xr, halo_ref, cw_ref, cb_ref, reverse):
    row = lax.broadcasted_iota(jnp.int32, (SUBLANES, RG_WIDTH), 0)
    edge = []
    for g in range(RG_CONV - 1):
        grp = slice(g * SUBLANES, (g + 1) * SUBLANES)
        if reverse:
            edge.append(jnp.where(row == SUBLANES - 1, pltpu.roll(halo_ref[grp, :], SUBLANES - 1, 0),
                                  pltpu.roll(xr[grp, :], SUBLANES - 1, 0)))
        else:
            cur = xr[TM - (RG_CONV - 1 - g) * SUBLANES:TM - (RG_CONV - 2 - g) * SUBLANES, :]
            edge.append(jnp.where(row == 0, pltpu.roll(halo_ref[grp, :], 1, 0), pltpu.roll(cur, 1, 0)))
    xc = cb_ref[...] + cw_ref[RG_CONV - 1:RG_CONV, :] * xr
    for k in range(1, RG_CONV):
        if reverse:
            tap = jnp.concatenate([xr[k * SUBLANES:, :]] + edge[:k], axis=0)
        else:
            tap = jnp.concatenate(edge[RG_CONV - 1 - k:] + [xr[:TM - k * SUBLANES, :]], axis=0)
        xc = xc + cw_ref[RG_CONV - 1 - k:RG_CONV - k, :] * tap
    halo = xr[:HALO_ROWS, :] if reverse else xr[TM - HALO_ROWS:, :]
    return xc, halo


def _rg_gates(xc, wg_ref, ba_ref, bx_ref, decay_ref, a_ref, u_ref):
    for n in range(RG_BLOCKS):
        cols = slice(n * RG_BLOCK_W, (n + 1) * RG_BLOCK_W)
        xb = xc[:, cols]
        gates = _dot(xb.astype(BF16), wg_ref[n])
        th_r = jnp.tanh(0.5 * (gates[:, :RG_BLOCK_W] + ba_ref[:, cols]))
        th_i = jnp.tanh(0.5 * (gates[:, RG_BLOCK_W:] + bx_ref[:, cols]))
        half_decay = decay_ref[:, cols]
        a = jnp.exp2(th_r * half_decay + half_decay)
        a_ref[:, cols] = a
        y = 1.0 - a * a
        root = y * lax.rsqrt(jnp.maximum(y, F32_TINY))
        u_ref[:, cols] = root * ((0.5 * th_i + 0.5) * xb)


def _rg_scan(a_ref, u_ref, h_ref, carry_ref, reverse):
    row = lax.broadcasted_iota(jnp.int32, (SUBLANES, SCAN_LANES), 0)
    shift = SUBLANES - 1 if reverse else 1
    for c0 in range(0, RG_WIDTH, SCAN_LANES):
        cols = slice(c0, c0 + SCAN_LANES)

        def local(i, state):
            h, prod = state
            j = SEG_LEN - 1 - i if reverse else i
            grp = pl.ds(pl.multiple_of(j * SUBLANES, SUBLANES), SUBLANES)
            a = a_ref[grp, cols]
            h = a * h + u_ref[grp, cols]
            prod = prod * a
            h_ref[grp, cols] = h
            a_ref[grp, cols] = prod
            return h, prod

        h_end, a_end = lax.fori_loop(
            0, SEG_LEN, local,
            (jnp.zeros((SUBLANES, SCAN_LANES), F32), jnp.ones((SUBLANES, SCAN_LANES), F32)),
            unroll=SCAN_UNROLL)

        entering = carry_ref[:, cols]
        for s in range(SUBLANES - 1):
            nxt = pltpu.roll(a_end * entering + h_end, shift, 0)
            target = SUBLANES - 2 - s if reverse else s + 1
            entering = jnp.where(row == target, nxt, entering)
        carry_ref[:, cols] = pltpu.roll(a_end * entering + h_end, shift, 0)

        def fixup(j, _):
            grp = pl.ds(pl.multiple_of(j * SUBLANES, SUBLANES), SUBLANES)
            h_ref[grp, cols] = h_ref[grp, cols] + a_ref[grp, cols] * entering
            return 0

        lax.fori_loop(0, SEG_LEN, fixup, 0, unroll=SCAN_UNROLL)


def _odd_fwd_kernel(xx_ref, mod_ref, g_ref, wx_ref, cw_ref, cb_ref, wg_ref, ba_ref, bx_ref, lam_ref,
                    xr_ref, hf_ref, stage_ref, halo_ref, a_ref, u_ref, carry_ref, decay_ref):
    t = pl.program_id(1)

    @pl.when(t == 0)
    def _():
        carry_ref[...] = jnp.zeros_like(carry_ref)
        decay_ref[...] = (-0.5 * RG_C * LOG2_E) * _softplus(-lam_ref[...])

    @pl.when(t <= 1)
    def _():
        halo_ref[...] = jnp.zeros_like(halo_ref)

    x = _to_segment_major(xx_ref, stage_ref, D_MODEL)
    h = _modulated_norm(x, g_ref[...], mod_ref[0:1, :], mod_ref[1:2, :])
    xr = _dot(h.astype(BF16), wx_ref[...])
    xr_ref[...] = xr
    xc, halo = _rg_conv(xr, halo_ref, cw_ref, cb_ref, reverse=False)
    halo_ref[...] = halo

    _rg_gates(xc, wg_ref, ba_ref, bx_ref, decay_ref, a_ref, u_ref)
    _rg_scan(a_ref, u_ref, hf_ref, carry_ref, reverse=False)


def _odd_fwd(xx, sel, g, wx, cw, cb, wg, ba, bx, lam, nt):
    bsz = xx.shape[0]
    tok = lambda n: pl.BlockSpec((None, TM, n), lambda b, t: (b, t, 0))
    consts = (g, wx, cw, cb, wg, ba, bx, lam)
    wide = jax.ShapeDtypeStruct((bsz, nt * TM, RG_WIDTH), F32)
    return pl.pallas_call(
        _odd_fwd_kernel,
        grid=(bsz, nt),
        in_specs=[tok(D_MODEL),
                  pl.BlockSpec((None, None, 3, D_MODEL), lambda b, t: (b, jnp.minimum(t, 1), 0, 0))]
        + [_const_spec(w.shape) for w in consts],
        out_specs=[tok(RG_WIDTH), tok(RG_WIDTH)],
        out_shape=[wide, wide],
        scratch_shapes=[pltpu.VMEM((D_MODEL // LANES, SUBLANES * SEG_PITCH, LANES), F32),
                        pltpu.VMEM((HALO_ROWS, RG_WIDTH), F32),
                        pltpu.VMEM((TM, RG_WIDTH), F32),
                        pltpu.VMEM((TM, RG_WIDTH), F32),
                        pltpu.VMEM((SUBLANES, RG_WIDTH), F32),
                        pltpu.VMEM((1, RG_WIDTH), F32)],
        compiler_params=_params(2),
        name="odd_fwd",
    )(xx, sel, *consts)


def _odd_bwd_kernel(xx_ref, mod_ref, g_ref, xr_ref, hf_ref, wgate_ref, cw_ref, cb_ref, wg_ref, ba_ref,
                    bx_ref, lam_ref, wout_ref, fg_ref, out_ref,
                    stage_ref, halo_ref, a_ref, u_ref, hb_ref, carry_ref, decay_ref):
    j = pl.program_id(1)

    @pl.when(j == 0)
    def _():
        carry_ref[...] = jnp.zeros_like(carry_ref)
        decay_ref[...] = (-0.5 * RG_C * LOG2_E) * _softplus(-lam_ref[...])

    @pl.when(j <= 1)
    def _():
        halo_ref[...] = jnp.zeros_like(halo_ref)

    xc, halo = _rg_conv(xr_ref[...], halo_ref, cw_ref, cb_ref, reverse=True)
    halo_ref[...] = halo

    _rg_gates(xc, wg_ref, ba_ref, bx_ref, decay_ref, a_ref, u_ref)
    _rg_scan(a_ref, u_ref, hb_ref, carry_ref, reverse=True)

    @pl.when(j > 0)
    def _():
        x = _to_segment_major(xx_ref, stage_ref, D_MODEL)
        h = _modulated_norm(x, g_ref[...], mod_ref[0:1, :], mod_ref[1:2, :])
        gate = _dot(h.astype(BF16), wgate_ref[...])
        yy = (hf_ref[...] + hb_ref[...]) * _silu(gate)
        xn = x + mod_ref[2:3, :] * _dot(yy.astype(BF16), wout_ref[...])
        ms = jnp.mean(xn * xn, axis=-1, keepdims=True)
        _from_segment_major(xn * lax.rsqrt(ms + EPS) * fg_ref[...], stage_ref, out_ref, D_MODEL)


def _odd_bwd(xx, sel, g, xr, hf, wgate, cw, cb, wg, ba, bx, lam, wout, fg, nt):
    bsz = xx.shape[0]
    tile = _bwd_tile(nt)
    tok = lambda n: pl.BlockSpec((None, TM, n), lambda b, j: (b, tile(j), 0))
    consts = (wgate, cw, cb, wg, ba, bx, lam, wout, fg)
    out_block = lambda b, j: (b, jnp.where(j == 0, nt - 2, nt - 1 - j), 0)
    return pl.pallas_call(
        _odd_bwd_kernel,
        grid=(bsz, nt),
        in_specs=[tok(D_MODEL),
                  pl.BlockSpec((None, None, 3, D_MODEL), lambda b, j: (b, jnp.minimum(j, 1), 0, 0)),
                  _const_spec(g.shape), tok(RG_WIDTH), tok(RG_WIDTH)]
        + [_const_spec(w.shape) for w in consts],
        out_specs=pl.BlockSpec((None, TM, D_MODEL), out_block),
        out_shape=jax.ShapeDtypeStruct((bsz, (nt - 1) * TM, D_MODEL), F32),
        scratch_shapes=[pltpu.VMEM((D_MODEL // LANES, SUBLANES * SEG_PITCH, LANES), F32),
                        pltpu.VMEM((HALO_ROWS, RG_WIDTH), F32),
                        pltpu.VMEM((TM, RG_WIDTH), F32),
                        pltpu.VMEM((TM, RG_WIDTH), F32),
                        pltpu.VMEM((TM, RG_WIDTH), F32),
                        pltpu.VMEM((SUBLANES, RG_WIDTH), F32),
                        pltpu.VMEM((1, RG_WIDTH), F32)],
        compiler_params=_params(2),
        name="odd_bwd",
    )(xx, sel, g, xr, hf, *consts)


def _select_mods(m, bsz):
    per = m[:bsz].reshape(bsz, 1, 3, D_MODEL)
    ctx = jnp.broadcast_to(m[bsz].reshape(1, 1, 3, D_MODEL), (bsz, 1, 3, D_MODEL))
    return jnp.concatenate([ctx, per], axis=1)


def _gate_weights(w_a, w_x):
    return jnp.concatenate([w_a, w_x], axis=-1).astype(BF16)


def kernel(x, c, ctx, c_ctx, norm_g, w_mod, b_mod, e_w_in, e_w_a2, e_b_a2, e_gla_g, e_conv_w, e_w_out,
           o_w_in, o_conv_w, o_conv_b, o_w_a, o_b_a, o_w_x, o_b_x, o_lam, o_w_out, final_g):
    bsz, seq, _ = x.shape
    assert ctx.shape[1] == TM and seq % TM == 0 and bsz < SUBLANES
    assert w_mod.shape[0] == 2
    nt = 1 + seq // TM

    rows = jnp.concatenate([c, c_ctx[None, :], jnp.zeros((SUBLANES - bsz - 1, D_MODEL), F32)], axis=0)
    mods = _mods(rows, w_mod, b_mod)
    sel0 = _select_mods(mods[0], bsz)
    sel1 = _select_mods(mods[1], bsz)

    xx = jnp.concatenate([ctx, x], axis=1)

    w_in = e_w_in[0]
    offs = {}
    off = 0
    for name, size in (("q", GLA_QK), ("k", GLA_QK), ("v", GLA_V), ("ga", GLA_V), ("af", GLA_LOWRANK),
                       ("ab", GLA_LOWRANK), ("cb", SC_WIDTH), ("cc", SC_WIDTH), ("cx", SC_WIDTH),
                       ("gb", SC_WIDTH)):
        offs[name] = (off, off + size)
        off += size
    cols = lambda lo, hi: w_in[:, lo:hi].astype(BF16)
    wqk = cols(offs["q"][0], offs["k"][1])
    wv = cols(*offs["v"])
    wga = cols(*offs["ga"])
    wa = jnp.pad(w_in[:, offs["af"][0]:offs["ab"][1]], ((0, 0), (0, LANES - 2 * GLA_LOWRANK))).astype(BF16)
    w2 = jnp.zeros((LANES, 2 * GLA_QK), F32)
    w2 = w2.at[:GLA_LOWRANK, :GLA_QK].set(e_w_a2[0, 0])
    w2 = w2.at[GLA_LOWRANK:2 * GLA_LOWRANK, GLA_QK:].set(e_w_a2[0, 1]).astype(BF16)
    b2 = e_b_a2[0].reshape(1, 2 * GLA_QK)
    wts = (wqk, wv, wga, wa, w2, b2, cols(*offs["cb"]), cols(*offs["cc"]), cols(*offs["cx"]),
           cols(*offs["gb"]), e_conv_w[0])
    (qi_f, ki_f, ke_f, dec_f, qi_b, ki_b, ke_b, dec_b, v, sga, y) = _even_in(
        xx, sel0, norm_g[0:1], wts, nt)
    o_f = _gla_fwd(qi_f, ki_f, ke_f, v, dec_f, nt)
    w_out = e_w_out[0].astype(BF16)
    xx = _gla_bwd(qi_b, ki_b, ke_b, v, dec_b, o_f, sga, y, xx, sel0, e_gla_g[0:1],
                  w_out[:GLA_V], w_out[GLA_V:], nt)

    w_in = o_w_in[0]
    wide = lambda a: a.reshape(1, RG_WIDTH)
    xr, h_f = _odd_fwd(xx, sel1, norm_g[1:2], w_in[:, :RG_WIDTH].astype(BF16), o_conv_w[0, 0],
                       wide(o_conv_b[0, 0]), _gate_weights(o_w_a[0, 0], o_w_x[0, 0]),
                       wide(o_b_a[0, 0]), wide(o_b_x[0, 0]), wide(o_lam[0, 0]), nt)
    return _odd_bwd(xx, sel1, norm_g[1:2], xr, h_f, w_in[:, RG_WIDTH:].astype(BF16), o_conv_w[0, 1],
                    wide(o_conv_b[0, 1]), _gate_weights(o_w_a[0, 1], o_w_x[0, 1]),
                    wide(o_b_a[0, 1]), wide(o_b_x[0, 1]), wide(o_lam[0, 1]), o_w_out[0].astype(BF16),
                    final_g.reshape(1, D_MODEL), nt)
```

```python
import functools

import jax
import jax.numpy as jnp
from jax import lax
from jax.experimental import pallas as pl
from jax.experimental.pallas import tpu as pltpu

F32 = jnp.float32
BF16 = jnp.bfloat16

D_MODEL = 1024
EPS = 1e-6
GLA_HEADS = 4
GLA_DK = 128
GLA_DV = 256
GLA_QK = GLA_HEADS * GLA_DK
GLA_V = GLA_HEADS * GLA_DV
GLA_LOWRANK = 16
GLA_TAU = 16.0
GLA_CHUNK = 64
GRID_W = 64
SC_WIDTH = D_MODEL
RG_WIDTH = 2 * D_MODEL
RG_BLOCKS = 16
RG_BLOCK_W = 128
RG_C = 8.0
RG_CONV = 4

TM = 256
CHUNKS = TM // GLA_CHUNK
SUBLANES = 8
LANES = 128
VMEM_LIMIT = 56 * 1024 * 1024
SEG_LEN = TM // SUBLANES
SEG_PITCH = SEG_LEN + SUBLANES
HALO_ROWS = (RG_CONV - 1) * SUBLANES
LOG2_E = 1.4426950408889634
F32_TINY = 1.1754943508222875e-38


def _silu(x):
    return x * jax.nn.sigmoid(x)


def _log_sigmoid(z):
    return jnp.minimum(z, 0.0) - jnp.log1p(jnp.exp(-jnp.abs(z)))


def _softplus(z):
    return jnp.maximum(z, 0.0) + jnp.log1p(jnp.exp(-jnp.abs(z)))


def _dot(a, b):
    return jnp.dot(a, b, preferred_element_type=F32)


def _dot_nt(a, b):
    return lax.dot_general(a, b, (((1,), (1,)), ((), ())), preferred_element_type=F32)


def _dot_tn(a, b):
    return lax.dot_general(a, b, (((0,), (0,)), ((), ())), preferred_element_type=F32)


def _modulated_norm(x, g, shift, scale):
    ms = jnp.mean(x * x, axis=-1, keepdims=True)
    return (x * lax.rsqrt(ms + EPS) * g) * (1.0 + scale) + shift


def _params(n_axes):
    return pltpu.CompilerParams(
        dimension_semantics=("arbitrary",) * n_axes,
        vmem_limit_bytes=VMEM_LIMIT)


def _const_spec(shape):
    zeros = (0,) * len(shape)
    return pl.BlockSpec(shape, lambda *_: zeros, pipeline_mode=pl.Buffered(1))


def _mods_kernel(s_ref, w_ref, b_ref, o_ref):
    s = _silu(s_ref[...])
    o_ref[0] = jnp.dot(s, w_ref[0], preferred_element_type=F32,
                       precision=lax.Precision.HIGHEST) + b_ref[0]


def _mods(rows, w_mod, b_mod):
    depth = w_mod.shape[0]
    nb = 3
    return pl.pallas_call(
        _mods_kernel,
        grid=(depth, nb),
        in_specs=[
            pl.BlockSpec((SUBLANES, D_MODEL), lambda l, j: (0, 0)),
            pl.BlockSpec((1, D_MODEL, D_MODEL), lambda l, j: (l, 0, j)),
            pl.BlockSpec((1, 1, D_MODEL), lambda l, j: (l, 0, j)),
        ],
        out_specs=pl.BlockSpec((1, SUBLANES, D_MODEL), lambda l, j: (l, 0, j)),
        out_shape=jax.ShapeDtypeStruct((depth, SUBLANES, 3 * D_MODEL), F32),
        compiler_params=_params(2),
        name="mods",
    )(rows, w_mod, b_mod.reshape(depth, 1, 3 * D_MODEL))


def _split_dot(mat, x):
    hi = x.astype(BF16)
    lo = (x - hi.astype(F32)).astype(BF16)
    return _dot(mat, hi) + _dot(mat, lo)


def _even_in_kernel(xx_ref, mod_ref, g_ref, wqk_ref, wv_ref, wga_ref, wa_ref, w2_ref, b2_ref,
                    wcb_ref, wcc_ref, wcx_ref, wgb_ref, cw_ref,
                    qif_ref, kif_ref, kef_ref, decf_ref, qib_ref, kib_ref, keb_ref, decb_ref,
                    v_ref, sga_ref, y_ref):
    t = pl.program_id(1)
    h = _modulated_norm(xx_ref[...], g_ref[...], mod_ref[0:1, :], mod_ref[1:2, :])
    hb = h.astype(BF16)

    a_lr = _dot(hb, wa_ref[...])
    z = _dot(a_lr.astype(BF16), w2_ref[...]) + b2_ref[...]
    lg = _log_sigmoid(z) * (1.0 / GLA_TAU)

    row = lax.broadcasted_iota(jnp.int32, (TM, TM), 0)
    col = lax.broadcasted_iota(jnp.int32, (TM, TM), 1)
    same = (row // GLA_CHUNK) == (col // GLA_CHUNK)
    lower = jnp.where(same & (col <= row), 1.0, 0.0).astype(BF16)
    upper = jnp.where(same & (col >= row), 1.0, 0.0).astype(BF16)
    ones = jnp.where(same, 1.0, 0.0).astype(BF16)

    lg_f = lg[:, :GLA_QK]
    lg_b = lg[:, GLA_QK:]
    cs_f = _split_dot(lower, lg_f)
    cs_b = _split_dot(upper, lg_b)
    tot_f = _split_dot(ones, lg_f)
    tot_b = _split_dot(ones, lg_b)

    qk = _dot(hb, wqk_ref[...])
    q = qk[:, :GLA_QK] * (GLA_DK ** -0.5)
    k = qk[:, GLA_QK:]
    qif_ref[...] = (q * jnp.exp(cs_f)).astype(BF16)
    kif_ref[...] = (k * jnp.exp(-cs_f)).astype(BF16)
    kef_ref[...] = (k * jnp.exp(tot_f - cs_f)).astype(BF16)
    qib_ref[...] = (q * jnp.exp(cs_b)).astype(BF16)
    kib_ref[...] = (k * jnp.exp(-cs_b)).astype(BF16)
    keb_ref[...] = (k * jnp.exp(tot_b - cs_b)).astype(BF16)
    for c in range(CHUNKS):
        r = c * GLA_CHUNK
        decf_ref[c:c + 1, :] = jnp.exp(tot_f[r:r + 1, :])
        decb_ref[c:c + 1, :] = jnp.exp(tot_b[r:r + 1, :])

    v_ref[...] = _dot(hb, wv_ref[...]).astype(BF16)
    sga_ref[...] = _silu(_dot(hb, wga_ref[...])).astype(BF16)

    zz = _dot(hb, wcc_ref[...]) * _dot(hb, wcx_ref[...])
    pos = lax.broadcasted_iota(jnp.int32, (TM, SC_WIDTH), 0)
    row_len = jnp.where(t == 0, TM, GRID_W)
    in_row = pos & (row_len - 1)
    z_prev = jnp.where(in_row != 0, pltpu.roll(zz, 1, 0), 0.0)
    z_next = jnp.where(in_row != row_len - 1, pltpu.roll(zz, TM - 1, 0), 0.0)
    zc = cw_ref[0:1, :] * z_prev + cw_ref[1:2, :] * zz + cw_ref[2:3, :] * z_next
    y = _dot(hb, wcb_ref[...]) * zc * _silu(_dot(hb, wgb_ref[...]))
    y_ref[...] = y.astype(BF16)


def _even_in(xx, sel, g, wts, nt):
    bsz = xx.shape[0]
    tok = lambda n: pl.BlockSpec((None, TM, n), lambda b, t: (b, t, 0))
    dec = pl.BlockSpec((None, None, CHUNKS, GLA_QK), lambda b, t: (b, t, 0, 0))
    tok_shape = lambda n: jax.ShapeDtypeStruct((bsz, nt * TM, n), BF16)
    dec_shape = jax.ShapeDtypeStruct((bsz, nt, CHUNKS, GLA_QK), F32)
    return pl.pallas_call(
        _even_in_kernel,
        grid=(bsz, nt),
        in_specs=[
            pl.BlockSpec((None, TM, D_MODEL), lambda b, t: (b, t, 0)),
            pl.BlockSpec((None, None, 3, D_MODEL), lambda b, t: (b, jnp.minimum(t, 1), 0, 0)),
        ] + [_const_spec(w.shape) for w in (g,) + tuple(wts)],
        out_specs=[tok(GLA_QK), tok(GLA_QK), tok(GLA_QK), dec,
                   tok(GLA_QK), tok(GLA_QK), tok(GLA_QK), dec,
                   tok(GLA_V), tok(GLA_V), tok(SC_WIDTH)],
        out_shape=[tok_shape(GLA_QK), tok_shape(GLA_QK), tok_shape(GLA_QK), dec_shape,
                   tok_shape(GLA_QK), tok_shape(GLA_QK), tok_shape(GLA_QK), dec_shape,
                   tok_shape(GLA_V), tok_shape(GLA_V), tok_shape(SC_WIDTH)],
        compiler_params=_params(2),
        name="even_in",
    )(xx, sel, g, *wts)


def _gla_tile(qi_ref, ki_ref, ke_ref, v_ref, dec_ref, st_ref, emit, reverse):
    r = lax.broadcasted_iota(jnp.int32, (TM, TM), 0)
    c = lax.broadcasted_iota(jnp.int32, (TM, TM), 1)
    same_chunk = (r // GLA_CHUNK) == (c // GLA_CHUNK)
    mask = same_chunk & ((c >= r) if reverse else (c <= r))
    order = range(CHUNKS - 1, -1, -1) if reverse else range(CHUNKS)
    chunk_rows = lambda ch: slice(ch * GLA_CHUNK, (ch + 1) * GLA_CHUNK)
    for hd in range(GLA_HEADS):
        kcols = slice(hd * GLA_DK, (hd + 1) * GLA_DK)
        vcols = slice(hd * GLA_DV, (hd + 1) * GLA_DV)
        incr = [_dot_tn(v_ref[chunk_rows(ch), vcols], ke_ref[chunk_rows(ch), kcols]) for ch in range(CHUNKS)]
        state = st_ref[hd]
        before = {}
        for ch in order:
            before[ch] = state.astype(BF16)
            state = dec_ref[ch:ch + 1, kcols] * state + incr[ch]
        st_ref[hd] = state
        qi = qi_ref[:, kcols]
        scores = jnp.where(mask, _dot_nt(qi, ki_ref[:, kcols]), 0.0)
        intra = _dot(scores.astype(BF16), v_ref[:, vcols])
        for ch in range(CHUNKS):
            rows = chunk_rows(ch)
            emit(rows, vcols, intra[rows, :] + _dot_nt(qi[rows, :], before[ch]))


def _gla_fwd_kernel(qi_ref, ki_ref, ke_ref, v_ref, dec_ref, o_ref, st_ref):
    @pl.when(pl.program_id(1) == 0)
    def _():
        st_ref[...] = jnp.zeros_like(st_ref)

    def emit(rows, vcols, o):
        o_ref[rows, vcols] = o

    _gla_tile(qi_ref, ki_ref, ke_ref, v_ref, dec_ref, st_ref, emit, reverse=False)


def _gla_fwd(qi, ki, ke, v, dec, nt):
    bsz = qi.shape[0]
    tok = lambda n: pl.BlockSpec((None, TM, n), lambda b, t: (b, t, 0))
    return pl.pallas_call(
        _gla_fwd_kernel,
        grid=(bsz, nt),
        in_specs=[tok(GLA_QK), tok(GLA_QK), tok(GLA_QK), tok(GLA_V),
                  pl.BlockSpec((None, None, CHUNKS, GLA_QK), lambda b, t: (b, t, 0, 0))],
        out_specs=tok(GLA_V),
        out_shape=jax.ShapeDtypeStruct((bsz, nt * TM, GLA_V), F32),
        scratch_shapes=[pltpu.VMEM((GLA_HEADS, GLA_DV, GLA_DK), F32)],
        compiler_params=_params(2),
        name="gla_fwd",
    )(qi, ki, ke, v, dec)


def _gla_bwd_kernel(qi_ref, ki_ref, ke_ref, v_ref, dec_ref, of_ref, sga_ref, y_ref, xx_ref,
                    mod_ref, gg_ref, wo_ref, wy_ref, out_ref, st_ref, ob_ref):
    @pl.when(pl.program_id(1) == 0)
    def _():
        st_ref[...] = jnp.zeros_like(st_ref)

    def emit(rows, vcols, o):
        ob_ref[rows, vcols] = o

    _gla_tile(qi_ref, ki_ref, ke_ref, v_ref, dec_ref, st_ref, emit, reverse=True)

    heads = []
    for hd in range(GLA_HEADS):
        vcols = slice(hd * GLA_DV, (hd + 1) * GLA_DV)
        o = of_ref[:, vcols] + ob_ref[:, vcols]
        ms = jnp.mean(o * o, axis=-1, keepdims=True)
        on = o * lax.rsqrt(ms + EPS) * gg_ref[...]
        heads.append((on * sga_ref[:, vcols].astype(F32)).astype(BF16))
    inner = jnp.concatenate(heads, axis=-1)
    proj = _dot(inner, wo_ref[...]) + _dot(y_ref[...], wy_ref[...])
    out_ref[...] = xx_ref[...] + mod_ref[2:3, :] * proj


def _bwd_tile(nt):
    return lambda j: jnp.where(j == 0, 0, nt - j)


def _gla_bwd(qi, ki, ke, v, dec, o_f, sga, y, xx, sel, gg, wo, wy, nt):
    bsz = qi.shape[0]
    tile = _bwd_tile(nt)
    tok = lambda n: pl.BlockSpec((None, TM, n), lambda b, j: (b, tile(j), 0))
    return pl.pallas_call(
        _gla_bwd_kernel,
        grid=(bsz, nt),
        in_specs=[tok(GLA_QK), tok(GLA_QK), tok(GLA_QK), tok(GLA_V),
                  pl.BlockSpec((None, None, CHUNKS, GLA_QK), lambda b, j: (b, tile(j), 0, 0)),
                  tok(GLA_V), tok(GLA_V), tok(SC_WIDTH), tok(D_MODEL),
                  pl.BlockSpec((None, None, 3, D_MODEL), lambda b, j: (b, jnp.minimum(j, 1), 0, 0)),
                  _const_spec(gg.shape), _const_spec(wo.shape), _const_spec(wy.shape)],
        out_specs=tok(D_MODEL),
        out_shape=jax.ShapeDtypeStruct((bsz, nt * TM, D_MODEL), F32),
        scratch_shapes=[pltpu.VMEM((GLA_HEADS, GLA_DV, GLA_DK), F32),
                        pltpu.VMEM((TM, GLA_V), F32)],
        compiler_params=_params(2),
        name="gla_bwd",
    )(qi, ki, ke, v, dec, o_f, sga, y, xx, sel, gg, wo, wy)


def _to_segment_major(src_ref, stage_ref, width):
    slabs = width // LANES
    for n in range(slabs):
        for s in range(SUBLANES):
            stage_ref[n, s * SEG_PITCH:s * SEG_PITCH + SEG_LEN, :] = (
                src_ref[s * SEG_LEN:(s + 1) * SEG_LEN, n * LANES:(n + 1) * LANES])
    groups = [
        jnp.concatenate([stage_ref[n, pl.ds(j, SUBLANES, stride=SEG_PITCH), :] for n in range(slabs)], axis=1)
        for j in range(SEG_LEN)]
    return jnp.concatenate(groups, axis=0)


def _from_segment_major(val, stage_ref, dst_ref, width):
    slabs = width // LANES
    for j in range(SEG_LEN):
        for n in range(slabs):
            stage_ref[n, pl.ds(j, SUBLANES, stride=SEG_PITCH), :] = (
                val[j * SUBLANES:(j + 1) * SUBLANES, n * LANES:(n + 1) * LANES])
    for n in range(slabs):
        for s in range(SUBLANES):
            dst_ref[s * SEG_LEN:(s + 1) * SEG_LEN, n * LANES:(n + 1) * LANES] = (
                stage_ref[n, s * SEG_PITCH:s * SEG_PITCH + SEG_LEN, :])


def _rg_conv(xr, halo_ref, cw_ref, cb_ref, reverse):
    row = lax.broadcasted_iota(jnp.int32, (SUBLANES, RG_WIDTH), 0)
    edge = []
    for g in range(RG_CONV - 1):
        grp = slice(g * SUBLANES, (g + 1) * SUBLANES)
        if reverse:
            edge.append(jnp.where(row == SUBLANES - 1, pltpu.roll(halo_ref[grp, :], SUBLANES - 1, 0),
                                  pltpu.roll(xr[grp, :], SUBLANES - 1, 0)))
        else:
            cur = xr[TM - (RG_CONV - 1 - g) * SUBLANES:TM - (RG_CONV - 2 - g) * SUBLANES, :]
            edge.append(jnp.where(row == 0, pltpu.roll(halo_ref[grp, :], 1, 0), pltpu.roll(cur, 1, 0)))
    xc = cb_ref[...] + cw_ref[RG_CONV - 1:RG_CONV, :] * xr
    for k in range(1, RG_CONV):
        if reverse:
            tap = jnp.concatenate([xr[k * SUBLANES:, :]] + edge[:k], axis=0)
        else:
            tap = jnp.concatenate(edge[RG_CONV - 1 - k:] + [xr[:TM - k * SUBLANES, :]], axis=0)
        xc = xc + cw_ref[RG_CONV - 1 - k:RG_CONV - k, :] * tap
    halo = xr[:HALO_ROWS, :] if reverse else xr[TM - HALO_ROWS:, :]
    return xc, halo


def _rg_gates_scan(xc, wg_ref, ba_ref, bx_ref, decay_ref, h_ref, carry_ref, reverse):
    row = lax.broadcasted_iota(jnp.int32, (SUBLANES, RG_BLOCK_W), 0)
    shift = SUBLANES - 1 if reverse else 1
    steps = range(SEG_LEN - 1, -1, -1) if reverse else range(SEG_LEN)
    for n in range(RG_BLOCKS):
        cols = slice(n * RG_BLOCK_W, (n + 1) * RG_BLOCK_W)
        xb = xc[:, cols]
        gates = _dot(xb.astype(BF16), wg_ref[n])
        th_r = jnp.tanh(0.5 * (gates[:, :RG_BLOCK_W] + ba_ref[:, cols]))
        th_i = jnp.tanh(0.5 * (gates[:, RG_BLOCK_W:] + bx_ref[:, cols]))
        half_decay = decay_ref[:, cols]
        a = jnp.exp2(th_r * half_decay + half_decay)
        y = 1.0 - a * a
        root = y * lax.rsqrt(jnp.maximum(y, F32_TINY))
        u = root * ((0.5 * th_i + 0.5) * xb)

        h = jnp.zeros((SUBLANES, RG_BLOCK_W), F32)
        prod = jnp.ones((SUBLANES, RG_BLOCK_W), F32)
        local, prods = {}, {}
        for j in steps:
            grp = slice(j * SUBLANES, (j + 1) * SUBLANES)
            h = a[grp, :] * h + u[grp, :]
            prod = prod * a[grp, :]
            local[j], prods[j] = h, prod

        entering = carry_ref[:, cols]
        for s in range(SUBLANES - 1):
            nxt = pltpu.roll(prod * entering + h, shift, 0)
            target = SUBLANES - 2 - s if reverse else s + 1
            entering = jnp.where(row == target, nxt, entering)
        carry_ref[:, cols] = pltpu.roll(prod * entering + h, shift, 0)

        h_ref[:, cols] = jnp.concatenate(
            [local[j] + prods[j] * entering for j in range(SEG_LEN)], axis=0)


def _odd_fwd_kernel(xx_ref, mod_ref, g_ref, wx_ref, cw_ref, cb_ref, wg_ref, ba_ref, bx_ref, lam_ref,
                    xr_ref, hf_ref, stage_ref, halo_ref, carry_ref, decay_ref):
    t = pl.program_id(1)

    @pl.when(t == 0)
    def _():
        carry_ref[...] = jnp.zeros_like(carry_ref)
        decay_ref[...] = (-0.5 * RG_C * LOG2_E) * _softplus(-lam_ref[...])

    @pl.when(t <= 1)
    def _():
        halo_ref[...] = jnp.zeros_like(halo_ref)

    x = _to_segment_major(xx_ref, stage_ref, D_MODEL)
    h = _modulated_norm(x, g_ref[...], mod_ref[0:1, :], mod_ref[1:2, :])
    xr = _dot(h.astype(BF16), wx_ref[...])
    xr_ref[...] = xr
    xc, halo = _rg_conv(xr, halo_ref, cw_ref, cb_ref, reverse=False)
    halo_ref[...] = halo

    _rg_gates_scan(xc, wg_ref, ba_ref, bx_ref, decay_ref, hf_ref, carry_ref, reverse=False)


def _odd_fwd(xx, sel, g, wx, cw, cb, wg, ba, bx, lam, nt):
    bsz = xx.shape[0]
    tok = lambda n: pl.BlockSpec((None, TM, n), lambda b, t: (b, t, 0))
    consts = (g, wx, cw, cb, wg, ba, bx, lam)
    wide = jax.ShapeDtypeStruct((bsz, nt * TM, RG_WIDTH), F32)
    return pl.pallas_call(
        _odd_fwd_kernel,
        grid=(bsz, nt),
        in_specs=[tok(D_MODEL),
                  pl.BlockSpec((None, None, 3, D_MODEL), lambda b, t: (b, jnp.minimum(t, 1), 0, 0))]
        + [_const_spec(w.shape) for w in consts],
        out_specs=[tok(RG_WIDTH), tok(RG_WIDTH)],
        out_shape=[wide, wide],
        scratch_shapes=[pltpu.VMEM((D_MODEL // LANES, SUBLANES * SEG_PITCH, LANES), F32),
                        pltpu.VMEM((HALO_ROWS, RG_WIDTH), F32),
                        pltpu.VMEM((SUBLANES, RG_WIDTH), F32),
                        pltpu.VMEM((1, RG_WIDTH), F32)],
        compiler_params=_params(2),
        name="odd_fwd",
    )(xx, sel, *consts)


def _odd_bwd_kernel(xx_ref, mod_ref, g_ref, xr_ref, hf_ref, wgate_ref, cw_ref, cb_ref, wg_ref, ba_ref,
                    bx_ref, lam_ref, wout_ref, fg_ref, out_ref,
                    stage_ref, halo_ref, hb_ref, carry_ref, decay_ref):
    j = pl.program_id(1)

    @pl.when(j == 0)
    def _():
        carry_ref[...] = jnp.zeros_like(carry_ref)
        decay_ref[...] = (-0.5 * RG_C * LOG2_E) * _softplus(-lam_ref[...])

    @pl.when(j <= 1)
    def _():
        halo_ref[...] = jnp.zeros_like(halo_ref)

    xc, halo = _rg_conv(xr_ref[...], halo_ref, cw_ref, cb_ref, reverse=True)
    halo_ref[...] = halo

    _rg_gates_scan(xc, wg_ref, ba_ref, bx_ref, decay_ref, hb_ref, carry_ref, reverse=True)

    @pl.when(j > 0)
    def _():
        x = _to_segment_major(xx_ref, stage_ref, D_MODEL)
        h = _modulated_norm(x, g_ref[...], mod_ref[0:1, :], mod_ref[1:2, :])
        gate = _dot(h.astype(BF16), wgate_ref[...])
        yy = (hf_ref[...] + hb_ref[...]) * _silu(gate)
        xn = x + mod_ref[2:3, :] * _dot(yy.astype(BF16), wout_ref[...])
        ms = jnp.mean(xn * xn, axis=-1, keepdims=True)
        _from_segment_major(xn * lax.rsqrt(ms + EPS) * fg_ref[...], stage_ref, out_ref, D_MODEL)


def _odd_bwd(xx, sel, g, xr, hf, wgate, cw, cb, wg, ba, bx, lam, wout, fg, nt):
    bsz = xx.shape[0]
    tile = _bwd_tile(nt)
    tok = lambda n: pl.BlockSpec((None, TM, n), lambda b, j: (b, tile(j), 0))
    consts = (wgate, cw, cb, wg, ba, bx, lam, wout, fg)
    out_block = lambda b, j: (b, jnp.where(j == 0, nt - 2, nt - 1 - j), 0)
    return pl.pallas_call(
        _odd_bwd_kernel,
        grid=(bsz, nt),
        in_specs=[tok(D_MODEL),
                  pl.BlockSpec((None, None, 3, D_MODEL), lambda b, j: (b, jnp.minimum(j, 1), 0, 0)),
                  _const_spec(g.shape), tok(RG_WIDTH), tok(RG_WIDTH)]
        + [_const_spec(w.shape) for w in consts],
        out_specs=pl.BlockSpec((None, TM, D_MODEL), out_block),
        out_shape=jax.ShapeDtypeStruct((bsz, (nt - 1) * TM, D_MODEL), F32),
        scratch_shapes=[pltpu.VMEM((D_MODEL // LANES, SUBLANES * SEG_PITCH, LANES), F32),
                        pltpu.VMEM((HALO_ROWS, RG_WIDTH), F32),
                        pltpu.VMEM((TM, RG_WIDTH), F32),
                        pltpu.VMEM((SUBLANES, RG_WIDTH), F32),
                        pltpu.VMEM((1, RG_WIDTH), F32)],
        compiler_params=_params(2),
        name="odd_bwd",
    )(xx, sel, g, xr, hf, *consts)


def _select_mods(m, bsz):
    per = m[:bsz].reshape(bsz, 1, 3, D_MODEL)
    ctx = jnp.broadcast_to(m[bsz].reshape(1, 1, 3, D_MODEL), (bsz, 1, 3, D_MODEL))
    return jnp.concatenate([ctx, per], axis=1)


def _gate_weights(w_a, w_x):
    return jnp.concatenate([w_a, w_x], axis=-1).astype(BF16)


def kernel(x, c, ctx, c_ctx, norm_g, w_mod, b_mod, e_w_in, e_w_a2, e_b_a2, e_gla_g, e_conv_w, e_w_out,
           o_w_in, o_conv_w, o_conv_b, o_w_a, o_b_a, o_w_x, o_b_x, o_lam, o_w_out, final_g):
    bsz, seq, _ = x.shape
    assert ctx.shape[1] == TM and seq % TM == 0 and bsz < SUBLANES
    assert w_mod.shape[0] == 2
    nt = 1 + seq // TM

    rows = jnp.concatenate([c, c_ctx[None, :], jnp.zeros((SUBLANES - bsz - 1, D_MODEL), F32)], axis=0)
    mods = _mods(rows, w_mod, b_mod)
    sel0 = _select_mods(mods[0], bsz)
    sel1 = _select_mods(mods[1], bsz)

    xx = jnp.concatenate([ctx, x], axis=1)

    w_in = e_w_in[0]
    offs = {}
    off = 0
    for name, size in (("q", GLA_QK), ("k", GLA_QK), ("v", GLA_V), ("ga", GLA_V), ("af", GLA_LOWRANK),
                       ("ab", GLA_LOWRANK), ("cb", SC_WIDTH), ("cc", SC_WIDTH), ("cx", SC_WIDTH),
                       ("gb", SC_WIDTH)):
        offs[name] = (off, off + size)
        off += size
    cols = lambda lo, hi: w_in[:, lo:hi].astype(BF16)
    wqk = cols(offs["q"][0], offs["k"][1])
    wv = cols(*offs["v"])
    wga = cols(*offs["ga"])
    wa = jnp.pad(w_in[:, offs["af"][0]:offs["ab"][1]], ((0, 0), (0, LANES - 2 * GLA_LOWRANK))).astype(BF16)
    w2 = jnp.zeros((LANES, 2 * GLA_QK), F32)
    w2 = w2.at[:GLA_LOWRANK, :GLA_QK].set(e_w_a2[0, 0])
    w2 = w2.at[GLA_LOWRANK:2 * GLA_LOWRANK, GLA_QK:].set(e_w_a2[0, 1]).astype(BF16)
    b2 = e_b_a2[0].reshape(1, 2 * GLA_QK)
    wts = (wqk, wv, wga, wa, w2, b2, cols(*offs["cb"]), cols(*offs["cc"]), cols(*offs["cx"]),
           cols(*offs["gb"]), e_conv_w[0])
    (qi_f, ki_f, ke_f, dec_f, qi_b, ki_b, ke_b, dec_b, v, sga, y) = _even_in(
        xx, sel0, norm_g[0:1], wts, nt)
    o_f = _gla_fwd(qi_f, ki_f, ke_f, v, dec_f, nt)
    w_out = e_w_out[0].astype(BF16)
    xx = _gla_bwd(qi_b, ki_b, ke_b, v, dec_b, o_f, sga, y, xx, sel0, e_gla_g[0:1],
                  w_out[:GLA_V], w_out[GLA_V:], nt)

    w_in = o_w_in[0]
    wide = lambda a: a.reshape(1, RG_WIDTH)
    xr, h_f = _odd_fwd(xx, sel1, norm_g[1:2], w_in[:, :RG_WIDTH].astype(BF16), o_conv_w[0, 0],
                       wide(o_conv_b[0, 0]), _gate_weights(o_w_a[0, 0], o_w_x[0, 0]),
                       wide(o_b_a[0, 0]), wide(o_b_x[0, 0]), wide(o_lam[0, 0]), nt)
    return _odd_bwd(xx, sel1, norm_g[1:2], xr, h_f, w_in[:, RG_WIDTH:].astype(BF16), o_conv_w[0, 1],
                    wide(o_conv_b[0, 1]), _gate_weights(o_w_a[0, 1], o_w_x[0, 1]),
                    wide(o_b_a[0, 1]), wide(o_b_x[0, 1]), wide(o_lam[0, 1]), o_w_out[0].astype(BF16),
                    final_g.reshape(1, D_MODEL), nt)
```

```python
import functools

import jax
import jax.numpy as jnp
from jax import lax
from jax.experimental import pallas as pl
from jax.experimental.pallas import tpu as pltpu

F32 = jnp.float32
BF16 = jnp.bfloat16

D_MODEL = 1024
EPS = 1e-6
GLA_HEADS = 4
GLA_DK = 128
GLA_DV = 256
GLA_QK = GLA_HEADS * GLA_DK
GLA_V = GLA_HEADS * GLA_DV
GLA_LOWRANK = 16
GLA_TAU = 16.0
GLA_CHUNK = 64
GRID_W = 64
SC_WIDTH = D_MODEL
RG_WIDTH = 2 * D_MODEL
RG_BLOCKS = 16
RG_BLOCK_W = 128
RG_C = 8.0
RG_CONV = 4

TM = 256
CHUNKS = TM // GLA_CHUNK
SUBLANES = 8
LANES = 128
VMEM_LIMIT = 56 * 1024 * 1024
SEG_LEN = TM // SUBLANES
SEG_PITCH = SEG_LEN + SUBLANES
HALO_ROWS = (RG_CONV - 1) * SUBLANES
LOG2_E = 1.4426950408889634
F32_TINY = 1.1754943508222875e-38


def _silu(x):
    return x * jax.nn.sigmoid(x)


def _log_sigmoid(z):
    return jnp.minimum(z, 0.0) - jnp.log(1.0 + jnp.exp(-jnp.abs(z)))


def _softplus(z):
    return jnp.maximum(z, 0.0) + jnp.log1p(jnp.exp(-jnp.abs(z)))


def _dot(a, b):
    return jnp.dot(a, b, preferred_element_type=F32)


def _dot_nt(a, b):
    return lax.dot_general(a, b, (((1,), (1,)), ((), ())), preferred_element_type=F32)


def _dot_tn(a, b):
    return lax.dot_general(a, b, (((0,), (0,)), ((), ())), preferred_element_type=F32)


def _modulated_norm(x, g, shift, scale):
    ms = jnp.mean(x * x, axis=-1, keepdims=True)
    return (x * lax.rsqrt(ms + EPS) * g) * (1.0 + scale) + shift


def _params(n_axes):
    return pltpu.CompilerParams(
        dimension_semantics=("arbitrary",) * n_axes,
        vmem_limit_bytes=VMEM_LIMIT)


def _const_spec(shape):
    zeros = (0,) * len(shape)
    return pl.BlockSpec(shape, lambda *_: zeros, pipeline_mode=pl.Buffered(1))


def _mods_kernel(s_ref, w_ref, b_ref, o_ref):
    s = _silu(s_ref[...])
    o_ref[0] = jnp.dot(s, w_ref[0], preferred_element_type=F32,
                       precision=lax.Precision.HIGHEST) + b_ref[0]


def _mods(rows, w_mod, b_mod):
    depth = w_mod.shape[0]
    nb = 3
    return pl.pallas_call(
        _mods_kernel,
        grid=(depth, nb),
        in_specs=[
            pl.BlockSpec((SUBLANES, D_MODEL), lambda l, j: (0, 0)),
            pl.BlockSpec((1, D_MODEL, D_MODEL), lambda l, j: (l, 0, j)),
            pl.BlockSpec((1, 1, D_MODEL), lambda l, j: (l, 0, j)),
        ],
        out_specs=pl.BlockSpec((1, SUBLANES, D_MODEL), lambda l, j: (l, 0, j)),
        out_shape=jax.ShapeDtypeStruct((depth, SUBLANES, 3 * D_MODEL), F32),
        compiler_params=_params(2),
        name="mods",
    )(rows, w_mod, b_mod.reshape(depth, 1, 3 * D_MODEL))


def _split_dot(mat, x):
    hi = x.astype(BF16)
    lo = (x - hi.astype(F32)).astype(BF16)
    return _dot(mat, hi) + _dot(mat, lo)


def _chunk_rows(x, offset):
    return jnp.concatenate(
        [jnp.broadcast_to(x[c * GLA_CHUNK + offset:c * GLA_CHUNK + offset + 1, :], (GLA_CHUNK, x.shape[1]))
         for c in range(CHUNKS)], axis=0)


def _even_in_kernel(ctx_ref, x_ref, mod_ref, g_ref, lower_ref, upper_ref,
                    wqk_ref, wv_ref, wga_ref, wa_ref, w2_ref, b2_ref,
                    wcb_ref, wcc_ref, wcx_ref, wgb_ref, cw_ref,
                    qif_ref, kif_ref, kef_ref, decf_ref, qib_ref, kib_ref, keb_ref, decb_ref,
                    v_ref, sga_ref, y_ref):
    t = pl.program_id(1)
    xt = jnp.where(t == 0, ctx_ref[...], x_ref[...])
    h = _modulated_norm(xt, g_ref[...], mod_ref[0:1, :], mod_ref[1:2, :])
    hb = h.astype(BF16)

    a_lr = _dot(hb, wa_ref[...])
    z = _dot(a_lr.astype(BF16), w2_ref[...]) + b2_ref[...]
    qk = _dot(hb, wqk_ref[...])
    v_ref[...] = _dot(hb, wv_ref[...]).astype(BF16)

    lg = _log_sigmoid(z) * (1.0 / GLA_TAU)
    cs_f = _split_dot(lower_ref[...], lg[:, :GLA_QK])
    cs_b = _split_dot(upper_ref[...], lg[:, GLA_QK:])
    tot_f = _chunk_rows(cs_f, GLA_CHUNK - 1)
    tot_b = _chunk_rows(cs_b, 0)

    sga_ref[...] = _silu(_dot(hb, wga_ref[...])).astype(BF16)

    q = qk[:, :GLA_QK] * (GLA_DK ** -0.5)
    k = qk[:, GLA_QK:]
    qif_ref[...] = (q * jnp.exp(cs_f)).astype(BF16)
    kif_ref[...] = (k * jnp.exp(-cs_f)).astype(BF16)
    kef_ref[...] = (k * jnp.exp(tot_f - cs_f)).astype(BF16)
    qib_ref[...] = (q * jnp.exp(cs_b)).astype(BF16)
    kib_ref[...] = (k * jnp.exp(-cs_b)).astype(BF16)
    keb_ref[...] = (k * jnp.exp(tot_b - cs_b)).astype(BF16)
    for c in range(CHUNKS):
        r = c * GLA_CHUNK
        decf_ref[c:c + 1, :] = jnp.exp(tot_f[r:r + 1, :])
        decb_ref[c:c + 1, :] = jnp.exp(tot_b[r:r + 1, :])

    zz = _dot(hb, wcc_ref[...]) * _dot(hb, wcx_ref[...])
    pos = lax.broadcasted_iota(jnp.int32, (TM, SC_WIDTH), 0)
    row_len = jnp.where(t == 0, TM, GRID_W)
    in_row = pos & (row_len - 1)
    z_prev = jnp.where(in_row != 0, pltpu.roll(zz, 1, 0), 0.0)
    z_next = jnp.where(in_row != row_len - 1, pltpu.roll(zz, TM - 1, 0), 0.0)
    zc = cw_ref[0:1, :] * z_prev + cw_ref[1:2, :] * zz + cw_ref[2:3, :] * z_next
    y = _dot(hb, wcb_ref[...]) * zc * _silu(_dot(hb, wgb_ref[...]))
    y_ref[...] = y.astype(BF16)


def _chunk_triangles():
    r = lax.broadcasted_iota(jnp.int32, (TM, TM), 0)
    c = lax.broadcasted_iota(jnp.int32, (TM, TM), 1)
    same = (r // GLA_CHUNK) == (c // GLA_CHUNK)
    return (same & (c <= r)).astype(BF16), (same & (c >= r)).astype(BF16)


def _ctx_spec():
    return pl.BlockSpec((None, TM, D_MODEL), lambda b, t: (b, 0, 0))


def _latent_spec(tile, park):
    return pl.BlockSpec((None, TM, D_MODEL),
                        lambda b, t: (b, jnp.where(tile(t) == 0, park, tile(t) - 1), 0))


def _even_in(ctx, x, sel, g, wts, nt):
    bsz = x.shape[0]
    tok = lambda n: pl.BlockSpec((None, TM, n), lambda b, t: (b, t, 0))
    dec = pl.BlockSpec((None, None, CHUNKS, GLA_QK), lambda b, t: (b, t, 0, 0))
    tok_shape = lambda n: jax.ShapeDtypeStruct((bsz, nt * TM, n), BF16)
    dec_shape = jax.ShapeDtypeStruct((bsz, nt, CHUNKS, GLA_QK), F32)
    consts = (g,) + _chunk_triangles() + tuple(wts)
    return pl.pallas_call(
        _even_in_kernel,
        grid=(bsz, nt),
        in_specs=[
            _ctx_spec(), _latent_spec(lambda t: t, 0),
            pl.BlockSpec((None, None, 3, D_MODEL), lambda b, t: (b, jnp.minimum(t, 1), 0, 0)),
        ] + [_const_spec(w.shape) for w in consts],
        out_specs=[tok(GLA_QK), tok(GLA_QK), tok(GLA_QK), dec,
                   tok(GLA_QK), tok(GLA_QK), tok(GLA_QK), dec,
                   tok(GLA_V), tok(GLA_V), tok(SC_WIDTH)],
        out_shape=[tok_shape(GLA_QK), tok_shape(GLA_QK), tok_shape(GLA_QK), dec_shape,
                   tok_shape(GLA_QK), tok_shape(GLA_QK), tok_shape(GLA_QK), dec_shape,
                   tok_shape(GLA_V), tok_shape(GLA_V), tok_shape(SC_WIDTH)],
        compiler_params=_params(2),
        name="even_in",
    )(ctx, x, sel, *consts)


def _gla_tile(qi_ref, ki_ref, ke_ref, v_ref, dec_ref, st_ref, emit, reverse):
    r = lax.broadcasted_iota(jnp.int32, (TM, TM), 0)
    c = lax.broadcasted_iota(jnp.int32, (TM, TM), 1)
    same_chunk = (r // GLA_CHUNK) == (c // GLA_CHUNK)
    mask = same_chunk & ((c >= r) if reverse else (c <= r))
    order = range(CHUNKS - 1, -1, -1) if reverse else range(CHUNKS)
    chunk_rows = lambda ch: slice(ch * GLA_CHUNK, (ch + 1) * GLA_CHUNK)
    for hd in range(GLA_HEADS):
        kcols = slice(hd * GLA_DK, (hd + 1) * GLA_DK)
        vcols = slice(hd * GLA_DV, (hd + 1) * GLA_DV)
        incr = [_dot_tn(v_ref[chunk_rows(ch), vcols], ke_ref[chunk_rows(ch), kcols]) for ch in range(CHUNKS)]
        state = st_ref[hd]
        before = {}
        for ch in order:
            before[ch] = state.astype(BF16)
            state = dec_ref[ch:ch + 1, kcols] * state + incr[ch]
        st_ref[hd] = state
        qi = qi_ref[:, kcols]
        scores = jnp.where(mask, _dot_nt(qi, ki_ref[:, kcols]), 0.0)
        intra = _dot(scores.astype(BF16), v_ref[:, vcols])
        for ch in range(CHUNKS):
            rows = chunk_rows(ch)
            emit(rows, vcols, intra[rows, :] + _dot_nt(qi[rows, :], before[ch]))


def _gla_fwd_kernel(qi_ref, ki_ref, ke_ref, v_ref, dec_ref, o_ref, st_ref):
    @pl.when(pl.program_id(1) == 0)
    def _():
        st_ref[...] = jnp.zeros_like(st_ref)

    def emit(rows, vcols, o):
        o_ref[rows, vcols] = o

    _gla_tile(qi_ref, ki_ref, ke_ref, v_ref, dec_ref, st_ref, emit, reverse=False)


def _gla_fwd(qi, ki, ke, v, dec, nt):
    bsz = qi.shape[0]
    tok = lambda n: pl.BlockSpec((None, TM, n), lambda b, t: (b, t, 0))
    return pl.pallas_call(
        _gla_fwd_kernel,
        grid=(bsz, nt),
        in_specs=[tok(GLA_QK), tok(GLA_QK), tok(GLA_QK), tok(GLA_V),
                  pl.BlockSpec((None, None, CHUNKS, GLA_QK), lambda b, t: (b, t, 0, 0))],
        out_specs=tok(GLA_V),
        out_shape=jax.ShapeDtypeStruct((bsz, nt * TM, GLA_V), F32),
        scratch_shapes=[pltpu.VMEM((GLA_HEADS, GLA_DV, GLA_DK), F32)],
        compiler_params=_params(2),
        name="gla_fwd",
    )(qi, ki, ke, v, dec)


def _gla_bwd_kernel(qi_ref, ki_ref, ke_ref, v_ref, dec_ref, of_ref, sga_ref, y_ref, ctx_ref, x_ref,
                    mod_ref, gg_ref, wo_ref, wy_ref, out_ref, st_ref, ob_ref):
    j = pl.program_id(1)

    @pl.when(j == 0)
    def _():
        st_ref[...] = jnp.zeros_like(st_ref)

    def emit(rows, vcols, o):
        ob_ref[rows, vcols] = o

    _gla_tile(qi_ref, ki_ref, ke_ref, v_ref, dec_ref, st_ref, emit, reverse=True)

    heads = []
    for hd in range(GLA_HEADS):
        vcols = slice(hd * GLA_DV, (hd + 1) * GLA_DV)
        o = of_ref[:, vcols] + ob_ref[:, vcols]
        ms = jnp.mean(o * o, axis=-1, keepdims=True)
        on = o * lax.rsqrt(ms + EPS) * gg_ref[...]
        heads.append((on * sga_ref[:, vcols].astype(F32)).astype(BF16))
    inner = jnp.concatenate(heads, axis=-1)
    proj = _dot(inner, wo_ref[...]) + _dot(y_ref[...], wy_ref[...])
    out_ref[...] = jnp.where(j == 0, ctx_ref[...], x_ref[...]) + mod_ref[2:3, :] * proj


def _bwd_tile(nt):
    return lambda j: jnp.where(j == 0, 0, nt - j)


def _gla_bwd(qi, ki, ke, v, dec, o_f, sga, y, ctx, x, sel, gg, wo, wy, nt):
    bsz = qi.shape[0]
    tile = _bwd_tile(nt)
    tok = lambda n: pl.BlockSpec((None, TM, n), lambda b, j: (b, tile(j), 0))
    return pl.pallas_call(
        _gla_bwd_kernel,
        grid=(bsz, nt),
        in_specs=[tok(GLA_QK), tok(GLA_QK), tok(GLA_QK), tok(GLA_V),
                  pl.BlockSpec((None, None, CHUNKS, GLA_QK), lambda b, j: (b, tile(j), 0, 0)),
                  tok(GLA_V), tok(GLA_V), tok(SC_WIDTH), _ctx_spec(), _latent_spec(tile, nt - 2),
                  pl.BlockSpec((None, None, 3, D_MODEL), lambda b, j: (b, jnp.minimum(j, 1), 0, 0)),
                  _const_spec(gg.shape), _const_spec(wo.shape), _const_spec(wy.shape)],
        out_specs=tok(D_MODEL),
        out_shape=jax.ShapeDtypeStruct((bsz, nt * TM, D_MODEL), F32),
        scratch_shapes=[pltpu.VMEM((GLA_HEADS, GLA_DV, GLA_DK), F32),
                        pltpu.VMEM((TM, GLA_V), F32)],
        compiler_params=_params(2),
        name="gla_bwd",
    )(qi, ki, ke, v, dec, o_f, sga, y, ctx, x, sel, gg, wo, wy)


def _to_segment_major(src_ref, stage_ref, width):
    slabs = width // LANES
    for n in range(slabs):
        for s in range(SUBLANES):
            stage_ref[n, s * SEG_PITCH:s * SEG_PITCH + SEG_LEN, :] = (
                src_ref[s * SEG_LEN:(s + 1) * SEG_LEN, n * LANES:(n + 1) * LANES])
    groups = [
        jnp.concatenate([stage_ref[n, pl.ds(j, SUBLANES, stride=SEG_PITCH), :] for n in range(slabs)], axis=1)
        for j in range(SEG_LEN)]
    return jnp.concatenate(groups, axis=0)


def _from_segment_major(val, stage_ref, dst_ref, width):
    slabs = width // LANES
    for j in range(SEG_LEN):
        for n in range(slabs):
            stage_ref[n, pl.ds(j, SUBLANES, stride=SEG_PITCH), :] = (
                val[j * SUBLANES:(j + 1) * SUBLANES, n * LANES:(n + 1) * LANES])
    for n in range(slabs):
        for s in range(SUBLANES):
            dst_ref[s * SEG_LEN:(s + 1) * SEG_LEN, n * LANES:(n + 1) * LANES] = (
                stage_ref[n, s * SEG_PITCH:s * SEG_PITCH + SEG_LEN, :])


def _rg_conv(xr, halo_ref, cw_ref, cb_ref, reverse):
    row = lax.broadcasted_iota(jnp.int32, (SUBLANES, RG_WIDTH), 0)
    edge = []
    for g in range(RG_CONV - 1):
        grp = slice(g * SUBLANES, (g + 1) * SUBLANES)
        if reverse:
            edge.append(jnp.where(row == SUBLANES - 1, pltpu.roll(halo_ref[grp, :], SUBLANES - 1, 0),
                                  pltpu.roll(xr[grp, :], SUBLANES - 1, 0)))
        else:
            cur = xr[TM - (RG_CONV - 1 - g) * SUBLANES:TM - (RG_CONV - 2 - g) * SUBLANES, :]
            edge.append(jnp.where(row == 0, pltpu.roll(halo_ref[grp, :], 1, 0), pltpu.roll(cur, 1, 0)))
    xc = cb_ref[...] + cw_ref[RG_CONV - 1:RG_CONV, :] * xr
    for k in range(1, RG_CONV):
        if reverse:
            tap = jnp.concatenate([xr[k * SUBLANES:, :]] + edge[:k], axis=0)
        else:
            tap = jnp.concatenate(edge[RG_CONV - 1 - k:] + [xr[:TM - k * SUBLANES, :]], axis=0)
        xc = xc + cw_ref[RG_CONV - 1 - k:RG_CONV - k, :] * tap
    halo = xr[:HALO_ROWS, :] if reverse else xr[TM - HALO_ROWS:, :]
    return xc, halo


def _rg_gates_scan(xc, wg_ref, ba_ref, bx_ref, decay_ref, h_ref, carry_ref, reverse):
    row = lax.broadcasted_iota(jnp.int32, (SUBLANES, RG_BLOCK_W), 0)
    shift = SUBLANES - 1 if reverse else 1
    steps = range(SEG_LEN - 1, -1, -1) if reverse else range(SEG_LEN)
    for n in range(RG_BLOCKS):
        cols = slice(n * RG_BLOCK_W, (n + 1) * RG_BLOCK_W)
        xb = xc[:, cols]
        gates = _dot(xb.astype(BF16), wg_ref[n])
        th_r = jnp.tanh(0.5 * (gates[:, :RG_BLOCK_W] + ba_ref[:, cols]))
        th_i = jnp.tanh(0.5 * (gates[:, RG_BLOCK_W:] + bx_ref[:, cols]))
        half_decay = decay_ref[:, cols]
        a = jnp.exp2(th_r * half_decay + half_decay)
        y = 1.0 - a * a
        root = y * lax.rsqrt(jnp.maximum(y, F32_TINY))
        u = root * ((0.5 * th_i + 0.5) * xb)

        h = jnp.zeros((SUBLANES, RG_BLOCK_W), F32)
        prod = jnp.ones((SUBLANES, RG_BLOCK_W), F32)
        local, prods = {}, {}
        for j in steps:
            grp = slice(j * SUBLANES, (j + 1) * SUBLANES)
            h = a[grp, :] * h + u[grp, :]
            prod = prod * a[grp, :]
            local[j], prods[j] = h, prod

        entering = carry_ref[:, cols]
        for s in range(SUBLANES - 1):
            nxt = pltpu.roll(prod * entering + h, shift, 0)
            target = SUBLANES - 2 - s if reverse else s + 1
            entering = jnp.where(row == target, nxt, entering)
        carry_ref[:, cols] = pltpu.roll(prod * entering + h, shift, 0)

        h_ref[:, cols] = jnp.concatenate(
            [local[j] + prods[j] * entering for j in range(SEG_LEN)], axis=0)


def _odd_fwd_kernel(xx_ref, mod_ref, g_ref, wx_ref, cw_ref, cb_ref, wg_ref, ba_ref, bx_ref, lam_ref,
                    xr_ref, hf_ref, stage_ref, halo_ref, carry_ref, decay_ref):
    t = pl.program_id(1)

    @pl.when(t == 0)
    def _():
        carry_ref[...] = jnp.zeros_like(carry_ref)
        decay_ref[...] = (-0.5 * RG_C * LOG2_E) * _softplus(-lam_ref[...])

    @pl.when(t <= 1)
    def _():
        halo_ref[...] = jnp.zeros_like(halo_ref)

    x = _to_segment_major(xx_ref, stage_ref, D_MODEL)
    h = _modulated_norm(x, g_ref[...], mod_ref[0:1, :], mod_ref[1:2, :])
    xr = _dot(h.astype(BF16), wx_ref[...])
    xr_ref[...] = xr
    xc, halo = _rg_conv(xr, halo_ref, cw_ref, cb_ref, reverse=False)
    halo_ref[...] = halo

    _rg_gates_scan(xc, wg_ref, ba_ref, bx_ref, decay_ref, hf_ref, carry_ref, reverse=False)


def _odd_fwd(xx, sel, g, wx, cw, cb, wg, ba, bx, lam, nt):
    bsz = xx.shape[0]
    tok = lambda n: pl.BlockSpec((None, TM, n), lambda b, t: (b, t, 0))
    consts = (g, wx, cw, cb, wg, ba, bx, lam)
    wide = jax.ShapeDtypeStruct((bsz, nt * TM, RG_WIDTH), F32)
    return pl.pallas_call(
        _odd_fwd_kernel,
        grid=(bsz, nt),
        in_specs=[tok(D_MODEL),
                  pl.BlockSpec((None, None, 3, D_MODEL), lambda b, t: (b, jnp.minimum(t, 1), 0, 0))]
        + [_const_spec(w.shape) for w in consts],
        out_specs=[tok(RG_WIDTH), tok(RG_WIDTH)],
        out_shape=[wide, wide],
        scratch_shapes=[pltpu.VMEM((D_MODEL // LANES, SUBLANES * SEG_PITCH, LANES), F32),
                        pltpu.VMEM((HALO_ROWS, RG_WIDTH), F32),
                        pltpu.VMEM((SUBLANES, RG_WIDTH), F32),
                        pltpu.VMEM((1, RG_WIDTH), F32)],
        compiler_params=_params(2),
        name="odd_fwd",
    )(xx, sel, *consts)


def _odd_bwd_kernel(xx_ref, mod_ref, g_ref, xr_ref, hf_ref, wgate_ref, cw_ref, cb_ref, wg_ref, ba_ref,
                    bx_ref, lam_ref, wout_ref, fg_ref, out_ref,
                    stage_ref, halo_ref, hb_ref, carry_ref, decay_ref):
    j = pl.program_id(1)

    @pl.when(j == 0)
    def _():
        carry_ref[...] = jnp.zeros_like(carry_ref)
        decay_ref[...] = (-0.5 * RG_C * LOG2_E) * _softplus(-lam_ref[...])

    @pl.when(j <= 1)
    def _():
        halo_ref[...] = jnp.zeros_like(halo_ref)

    xc, halo = _rg_conv(xr_ref[...], halo_ref, cw_ref, cb_ref, reverse=True)
    halo_ref[...] = halo

    _rg_gates_scan(xc, wg_ref, ba_ref, bx_ref, decay_ref, hb_ref, carry_ref, reverse=True)

    @pl.when(j > 0)
    def _():
        x = _to_segment_major(xx_ref, stage_ref, D_MODEL)
        h = _modulated_norm(x, g_ref[...], mod_ref[0:1, :], mod_ref[1:2, :])
        gate = _dot(h.astype(BF16), wgate_ref[...])
        yy = (hf_ref[...] + hb_ref[...]) * _silu(gate)
        xn = x + mod_ref[2:3, :] * _dot(yy.astype(BF16), wout_ref[...])
        ms = jnp.mean(xn * xn, axis=-1, keepdims=True)
        _from_segment_major(xn * lax.rsqrt(ms + EPS) * fg_ref[...], stage_ref, out_ref, D_MODEL)


def _odd_bwd(xx, sel, g, xr, hf, wgate, cw, cb, wg, ba, bx, lam, wout, fg, nt):
    bsz = xx.shape[0]
    tile = _bwd_tile(nt)
    tok = lambda n: pl.BlockSpec((None, TM, n), lambda b, j: (b, tile(j), 0))
    consts = (wgate, cw, cb, wg, ba, bx, lam, wout, fg)
    out_block = lambda b, j: (b, jnp.where(j == 0, nt - 2, nt - 1 - j), 0)
    return pl.pallas_call(
        _odd_bwd_kernel,
        grid=(bsz, nt),
        in_specs=[tok(D_MODEL),
                  pl.BlockSpec((None, None, 3, D_MODEL), lambda b, j: (b, jnp.minimum(j, 1), 0, 0)),
                  _const_spec(g.shape), tok(RG_WIDTH), tok(RG_WIDTH)]
        + [_const_spec(w.shape) for w in consts],
        out_specs=pl.BlockSpec((None, TM, D_MODEL), out_block),
        out_shape=jax.ShapeDtypeStruct((bsz, (nt - 1) * TM, D_MODEL), F32),
        scratch_shapes=[pltpu.VMEM((D_MODEL // LANES, SUBLANES * SEG_PITCH, LANES), F32),
                        pltpu.VMEM((HALO_ROWS, RG_WIDTH), F32),
                        pltpu.VMEM((TM, RG_WIDTH), F32),
                        pltpu.VMEM((SUBLANES, RG_WIDTH), F32),
                        pltpu.VMEM((1, RG_WIDTH), F32)],
        compiler_params=_params(2),
        name="odd_bwd",
    )(xx, sel, g, xr, hf, *consts)


def _select_mods(m, bsz):
    per = m[:bsz].reshape(bsz, 1, 3, D_MODEL)
    ctx = jnp.broadcast_to(m[bsz].reshape(1, 1, 3, D_MODEL), (bsz, 1, 3, D_MODEL))
    return jnp.concatenate([ctx, per], axis=1)


def _gate_weights(w_a, w_x):
    return jnp.concatenate([w_a, w_x], axis=-1).astype(BF16)


def kernel(x, c, ctx, c_ctx, norm_g, w_mod, b_mod, e_w_in, e_w_a2, e_b_a2, e_gla_g, e_conv_w, e_w_out,
           o_w_in, o_conv_w, o_conv_b, o_w_a, o_b_a, o_w_x, o_b_x, o_lam, o_w_out, final_g):
    bsz, seq, _ = x.shape
    assert ctx.shape[1] == TM and seq % TM == 0 and bsz < SUBLANES
    assert w_mod.shape[0] == 2
    nt = 1 + seq // TM

    rows = jnp.concatenate([c, c_ctx[None, :], jnp.zeros((SUBLANES - bsz - 1, D_MODEL), F32)], axis=0)
    mods = _mods(rows, w_mod, b_mod)
    sel0 = _select_mods(mods[0], bsz)
    sel1 = _select_mods(mods[1], bsz)

    w_in = e_w_in[0]
    offs = {}
    off = 0
    for name, size in (("q", GLA_QK), ("k", GLA_QK), ("v", GLA_V), ("ga", GLA_V), ("af", GLA_LOWRANK),
                       ("ab", GLA_LOWRANK), ("cb", SC_WIDTH), ("cc", SC_WIDTH), ("cx", SC_WIDTH),
                       ("gb", SC_WIDTH)):
        offs[name] = (off, off + size)
        off += size
    cols = lambda lo, hi: w_in[:, lo:hi].astype(BF16)
    wqk = cols(offs["q"][0], offs["k"][1])
    wv = cols(*offs["v"])
    wga = cols(*offs["ga"])
    wa = jnp.pad(w_in[:, offs["af"][0]:offs["ab"][1]], ((0, 0), (0, LANES - 2 * GLA_LOWRANK))).astype(BF16)
    w2 = jnp.zeros((LANES, 2 * GLA_QK), F32)
    w2 = w2.at[:GLA_LOWRANK, :GLA_QK].set(e_w_a2[0, 0])
    w2 = w2.at[GLA_LOWRANK:2 * GLA_LOWRANK, GLA_QK:].set(e_w_a2[0, 1]).astype(BF16)
    b2 = e_b_a2[0].reshape(1, 2 * GLA_QK)
    wts = (wqk, wv, wga, wa, w2, b2, cols(*offs["cb"]), cols(*offs["cc"]), cols(*offs["cx"]),
           cols(*offs["gb"]), e_conv_w[0])
    (qi_f, ki_f, ke_f, dec_f, qi_b, ki_b, ke_b, dec_b, v, sga, y) = _even_in(
        ctx, x, sel0, norm_g[0:1], wts, nt)
    o_f = _gla_fwd(qi_f, ki_f, ke_f, v, dec_f, nt)
    w_out = e_w_out[0].astype(BF16)
    xx = _gla_bwd(qi_b, ki_b, ke_b, v, dec_b, o_f, sga, y, ctx, x, sel0, e_gla_g[0:1],
                  w_out[:GLA_V], w_out[GLA_V:], nt)

    w_in = o_w_in[0]
    wide = lambda a: a.reshape(1, RG_WIDTH)
    xr, h_f = _odd_fwd(xx, sel1, norm_g[1:2], w_in[:, :RG_WIDTH].astype(BF16), o_conv_w[0, 0],
                       wide(o_conv_b[0, 0]), _gate_weights(o_w_a[0, 0], o_w_x[0, 0]),
                       wide(o_b_a[0, 0]), wide(o_b_x[0, 0]), wide(o_lam[0, 0]), nt)
    return _odd_bwd(xx, sel1, norm_g[1:2], xr, h_f, w_in[:, RG_WIDTH:].astype(BF16), o_conv_w[0, 1],
                    wide(o_conv_b[0, 1]), _gate_weights(o_w_a[0, 1], o_w_x[0, 1]),
                    wide(o_b_a[0, 1]), wide(o_b_x[0, 1]), wide(o_lam[0, 1]), o_w_out[0].astype(BF16),
                    final_g.reshape(1, D_MODEL), nt)
```

```python
import functools

import jax
import jax.numpy as jnp
from jax import lax
from jax.experimental import pallas as pl
from jax.experimental.pallas import tpu as pltpu

F32 = jnp.float32
BF16 = jnp.bfloat16

D_MODEL = 1024
EPS = 1e-6
GLA_HEADS = 4
GLA_DK = 128
GLA_DV = 256
GLA_QK = GLA_HEADS * GLA_DK
GLA_V = GLA_HEADS * GLA_DV
GLA_LOWRANK = 16
GLA_TAU = 16.0
GLA_CHUNK = 64
GRID_W = 64
SC_WIDTH = D_MODEL
RG_WIDTH = 2 * D_MODEL
RG_BLOCKS = 16
RG_BLOCK_W = 128
RG_C = 8.0
RG_CONV = 4

TM = 256
CHUNKS = TM // GLA_CHUNK
SUBLANES = 8
LANES = 128
VMEM_LIMIT = 56 * 1024 * 1024
SEG_LEN = TM // SUBLANES
SEG_PITCH = SEG_LEN + SUBLANES
HALO_ROWS = (RG_CONV - 1) * SUBLANES
LOG2_E = 1.4426950408889634
F32_TINY = 1.1754943508222875e-38


def _silu(x):
    return x * jax.nn.sigmoid(x)


def _log_sigmoid(z):
    return jnp.minimum(z, 0.0) - jnp.log(1.0 + jnp.exp(-jnp.abs(z)))


def _softplus(z):
    return jnp.maximum(z, 0.0) + jnp.log1p(jnp.exp(-jnp.abs(z)))


def _dot(a, b):
    return jnp.dot(a, b, preferred_element_type=F32)


def _dot_nt(a, b):
    return lax.dot_general(a, b, (((1,), (1,)), ((), ())), preferred_element_type=F32)


def _dot_tn(a, b):
    return lax.dot_general(a, b, (((0,), (0,)), ((), ())), preferred_element_type=F32)


def _modulated_norm(x, g, shift, scale):
    ms = jnp.mean(x * x, axis=-1, keepdims=True)
    return (x * lax.rsqrt(ms + EPS) * g) * (1.0 + scale) + shift


def _params(n_axes):
    return pltpu.CompilerParams(
        dimension_semantics=("arbitrary",) * n_axes,
        vmem_limit_bytes=VMEM_LIMIT)


def _const_spec(shape):
    zeros = (0,) * len(shape)
    return pl.BlockSpec(shape, lambda *_: zeros, pipeline_mode=pl.Buffered(1))


def _mods_kernel(s_ref, w_ref, b_ref, o_ref):
    s = _silu(s_ref[...])
    o_ref[0] = jnp.dot(s, w_ref[0], preferred_element_type=F32,
                       precision=lax.Precision.HIGHEST) + b_ref[0]


def _mods(rows, w_mod, b_mod):
    depth = w_mod.shape[0]
    nb = 3
    return pl.pallas_call(
        _mods_kernel,
        grid=(depth, nb),
        in_specs=[
            pl.BlockSpec((SUBLANES, D_MODEL), lambda l, j: (0, 0)),
            pl.BlockSpec((1, D_MODEL, D_MODEL), lambda l, j: (l, 0, j)),
            pl.BlockSpec((1, 1, D_MODEL), lambda l, j: (l, 0, j)),
        ],
        out_specs=pl.BlockSpec((1, SUBLANES, D_MODEL), lambda l, j: (l, 0, j)),
        out_shape=jax.ShapeDtypeStruct((depth, SUBLANES, 3 * D_MODEL), F32),
        compiler_params=_params(2),
        name="mods",
    )(rows, w_mod, b_mod.reshape(depth, 1, 3 * D_MODEL))


def _split_dot(mat, x):
    hi = x.astype(BF16)
    lo = (x - hi.astype(F32)).astype(BF16)
    return _dot(mat, hi) + _dot(mat, lo)


def _chunk_rows(x, offset):
    return jnp.concatenate(
        [jnp.broadcast_to(x[c * GLA_CHUNK + offset:c * GLA_CHUNK + offset + 1, :], (GLA_CHUNK, x.shape[1]))
         for c in range(CHUNKS)], axis=0)


def _even_in_kernel(ctx_ref, x_ref, mod_ref, g_ref, b2_ref, cw_ref,
                    lower_in, upper_in, wqk_in, wv_in, wga_in, wa_in, w2_in, wcb_in, wcc_in, wcx_in, wgb_in,
                    qif_ref, kif_ref, kef_ref, decf_ref, qib_ref, kib_ref, keb_ref, decb_ref,
                    v_ref, sga_ref, y_ref,
                    lower_ref, upper_ref, wqk_ref, wv_ref, wga_ref, wa_ref, w2_ref, wcb_ref, wcc_ref, wcx_ref,
                    wgb_ref):
    t = pl.program_id(1)

    @pl.when(t == 0)
    def _():
        for src, dst in ((lower_in, lower_ref), (upper_in, upper_ref), (wqk_in, wqk_ref), (wv_in, wv_ref),
                         (wga_in, wga_ref), (wa_in, wa_ref), (w2_in, w2_ref), (wcb_in, wcb_ref),
                         (wcc_in, wcc_ref), (wcx_in, wcx_ref), (wgb_in, wgb_ref)):
            dst[...] = src[...]

    xt = jnp.where(t == 0, ctx_ref[...], x_ref[...])
    h = _modulated_norm(xt, g_ref[...], mod_ref[0:1, :], mod_ref[1:2, :])
    hb = h.astype(BF16)

    a_lr = _dot(hb, wa_ref[...])
    z = _dot(a_lr.astype(BF16), w2_ref[...]) + b2_ref[...]
    qk = _dot(hb, wqk_ref[...])
    v_ref[...] = _dot(hb, wv_ref[...]).astype(BF16)

    lg = _log_sigmoid(z) * (1.0 / GLA_TAU)
    cs_f = _split_dot(lower_ref[...], lg[:, :GLA_QK])
    cs_b = _split_dot(upper_ref[...], lg[:, GLA_QK:])
    tot_f = _chunk_rows(cs_f, GLA_CHUNK - 1)
    tot_b = _chunk_rows(cs_b, 0)

    sga_ref[...] = _silu(_dot(hb, wga_ref[...])).astype(BF16)

    q = qk[:, :GLA_QK] * (GLA_DK ** -0.5)
    k = qk[:, GLA_QK:]
    qif_ref[...] = (q * jnp.exp(cs_f)).astype(BF16)
    kif_ref[...] = (k * jnp.exp(-cs_f)).astype(BF16)
    kef_ref[...] = (k * jnp.exp(tot_f - cs_f)).astype(BF16)
    qib_ref[...] = (q * jnp.exp(cs_b)).astype(BF16)
    kib_ref[...] = (k * jnp.exp(-cs_b)).astype(BF16)
    keb_ref[...] = (k * jnp.exp(tot_b - cs_b)).astype(BF16)
    for c in range(CHUNKS):
        r = c * GLA_CHUNK
        decf_ref[c:c + 1, :] = jnp.exp(tot_f[r:r + 1, :])
        decb_ref[c:c + 1, :] = jnp.exp(tot_b[r:r + 1, :])

    zz = _dot(hb, wcc_ref[...]) * _dot(hb, wcx_ref[...])
    pos = lax.broadcasted_iota(jnp.int32, (TM, SC_WIDTH), 0)
    row_len = jnp.where(t == 0, TM, GRID_W)
    in_row = pos & (row_len - 1)
    z_prev = jnp.where(in_row != 0, pltpu.roll(zz, 1, 0), 0.0)
    z_next = jnp.where(in_row != row_len - 1, pltpu.roll(zz, TM - 1, 0), 0.0)
    zc = cw_ref[0:1, :] * z_prev + cw_ref[1:2, :] * zz + cw_ref[2:3, :] * z_next
    y = _dot(hb, wcb_ref[...]) * zc * _silu(_dot(hb, wgb_ref[...]))
    y_ref[...] = y.astype(BF16)


def _chunk_triangles():
    r = lax.broadcasted_iota(jnp.int32, (TM, TM), 0)
    c = lax.broadcasted_iota(jnp.int32, (TM, TM), 1)
    same = (r // GLA_CHUNK) == (c // GLA_CHUNK)
    return (same & (c <= r)).astype(BF16), (same & (c >= r)).astype(BF16)


def _ctx_spec():
    return pl.BlockSpec((None, TM, D_MODEL), lambda b, t: (b, 0, 0))


def _latent_spec(tile, park):
    return pl.BlockSpec((None, TM, D_MODEL),
                        lambda b, t: (b, jnp.where(tile(t) == 0, park, tile(t) - 1), 0))


def _even_in(ctx, x, sel, rows, mats, nt):
    bsz = x.shape[0]
    tok = lambda n: pl.BlockSpec((None, TM, n), lambda b, t: (b, t, 0))
    dec = pl.BlockSpec((None, None, CHUNKS, GLA_QK), lambda b, t: (b, t, 0, 0))
    tok_shape = lambda n: jax.ShapeDtypeStruct((bsz, nt * TM, n), BF16)
    dec_shape = jax.ShapeDtypeStruct((bsz, nt, CHUNKS, GLA_QK), F32)
    mats = _chunk_triangles() + tuple(mats)
    consts = tuple(rows) + mats
    return pl.pallas_call(
        _even_in_kernel,
        grid=(bsz, nt),
        in_specs=[
            _ctx_spec(), _latent_spec(lambda t: t, 0),
            pl.BlockSpec((None, None, 3, D_MODEL), lambda b, t: (b, jnp.minimum(t, 1), 0, 0)),
        ] + [_const_spec(w.shape) for w in consts],
        out_specs=[tok(GLA_QK), tok(GLA_QK), tok(GLA_QK), dec,
                   tok(GLA_QK), tok(GLA_QK), tok(GLA_QK), dec,
                   tok(GLA_V), tok(GLA_V), tok(SC_WIDTH)],
        out_shape=[tok_shape(GLA_QK), tok_shape(GLA_QK), tok_shape(GLA_QK), dec_shape,
                   tok_shape(GLA_QK), tok_shape(GLA_QK), tok_shape(GLA_QK), dec_shape,
                   tok_shape(GLA_V), tok_shape(GLA_V), tok_shape(SC_WIDTH)],
        scratch_shapes=[pltpu.VMEM(m.shape, BF16) for m in mats],
        compiler_params=_params(2),
        name="even_in",
    )(ctx, x, sel, *consts)


def _gla_tile(qi_ref, ki_ref, ke_ref, v_ref, dec_ref, st_ref, emit, reverse):
    r = lax.broadcasted_iota(jnp.int32, (TM, TM), 0)
    c = lax.broadcasted_iota(jnp.int32, (TM, TM), 1)
    same_chunk = (r // GLA_CHUNK) == (c // GLA_CHUNK)
    mask = same_chunk & ((c >= r) if reverse else (c <= r))
    order = range(CHUNKS - 1, -1, -1) if reverse else range(CHUNKS)
    chunk_rows = lambda ch: slice(ch * GLA_CHUNK, (ch + 1) * GLA_CHUNK)
    for hd in range(GLA_HEADS):
        kcols = slice(hd * GLA_DK, (hd + 1) * GLA_DK)
        vcols = slice(hd * GLA_DV, (hd + 1) * GLA_DV)
        incr = [_dot_tn(v_ref[chunk_rows(ch), vcols], ke_ref[chunk_rows(ch), kcols]) for ch in range(CHUNKS)]
        state = st_ref[hd]
        before = {}
        for ch in order:
            before[ch] = state.astype(BF16)
            state = dec_ref[ch:ch + 1, kcols] * state + incr[ch]
        st_ref[hd] = state
        qi = qi_ref[:, kcols]
        scores = jnp.where(mask, _dot_nt(qi, ki_ref[:, kcols]), 0.0)
        intra = _dot(scores.astype(BF16), v_ref[:, vcols])
        for ch in range(CHUNKS):
            rows = chunk_rows(ch)
            emit(rows, vcols, intra[rows, :] + _dot_nt(qi[rows, :], before[ch]))


def _gla_fwd_kernel(qi_ref, ki_ref, ke_ref, v_ref, dec_ref, o_ref, st_ref):
    @pl.when(pl.program_id(1) == 0)
    def _():
        st_ref[...] = jnp.zeros_like(st_ref)

    def emit(rows, vcols, o):
        o_ref[rows, vcols] = o

    _gla_tile(qi_ref, ki_ref, ke_ref, v_ref, dec_ref, st_ref, emit, reverse=False)


def _gla_fwd(qi, ki, ke, v, dec, nt):
    bsz = qi.shape[0]
    tok = lambda n: pl.BlockSpec((None, TM, n), lambda b, t: (b, t, 0))
    return pl.pallas_call(
        _gla_fwd_kernel,
        grid=(bsz, nt),
        in_specs=[tok(GLA_QK), tok(GLA_QK), tok(GLA_QK), tok(GLA_V),
                  pl.BlockSpec((None, None, CHUNKS, GLA_QK), lambda b, t: (b, t, 0, 0))],
        out_specs=tok(GLA_V),
        out_shape=jax.ShapeDtypeStruct((bsz, nt * TM, GLA_V), F32),
        scratch_shapes=[pltpu.VMEM((GLA_HEADS, GLA_DV, GLA_DK), F32)],
        compiler_params=_params(2),
        name="gla_fwd",
    )(qi, ki, ke, v, dec)


def _gla_bwd_kernel(qi_ref, ki_ref, ke_ref, v_ref, dec_ref, of_ref, sga_ref, y_ref, ctx_ref, x_ref,
                    mod_ref, gg_ref, wo_ref, wy_ref, out_ref, st_ref, ob_ref):
    j = pl.program_id(1)

    @pl.when(j == 0)
    def _():
        st_ref[...] = jnp.zeros_like(st_ref)

    def emit(rows, vcols, o):
        ob_ref[rows, vcols] = o

    _gla_tile(qi_ref, ki_ref, ke_ref, v_ref, dec_ref, st_ref, emit, reverse=True)

    heads = []
    for hd in range(GLA_HEADS):
        vcols = slice(hd * GLA_DV, (hd + 1) * GLA_DV)
        o = of_ref[:, vcols] + ob_ref[:, vcols]
        ms = jnp.mean(o * o, axis=-1, keepdims=True)
        on = o * lax.rsqrt(ms + EPS) * gg_ref[...]
        heads.append((on * sga_ref[:, vcols].astype(F32)).astype(BF16))
    inner = jnp.concatenate(heads, axis=-1)
    proj = _dot(inner, wo_ref[...]) + _dot(y_ref[...], wy_ref[...])
    out_ref[...] = jnp.where(j == 0, ctx_ref[...], x_ref[...]) + mod_ref[2:3, :] * proj


def _bwd_tile(nt):
    return lambda j: jnp.where(j == 0, 0, nt - j)


def _gla_bwd(qi, ki, ke, v, dec, o_f, sga, y, ctx, x, sel, gg, wo, wy, nt):
    bsz = qi.shape[0]
    tile = _bwd_tile(nt)
    tok = lambda n: pl.BlockSpec((None, TM, n), lambda b, j: (b, tile(j), 0))
    return pl.pallas_call(
        _gla_bwd_kernel,
        grid=(bsz, nt),
        in_specs=[tok(GLA_QK), tok(GLA_QK), tok(GLA_QK), tok(GLA_V),
                  pl.BlockSpec((None, None, CHUNKS, GLA_QK), lambda b, j: (b, tile(j), 0, 0)),
                  tok(GLA_V), tok(GLA_V), tok(SC_WIDTH), _ctx_spec(), _latent_spec(tile, nt - 2),
                  pl.BlockSpec((None, None, 3, D_MODEL), lambda b, j: (b, jnp.minimum(j, 1), 0, 0)),
                  _const_spec(gg.shape), _const_spec(wo.shape), _const_spec(wy.shape)],
        out_specs=tok(D_MODEL),
        out_shape=jax.ShapeDtypeStruct((bsz, nt * TM, D_MODEL), F32),
        scratch_shapes=[pltpu.VMEM((GLA_HEADS, GLA_DV, GLA_DK), F32),
                        pltpu.VMEM((TM, GLA_V), F32)],
        compiler_params=_params(2),
        name="gla_bwd",
    )(qi, ki, ke, v, dec, o_f, sga, y, ctx, x, sel, gg, wo, wy)


def _to_segment_major(src_ref, stage_ref, width):
    slabs = width // LANES
    for n in range(slabs):
        for s in range(SUBLANES):
            stage_ref[n, s * SEG_PITCH:s * SEG_PITCH + SEG_LEN, :] = (
                src_ref[s * SEG_LEN:(s + 1) * SEG_LEN, n * LANES:(n + 1) * LANES])
    groups = [
        jnp.concatenate([stage_ref[n, pl.ds(j, SUBLANES, stride=SEG_PITCH), :] for n in range(slabs)], axis=1)
        for j in range(SEG_LEN)]
    return jnp.concatenate(groups, axis=0)


def _from_segment_major(val, stage_ref, dst_ref, width):
    slabs = width // LANES
    for j in range(SEG_LEN):
        for n in range(slabs):
            stage_ref[n, pl.ds(j, SUBLANES, stride=SEG_PITCH), :] = (
                val[j * SUBLANES:(j + 1) * SUBLANES, n * LANES:(n + 1) * LANES])
    for n in range(slabs):
        for s in range(SUBLANES):
            dst_ref[s * SEG_LEN:(s + 1) * SEG_LEN, n * LANES:(n + 1) * LANES] = (
                stage_ref[n, s * SEG_PITCH:s * SEG_PITCH + SEG_LEN, :])


def _rg_conv(xr, halo_ref, cw_ref, cb_ref, reverse):
    row = lax.broadcasted_iota(jnp.int32, (SUBLANES, RG_WIDTH), 0)
    edge = []
    for g in range(RG_CONV - 1):
        grp = slice(g * SUBLANES, (g + 1) * SUBLANES)
        if reverse:
            edge.append(jnp.where(row == SUBLANES - 1, pltpu.roll(halo_ref[grp, :], SUBLANES - 1, 0),
                                  pltpu.roll(xr[grp, :], SUBLANES - 1, 0)))
        else:
            cur = xr[TM - (RG_CONV - 1 - g) * SUBLANES:TM - (RG_CONV - 2 - g) * SUBLANES, :]
            edge.append(jnp.where(row == 0, pltpu.roll(halo_ref[grp, :], 1, 0), pltpu.roll(cur, 1, 0)))
    xc = cb_ref[...] + cw_ref[RG_CONV - 1:RG_CONV, :] * xr
    for k in range(1, RG_CONV):
        if reverse:
            tap = jnp.concatenate([xr[k * SUBLANES:, :]] + edge[:k], axis=0)
        else:
            tap = jnp.concatenate(edge[RG_CONV - 1 - k:] + [xr[:TM - k * SUBLANES, :]], axis=0)
        xc = xc + cw_ref[RG_CONV - 1 - k:RG_CONV - k, :] * tap
    halo = xr[:HALO_ROWS, :] if reverse else xr[TM - HALO_ROWS:, :]
    return xc, halo


def _rg_setup(lam_ref, ba_ref, bx_ref, rows_ref, copies):
    rows_ref[0:1, :] = (-0.5 * RG_C * LOG2_E) * _softplus(-lam_ref[...])
    rows_ref[1:2, :] = 0.5 * ba_ref[...]
    rows_ref[2:3, :] = 0.5 * bx_ref[...]
    for src_ref, dst_ref in copies:
        dst_ref[...] = src_ref[...]


def _rg_gates_scan(xc, wg_ref, rows_ref, h_ref, carry_ref, reverse):
    row = lax.broadcasted_iota(jnp.int32, (SUBLANES, RG_BLOCK_W), 0)
    shift = SUBLANES - 1 if reverse else 1
    steps = range(SEG_LEN - 1, -1, -1) if reverse else range(SEG_LEN)
    for n in range(RG_BLOCKS):
        cols = slice(n * RG_BLOCK_W, (n + 1) * RG_BLOCK_W)
        xb = xc[:, cols]
        gates = _dot(xb.astype(BF16), wg_ref[n])
        th_r = jnp.tanh(gates[:, :RG_BLOCK_W] + rows_ref[1:2, cols])
        th_i = jnp.tanh(gates[:, RG_BLOCK_W:] + rows_ref[2:3, cols])
        half_decay = rows_ref[0:1, cols]
        a = jnp.exp2(th_r * half_decay + half_decay)
        y = 1.0 - a * a
        root = y * lax.rsqrt(jnp.maximum(y, F32_TINY))
        u = root * ((0.5 * th_i + 0.5) * xb)

        h = jnp.zeros((SUBLANES, RG_BLOCK_W), F32)
        prod = jnp.ones((SUBLANES, RG_BLOCK_W), F32)
        local, prods = {}, {}
        for j in steps:
            grp = slice(j * SUBLANES, (j + 1) * SUBLANES)
            h = a[grp, :] * h + u[grp, :]
            prod = prod * a[grp, :]
            local[j], prods[j] = h, prod

        entering = carry_ref[:, cols]
        for s in range(SUBLANES - 1):
            nxt = pltpu.roll(prod * entering + h, shift, 0)
            target = SUBLANES - 2 - s if reverse else s + 1
            entering = jnp.where(row == target, nxt, entering)
        carry_ref[:, cols] = pltpu.roll(prod * entering + h, shift, 0)

        h_ref[:, cols] = jnp.concatenate(
            [local[j] + prods[j] * entering for j in range(SEG_LEN)], axis=0)


def _odd_fwd_kernel(xx_ref, mod_ref, g_ref, wx_ref, cw_ref, cb_ref, wg_ref, ba_ref, bx_ref, lam_ref,
                    xr_ref, hf_ref, stage_ref, halo_ref, carry_ref, rows_ref, wx_s, wg_s):
    t = pl.program_id(1)

    @pl.when(t == 0)
    def _():
        carry_ref[...] = jnp.zeros_like(carry_ref)
        _rg_setup(lam_ref, ba_ref, bx_ref, rows_ref, ((wx_ref, wx_s), (wg_ref, wg_s)))

    @pl.when(t <= 1)
    def _():
        halo_ref[...] = jnp.zeros_like(halo_ref)

    x = _to_segment_major(xx_ref, stage_ref, D_MODEL)
    h = _modulated_norm(x, g_ref[...], mod_ref[0:1, :], mod_ref[1:2, :])
    xr = _dot(h.astype(BF16), wx_s[...])
    xr_ref[...] = xr
    xc, halo = _rg_conv(xr, halo_ref, cw_ref, cb_ref, reverse=False)
    halo_ref[...] = halo

    _rg_gates_scan(xc, wg_s, rows_ref, hf_ref, carry_ref, reverse=False)


def _odd_fwd(xx, sel, g, wx, cw, cb, wg, ba, bx, lam, nt):
    bsz = xx.shape[0]
    tok = lambda n: pl.BlockSpec((None, TM, n), lambda b, t: (b, t, 0))
    consts = (g, wx, cw, cb, wg, ba, bx, lam)
    wide = jax.ShapeDtypeStruct((bsz, nt * TM, RG_WIDTH), F32)
    return pl.pallas_call(
        _odd_fwd_kernel,
        grid=(bsz, nt),
        in_specs=[tok(D_MODEL),
                  pl.BlockSpec((None, None, 3, D_MODEL), lambda b, t: (b, jnp.minimum(t, 1), 0, 0))]
        + [_const_spec(w.shape) for w in consts],
        out_specs=[tok(RG_WIDTH), tok(RG_WIDTH)],
        out_shape=[wide, wide],
        scratch_shapes=[pltpu.VMEM((D_MODEL // LANES, SUBLANES * SEG_PITCH, LANES), F32),
                        pltpu.VMEM((HALO_ROWS, RG_WIDTH), F32),
                        pltpu.VMEM((SUBLANES, RG_WIDTH), F32),
                        pltpu.VMEM((3, RG_WIDTH), F32),
                        pltpu.VMEM(wx.shape, BF16),
                        pltpu.VMEM(wg.shape, BF16)],
        compiler_params=_params(2),
        name="odd_fwd",
    )(xx, sel, *consts)


def _odd_bwd_kernel(xx_ref, mod_ref, g_ref, xr_ref, hf_ref, wgate_ref, cw_ref, cb_ref, wg_ref, ba_ref,
                    bx_ref, lam_ref, wout_ref, fg_ref, out_ref,
                    stage_ref, halo_ref, hb_ref, carry_ref, rows_ref, wgate_s, wg_s, wout_s):
    j = pl.program_id(1)

    @pl.when(j == 0)
    def _():
        carry_ref[...] = jnp.zeros_like(carry_ref)
        _rg_setup(lam_ref, ba_ref, bx_ref, rows_ref,
                  ((wgate_ref, wgate_s), (wg_ref, wg_s), (wout_ref, wout_s)))

    @pl.when(j <= 1)
    def _():
        halo_ref[...] = jnp.zeros_like(halo_ref)

    xc, halo = _rg_conv(xr_ref[...], halo_ref, cw_ref, cb_ref, reverse=True)
    halo_ref[...] = halo

    _rg_gates_scan(xc, wg_s, rows_ref, hb_ref, carry_ref, reverse=True)

    @pl.when(j > 0)
    def _():
        x = _to_segment_major(xx_ref, stage_ref, D_MODEL)
        h = _modulated_norm(x, g_ref[...], mod_ref[0:1, :], mod_ref[1:2, :])
        gate = _dot(h.astype(BF16), wgate_s[...])
        yy = (hf_ref[...] + hb_ref[...]) * _silu(gate)
        xn = x + mod_ref[2:3, :] * _dot(yy.astype(BF16), wout_s[...])
        ms = jnp.mean(xn * xn, axis=-1, keepdims=True)
        _from_segment_major(xn * lax.rsqrt(ms + EPS) * fg_ref[...], stage_ref, out_ref, D_MODEL)


def _odd_bwd(xx, sel, g, xr, hf, wgate, cw, cb, wg, ba, bx, lam, wout, fg, nt):
    bsz = xx.shape[0]
    tile = _bwd_tile(nt)
    tok = lambda n: pl.BlockSpec((None, TM, n), lambda b, j: (b, tile(j), 0))
    consts = (wgate, cw, cb, wg, ba, bx, lam, wout, fg)
    out_block = lambda b, j: (b, jnp.where(j == 0, nt - 2, nt - 1 - j), 0)
    return pl.pallas_call(
        _odd_bwd_kernel,
        grid=(bsz, nt),
        in_specs=[tok(D_MODEL),
                  pl.BlockSpec((None, None, 3, D_MODEL), lambda b, j: (b, jnp.minimum(j, 1), 0, 0)),
                  _const_spec(g.shape), tok(RG_WIDTH), tok(RG_WIDTH)]
        + [_const_spec(w.shape) for w in consts],
        out_specs=pl.BlockSpec((None, TM, D_MODEL), out_block),
        out_shape=jax.ShapeDtypeStruct((bsz, (nt - 1) * TM, D_MODEL), F32),
        scratch_shapes=[pltpu.VMEM((D_MODEL // LANES, SUBLANES * SEG_PITCH, LANES), F32),
                        pltpu.VMEM((HALO_ROWS, RG_WIDTH), F32),
                        pltpu.VMEM((TM, RG_WIDTH), F32),
                        pltpu.VMEM((SUBLANES, RG_WIDTH), F32),
                        pltpu.VMEM((3, RG_WIDTH), F32),
                        pltpu.VMEM(wgate.shape, BF16),
                        pltpu.VMEM(wg.shape, BF16),
                        pltpu.VMEM(wout.shape, BF16)],
        compiler_params=_params(2),
        name="odd_bwd",
    )(xx, sel, g, xr, hf, *consts)


def _select_mods(m, bsz):
    per = m[:bsz].reshape(bsz, 1, 3, D_MODEL)
    ctx = jnp.broadcast_to(m[bsz].reshape(1, 1, 3, D_MODEL), (bsz, 1, 3, D_MODEL))
    return jnp.concatenate([ctx, per], axis=1)


def _gate_weights(w_a, w_x):
    return (0.5 * jnp.concatenate([w_a, w_x], axis=-1)).astype(BF16)


def kernel(x, c, ctx, c_ctx, norm_g, w_mod, b_mod, e_w_in, e_w_a2, e_b_a2, e_gla_g, e_conv_w, e_w_out,
           o_w_in, o_conv_w, o_conv_b, o_w_a, o_b_a, o_w_x, o_b_x, o_lam, o_w_out, final_g):
    bsz, seq, _ = x.shape
    assert ctx.shape[1] == TM and seq % TM == 0 and bsz < SUBLANES
    assert w_mod.shape[0] == 2
    nt = 1 + seq // TM

    rows = jnp.concatenate([c, c_ctx[None, :], jnp.zeros((SUBLANES - bsz - 1, D_MODEL), F32)], axis=0)
    mods = _mods(rows, w_mod, b_mod)
    sel0 = _select_mods(mods[0], bsz)
    sel1 = _select_mods(mods[1], bsz)

    w_in = e_w_in[0]
    offs = {}
    off = 0
    for name, size in (("q", GLA_QK), ("k", GLA_QK), ("v", GLA_V), ("ga", GLA_V), ("af", GLA_LOWRANK),
                       ("ab", GLA_LOWRANK), ("cb", SC_WIDTH), ("cc", SC_WIDTH), ("cx", SC_WIDTH),
                       ("gb", SC_WIDTH)):
        offs[name] = (off, off + size)
        off += size
    cols = lambda lo, hi: w_in[:, lo:hi].astype(BF16)
    wqk = cols(offs["q"][0], offs["k"][1])
    wv = cols(*offs["v"])
    wga = cols(*offs["ga"])
    wa = jnp.pad(w_in[:, offs["af"][0]:offs["ab"][1]], ((0, 0), (0, LANES - 2 * GLA_LOWRANK))).astype(BF16)
    w2 = jnp.zeros((LANES, 2 * GLA_QK), F32)
    w2 = w2.at[:GLA_LOWRANK, :GLA_QK].set(e_w_a2[0, 0])
    w2 = w2.at[GLA_LOWRANK:2 * GLA_LOWRANK, GLA_QK:].set(e_w_a2[0, 1]).astype(BF16)
    b2 = e_b_a2[0].reshape(1, 2 * GLA_QK)
    mats = (wqk, wv, wga, wa, w2, cols(*offs["cb"]), cols(*offs["cc"]), cols(*offs["cx"]), cols(*offs["gb"]))
    (qi_f, ki_f, ke_f, dec_f, qi_b, ki_b, ke_b, dec_b, v, sga, y) = _even_in(
        ctx, x, sel0, (norm_g[0:1], b2, e_conv_w[0]), mats, nt)
    o_f = _gla_fwd(qi_f, ki_f, ke_f, v, dec_f, nt)
    w_out = e_w_out[0].astype(BF16)
    xx = _gla_bwd(qi_b, ki_b, ke_b, v, dec_b, o_f, sga, y, ctx, x, sel0, e_gla_g[0:1],
                  w_out[:GLA_V], w_out[GLA_V:], nt)

    w_in = o_w_in[0]
    wide = lambda a: a.reshape(1, RG_WIDTH)
    xr, h_f = _odd_fwd(xx, sel1, norm_g[1:2], w_in[:, :RG_WIDTH].astype(BF16), o_conv_w[0, 0],
                       wide(o_conv_b[0, 0]), _gate_weights(o_w_a[0, 0], o_w_x[0, 0]),
                       wide(o_b_a[0, 0]), wide(o_b_x[0, 0]), wide(o_lam[0, 0]), nt)
    return _odd_bwd(xx, sel1, norm_g[1:2], xr, h_f, w_in[:, RG_WIDTH:].astype(BF16), o_conv_w[0, 1],
                    wide(o_conv_b[0, 1]), _gate_weights(o_w_a[0, 1], o_w_x[0, 1]),
                    wide(o_b_a[0, 1]), wide(o_b_x[0, 1]), wide(o_lam[0, 1]), o_w_out[0].astype(BF16),
                    final_g.reshape(1, D_MODEL), nt)
```

```python
import functools

import jax
import jax.numpy as jnp
from jax import lax
from jax.experimental import pallas as pl
from jax.experimental.pallas import tpu as pltpu

F32 = jnp.float32
BF16 = jnp.bfloat16

D_MODEL = 1024
EPS = 1e-6
GLA_HEADS = 4
GLA_DK = 128
GLA_DV = 256
GLA_QK = GLA_HEADS * GLA_DK
GLA_V = GLA_HEADS * GLA_DV
GLA_LOWRANK = 16
GLA_TAU = 16.0
GLA_CHUNK = 64
GRID_W = 64
SC_WIDTH = D_MODEL
RG_WIDTH = 2 * D_MODEL
RG_BLOCKS = 16
RG_BLOCK_W = 128
RG_C = 8.0
RG_CONV = 4

TM = 256
CHUNKS = TM // GLA_CHUNK
SUBLANES = 8
LANES = 128
VMEM_LIMIT = 56 * 1024 * 1024
SEG_LEN = TM // SUBLANES
SEG_PITCH = SEG_LEN + SUBLANES
HALO_ROWS = (RG_CONV - 1) * SUBLANES
RG_CHUNK_W = 256
LOG2_E = 1.4426950408889634
F32_TINY = 1.1754943508222875e-38


def _silu(x):
    return x * jax.nn.sigmoid(x)


def _log_sigmoid(z):
    return jnp.minimum(z, 0.0) - jnp.log(1.0 + jnp.exp(-jnp.abs(z)))


def _softplus(z):
    return jnp.maximum(z, 0.0) + jnp.log1p(jnp.exp(-jnp.abs(z)))


def _dot(a, b):
    return jnp.dot(a, b, preferred_element_type=F32)


def _dot_nt(a, b):
    return lax.dot_general(a, b, (((1,), (1,)), ((), ())), preferred_element_type=F32)


def _dot_tn(a, b):
    return lax.dot_general(a, b, (((0,), (0,)), ((), ())), preferred_element_type=F32)


def _modulated_norm(x, g, shift, scale):
    ms = jnp.mean(x * x, axis=-1, keepdims=True)
    return (x * lax.rsqrt(ms + EPS) * g) * (1.0 + scale) + shift


def _params(n_axes):
    return pltpu.CompilerParams(
        dimension_semantics=("arbitrary",) * n_axes,
        vmem_limit_bytes=VMEM_LIMIT)


def _const_spec(shape):
    zeros = (0,) * len(shape)
    return pl.BlockSpec(shape, lambda *_: zeros, pipeline_mode=pl.Buffered(1))


def _mods_kernel(s_ref, w_ref, b_ref, o_ref):
    s = _silu(s_ref[...])
    o_ref[0] = jnp.dot(s, w_ref[0], preferred_element_type=F32,
                       precision=lax.Precision.HIGHEST) + b_ref[0]


def _mods(rows, w_mod, b_mod):
    depth = w_mod.shape[0]
    nb = 3
    return pl.pallas_call(
        _mods_kernel,
        grid=(depth, nb),
        in_specs=[
            pl.BlockSpec((SUBLANES, D_MODEL), lambda l, j: (0, 0)),
            pl.BlockSpec((1, D_MODEL, D_MODEL), lambda l, j: (l, 0, j)),
            pl.BlockSpec((1, 1, D_MODEL), lambda l, j: (l, 0, j)),
        ],
        out_specs=pl.BlockSpec((1, SUBLANES, D_MODEL), lambda l, j: (l, 0, j)),
        out_shape=jax.ShapeDtypeStruct((depth, SUBLANES, 3 * D_MODEL), F32),
        compiler_params=_params(2),
        name="mods",
    )(rows, w_mod, b_mod.reshape(depth, 1, 3 * D_MODEL))


def _split_dot(mat, x):
    hi = x.astype(BF16)
    lo = (x - hi.astype(F32)).astype(BF16)
    return _dot(mat, hi) + _dot(mat, lo)


def _chunk_rows(x, offset):
    return jnp.concatenate(
        [jnp.broadcast_to(x[c * GLA_CHUNK + offset:c * GLA_CHUNK + offset + 1, :], (GLA_CHUNK, x.shape[1]))
         for c in range(CHUNKS)], axis=0)


def _even_in_kernel(ctx_ref, x_ref, mod_ref, g_ref, b2_ref, cw_ref,
                    lower_in, upper_in, wqk_in, wv_in, wga_in, wa_in, w2_in, wcb_in, wcc_in, wcx_in, wgb_in,
                    qif_ref, kif_ref, kef_ref, decf_ref, qib_ref, kib_ref, keb_ref, decb_ref,
                    v_ref, sga_ref, y_ref,
                    lower_ref, upper_ref, wqk_ref, wv_ref, wga_ref, wa_ref, w2_ref, wcb_ref, wcc_ref, wcx_ref,
                    wgb_ref):
    t = pl.program_id(1)

    @pl.when(t == 0)
    def _():
        for src, dst in ((lower_in, lower_ref), (upper_in, upper_ref), (wqk_in, wqk_ref), (wv_in, wv_ref),
                         (wga_in, wga_ref), (wa_in, wa_ref), (w2_in, w2_ref), (wcb_in, wcb_ref),
                         (wcc_in, wcc_ref), (wcx_in, wcx_ref), (wgb_in, wgb_ref)):
            dst[...] = src[...]

    xt = jnp.where(t == 0, ctx_ref[...], x_ref[...])
    h = _modulated_norm(xt, g_ref[...], mod_ref[0:1, :], mod_ref[1:2, :])
    hb = h.astype(BF16)

    a_lr = _dot(hb, wa_ref[...])
    z = _dot(a_lr.astype(BF16), w2_ref[...]) + b2_ref[...]
    qk = _dot(hb, wqk_ref[...])
    v_ref[...] = _dot(hb, wv_ref[...]).astype(BF16)

    lg = _log_sigmoid(z) * (1.0 / GLA_TAU)
    cs_f = _split_dot(lower_ref[...], lg[:, :GLA_QK])
    cs_b = _split_dot(upper_ref[...], lg[:, GLA_QK:])
    tot_f = _chunk_rows(cs_f, GLA_CHUNK - 1)
    tot_b = _chunk_rows(cs_b, 0)

    sga_ref[...] = _silu(_dot(hb, wga_ref[...])).astype(BF16)

    q = qk[:, :GLA_QK] * (GLA_DK ** -0.5)
    k = qk[:, GLA_QK:]
    qif_ref[...] = (q * jnp.exp(cs_f)).astype(BF16)
    kif_ref[...] = (k * jnp.exp(-cs_f)).astype(BF16)
    kef_ref[...] = (k * jnp.exp(tot_f - cs_f)).astype(BF16)
    qib_ref[...] = (q * jnp.exp(cs_b)).astype(BF16)
    kib_ref[...] = (k * jnp.exp(-cs_b)).astype(BF16)
    keb_ref[...] = (k * jnp.exp(tot_b - cs_b)).astype(BF16)
    for c in range(CHUNKS):
        r = c * GLA_CHUNK
        decf_ref[c:c + 1, :] = jnp.exp(tot_f[r:r + 1, :])
        decb_ref[c:c + 1, :] = jnp.exp(tot_b[r:r + 1, :])

    zz = _dot(hb, wcc_ref[...]) * _dot(hb, wcx_ref[...])
    pos = lax.broadcasted_iota(jnp.int32, (TM, SC_WIDTH), 0)
    row_len = jnp.where(t == 0, TM, GRID_W)
    in_row = pos & (row_len - 1)
    z_prev = jnp.where(in_row != 0, pltpu.roll(zz, 1, 0), 0.0)
    z_next = jnp.where(in_row != row_len - 1, pltpu.roll(zz, TM - 1, 0), 0.0)
    zc = cw_ref[0:1, :] * z_prev + cw_ref[1:2, :] * zz + cw_ref[2:3, :] * z_next
    y = _dot(hb, wcb_ref[...]) * zc * _silu(_dot(hb, wgb_ref[...]))
    y_ref[...] = y.astype(BF16)


def _chunk_triangles():
    r = lax.broadcasted_iota(jnp.int32, (TM, TM), 0)
    c = lax.broadcasted_iota(jnp.int32, (TM, TM), 1)
    same = (r // GLA_CHUNK) == (c // GLA_CHUNK)
    return (same & (c <= r)).astype(BF16), (same & (c >= r)).astype(BF16)


def _ctx_spec():
    return pl.BlockSpec((None, TM, D_MODEL), lambda b, t: (b, 0, 0))


def _latent_spec(tile, park):
    return pl.BlockSpec((None, TM, D_MODEL),
                        lambda b, t: (b, jnp.where(tile(t) == 0, park, tile(t) - 1), 0))


def _even_in(ctx, x, sel, rows, mats, nt):
    bsz = x.shape[0]
    tok = lambda n: pl.BlockSpec((None, TM, n), lambda b, t: (b, t, 0))
    dec = pl.BlockSpec((None, None, CHUNKS, GLA_QK), lambda b, t: (b, t, 0, 0))
    tok_shape = lambda n: jax.ShapeDtypeStruct((bsz, nt * TM, n), BF16)
    dec_shape = jax.ShapeDtypeStruct((bsz, nt, CHUNKS, GLA_QK), F32)
    mats = _chunk_triangles() + tuple(mats)
    consts = tuple(rows) + mats
    return pl.pallas_call(
        _even_in_kernel,
        grid=(bsz, nt),
        in_specs=[
            _ctx_spec(), _latent_spec(lambda t: t, 0),
            pl.BlockSpec((None, None, 3, D_MODEL), lambda b, t: (b, jnp.minimum(t, 1), 0, 0)),
        ] + [_const_spec(w.shape) for w in consts],
        out_specs=[tok(GLA_QK), tok(GLA_QK), tok(GLA_QK), dec,
                   tok(GLA_QK), tok(GLA_QK), tok(GLA_QK), dec,
                   tok(GLA_V), tok(GLA_V), tok(SC_WIDTH)],
        out_shape=[tok_shape(GLA_QK), tok_shape(GLA_QK), tok_shape(GLA_QK), dec_shape,
                   tok_shape(GLA_QK), tok_shape(GLA_QK), tok_shape(GLA_QK), dec_shape,
                   tok_shape(GLA_V), tok_shape(GLA_V), tok_shape(SC_WIDTH)],
        scratch_shapes=[pltpu.VMEM(m.shape, BF16) for m in mats],
        compiler_params=_params(2),
        name="even_in",
    )(ctx, x, sel, *consts)


def _gla_tile(qi_ref, ki_ref, ke_ref, v_ref, dec_ref, st_ref, emit, reverse):
    r = lax.broadcasted_iota(jnp.int32, (TM, TM), 0)
    c = lax.broadcasted_iota(jnp.int32, (TM, TM), 1)
    same_chunk = (r // GLA_CHUNK) == (c // GLA_CHUNK)
    mask = same_chunk & ((c >= r) if reverse else (c <= r))
    order = range(CHUNKS - 1, -1, -1) if reverse else range(CHUNKS)
    chunk_rows = lambda ch: slice(ch * GLA_CHUNK, (ch + 1) * GLA_CHUNK)
    for hd in range(GLA_HEADS):
        kcols = slice(hd * GLA_DK, (hd + 1) * GLA_DK)
        vcols = slice(hd * GLA_DV, (hd + 1) * GLA_DV)
        incr = [_dot_tn(v_ref[chunk_rows(ch), vcols], ke_ref[chunk_rows(ch), kcols]) for ch in range(CHUNKS)]
        state = st_ref[hd]
        before = {}
        for ch in order:
            before[ch] = state.astype(BF16)
            state = dec_ref[ch:ch + 1, kcols] * state + incr[ch]
        st_ref[hd] = state
        qi = qi_ref[:, kcols]
        scores = jnp.where(mask, _dot_nt(qi, ki_ref[:, kcols]), 0.0)
        intra = _dot(scores.astype(BF16), v_ref[:, vcols])
        for ch in range(CHUNKS):
            rows = chunk_rows(ch)
            emit(rows, vcols, intra[rows, :] + _dot_nt(qi[rows, :], before[ch]))


def _gla_fwd_kernel(qi_ref, ki_ref, ke_ref, v_ref, dec_ref, o_ref, st_ref):
    @pl.when(pl.program_id(1) == 0)
    def _():
        st_ref[...] = jnp.zeros_like(st_ref)

    def emit(rows, vcols, o):
        o_ref[rows, vcols] = o

    _gla_tile(qi_ref, ki_ref, ke_ref, v_ref, dec_ref, st_ref, emit, reverse=False)


def _gla_fwd(qi, ki, ke, v, dec, nt):
    bsz = qi.shape[0]
    tok = lambda n: pl.BlockSpec((None, TM, n), lambda b, t: (b, t, 0))
    return pl.pallas_call(
        _gla_fwd_kernel,
        grid=(bsz, nt),
        in_specs=[tok(GLA_QK), tok(GLA_QK), tok(GLA_QK), tok(GLA_V),
                  pl.BlockSpec((None, None, CHUNKS, GLA_QK), lambda b, t: (b, t, 0, 0))],
        out_specs=tok(GLA_V),
        out_shape=jax.ShapeDtypeStruct((bsz, nt * TM, GLA_V), F32),
        scratch_shapes=[pltpu.VMEM((GLA_HEADS, GLA_DV, GLA_DK), F32)],
        compiler_params=_params(2),
        name="gla_fwd",
    )(qi, ki, ke, v, dec)


def _gla_bwd_kernel(qi_ref, ki_ref, ke_ref, v_ref, dec_ref, of_ref, sga_ref, y_ref, ctx_ref, x_ref,
                    mod_ref, gg_ref, wo_ref, wy_ref, out_ref, st_ref, ob_ref):
    j = pl.program_id(1)

    @pl.when(j == 0)
    def _():
        st_ref[...] = jnp.zeros_like(st_ref)

    def emit(rows, vcols, o):
        ob_ref[rows, vcols] = o

    _gla_tile(qi_ref, ki_ref, ke_ref, v_ref, dec_ref, st_ref, emit, reverse=True)

    heads = []
    for hd in range(GLA_HEADS):
        vcols = slice(hd * GLA_DV, (hd + 1) * GLA_DV)
        o = of_ref[:, vcols] + ob_ref[:, vcols]
        ms = jnp.mean(o * o, axis=-1, keepdims=True)
        on = o * lax.rsqrt(ms + EPS) * gg_ref[...]
        heads.append((on * sga_ref[:, vcols].astype(F32)).astype(BF16))
    inner = jnp.concatenate(heads, axis=-1)
    proj = _dot(inner, wo_ref[...]) + _dot(y_ref[...], wy_ref[...])
    out_ref[...] = jnp.where(j == 0, ctx_ref[...], x_ref[...]) + mod_ref[2:3, :] * proj


def _bwd_tile(nt):
    return lambda j: jnp.where(j == 0, 0, nt - j)


def _gla_bwd(qi, ki, ke, v, dec, o_f, sga, y, ctx, x, sel, gg, wo, wy, nt):
    bsz = qi.shape[0]
    tile = _bwd_tile(nt)
    tok = lambda n: pl.BlockSpec((None, TM, n), lambda b, j: (b, tile(j), 0))
    return pl.pallas_call(
        _gla_bwd_kernel,
        grid=(bsz, nt),
        in_specs=[tok(GLA_QK), tok(GLA_QK), tok(GLA_QK), tok(GLA_V),
                  pl.BlockSpec((None, None, CHUNKS, GLA_QK), lambda b, j: (b, tile(j), 0, 0)),
                  tok(GLA_V), tok(GLA_V), tok(SC_WIDTH), _ctx_spec(), _latent_spec(tile, nt - 2),
                  pl.BlockSpec((None, None, 3, D_MODEL), lambda b, j: (b, jnp.minimum(j, 1), 0, 0)),
                  _const_spec(gg.shape), _const_spec(wo.shape), _const_spec(wy.shape)],
        out_specs=tok(D_MODEL),
        out_shape=jax.ShapeDtypeStruct((bsz, nt * TM, D_MODEL), F32),
        scratch_shapes=[pltpu.VMEM((GLA_HEADS, GLA_DV, GLA_DK), F32),
                        pltpu.VMEM((TM, GLA_V), F32)],
        compiler_params=_params(2),
        name="gla_bwd",
    )(qi, ki, ke, v, dec, o_f, sga, y, ctx, x, sel, gg, wo, wy)


def _to_segment_major(src_ref, stage_ref, width):
    slabs = width // LANES
    for n in range(slabs):
        for s in range(SUBLANES):
            stage_ref[n, s * SEG_PITCH:s * SEG_PITCH + SEG_LEN, :] = (
                src_ref[s * SEG_LEN:(s + 1) * SEG_LEN, n * LANES:(n + 1) * LANES])
    groups = [
        jnp.concatenate([stage_ref[n, pl.ds(j, SUBLANES, stride=SEG_PITCH), :] for n in range(slabs)], axis=1)
        for j in range(SEG_LEN)]
    return jnp.concatenate(groups, axis=0)


def _from_segment_major(val, stage_ref, dst_ref, width):
    slabs = width // LANES
    for j in range(SEG_LEN):
        for n in range(slabs):
            stage_ref[n, pl.ds(j, SUBLANES, stride=SEG_PITCH), :] = (
                val[j * SUBLANES:(j + 1) * SUBLANES, n * LANES:(n + 1) * LANES])
    for n in range(slabs):
        for s in range(SUBLANES):
            dst_ref[s * SEG_LEN:(s + 1) * SEG_LEN, n * LANES:(n + 1) * LANES] = (
                stage_ref[n, s * SEG_PITCH:s * SEG_PITCH + SEG_LEN, :])


def _rg_conv(xr, halo_ref, cw_ref, cb_ref, cols, reverse):
    row = lax.broadcasted_iota(jnp.int32, (SUBLANES, xr.shape[1]), 0)
    edge = []
    for g in range(RG_CONV - 1):
        grp = slice(g * SUBLANES, (g + 1) * SUBLANES)
        if reverse:
            edge.append(jnp.where(row == SUBLANES - 1, pltpu.roll(halo_ref[grp, cols], SUBLANES - 1, 0),
                                  pltpu.roll(xr[grp, :], SUBLANES - 1, 0)))
        else:
            cur = xr[TM - (RG_CONV - 1 - g) * SUBLANES:TM - (RG_CONV - 2 - g) * SUBLANES, :]
            edge.append(jnp.where(row == 0, pltpu.roll(halo_ref[grp, cols], 1, 0), pltpu.roll(cur, 1, 0)))
    xc = cb_ref[:, cols] + cw_ref[RG_CONV - 1:RG_CONV, cols] * xr
    for k in range(1, RG_CONV):
        if reverse:
            tap = jnp.concatenate([xr[k * SUBLANES:, :]] + edge[:k], axis=0)
        else:
            tap = jnp.concatenate(edge[RG_CONV - 1 - k:] + [xr[:TM - k * SUBLANES, :]], axis=0)
        xc = xc + cw_ref[RG_CONV - 1 - k:RG_CONV - k, cols] * tap
    halo_ref[:, cols] = xr[:HALO_ROWS, :] if reverse else xr[TM - HALO_ROWS:, :]
    return xc


def _rg_setup(lam_ref, ba_ref, bx_ref, rows_ref, copies):
    rows_ref[0:1, :] = (-0.5 * RG_C * LOG2_E) * _softplus(-lam_ref[...])
    rows_ref[1:2, :] = 0.5 * ba_ref[...]
    rows_ref[2:3, :] = 0.5 * bx_ref[...]
    for src_ref, dst_ref in copies:
        dst_ref[...] = src_ref[...]


def _rg_gates_scan(xc, first_block, wg_ref, rows_ref, carry_ref, reverse):
    row = lax.broadcasted_iota(jnp.int32, (SUBLANES, RG_BLOCK_W), 0)
    shift = SUBLANES - 1 if reverse else 1
    steps = range(SEG_LEN - 1, -1, -1) if reverse else range(SEG_LEN)
    out = []
    for i in range(xc.shape[1] // RG_BLOCK_W):
        n = first_block + i
        cols = slice(n * RG_BLOCK_W, (n + 1) * RG_BLOCK_W)
        xb = xc[:, i * RG_BLOCK_W:(i + 1) * RG_BLOCK_W]
        gates = _dot(xb.astype(BF16), wg_ref[n])
        th_r = jnp.tanh(gates[:, :RG_BLOCK_W] + rows_ref[1:2, cols])
        th_i = jnp.tanh(gates[:, RG_BLOCK_W:] + rows_ref[2:3, cols])
        half_decay = rows_ref[0:1, cols]
        a = jnp.exp2(th_r * half_decay + half_decay)
        y = 1.0 - a * a
        root = y * lax.rsqrt(jnp.maximum(y, F32_TINY))
        u = root * ((0.5 * th_i + 0.5) * xb)

        h = jnp.zeros((SUBLANES, RG_BLOCK_W), F32)
        prod = jnp.ones((SUBLANES, RG_BLOCK_W), F32)
        local, prods = {}, {}
        for j in steps:
            grp = slice(j * SUBLANES, (j + 1) * SUBLANES)
            h = a[grp, :] * h + u[grp, :]
            prod = prod * a[grp, :]
            local[j], prods[j] = h, prod

        entering = carry_ref[:, cols]
        for s in range(SUBLANES - 1):
            nxt = pltpu.roll(prod * entering + h, shift, 0)
            target = SUBLANES - 2 - s if reverse else s + 1
            entering = jnp.where(row == target, nxt, entering)
        carry_ref[:, cols] = pltpu.roll(prod * entering + h, shift, 0)

        out.append(jnp.concatenate([local[j] + prods[j] * entering for j in range(SEG_LEN)], axis=0))
    return jnp.concatenate(out, axis=1)


def _odd_fwd_kernel(xx_ref, mod_ref, g_ref, wx_ref, cw_ref, cb_ref, wg_ref, ba_ref, bx_ref, lam_ref,
                    xr_ref, hf_ref, stage_ref, halo_ref, carry_ref, rows_ref, wx_s, wg_s):
    t = pl.program_id(1)

    @pl.when(t == 0)
    def _():
        carry_ref[...] = jnp.zeros_like(carry_ref)
        _rg_setup(lam_ref, ba_ref, bx_ref, rows_ref, ((wx_ref, wx_s), (wg_ref, wg_s)))

    @pl.when(t <= 1)
    def _():
        halo_ref[...] = jnp.zeros_like(halo_ref)

    x = _to_segment_major(xx_ref, stage_ref, D_MODEL)
    hb = _modulated_norm(x, g_ref[...], mod_ref[0:1, :], mod_ref[1:2, :]).astype(BF16)
    chunks = RG_WIDTH // RG_CHUNK_W
    project = lambda m: _dot(hb, wx_s[:, m * RG_CHUNK_W:(m + 1) * RG_CHUNK_W])
    xr_next = project(0)
    for m in range(chunks):
        cols = slice(m * RG_CHUNK_W, (m + 1) * RG_CHUNK_W)
        xr = xr_next
        if m + 1 < chunks:
            xr_next = project(m + 1)
        xr_ref[:, cols] = xr
        xc = _rg_conv(xr, halo_ref, cw_ref, cb_ref, cols, reverse=False)
        hf_ref[:, cols] = _rg_gates_scan(xc, m * (RG_CHUNK_W // RG_BLOCK_W), wg_s, rows_ref, carry_ref,
                                         reverse=False)


def _odd_fwd(xx, sel, g, wx, cw, cb, wg, ba, bx, lam, nt):
    bsz = xx.shape[0]
    tok = lambda n: pl.BlockSpec((None, TM, n), lambda b, t: (b, t, 0))
    consts = (g, wx, cw, cb, wg, ba, bx, lam)
    wide = jax.ShapeDtypeStruct((bsz, nt * TM, RG_WIDTH), F32)
    return pl.pallas_call(
        _odd_fwd_kernel,
        grid=(bsz, nt),
        in_specs=[tok(D_MODEL),
                  pl.BlockSpec((None, None, 3, D_MODEL), lambda b, t: (b, jnp.minimum(t, 1), 0, 0))]
        + [_const_spec(w.shape) for w in consts],
        out_specs=[tok(RG_WIDTH), tok(RG_WIDTH)],
        out_shape=[wide, wide],
        scratch_shapes=[pltpu.VMEM((D_MODEL // LANES, SUBLANES * SEG_PITCH, LANES), F32),
                        pltpu.VMEM((HALO_ROWS, RG_WIDTH), F32),
                        pltpu.VMEM((SUBLANES, RG_WIDTH), F32),
                        pltpu.VMEM((3, RG_WIDTH), F32),
                        pltpu.VMEM(wx.shape, BF16),
                        pltpu.VMEM(wg.shape, BF16)],
        compiler_params=_params(2),
        name="odd_fwd",
    )(xx, sel, *consts)


def _odd_bwd_kernel(xx_ref, mod_ref, g_ref, xr_ref, hf_ref, wgate_ref, cw_ref, cb_ref, wg_ref, ba_ref,
                    bx_ref, lam_ref, wout_ref, fg_ref, out_ref,
                    stage_ref, halo_ref, yy_ref, carry_ref, rows_ref, wgate_s, wg_s, wout_s):
    j = pl.program_id(1)

    @pl.when(j == 0)
    def _():
        carry_ref[...] = jnp.zeros_like(carry_ref)
        _rg_setup(lam_ref, ba_ref, bx_ref, rows_ref,
                  ((wgate_ref, wgate_s), (wg_ref, wg_s), (wout_ref, wout_s)))

    @pl.when(j <= 1)
    def _():
        halo_ref[...] = jnp.zeros_like(halo_ref)

    x = _to_segment_major(xx_ref, stage_ref, D_MODEL)
    hb = _modulated_norm(x, g_ref[...], mod_ref[0:1, :], mod_ref[1:2, :]).astype(BF16)
    for m in range(RG_WIDTH // RG_CHUNK_W):
        cols = slice(m * RG_CHUNK_W, (m + 1) * RG_CHUNK_W)
        gate = _dot(hb, wgate_s[:, cols])
        xc = _rg_conv(xr_ref[:, cols], halo_ref, cw_ref, cb_ref, cols, reverse=True)
        h_bwd = _rg_gates_scan(xc, m * (RG_CHUNK_W // RG_BLOCK_W), wg_s, rows_ref, carry_ref, reverse=True)
        yy_ref[:, cols] = ((hf_ref[:, cols] + h_bwd) * _silu(gate)).astype(BF16)
    xn = x + mod_ref[2:3, :] * _dot(yy_ref[...], wout_s[...])
    ms = jnp.mean(xn * xn, axis=-1, keepdims=True)
    _from_segment_major(xn * lax.rsqrt(ms + EPS) * fg_ref[...], stage_ref, out_ref, D_MODEL)


def _odd_bwd(xx, sel, g, xr, hf, wgate, cw, cb, wg, ba, bx, lam, wout, fg, nt):
    bsz = xx.shape[0]
    tile = _bwd_tile(nt)
    tok = lambda n: pl.BlockSpec((None, TM, n), lambda b, j: (b, tile(j), 0))
    consts = (wgate, cw, cb, wg, ba, bx, lam, wout, fg)
    out_block = lambda b, j: (b, jnp.where(j == 0, nt - 2, nt - 1 - j), 0)
    return pl.pallas_call(
        _odd_bwd_kernel,
        grid=(bsz, nt),
        in_specs=[tok(D_MODEL),
                  pl.BlockSpec((None, None, 3, D_MODEL), lambda b, j: (b, jnp.minimum(j, 1), 0, 0)),
                  _const_spec(g.shape), tok(RG_WIDTH), tok(RG_WIDTH)]
        + [_const_spec(w.shape) for w in consts],
        out_specs=pl.BlockSpec((None, TM, D_MODEL), out_block),
        out_shape=jax.ShapeDtypeStruct((bsz, (nt - 1) * TM, D_MODEL), F32),
        scratch_shapes=[pltpu.VMEM((D_MODEL // LANES, SUBLANES * SEG_PITCH, LANES), F32),
                        pltpu.VMEM((HALO_ROWS, RG_WIDTH), F32),
                        pltpu.VMEM((TM, RG_WIDTH), BF16),
                        pltpu.VMEM((SUBLANES, RG_WIDTH), F32),
                        pltpu.VMEM((3, RG_WIDTH), F32),
                        pltpu.VMEM(wgate.shape, BF16),
                        pltpu.VMEM(wg.shape, BF16),
                        pltpu.VMEM(wout.shape, BF16)],
        compiler_params=_params(2),
        name="odd_bwd",
    )(xx, sel, g, xr, hf, *consts)


def _select_mods(m, bsz):
    per = m[:bsz].reshape(bsz, 1, 3, D_MODEL)
    ctx = jnp.broadcast_to(m[bsz].reshape(1, 1, 3, D_MODEL), (bsz, 1, 3, D_MODEL))
    return jnp.concatenate([ctx, per], axis=1)


def _gate_weights(w_a, w_x):
    return (0.5 * jnp.concatenate([w_a, w_x], axis=-1)).astype(BF16)


def kernel(x, c, ctx, c_ctx, norm_g, w_mod, b_mod, e_w_in, e_w_a2, e_b_a2, e_gla_g, e_conv_w, e_w_out,
           o_w_in, o_conv_w, o_conv_b, o_w_a, o_b_a, o_w_x, o_b_x, o_lam, o_w_out, final_g):
    bsz, seq, _ = x.shape
    assert ctx.shape[1] == TM and seq % TM == 0 and bsz < SUBLANES
    assert w_mod.shape[0] == 2
    nt = 1 + seq // TM

    rows = jnp.concatenate([c, c_ctx[None, :], jnp.zeros((SUBLANES - bsz - 1, D_MODEL), F32)], axis=0)
    mods = _mods(rows, w_mod, b_mod)
    sel0 = _select_mods(mods[0], bsz)
    sel1 = _select_mods(mods[1], bsz)

    w_in = e_w_in[0]
    offs = {}
    off = 0
    for name, size in (("q", GLA_QK), ("k", GLA_QK), ("v", GLA_V), ("ga", GLA_V), ("af", GLA_LOWRANK),
                       ("ab", GLA_LOWRANK), ("cb", SC_WIDTH), ("cc", SC_WIDTH), ("cx", SC_WIDTH),
                       ("gb", SC_WIDTH)):
        offs[name] = (off, off + size)
        off += size
    cols = lambda lo, hi: w_in[:, lo:hi].astype(BF16)
    wqk = cols(offs["q"][0], offs["k"][1])
    wv = cols(*offs["v"])
    wga = cols(*offs["ga"])
    wa = jnp.pad(w_in[:, offs["af"][0]:offs["ab"][1]], ((0, 0), (0, LANES - 2 * GLA_LOWRANK))).astype(BF16)
    w2 = jnp.zeros((LANES, 2 * GLA_QK), F32)
    w2 = w2.at[:GLA_LOWRANK, :GLA_QK].set(e_w_a2[0, 0])
    w2 = w2.at[GLA_LOWRANK:2 * GLA_LOWRANK, GLA_QK:].set(e_w_a2[0, 1]).astype(BF16)
    b2 = e_b_a2[0].reshape(1, 2 * GLA_QK)
    mats = (wqk, wv, wga, wa, w2, cols(*offs["cb"]), cols(*offs["cc"]), cols(*offs["cx"]), cols(*offs["gb"]))
    (qi_f, ki_f, ke_f, dec_f, qi_b, ki_b, ke_b, dec_b, v, sga, y) = _even_in(
        ctx, x, sel0, (norm_g[0:1], b2, e_conv_w[0]), mats, nt)
    o_f = _gla_fwd(qi_f, ki_f, ke_f, v, dec_f, nt)
    w_out = e_w_out[0].astype(BF16)
    xx = _gla_bwd(qi_b, ki_b, ke_b, v, dec_b, o_f, sga, y, ctx, x, sel0, e_gla_g[0:1],
                  w_out[:GLA_V], w_out[GLA_V:], nt)

    w_in = o_w_in[0]
    wide = lambda a: a.reshape(1, RG_WIDTH)
    xr, h_f = _odd_fwd(xx, sel1, norm_g[1:2], w_in[:, :RG_WIDTH].astype(BF16), o_conv_w[0, 0],
                       wide(o_conv_b[0, 0]), _gate_weights(o_w_a[0, 0], o_w_x[0, 0]),
                       wide(o_b_a[0, 0]), wide(o_b_x[0, 0]), wide(o_lam[0, 0]), nt)
    return _odd_bwd(xx, sel1, norm_g[1:2], xr, h_f, w_in[:, RG_WIDTH:].astype(BF16), o_conv_w[0, 1],
                    wide(o_conv_b[0, 1]), _gate_weights(o_w_a[0, 1], o_w_x[0, 1]),
                    wide(o_b_a[0, 1]), wide(o_b_x[0, 1]), wide(o_lam[0, 1]), o_w_out[0].astype(BF16),
                    final_g.reshape(1, D_MODEL), nt)
```

```python
import functools

import jax
import jax.numpy as jnp
from jax import lax
from jax.experimental import pallas as pl
from jax.experimental.pallas import tpu as pltpu

F32 = jnp.float32
BF16 = jnp.bfloat16

D_MODEL = 1024
EPS = 1e-6
GLA_HEADS = 4
GLA_DK = 128
GLA_DV = 256
GLA_QK = GLA_HEADS * GLA_DK
GLA_V = GLA_HEADS * GLA_DV
GLA_LOWRANK = 16
GLA_TAU = 16.0
GLA_CHUNK = 64
GRID_W = 64
SC_WIDTH = D_MODEL
RG_WIDTH = 2 * D_MODEL
RG_BLOCKS = 16
RG_BLOCK_W = 128
RG_C = 8.0
RG_CONV = 4

TM = 256
CHUNKS = TM // GLA_CHUNK
SUBLANES = 8
LANES = 128
VMEM_LIMIT = 56 * 1024 * 1024
SEG_LEN = TM // SUBLANES
SEG_PITCH = SEG_LEN + SUBLANES
HALO_ROWS = (RG_CONV - 1) * SUBLANES
RG_CHUNK_W = 256
LOG2_E = 1.4426950408889634
F32_TINY = 1.1754943508222875e-38


def _silu(x):
    return x * jax.nn.sigmoid(x)


def _log_sigmoid(z):
    return jnp.minimum(z, 0.0) - jnp.log(1.0 + jnp.exp(-jnp.abs(z)))


def _softplus(z):
    return jnp.maximum(z, 0.0) + jnp.log1p(jnp.exp(-jnp.abs(z)))


def _dot(a, b):
    return jnp.dot(a, b, preferred_element_type=F32)


def _dot_nt(a, b):
    return lax.dot_general(a, b, (((1,), (1,)), ((), ())), preferred_element_type=F32)


def _dot_tn(a, b):
    return lax.dot_general(a, b, (((0,), (0,)), ((), ())), preferred_element_type=F32)


def _modulated_norm(x, g, shift, scale):
    ms = jnp.mean(x * x, axis=-1, keepdims=True)
    return (x * lax.rsqrt(ms + EPS) * g) * (1.0 + scale) + shift


def _params(n_axes):
    return pltpu.CompilerParams(
        dimension_semantics=("arbitrary",) * n_axes,
        vmem_limit_bytes=VMEM_LIMIT)


def _const_spec(shape):
    zeros = (0,) * len(shape)
    return pl.BlockSpec(shape, lambda *_: zeros, pipeline_mode=pl.Buffered(1))


def _mods_kernel(s_ref, w_ref, b_ref, o_ref):
    s = _silu(s_ref[...])
    o_ref[0] = jnp.dot(s, w_ref[0], preferred_element_type=F32,
                       precision=lax.Precision.HIGHEST) + b_ref[0]


def _mods(rows, w_mod, b_mod):
    depth = w_mod.shape[0]
    nb = 3
    return pl.pallas_call(
        _mods_kernel,
        grid=(depth, nb),
        in_specs=[
            pl.BlockSpec((SUBLANES, D_MODEL), lambda l, j: (0, 0)),
            pl.BlockSpec((1, D_MODEL, D_MODEL), lambda l, j: (l, 0, j)),
            pl.BlockSpec((1, 1, D_MODEL), lambda l, j: (l, 0, j)),
        ],
        out_specs=pl.BlockSpec((1, SUBLANES, D_MODEL), lambda l, j: (l, 0, j)),
        out_shape=jax.ShapeDtypeStruct((depth, SUBLANES, 3 * D_MODEL), F32),
        compiler_params=_params(2),
        name="mods",
    )(rows, w_mod, b_mod.reshape(depth, 1, 3 * D_MODEL))


def _split_dot(mat, x):
    hi = x.astype(BF16)
    lo = (x - hi.astype(F32)).astype(BF16)
    return _dot(mat, hi) + _dot(mat, lo)


def _chunk_rows(x, offset):
    return jnp.concatenate(
        [jnp.broadcast_to(x[c * GLA_CHUNK + offset:c * GLA_CHUNK + offset + 1, :], (GLA_CHUNK, x.shape[1]))
         for c in range(CHUNKS)], axis=0)


def _even_in_kernel(ctx_ref, x_ref, mod_ref, g_ref, b2_ref, cw_ref,
                    lower_in, upper_in, wqk_in, wv_in, wga_in, wa_in, w2_in, wcb_in, wcc_in, wcx_in, wgb_in,
                    qif_ref, kif_ref, kef_ref, decf_ref, qib_ref, kib_ref, keb_ref, decb_ref,
                    v_ref, sga_ref, y_ref,
                    lower_ref, upper_ref, wqk_ref, wv_ref, wga_ref, wa_ref, w2_ref, wcb_ref, wcc_ref, wcx_ref,
                    wgb_ref):
    t = pl.program_id(1)

    @pl.when(t == 0)
    def _():
        for src, dst in ((lower_in, lower_ref), (upper_in, upper_ref), (wqk_in, wqk_ref), (wv_in, wv_ref),
                         (wga_in, wga_ref), (wa_in, wa_ref), (w2_in, w2_ref), (wcb_in, wcb_ref),
                         (wcc_in, wcc_ref), (wcx_in, wcx_ref), (wgb_in, wgb_ref)):
            dst[...] = src[...]

    xt = jnp.where(t == 0, ctx_ref[...], x_ref[...])
    h = _modulated_norm(xt, g_ref[...], mod_ref[0:1, :], mod_ref[1:2, :])
    hb = h.astype(BF16)

    a_lr = _dot(hb, wa_ref[...])
    z = _dot(a_lr.astype(BF16), w2_ref[...]) + b2_ref[...]
    qk = _dot(hb, wqk_ref[...])
    v_ref[...] = _dot(hb, wv_ref[...]).astype(BF16)

    lg = _log_sigmoid(z) * (1.0 / GLA_TAU)
    cs_f = _split_dot(lower_ref[...], lg[:, :GLA_QK])
    cs_b = _split_dot(upper_ref[...], lg[:, GLA_QK:])
    tot_f = _chunk_rows(cs_f, GLA_CHUNK - 1)
    tot_b = _chunk_rows(cs_b, 0)

    sga_ref[...] = _silu(_dot(hb, wga_ref[...])).astype(BF16)

    q = qk[:, :GLA_QK] * (GLA_DK ** -0.5)
    k = qk[:, GLA_QK:]
    qif_ref[...] = (q * jnp.exp(cs_f)).astype(BF16)
    kif_ref[...] = (k * jnp.exp(-cs_f)).astype(BF16)
    kef_ref[...] = (k * jnp.exp(tot_f - cs_f)).astype(BF16)
    qib_ref[...] = (q * jnp.exp(cs_b)).astype(BF16)
    kib_ref[...] = (k * jnp.exp(-cs_b)).astype(BF16)
    keb_ref[...] = (k * jnp.exp(tot_b - cs_b)).astype(BF16)
    for c in range(CHUNKS):
        r = c * GLA_CHUNK
        decf_ref[c:c + 1, :] = jnp.exp(tot_f[r:r + 1, :])
        decb_ref[c:c + 1, :] = jnp.exp(tot_b[r:r + 1, :])

    zz = _dot(hb, wcc_ref[...]) * _dot(hb, wcx_ref[...])
    pos = lax.broadcasted_iota(jnp.int32, (TM, SC_WIDTH), 0)
    row_len = jnp.where(t == 0, TM, GRID_W)
    in_row = pos & (row_len - 1)
    z_prev = jnp.where(in_row != 0, pltpu.roll(zz, 1, 0), 0.0)
    z_next = jnp.where(in_row != row_len - 1, pltpu.roll(zz, TM - 1, 0), 0.0)
    zc = cw_ref[0:1, :] * z_prev + cw_ref[1:2, :] * zz + cw_ref[2:3, :] * z_next
    y = _dot(hb, wcb_ref[...]) * zc * _silu(_dot(hb, wgb_ref[...]))
    y_ref[...] = y.astype(BF16)


def _chunk_triangles():
    r = lax.broadcasted_iota(jnp.int32, (TM, TM), 0)
    c = lax.broadcasted_iota(jnp.int32, (TM, TM), 1)
    same = (r // GLA_CHUNK) == (c // GLA_CHUNK)
    return (same & (c <= r)).astype(BF16), (same & (c >= r)).astype(BF16)


def _ctx_spec():
    return pl.BlockSpec((None, TM, D_MODEL), lambda b, t: (b, 0, 0))


def _latent_spec(tile, park):
    return pl.BlockSpec((None, TM, D_MODEL),
                        lambda b, t: (b, jnp.where(tile(t) == 0, park, tile(t) - 1), 0))


def _even_in(ctx, x, sel, rows, mats, nt):
    bsz = x.shape[0]
    tok = lambda n: pl.BlockSpec((None, TM, n), lambda b, t: (b, t, 0))
    dec = pl.BlockSpec((None, None, CHUNKS, GLA_QK), lambda b, t: (b, t, 0, 0))
    tok_shape = lambda n: jax.ShapeDtypeStruct((bsz, nt * TM, n), BF16)
    dec_shape = jax.ShapeDtypeStruct((bsz, nt, CHUNKS, GLA_QK), F32)
    mats = _chunk_triangles() + tuple(mats)
    consts = tuple(rows) + mats
    return pl.pallas_call(
        _even_in_kernel,
        grid=(bsz, nt),
        in_specs=[
            _ctx_spec(), _latent_spec(lambda t: t, 0),
            pl.BlockSpec((None, None, 3, D_MODEL), lambda b, t: (b, jnp.minimum(t, 1), 0, 0)),
        ] + [_const_spec(w.shape) for w in consts],
        out_specs=[tok(GLA_QK), tok(GLA_QK), tok(GLA_QK), dec,
                   tok(GLA_QK), tok(GLA_QK), tok(GLA_QK), dec,
                   tok(GLA_V), tok(GLA_V), tok(SC_WIDTH)],
        out_shape=[tok_shape(GLA_QK), tok_shape(GLA_QK), tok_shape(GLA_QK), dec_shape,
                   tok_shape(GLA_QK), tok_shape(GLA_QK), tok_shape(GLA_QK), dec_shape,
                   tok_shape(GLA_V), tok_shape(GLA_V), tok_shape(SC_WIDTH)],
        scratch_shapes=[pltpu.VMEM(m.shape, BF16) for m in mats],
        compiler_params=_params(2),
        name="even_in",
    )(ctx, x, sel, *consts)


def _gla_tile(qi_ref, ki_ref, ke_ref, v_ref, dec_ref, st_ref, emit, reverse):
    r = lax.broadcasted_iota(jnp.int32, (TM, TM), 0)
    c = lax.broadcasted_iota(jnp.int32, (TM, TM), 1)
    same_chunk = (r // GLA_CHUNK) == (c // GLA_CHUNK)
    mask = same_chunk & ((c >= r) if reverse else (c <= r))
    order = range(CHUNKS - 1, -1, -1) if reverse else range(CHUNKS)
    chunk_rows = lambda ch: slice(ch * GLA_CHUNK, (ch + 1) * GLA_CHUNK)
    for hd in range(GLA_HEADS):
        kcols = slice(hd * GLA_DK, (hd + 1) * GLA_DK)
        vcols = slice(hd * GLA_DV, (hd + 1) * GLA_DV)
        incr = [_dot_tn(v_ref[chunk_rows(ch), vcols], ke_ref[chunk_rows(ch), kcols]) for ch in range(CHUNKS)]
        state = st_ref[hd]
        before = {}
        for ch in order:
            before[ch] = state.astype(BF16)
            state = dec_ref[ch:ch + 1, kcols] * state + incr[ch]
        st_ref[hd] = state
        qi = qi_ref[:, kcols]
        scores = jnp.where(mask, _dot_nt(qi, ki_ref[:, kcols]), 0.0)
        intra = _dot(scores.astype(BF16), v_ref[:, vcols])
        for ch in range(CHUNKS):
            rows = chunk_rows(ch)
            emit(rows, vcols, intra[rows, :] + _dot_nt(qi[rows, :], before[ch]))


def _gla_fwd_kernel(qi_ref, ki_ref, ke_ref, v_ref, dec_ref, o_ref, st_ref):
    @pl.when(pl.program_id(1) == 0)
    def _():
        st_ref[...] = jnp.zeros_like(st_ref)

    def emit(rows, vcols, o):
        o_ref[rows, vcols] = o.astype(BF16)

    _gla_tile(qi_ref, ki_ref, ke_ref, v_ref, dec_ref, st_ref, emit, reverse=False)


def _gla_fwd(qi, ki, ke, v, dec, nt):
    bsz = qi.shape[0]
    tok = lambda n: pl.BlockSpec((None, TM, n), lambda b, t: (b, t, 0))
    return pl.pallas_call(
        _gla_fwd_kernel,
        grid=(bsz, nt),
        in_specs=[tok(GLA_QK), tok(GLA_QK), tok(GLA_QK), tok(GLA_V),
                  pl.BlockSpec((None, None, CHUNKS, GLA_QK), lambda b, t: (b, t, 0, 0))],
        out_specs=tok(GLA_V),
        out_shape=jax.ShapeDtypeStruct((bsz, nt * TM, GLA_V), BF16),
        scratch_shapes=[pltpu.VMEM((GLA_HEADS, GLA_DV, GLA_DK), F32)],
        compiler_params=_params(2),
        name="gla_fwd",
    )(qi, ki, ke, v, dec)


def _gla_bwd_kernel(qi_ref, ki_ref, ke_ref, v_ref, dec_ref, of_ref, sga_ref, y_ref, ctx_ref, x_ref,
                    mod_ref, gg_ref, wo_ref, wy_ref, out_ref, st_ref, ob_ref):
    j = pl.program_id(1)

    @pl.when(j == 0)
    def _():
        st_ref[...] = jnp.zeros_like(st_ref)

    def emit(rows, vcols, o):
        ob_ref[rows, vcols] = o

    _gla_tile(qi_ref, ki_ref, ke_ref, v_ref, dec_ref, st_ref, emit, reverse=True)

    heads = []
    for hd in range(GLA_HEADS):
        vcols = slice(hd * GLA_DV, (hd + 1) * GLA_DV)
        o = of_ref[:, vcols].astype(F32) + ob_ref[:, vcols]
        ms = jnp.mean(o * o, axis=-1, keepdims=True)
        on = o * lax.rsqrt(ms + EPS) * gg_ref[...]
        heads.append((on * sga_ref[:, vcols].astype(F32)).astype(BF16))
    inner = jnp.concatenate(heads, axis=-1)
    proj = _dot(inner, wo_ref[...]) + _dot(y_ref[...], wy_ref[...])
    out_ref[...] = jnp.where(j == 0, ctx_ref[...], x_ref[...]) + mod_ref[2:3, :] * proj


def _bwd_tile(nt):
    return lambda j: jnp.where(j == 0, 0, nt - j)


def _gla_bwd(qi, ki, ke, v, dec, o_f, sga, y, ctx, x, sel, gg, wo, wy, nt):
    bsz = qi.shape[0]
    tile = _bwd_tile(nt)
    tok = lambda n: pl.BlockSpec((None, TM, n), lambda b, j: (b, tile(j), 0))
    return pl.pallas_call(
        _gla_bwd_kernel,
        grid=(bsz, nt),
        in_specs=[tok(GLA_QK), tok(GLA_QK), tok(GLA_QK), tok(GLA_V),
                  pl.BlockSpec((None, None, CHUNKS, GLA_QK), lambda b, j: (b, tile(j), 0, 0)),
                  tok(GLA_V), tok(GLA_V), tok(SC_WIDTH), _ctx_spec(), _latent_spec(tile, nt - 2),
                  pl.BlockSpec((None, None, 3, D_MODEL), lambda b, j: (b, jnp.minimum(j, 1), 0, 0)),
                  _const_spec(gg.shape), _const_spec(wo.shape), _const_spec(wy.shape)],
        out_specs=tok(D_MODEL),
        out_shape=jax.ShapeDtypeStruct((bsz, nt * TM, D_MODEL), F32),
        scratch_shapes=[pltpu.VMEM((GLA_HEADS, GLA_DV, GLA_DK), F32),
                        pltpu.VMEM((TM, GLA_V), F32)],
        compiler_params=_params(2),
        name="gla_bwd",
    )(qi, ki, ke, v, dec, o_f, sga, y, ctx, x, sel, gg, wo, wy)


def _to_segment_major(src_ref, stage_ref, width):
    slabs = width // LANES
    for n in range(slabs):
        for s in range(SUBLANES):
            stage_ref[n, s * SEG_PITCH:s * SEG_PITCH + SEG_LEN, :] = (
                src_ref[s * SEG_LEN:(s + 1) * SEG_LEN, n * LANES:(n + 1) * LANES])
    groups = [
        jnp.concatenate([stage_ref[n, pl.ds(j, SUBLANES, stride=SEG_PITCH), :] for n in range(slabs)], axis=1)
        for j in range(SEG_LEN)]
    return jnp.concatenate(groups, axis=0)


def _from_segment_major(val, stage_ref, dst_ref, width):
    slabs = width // LANES
    for j in range(SEG_LEN):
        for n in range(slabs):
            stage_ref[n, pl.ds(j, SUBLANES, stride=SEG_PITCH), :] = (
                val[j * SUBLANES:(j + 1) * SUBLANES, n * LANES:(n + 1) * LANES])
    for n in range(slabs):
        for s in range(SUBLANES):
            dst_ref[s * SEG_LEN:(s + 1) * SEG_LEN, n * LANES:(n + 1) * LANES] = (
                stage_ref[n, s * SEG_PITCH:s * SEG_PITCH + SEG_LEN, :])


def _rg_conv(xr, halo_ref, cw_ref, cb_ref, cols, reverse):
    row = lax.broadcasted_iota(jnp.int32, (SUBLANES, xr.shape[1]), 0)
    edge = []
    for g in range(RG_CONV - 1):
        grp = slice(g * SUBLANES, (g + 1) * SUBLANES)
        if reverse:
            edge.append(jnp.where(row == SUBLANES - 1, pltpu.roll(halo_ref[grp, cols], SUBLANES - 1, 0),
                                  pltpu.roll(xr[grp, :], SUBLANES - 1, 0)))
        else:
            cur = xr[TM - (RG_CONV - 1 - g) * SUBLANES:TM - (RG_CONV - 2 - g) * SUBLANES, :]
            edge.append(jnp.where(row == 0, pltpu.roll(halo_ref[grp, cols], 1, 0), pltpu.roll(cur, 1, 0)))
    xc = cb_ref[:, cols] + cw_ref[RG_CONV - 1:RG_CONV, cols] * xr
    for k in range(1, RG_CONV):
        if reverse:
            tap = jnp.concatenate([xr[k * SUBLANES:, :]] + edge[:k], axis=0)
        else:
            tap = jnp.concatenate(edge[RG_CONV - 1 - k:] + [xr[:TM - k * SUBLANES, :]], axis=0)
        xc = xc + cw_ref[RG_CONV - 1 - k:RG_CONV - k, cols] * tap
    halo_ref[:, cols] = xr[:HALO_ROWS, :] if reverse else xr[TM - HALO_ROWS:, :]
    return xc


def _rg_setup(lam_ref, ba_ref, bx_ref, rows_ref, copies):
    rows_ref[0:1, :] = (-0.5 * RG_C * LOG2_E) * _softplus(-lam_ref[...])
    rows_ref[1:2, :] = 0.5 * ba_ref[...]
    rows_ref[2:3, :] = 0.5 * bx_ref[...]
    for src_ref, dst_ref in copies:
        dst_ref[...] = src_ref[...]


def _rg_gates_scan(xc, first_block, wg_ref, rows_ref, carry_ref, reverse):
    row = lax.broadcasted_iota(jnp.int32, (SUBLANES, RG_BLOCK_W), 0)
    shift = SUBLANES - 1 if reverse else 1
    steps = range(SEG_LEN - 1, -1, -1) if reverse else range(SEG_LEN)
    out = []
    for i in range(xc.shape[1] // RG_BLOCK_W):
        n = first_block + i
        cols = slice(n * RG_BLOCK_W, (n + 1) * RG_BLOCK_W)
        xb = xc[:, i * RG_BLOCK_W:(i + 1) * RG_BLOCK_W]
        gates = _dot(xb.astype(BF16), wg_ref[n])
        th_r = jnp.tanh(gates[:, :RG_BLOCK_W] + rows_ref[1:2, cols])
        th_i = jnp.tanh(gates[:, RG_BLOCK_W:] + rows_ref[2:3, cols])
        half_decay = rows_ref[0:1, cols]
        a = jnp.exp2(th_r * half_decay + half_decay)
        y = 1.0 - a * a
        root = y * lax.rsqrt(jnp.maximum(y, F32_TINY))
        u = root * ((0.5 * th_i + 0.5) * xb)

        h = jnp.zeros((SUBLANES, RG_BLOCK_W), F32)
        prod = jnp.ones((SUBLANES, RG_BLOCK_W), F32)
        local, prods = {}, {}
        for j in steps:
            grp = slice(j * SUBLANES, (j + 1) * SUBLANES)
            h = a[grp, :] * h + u[grp, :]
            prod = prod * a[grp, :]
            local[j], prods[j] = h, prod

        entering = carry_ref[:, cols]
        for s in range(SUBLANES - 1):
            nxt = pltpu.roll(prod * entering + h, shift, 0)
            target = SUBLANES - 2 - s if reverse else s + 1
            entering = jnp.where(row == target, nxt, entering)
        carry_ref[:, cols] = pltpu.roll(prod * entering + h, shift, 0)

        out.append(jnp.concatenate([local[j] + prods[j] * entering for j in range(SEG_LEN)], axis=0))
    return jnp.concatenate(out, axis=1)


def _odd_fwd_kernel(xx_ref, mod_ref, g_ref, wx_ref, cw_ref, cb_ref, wg_ref, ba_ref, bx_ref, lam_ref,
                    xr_ref, hf_ref, stage_ref, halo_ref, carry_ref, rows_ref, wx_s, wg_s):
    t = pl.program_id(1)

    @pl.when(t == 0)
    def _():
        carry_ref[...] = jnp.zeros_like(carry_ref)
        _rg_setup(lam_ref, ba_ref, bx_ref, rows_ref, ((wx_ref, wx_s), (wg_ref, wg_s)))

    @pl.when(t <= 1)
    def _():
        halo_ref[...] = jnp.zeros_like(halo_ref)

    x = _to_segment_major(xx_ref, stage_ref, D_MODEL)
    hb = _modulated_norm(x, g_ref[...], mod_ref[0:1, :], mod_ref[1:2, :]).astype(BF16)
    chunks = RG_WIDTH // RG_CHUNK_W
    project = lambda m: _dot(hb, wx_s[:, m * RG_CHUNK_W:(m + 1) * RG_CHUNK_W])
    xr_next = project(0)
    for m in range(chunks):
        cols = slice(m * RG_CHUNK_W, (m + 1) * RG_CHUNK_W)
        xr = xr_next
        if m + 1 < chunks:
            xr_next = project(m + 1)
        xr_ref[:, cols] = xr.astype(BF16)
        xc = _rg_conv(xr, halo_ref, cw_ref, cb_ref, cols, reverse=False)
        hf_ref[:, cols] = _rg_gates_scan(xc, m * (RG_CHUNK_W // RG_BLOCK_W), wg_s, rows_ref, carry_ref,
                                         reverse=False).astype(BF16)


def _odd_fwd(xx, sel, g, wx, cw, cb, wg, ba, bx, lam, nt):
    bsz = xx.shape[0]
    tok = lambda n: pl.BlockSpec((None, TM, n), lambda b, t: (b, t, 0))
    consts = (g, wx, cw, cb, wg, ba, bx, lam)
    wide = jax.ShapeDtypeStruct((bsz, nt * TM, RG_WIDTH), BF16)
    return pl.pallas_call(
        _odd_fwd_kernel,
        grid=(bsz, nt),
        in_specs=[tok(D_MODEL),
                  pl.BlockSpec((None, None, 3, D_MODEL), lambda b, t: (b, jnp.minimum(t, 1), 0, 0))]
        + [_const_spec(w.shape) for w in consts],
        out_specs=[tok(RG_WIDTH), tok(RG_WIDTH)],
        out_shape=[wide, wide],
        scratch_shapes=[pltpu.VMEM((D_MODEL // LANES, SUBLANES * SEG_PITCH, LANES), F32),
                        pltpu.VMEM((HALO_ROWS, RG_WIDTH), F32),
                        pltpu.VMEM((SUBLANES, RG_WIDTH), F32),
                        pltpu.VMEM((3, RG_WIDTH), F32),
                        pltpu.VMEM(wx.shape, BF16),
                        pltpu.VMEM(wg.shape, BF16)],
        compiler_params=_params(2),
        name="odd_fwd",
    )(xx, sel, *consts)


def _odd_bwd_kernel(xx_ref, mod_ref, g_ref, xr_ref, hf_ref, wgate_ref, cw_ref, cb_ref, wg_ref, ba_ref,
                    bx_ref, lam_ref, wout_ref, fg_ref, out_ref,
                    stage_ref, halo_ref, yy_ref, carry_ref, rows_ref, wgate_s, wg_s, wout_s):
    j = pl.program_id(1)

    @pl.when(j == 0)
    def _():
        carry_ref[...] = jnp.zeros_like(carry_ref)
        _rg_setup(lam_ref, ba_ref, bx_ref, rows_ref,
                  ((wgate_ref, wgate_s), (wg_ref, wg_s), (wout_ref, wout_s)))

    @pl.when(j <= 1)
    def _():
        halo_ref[...] = jnp.zeros_like(halo_ref)

    x = _to_segment_major(xx_ref, stage_ref, D_MODEL)
    hb = _modulated_norm(x, g_ref[...], mod_ref[0:1, :], mod_ref[1:2, :]).astype(BF16)
    for m in range(RG_WIDTH // RG_CHUNK_W):
        cols = slice(m * RG_CHUNK_W, (m + 1) * RG_CHUNK_W)
        gate = _dot(hb, wgate_s[:, cols])
        xc = _rg_conv(xr_ref[:, cols].astype(F32), halo_ref, cw_ref, cb_ref, cols, reverse=True)
        h_bwd = _rg_gates_scan(xc, m * (RG_CHUNK_W // RG_BLOCK_W), wg_s, rows_ref, carry_ref, reverse=True)
        yy_ref[:, cols] = ((hf_ref[:, cols].astype(F32) + h_bwd) * _silu(gate)).astype(BF16)
    xn = x + mod_ref[2:3, :] * _dot(yy_ref[...], wout_s[...])
    ms = jnp.mean(xn * xn, axis=-1, keepdims=True)
    _from_segment_major(xn * lax.rsqrt(ms + EPS) * fg_ref[...], stage_ref, out_ref, D_MODEL)


def _odd_bwd(xx, sel, g, xr, hf, wgate, cw, cb, wg, ba, bx, lam, wout, fg, nt):
    bsz = xx.shape[0]
    tile = _bwd_tile(nt)
    tok = lambda n: pl.BlockSpec((None, TM, n), lambda b, j: (b, tile(j), 0))
    consts = (wgate, cw, cb, wg, ba, bx, lam, wout, fg)
    out_block = lambda b, j: (b, jnp.where(j == 0, nt - 2, nt - 1 - j), 0)
    return pl.pallas_call(
        _odd_bwd_kernel,
        grid=(bsz, nt),
        in_specs=[tok(D_MODEL),
                  pl.BlockSpec((None, None, 3, D_MODEL), lambda b, j: (b, jnp.minimum(j, 1), 0, 0)),
                  _const_spec(g.shape), tok(RG_WIDTH), tok(RG_WIDTH)]
        + [_const_spec(w.shape) for w in consts],
        out_specs=pl.BlockSpec((None, TM, D_MODEL), out_block),
        out_shape=jax.ShapeDtypeStruct((bsz, (nt - 1) * TM, D_MODEL), F32),
        scratch_shapes=[pltpu.VMEM((D_MODEL // LANES, SUBLANES * SEG_PITCH, LANES), F32),
                        pltpu.VMEM((HALO_ROWS, RG_WIDTH), F32),
                        pltpu.VMEM((TM, RG_WIDTH), BF16),
                        pltpu.VMEM((SUBLANES, RG_WIDTH), F32),
                        pltpu.VMEM((3, RG_WIDTH), F32),
                        pltpu.VMEM(wgate.shape, BF16),
                        pltpu.VMEM(wg.shape, BF16),
                        pltpu.VMEM(wout.shape, BF16)],
        compiler_params=_params(2),
        name="odd_bwd",
    )(xx, sel, g, xr, hf, *consts)


def _select_mods(m, bsz):
    per = m[:bsz].reshape(bsz, 1, 3, D_MODEL)
    ctx = jnp.broadcast_to(m[bsz].reshape(1, 1, 3, D_MODEL), (bsz, 1, 3, D_MODEL))
    return jnp.concatenate([ctx, per], axis=1)


def _gate_weights(w_a, w_x):
    return (0.5 * jnp.concatenate([w_a, w_x], axis=-1)).astype(BF16)


def kernel(x, c, ctx, c_ctx, norm_g, w_mod, b_mod, e_w_in, e_w_a2, e_b_a2, e_gla_g, e_conv_w, e_w_out,
           o_w_in, o_conv_w, o_conv_b, o_w_a, o_b_a, o_w_x, o_b_x, o_lam, o_w_out, final_g):
    bsz, seq, _ = x.shape
    assert ctx.shape[1] == TM and seq % TM == 0 and bsz < SUBLANES
    assert w_mod.shape[0] == 2
    nt = 1 + seq // TM

    rows = jnp.concatenate([c, c_ctx[None, :], jnp.zeros((SUBLANES - bsz - 1, D_MODEL), F32)], axis=0)
    mods = _mods(rows, w_mod, b_mod)
    sel0 = _select_mods(mods[0], bsz)
    sel1 = _select_mods(mods[1], bsz)

    w_in = e_w_in[0]
    offs = {}
    off = 0
    for name, size in (("q", GLA_QK), ("k", GLA_QK), ("v", GLA_V), ("ga", GLA_V), ("af", GLA_LOWRANK),
                       ("ab", GLA_LOWRANK), ("cb", SC_WIDTH), ("cc", SC_WIDTH), ("cx", SC_WIDTH),
                       ("gb", SC_WIDTH)):
        offs[name] = (off, off + size)
        off += size
    cols = lambda lo, hi: w_in[:, lo:hi].astype(BF16)
    wqk = cols(offs["q"][0], offs["k"][1])
    wv = cols(*offs["v"])
    wga = cols(*offs["ga"])
    wa = jnp.pad(w_in[:, offs["af"][0]:offs["ab"][1]], ((0, 0), (0, LANES - 2 * GLA_LOWRANK))).astype(BF16)
    w2 = jnp.zeros((LANES, 2 * GLA_QK), F32)
    w2 = w2.at[:GLA_LOWRANK, :GLA_QK].set(e_w_a2[0, 0])
    w2 = w2.at[GLA_LOWRANK:2 * GLA_LOWRANK, GLA_QK:].set(e_w_a2[0, 1]).astype(BF16)
    b2 = e_b_a2[0].reshape(1, 2 * GLA_QK)
    mats = (wqk, wv, wga, wa, w2, cols(*offs["cb"]), cols(*offs["cc"]), cols(*offs["cx"]), cols(*offs["gb"]))
    (qi_f, ki_f, ke_f, dec_f, qi_b, ki_b, ke_b, dec_b, v, sga, y) = _even_in(
        ctx, x, sel0, (norm_g[0:1], b2, e_conv_w[0]), mats, nt)
    o_f = _gla_fwd(qi_f, ki_f, ke_f, v, dec_f, nt)
    w_out = e_w_out[0].astype(BF16)
    xx = _gla_bwd(qi_b, ki_b, ke_b, v, dec_b, o_f, sga, y, ctx, x, sel0, e_gla_g[0:1],
                  w_out[:GLA_V], w_out[GLA_V:], nt)

    w_in = o_w_in[0]
    wide = lambda a: a.reshape(1, RG_WIDTH)
    xr, h_f = _odd_fwd(xx, sel1, norm_g[1:2], w_in[:, :RG_WIDTH].astype(BF16), o_conv_w[0, 0],
                       wide(o_conv_b[0, 0]), _gate_weights(o_w_a[0, 0], o_w_x[0, 0]),
                       wide(o_b_a[0, 0]), wide(o_b_x[0, 0]), wide(o_lam[0, 0]), nt)
    return _odd_bwd(xx, sel1, norm_g[1:2], xr, h_f, w_in[:, RG_WIDTH:].astype(BF16), o_conv_w[0, 1],
                    wide(o_conv_b[0, 1]), _gate_weights(o_w_a[0, 1], o_w_x[0, 1]),
                    wide(o_b_a[0, 1]), wide(o_b_x[0, 1]), wide(o_lam[0, 1]), o_w_out[0].astype(BF16),
                    final_g.reshape(1, D_MODEL), nt)
```

```python
import functools

import jax
import jax.numpy as jnp
from jax import lax
from jax.experimental import pallas as pl
from jax.experimental.pallas import tpu as pltpu

F32 = jnp.float32
BF16 = jnp.bfloat16

D_MODEL = 1024
EPS = 1e-6
GLA_HEADS = 4
GLA_DK = 128
GLA_DV = 256
GLA_QK = GLA_HEADS * GLA_DK
GLA_V = GLA_HEADS * GLA_DV
GLA_LOWRANK = 16
GLA_TAU = 16.0
GLA_CHUNK = 64
GRID_W = 64
SC_WIDTH = D_MODEL
RG_WIDTH = 2 * D_MODEL
RG_BLOCKS = 16
RG_BLOCK_W = 128
RG_C = 8.0
RG_CONV = 4

TM = 256
CHUNKS = TM // GLA_CHUNK
SUBLANES = 8
LANES = 128
VMEM_LIMIT = 56 * 1024 * 1024
SEG_LEN = TM // SUBLANES
SEG_PITCH = SEG_LEN + SUBLANES
HALO_ROWS = (RG_CONV - 1) * SUBLANES
RG_CHUNK_W = 256
LOG2_E = 1.4426950408889634
F32_TINY = 1.1754943508222875e-38


def _silu(x):
    return x * jax.nn.sigmoid(x)


def _log_sigmoid(z):
    return jnp.minimum(z, 0.0) - jnp.log(1.0 + jnp.exp(-jnp.abs(z)))


def _softplus(z):
    return jnp.maximum(z, 0.0) + jnp.log1p(jnp.exp(-jnp.abs(z)))


def _dot(a, b):
    return jnp.dot(a, b, preferred_element_type=F32)


def _dot_nt(a, b):
    return lax.dot_general(a, b, (((1,), (1,)), ((), ())), preferred_element_type=F32)


def _dot_tn(a, b):
    return lax.dot_general(a, b, (((0,), (0,)), ((), ())), preferred_element_type=F32)


def _modulated_norm(x, g, shift, scale):
    ms = jnp.mean(x * x, axis=-1, keepdims=True)
    return (x * lax.rsqrt(ms + EPS) * g) * (1.0 + scale) + shift


def _params(n_axes):
    return pltpu.CompilerParams(
        dimension_semantics=("arbitrary",) * n_axes,
        vmem_limit_bytes=VMEM_LIMIT)


def _const_spec(shape):
    zeros = (0,) * len(shape)
    return pl.BlockSpec(shape, lambda *_: zeros, pipeline_mode=pl.Buffered(1))


def _mods_kernel(s_ref, w_ref, b_ref, o_ref):
    s = _silu(s_ref[...])
    o_ref[0] = jnp.dot(s, w_ref[0], preferred_element_type=F32,
                       precision=lax.Precision.HIGHEST) + b_ref[0]


def _mods(rows, w_mod, b_mod):
    depth = w_mod.shape[0]
    nb = 3
    return pl.pallas_call(
        _mods_kernel,
        grid=(depth, nb),
        in_specs=[
            pl.BlockSpec((SUBLANES, D_MODEL), lambda l, j: (0, 0)),
            pl.BlockSpec((1, D_MODEL, D_MODEL), lambda l, j: (l, 0, j)),
            pl.BlockSpec((1, 1, D_MODEL), lambda l, j: (l, 0, j)),
        ],
        out_specs=pl.BlockSpec((1, SUBLANES, D_MODEL), lambda l, j: (l, 0, j)),
        out_shape=jax.ShapeDtypeStruct((depth, SUBLANES, 3 * D_MODEL), F32),
        compiler_params=_params(2),
        name="mods",
    )(rows, w_mod, b_mod.reshape(depth, 1, 3 * D_MODEL))


def _split_dot(mat, x):
    hi = x.astype(BF16)
    lo = (x - hi.astype(F32)).astype(BF16)
    return _dot(mat, hi) + _dot(mat, lo)


def _chunk_rows(x, offset):
    return jnp.concatenate(
        [jnp.broadcast_to(x[c * GLA_CHUNK + offset:c * GLA_CHUNK + offset + 1, :], (GLA_CHUNK, x.shape[1]))
         for c in range(CHUNKS)], axis=0)


def _even_in_kernel(ctx_ref, x_ref, mod_ref, g_ref, b2_ref, cw_ref,
                    lower_in, upper_in, wqk_in, wv_in, wga_in, wa_in, w2_in, wcb_in, wcc_in, wcx_in, wgb_in,
                    of_ref, qib_ref, kib_ref, keb_ref, decb_ref, v_ref, sga_ref, y_ref,
                    lower_ref, upper_ref, wqk_ref, wv_ref, wga_ref, wa_ref, w2_ref, wcb_ref, wcc_ref, wcx_ref,
                    wgb_ref, qif_ref, kif_ref, kef_ref, decf_ref, st_ref):
    t = pl.program_id(1)

    @pl.when(t == 0)
    def _():
        st_ref[...] = jnp.zeros_like(st_ref)
        for src, dst in ((lower_in, lower_ref), (upper_in, upper_ref), (wqk_in, wqk_ref), (wv_in, wv_ref),
                         (wga_in, wga_ref), (wa_in, wa_ref), (w2_in, w2_ref), (wcb_in, wcb_ref),
                         (wcc_in, wcc_ref), (wcx_in, wcx_ref), (wgb_in, wgb_ref)):
            dst[...] = src[...]

    xt = jnp.where(t == 0, ctx_ref[...], x_ref[...])
    h = _modulated_norm(xt, g_ref[...], mod_ref[0:1, :], mod_ref[1:2, :])
    hb = h.astype(BF16)

    a_lr = _dot(hb, wa_ref[...])
    z = _dot(a_lr.astype(BF16), w2_ref[...]) + b2_ref[...]
    qk = _dot(hb, wqk_ref[...])
    v_ref[...] = _dot(hb, wv_ref[...]).astype(BF16)

    lg = _log_sigmoid(z) * (1.0 / GLA_TAU)
    cs_f = _split_dot(lower_ref[...], lg[:, :GLA_QK])
    cs_b = _split_dot(upper_ref[...], lg[:, GLA_QK:])
    tot_f = _chunk_rows(cs_f, GLA_CHUNK - 1)
    tot_b = _chunk_rows(cs_b, 0)
    c_c = _dot(hb, wcc_ref[...])

    q = qk[:, :GLA_QK] * (GLA_DK ** -0.5)
    k = qk[:, GLA_QK:]
    qif_ref[...] = (q * jnp.exp(cs_f)).astype(BF16)
    kif_ref[...] = (k * jnp.exp(-cs_f)).astype(BF16)
    kef_ref[...] = (k * jnp.exp(tot_f - cs_f)).astype(BF16)
    qib_ref[...] = (q * jnp.exp(cs_b)).astype(BF16)
    kib_ref[...] = (k * jnp.exp(-cs_b)).astype(BF16)
    keb_ref[...] = (k * jnp.exp(tot_b - cs_b)).astype(BF16)
    for c in range(CHUNKS):
        r = c * GLA_CHUNK
        decf_ref[c:c + 1, :] = jnp.exp(tot_f[r:r + 1, :])
        decb_ref[c:c + 1, :] = jnp.exp(tot_b[r:r + 1, :])

    def emit(rows, vcols, o):
        of_ref[rows, vcols] = o.astype(BF16)

    mask = _gla_mask(reverse=False)
    gla = (qif_ref, kif_ref, kef_ref, v_ref)
    close = lambda hd, opened: _gla_head_close(hd, opened, qif_ref, v_ref, decf_ref, st_ref, emit, reverse=False)
    opened = _gla_head_open(0, *gla, mask)
    zz = c_c * _dot(hb, wcx_ref[...])
    close(0, opened)
    opened = _gla_head_open(1, *gla, mask)
    c_b = _dot(hb, wcb_ref[...])
    close(1, opened)
    opened = _gla_head_open(2, *gla, mask)
    g_b = _dot(hb, wgb_ref[...])
    close(2, opened)
    opened = _gla_head_open(3, *gla, mask)
    sga_ref[...] = _silu(_dot(hb, wga_ref[...])).astype(BF16)
    close(3, opened)

    pos = lax.broadcasted_iota(jnp.int32, (TM, SC_WIDTH), 0)
    row_len = jnp.where(t == 0, TM, GRID_W)
    in_row = pos & (row_len - 1)
    z_prev = jnp.where(in_row != 0, pltpu.roll(zz, 1, 0), 0.0)
    z_next = jnp.where(in_row != row_len - 1, pltpu.roll(zz, TM - 1, 0), 0.0)
    zc = cw_ref[0:1, :] * z_prev + cw_ref[1:2, :] * zz + cw_ref[2:3, :] * z_next
    y_ref[...] = (c_b * zc * _silu(g_b)).astype(BF16)


def _chunk_triangles():
    r = lax.broadcasted_iota(jnp.int32, (TM, TM), 0)
    c = lax.broadcasted_iota(jnp.int32, (TM, TM), 1)
    same = (r // GLA_CHUNK) == (c // GLA_CHUNK)
    return (same & (c <= r)).astype(BF16), (same & (c >= r)).astype(BF16)


def _ctx_spec():
    return pl.BlockSpec((None, TM, D_MODEL), lambda b, t: (b, 0, 0))


def _latent_spec(tile, park):
    return pl.BlockSpec((None, TM, D_MODEL),
                        lambda b, t: (b, jnp.where(tile(t) == 0, park, tile(t) - 1), 0))


def _even_in(ctx, x, sel, rows, mats, nt):
    bsz = x.shape[0]
    tok = lambda n: pl.BlockSpec((None, TM, n), lambda b, t: (b, t, 0))
    dec = pl.BlockSpec((None, None, CHUNKS, GLA_QK), lambda b, t: (b, t, 0, 0))
    tok_shape = lambda n: jax.ShapeDtypeStruct((bsz, nt * TM, n), BF16)
    dec_shape = jax.ShapeDtypeStruct((bsz, nt, CHUNKS, GLA_QK), F32)
    mats = _chunk_triangles() + tuple(mats)
    consts = tuple(rows) + mats
    return pl.pallas_call(
        _even_in_kernel,
        grid=(bsz, nt),
        in_specs=[
            _ctx_spec(), _latent_spec(lambda t: t, 0),
            pl.BlockSpec((None, None, 3, D_MODEL), lambda b, t: (b, jnp.minimum(t, 1), 0, 0)),
        ] + [_const_spec(w.shape) for w in consts],
        out_specs=[tok(GLA_V), tok(GLA_QK), tok(GLA_QK), tok(GLA_QK), dec,
                   tok(GLA_V), tok(GLA_V), tok(SC_WIDTH)],
        out_shape=[tok_shape(GLA_V), tok_shape(GLA_QK), tok_shape(GLA_QK), tok_shape(GLA_QK), dec_shape,
                   tok_shape(GLA_V), tok_shape(GLA_V), tok_shape(SC_WIDTH)],
        scratch_shapes=[pltpu.VMEM(m.shape, BF16) for m in mats]
        + [pltpu.VMEM((TM, GLA_QK), BF16)] * 3
        + [pltpu.VMEM((CHUNKS, GLA_QK), F32), pltpu.VMEM((GLA_HEADS, GLA_DV, GLA_DK), F32)],
        compiler_params=_params(2),
        name="even_in",
    )(ctx, x, sel, *consts)


def _chunk_slice(ch):
    return slice(ch * GLA_CHUNK, (ch + 1) * GLA_CHUNK)


def _gla_mask(reverse):
    r = lax.broadcasted_iota(jnp.int32, (TM, TM), 0)
    c = lax.broadcasted_iota(jnp.int32, (TM, TM), 1)
    same_chunk = (r // GLA_CHUNK) == (c // GLA_CHUNK)
    return same_chunk & ((c >= r) if reverse else (c <= r))


def _gla_head_open(hd, qi_ref, ki_ref, ke_ref, v_ref, mask):
    kcols = slice(hd * GLA_DK, (hd + 1) * GLA_DK)
    vcols = slice(hd * GLA_DV, (hd + 1) * GLA_DV)
    incr = [_dot_tn(v_ref[_chunk_slice(ch), vcols], ke_ref[_chunk_slice(ch), kcols]) for ch in range(CHUNKS)]
    scores = jnp.where(mask, _dot_nt(qi_ref[:, kcols], ki_ref[:, kcols]), 0.0).astype(BF16)
    return incr, scores


def _gla_head_close(hd, opened, qi_ref, v_ref, dec_ref, st_ref, emit, reverse):
    incr, scores = opened
    kcols = slice(hd * GLA_DK, (hd + 1) * GLA_DK)
    vcols = slice(hd * GLA_DV, (hd + 1) * GLA_DV)
    state = st_ref[hd]
    before = {}
    for ch in (range(CHUNKS - 1, -1, -1) if reverse else range(CHUNKS)):
        before[ch] = state.astype(BF16)
        state = dec_ref[ch:ch + 1, kcols] * state + incr[ch]
    st_ref[hd] = state
    intra = _dot(scores, v_ref[:, vcols])
    for ch in range(CHUNKS):
        rows = _chunk_slice(ch)
        emit(rows, vcols, intra[rows, :] + _dot_nt(qi_ref[rows, kcols], before[ch]))


def _gla_tile(qi_ref, ki_ref, ke_ref, v_ref, dec_ref, st_ref, emit, reverse):
    mask = _gla_mask(reverse)
    for hd in range(GLA_HEADS):
        opened = _gla_head_open(hd, qi_ref, ki_ref, ke_ref, v_ref, mask)
        _gla_head_close(hd, opened, qi_ref, v_ref, dec_ref, st_ref, emit, reverse)


def _gla_bwd_kernel(qi_ref, ki_ref, ke_ref, v_ref, dec_ref, of_ref, sga_ref, y_ref, ctx_ref, x_ref,
                    mod_ref, gg_ref, wo_ref, wy_ref, out_ref, st_ref, ob_ref):
    j = pl.program_id(1)

    @pl.when(j == 0)
    def _():
        st_ref[...] = jnp.zeros_like(st_ref)

    proj_y = _dot(y_ref[...], wy_ref[...])

    def emit(rows, vcols, o):
        ob_ref[rows, vcols] = o

    _gla_tile(qi_ref, ki_ref, ke_ref, v_ref, dec_ref, st_ref, emit, reverse=True)

    heads = []
    for hd in range(GLA_HEADS):
        vcols = slice(hd * GLA_DV, (hd + 1) * GLA_DV)
        o = of_ref[:, vcols].astype(F32) + ob_ref[:, vcols]
        ms = jnp.mean(o * o, axis=-1, keepdims=True)
        on = o * lax.rsqrt(ms + EPS) * gg_ref[...]
        heads.append((on * sga_ref[:, vcols].astype(F32)).astype(BF16))
    inner = jnp.concatenate(heads, axis=-1)
    proj = _dot(inner, wo_ref[...]) + proj_y
    out_ref[...] = jnp.where(j == 0, ctx_ref[...], x_ref[...]) + mod_ref[2:3, :] * proj


def _bwd_tile(nt):
    return lambda j: jnp.where(j == 0, 0, nt - j)


def _gla_bwd(qi, ki, ke, v, dec, o_f, sga, y, ctx, x, sel, gg, wo, wy, nt):
    bsz = qi.shape[0]
    tile = _bwd_tile(nt)
    tok = lambda n: pl.BlockSpec((None, TM, n), lambda b, j: (b, tile(j), 0))
    return pl.pallas_call(
        _gla_bwd_kernel,
        grid=(bsz, nt),
        in_specs=[tok(GLA_QK), tok(GLA_QK), tok(GLA_QK), tok(GLA_V),
                  pl.BlockSpec((None, None, CHUNKS, GLA_QK), lambda b, j: (b, tile(j), 0, 0)),
                  tok(GLA_V), tok(GLA_V), tok(SC_WIDTH), _ctx_spec(), _latent_spec(tile, nt - 2),
                  pl.BlockSpec((None, None, 3, D_MODEL), lambda b, j: (b, jnp.minimum(j, 1), 0, 0)),
                  _const_spec(gg.shape), _const_spec(wo.shape), _const_spec(wy.shape)],
        out_specs=tok(D_MODEL),
        out_shape=jax.ShapeDtypeStruct((bsz, nt * TM, D_MODEL), F32),
        scratch_shapes=[pltpu.VMEM((GLA_HEADS, GLA_DV, GLA_DK), F32),
                        pltpu.VMEM((TM, GLA_V), F32)],
        compiler_params=_params(2),
        name="gla_bwd",
    )(qi, ki, ke, v, dec, o_f, sga, y, ctx, x, sel, gg, wo, wy)


def _to_segment_major(src_ref, stage_ref, width):
    slabs = width // LANES
    for n in range(slabs):
        for s in range(SUBLANES):
            stage_ref[n, s * SEG_PITCH:s * SEG_PITCH + SEG_LEN, :] = (
                src_ref[s * SEG_LEN:(s + 1) * SEG_LEN, n * LANES:(n + 1) * LANES])
    groups = [
        jnp.concatenate([stage_ref[n, pl.ds(j, SUBLANES, stride=SEG_PITCH), :] for n in range(slabs)], axis=1)
        for j in range(SEG_LEN)]
    return jnp.concatenate(groups, axis=0)


def _from_segment_major(val, stage_ref, dst_ref, width):
    slabs = width // LANES
    for j in range(SEG_LEN):
        for n in range(slabs):
            stage_ref[n, pl.ds(j, SUBLANES, stride=SEG_PITCH), :] = (
                val[j * SUBLANES:(j + 1) * SUBLANES, n * LANES:(n + 1) * LANES])
    for n in range(slabs):
        for s in range(SUBLANES):
            dst_ref[s * SEG_LEN:(s + 1) * SEG_LEN, n * LANES:(n + 1) * LANES] = (
                stage_ref[n, s * SEG_PITCH:s * SEG_PITCH + SEG_LEN, :])


def _rg_conv(xr, halo_ref, cw_ref, cb_ref, cols, reverse):
    row = lax.broadcasted_iota(jnp.int32, (SUBLANES, xr.shape[1]), 0)
    edge = []
    for g in range(RG_CONV - 1):
        grp = slice(g * SUBLANES, (g + 1) * SUBLANES)
        if reverse:
            edge.append(jnp.where(row == SUBLANES - 1, pltpu.roll(halo_ref[grp, cols], SUBLANES - 1, 0),
                                  pltpu.roll(xr[grp, :], SUBLANES - 1, 0)))
        else:
            cur = xr[TM - (RG_CONV - 1 - g) * SUBLANES:TM - (RG_CONV - 2 - g) * SUBLANES, :]
            edge.append(jnp.where(row == 0, pltpu.roll(halo_ref[grp, cols], 1, 0), pltpu.roll(cur, 1, 0)))
    xc = cb_ref[:, cols] + cw_ref[RG_CONV - 1:RG_CONV, cols] * xr
    for k in range(1, RG_CONV):
        if reverse:
            tap = jnp.concatenate([xr[k * SUBLANES:, :]] + edge[:k], axis=0)
        else:
            tap = jnp.concatenate(edge[RG_CONV - 1 - k:] + [xr[:TM - k * SUBLANES, :]], axis=0)
        xc = xc + cw_ref[RG_CONV - 1 - k:RG_CONV - k, cols] * tap
    halo_ref[:, cols] = xr[:HALO_ROWS, :] if reverse else xr[TM - HALO_ROWS:, :]
    return xc


def _rg_setup(lam_ref, ba_ref, bx_ref, rows_ref, copies):
    rows_ref[0:1, :] = (-0.5 * RG_C * LOG2_E) * _softplus(-lam_ref[...])
    rows_ref[1:2, :] = 0.5 * ba_ref[...]
    rows_ref[2:3, :] = 0.5 * bx_ref[...]
    for src_ref, dst_ref in copies:
        dst_ref[...] = src_ref[...]


def _rg_gates_scan(xc, first_block, wg_ref, rows_ref, carry_ref, reverse):
    row = lax.broadcasted_iota(jnp.int32, (SUBLANES, RG_BLOCK_W), 0)
    shift = SUBLANES - 1 if reverse else 1
    steps = range(SEG_LEN - 1, -1, -1) if reverse else range(SEG_LEN)
    out = []
    for i in range(xc.shape[1] // RG_BLOCK_W):
        n = first_block + i
        cols = slice(n * RG_BLOCK_W, (n + 1) * RG_BLOCK_W)
        xb = xc[:, i * RG_BLOCK_W:(i + 1) * RG_BLOCK_W]
        gates = _dot(xb.astype(BF16), wg_ref[n])
        th_r = jnp.tanh(gates[:, :RG_BLOCK_W] + rows_ref[1:2, cols])
        th_i = jnp.tanh(gates[:, RG_BLOCK_W:] + rows_ref[2:3, cols])
        half_decay = rows_ref[0:1, cols]
        a = jnp.exp2(th_r * half_decay + half_decay)
        y = 1.0 - a * a
        root = y * lax.rsqrt(jnp.maximum(y, F32_TINY))
        u = root * ((0.5 * th_i + 0.5) * xb)

        h = jnp.zeros((SUBLANES, RG_BLOCK_W), F32)
        prod = jnp.ones((SUBLANES, RG_BLOCK_W), F32)
        local, prods = {}, {}
        for j in steps:
            grp = slice(j * SUBLANES, (j + 1) * SUBLANES)
            h = a[grp, :] * h + u[grp, :]
            prod = prod * a[grp, :]
            local[j], prods[j] = h, prod

        entering = carry_ref[:, cols]
        for s in range(SUBLANES - 1):
            nxt = pltpu.roll(prod * entering + h, shift, 0)
            target = SUBLANES - 2 - s if reverse else s + 1
            entering = jnp.where(row == target, nxt, entering)
        carry_ref[:, cols] = pltpu.roll(prod * entering + h, shift, 0)

        out.append(jnp.concatenate([local[j] + prods[j] * entering for j in range(SEG_LEN)], axis=0))
    return jnp.concatenate(out, axis=1)


def _odd_fwd_kernel(xx_ref, mod_ref, g_ref, wx_ref, cw_ref, cb_ref, wg_ref, ba_ref, bx_ref, lam_ref,
                    xr_ref, hf_ref, stage_ref, halo_ref, carry_ref, rows_ref, wx_s, wg_s):
    t = pl.program_id(1)

    @pl.when(t == 0)
    def _():
        carry_ref[...] = jnp.zeros_like(carry_ref)
        _rg_setup(lam_ref, ba_ref, bx_ref, rows_ref, ((wx_ref, wx_s), (wg_ref, wg_s)))

    @pl.when(t <= 1)
    def _():
        halo_ref[...] = jnp.zeros_like(halo_ref)

    x = _to_segment_major(xx_ref, stage_ref, D_MODEL)
    hb = _modulated_norm(x, g_ref[...], mod_ref[0:1, :], mod_ref[1:2, :]).astype(BF16)
    chunks = RG_WIDTH // RG_CHUNK_W
    project = lambda m: _dot(hb, wx_s[:, m * RG_CHUNK_W:(m + 1) * RG_CHUNK_W])
    xr_next = project(0)
    for m in range(chunks):
        cols = slice(m * RG_CHUNK_W, (m + 1) * RG_CHUNK_W)
        xr = xr_next
        if m + 1 < chunks:
            xr_next = project(m + 1)
        xr_ref[:, cols] = xr.astype(BF16)
        xc = _rg_conv(xr, halo_ref, cw_ref, cb_ref, cols, reverse=False)
        hf_ref[:, cols] = _rg_gates_scan(xc, m * (RG_CHUNK_W // RG_BLOCK_W), wg_s, rows_ref, carry_ref,
                                         reverse=False).astype(BF16)


def _odd_fwd(xx, sel, g, wx, cw, cb, wg, ba, bx, lam, nt):
    bsz = xx.shape[0]
    tok = lambda n: pl.BlockSpec((None, TM, n), lambda b, t: (b, t, 0))
    consts = (g, wx, cw, cb, wg, ba, bx, lam)
    wide = jax.ShapeDtypeStruct((bsz, nt * TM, RG_WIDTH), BF16)
    return pl.pallas_call(
        _odd_fwd_kernel,
        grid=(bsz, nt),
        in_specs=[tok(D_MODEL),
                  pl.BlockSpec((None, None, 3, D_MODEL), lambda b, t: (b, jnp.minimum(t, 1), 0, 0))]
        + [_const_spec(w.shape) for w in consts],
        out_specs=[tok(RG_WIDTH), tok(RG_WIDTH)],
        out_shape=[wide, wide],
        scratch_shapes=[pltpu.VMEM((D_MODEL // LANES, SUBLANES * SEG_PITCH, LANES), F32),
                        pltpu.VMEM((HALO_ROWS, RG_WIDTH), F32),
                        pltpu.VMEM((SUBLANES, RG_WIDTH), F32),
                        pltpu.VMEM((3, RG_WIDTH), F32),
                        pltpu.VMEM(wx.shape, BF16),
                        pltpu.VMEM(wg.shape, BF16)],
        compiler_params=_params(2),
        name="odd_fwd",
    )(xx, sel, *consts)


def _odd_bwd_kernel(xx_ref, mod_ref, g_ref, xr_ref, hf_ref, wgate_ref, cw_ref, cb_ref, wg_ref, ba_ref,
                    bx_ref, lam_ref, wout_ref, fg_ref, out_ref,
                    stage_ref, halo_ref, yy_ref, carry_ref, rows_ref, wgate_s, wg_s, wout_s):
    j = pl.program_id(1)

    @pl.when(j == 0)
    def _():
        carry_ref[...] = jnp.zeros_like(carry_ref)
        _rg_setup(lam_ref, ba_ref, bx_ref, rows_ref,
                  ((wgate_ref, wgate_s), (wg_ref, wg_s), (wout_ref, wout_s)))

    @pl.when(j <= 1)
    def _():
        halo_ref[...] = jnp.zeros_like(halo_ref)

    x = _to_segment_major(xx_ref, stage_ref, D_MODEL)
    hb = _modulated_norm(x, g_ref[...], mod_ref[0:1, :], mod_ref[1:2, :]).astype(BF16)
    for m in range(RG_WIDTH // RG_CHUNK_W):
        cols = slice(m * RG_CHUNK_W, (m + 1) * RG_CHUNK_W)
        gate = _dot(hb, wgate_s[:, cols])
        xc = _rg_conv(xr_ref[:, cols].astype(F32), halo_ref, cw_ref, cb_ref, cols, reverse=True)
        h_bwd = _rg_gates_scan(xc, m * (RG_CHUNK_W // RG_BLOCK_W), wg_s, rows_ref, carry_ref, reverse=True)
        yy_ref[:, cols] = ((hf_ref[:, cols].astype(F32) + h_bwd) * _silu(gate)).astype(BF16)
    xn = x + mod_ref[2:3, :] * _dot(yy_ref[...], wout_s[...])
    ms = jnp.mean(xn * xn, axis=-1, keepdims=True)
    _from_segment_major(xn * lax.rsqrt(ms + EPS) * fg_ref[...], stage_ref, out_ref, D_MODEL)


def _odd_bwd(xx, sel, g, xr, hf, wgate, cw, cb, wg, ba, bx, lam, wout, fg, nt):
    bsz = xx.shape[0]
    tile = _bwd_tile(nt)
    tok = lambda n: pl.BlockSpec((None, TM, n), lambda b, j: (b, tile(j), 0))
    consts = (wgate, cw, cb, wg, ba, bx, lam, wout, fg)
    out_block = lambda b, j: (b, jnp.where(j == 0, nt - 2, nt - 1 - j), 0)
    return pl.pallas_call(
        _odd_bwd_kernel,
        grid=(bsz, nt),
        in_specs=[tok(D_MODEL),
                  pl.BlockSpec((None, None, 3, D_MODEL), lambda b, j: (b, jnp.minimum(j, 1), 0, 0)),
                  _const_spec(g.shape), tok(RG_WIDTH), tok(RG_WIDTH)]
        + [_const_spec(w.shape) for w in consts],
        out_specs=pl.BlockSpec((None, TM, D_MODEL), out_block),
        out_shape=jax.ShapeDtypeStruct((bsz, (nt - 1) * TM, D_MODEL), F32),
        scratch_shapes=[pltpu.VMEM((D_MODEL // LANES, SUBLANES * SEG_PITCH, LANES), F32),
                        pltpu.VMEM((HALO_ROWS, RG_WIDTH), F32),
                        pltpu.VMEM((TM, RG_WIDTH), BF16),
                        pltpu.VMEM((SUBLANES, RG_WIDTH), F32),
                        pltpu.VMEM((3, RG_WIDTH), F32),
                        pltpu.VMEM(wgate.shape, BF16),
                        pltpu.VMEM(wg.shape, BF16),
                        pltpu.VMEM(wout.shape, BF16)],
        compiler_params=_params(2),
        name="odd_bwd",
    )(xx, sel, g, xr, hf, *consts)


def _select_mods(m, bsz):
    per = m[:bsz].reshape(bsz, 1, 3, D_MODEL)
    ctx = jnp.broadcast_to(m[bsz].reshape(1, 1, 3, D_MODEL), (bsz, 1, 3, D_MODEL))
    return jnp.concatenate([ctx, per], axis=1)


def _gate_weights(w_a, w_x):
    return (0.5 * jnp.concatenate([w_a, w_x], axis=-1)).astype(BF16)


def kernel(x, c, ctx, c_ctx, norm_g, w_mod, b_mod, e_w_in, e_w_a2, e_b_a2, e_gla_g, e_conv_w, e_w_out,
           o_w_in, o_conv_w, o_conv_b, o_w_a, o_b_a, o_w_x, o_b_x, o_lam, o_w_out, final_g):
    bsz, seq, _ = x.shape
    assert ctx.shape[1] == TM and seq % TM == 0 and bsz < SUBLANES
    assert w_mod.shape[0] == 2
    nt = 1 + seq // TM

    rows = jnp.concatenate([c, c_ctx[None, :], jnp.zeros((SUBLANES - bsz - 1, D_MODEL), F32)], axis=0)
    mods = _mods(rows, w_mod, b_mod)
    sel0 = _select_mods(mods[0], bsz)
    sel1 = _select_mods(mods[1], bsz)

    w_in = e_w_in[0]
    offs = {}
    off = 0
    for name, size in (("q", GLA_QK), ("k", GLA_QK), ("v", GLA_V), ("ga", GLA_V), ("af", GLA_LOWRANK),
                       ("ab", GLA_LOWRANK), ("cb", SC_WIDTH), ("cc", SC_WIDTH), ("cx", SC_WIDTH),
                       ("gb", SC_WIDTH)):
        offs[name] = (off, off + size)
        off += size
    cols = lambda lo, hi: w_in[:, lo:hi].astype(BF16)
    wqk = cols(offs["q"][0], offs["k"][1])
    wv = cols(*offs["v"])
    wga = cols(*offs["ga"])
    wa = jnp.pad(w_in[:, offs["af"][0]:offs["ab"][1]], ((0, 0), (0, LANES - 2 * GLA_LOWRANK))).astype(BF16)
    w2 = jnp.zeros((LANES, 2 * GLA_QK), F32)
    w2 = w2.at[:GLA_LOWRANK, :GLA_QK].set(e_w_a2[0, 0])
    w2 = w2.at[GLA_LOWRANK:2 * GLA_LOWRANK, GLA_QK:].set(e_w_a2[0, 1]).astype(BF16)
    b2 = e_b_a2[0].reshape(1, 2 * GLA_QK)
    mats = (wqk, wv, wga, wa, w2, cols(*offs["cb"]), cols(*offs["cc"]), cols(*offs["cx"]), cols(*offs["gb"]))
    (o_f, qi_b, ki_b, ke_b, dec_b, v, sga, y) = _even_in(
        ctx, x, sel0, (norm_g[0:1], b2, e_conv_w[0]), mats, nt)
    w_out = e_w_out[0].astype(BF16)
    xx = _gla_bwd(qi_b, ki_b, ke_b, v, dec_b, o_f, sga, y, ctx, x, sel0, e_gla_g[0:1],
                  w_out[:GLA_V], w_out[GLA_V:], nt)

    w_in = o_w_in[0]
    wide = lambda a: a.reshape(1, RG_WIDTH)
    xr, h_f = _odd_fwd(xx, sel1, norm_g[1:2], w_in[:, :RG_WIDTH].astype(BF16), o_conv_w[0, 0],
                       wide(o_conv_b[0, 0]), _gate_weights(o_w_a[0, 0], o_w_x[0, 0]),
                       wide(o_b_a[0, 0]), wide(o_b_x[0, 0]), wide(o_lam[0, 0]), nt)
    return _odd_bwd(xx, sel1, norm_g[1:2], xr, h_f, w_in[:, RG_WIDTH:].astype(BF16), o_conv_w[0, 1],
                    wide(o_conv_b[0, 1]), _gate_weights(o_w_a[0, 1], o_w_x[0, 1]),
                    wide(o_b_a[0, 1]), wide(o_b_x[0, 1]), wide(o_lam[0, 1]), o_w_out[0].astype(BF16),
                    final_g.reshape(1, D_MODEL), nt)
```

```python
import functools

import jax
import jax.numpy as jnp
from jax import lax
from jax.experimental import pallas as pl
from jax.experimental.pallas import tpu as pltpu

F32 = jnp.float32
BF16 = jnp.bfloat16

D_MODEL = 1024
EPS = 1e-6
GLA_HEADS = 4
GLA_DK = 128
GLA_DV = 256
GLA_QK = GLA_HEADS * GLA_DK
GLA_V = GLA_HEADS * GLA_DV
GLA_LOWRANK = 16
GLA_TAU = 16.0
GLA_CHUNK = 64
GRID_W = 64
SC_WIDTH = D_MODEL
RG_WIDTH = 2 * D_MODEL
RG_BLOCKS = 16
RG_BLOCK_W = 128
RG_C = 8.0
RG_CONV = 4

TM = 256
CHUNKS = TM // GLA_CHUNK
DEC_ROWS = 8
SUBLANES = 8
LANES = 128
VMEM_LIMIT = 56 * 1024 * 1024
SEG_LEN = TM // SUBLANES
SEG_PITCH = SEG_LEN + SUBLANES
HALO_ROWS = (RG_CONV - 1) * SUBLANES
RG_CHUNK_W = 256
ROW_DECAY, ROW_BIAS_A, ROW_BIAS_X, ROW_CONV_B, ROW_CONV_W = 0, 1, 2, 3, 4
RG_ROWS = ROW_CONV_W + RG_CONV
LOG2_E = 1.4426950408889634
F32_TINY = 1.1754943508222875e-38


def _silu(x):
    return x * jax.nn.sigmoid(x)


def _log_sigmoid(z):
    return jnp.minimum(z, 0.0) - jnp.log(1.0 + jnp.exp(-jnp.abs(z)))


def _softplus(z):
    return jnp.maximum(z, 0.0) + jnp.log1p(jnp.exp(-jnp.abs(z)))


def _dot(a, b):
    return jnp.dot(a, b, preferred_element_type=F32)


def _dot_nt(a, b):
    return lax.dot_general(a, b, (((1,), (1,)), ((), ())), preferred_element_type=F32)


def _dot_tn(a, b):
    return lax.dot_general(a, b, (((0,), (0,)), ((), ())), preferred_element_type=F32)


def _modulated_norm(x, g, shift, scale):
    ms = jnp.mean(x * x, axis=-1, keepdims=True)
    return (x * lax.rsqrt(ms + EPS) * g) * (1.0 + scale) + shift


def _params(n_axes):
    return pltpu.CompilerParams(
        dimension_semantics=("arbitrary",) * n_axes,
        vmem_limit_bytes=VMEM_LIMIT)


class _Window:
    def __init__(self, array, block_shape, block_index):
        self.array, self.shape, self.index = array, tuple(block_shape), tuple(block_index)


def _operand(w):
    return w.array if isinstance(w, _Window) else w


def _const_spec(w):
    index = w.index if isinstance(w, _Window) else (0,) * len(w.shape)
    return pl.BlockSpec(w.shape, lambda *_: index, pipeline_mode=pl.Buffered(1))


def _mod_spec(layer, bsz, tile):
    return pl.BlockSpec((None, None, 3, D_MODEL),
                        lambda b, t: (layer, jnp.where(tile(t) == 0, bsz, b), 0, 0))


def _mods_kernel(s_ref, w_ref, b_ref, o_ref):
    s = _silu(s_ref[...])
    o_ref[0] = jnp.dot(s, w_ref[0], preferred_element_type=F32,
                       precision=lax.Precision.HIGHEST) + b_ref[0]


def _mods(rows, w_mod, b_mod):
    depth = w_mod.shape[0]
    nb = 3
    return pl.pallas_call(
        _mods_kernel,
        grid=(depth, nb),
        in_specs=[
            pl.BlockSpec((SUBLANES, D_MODEL), lambda l, j: (0, 0)),
            pl.BlockSpec((1, D_MODEL, D_MODEL), lambda l, j: (l, 0, j)),
            pl.BlockSpec((1, 1, D_MODEL), lambda l, j: (l, 0, j)),
        ],
        out_specs=pl.BlockSpec((1, SUBLANES, D_MODEL), lambda l, j: (l, 0, j)),
        out_shape=jax.ShapeDtypeStruct((depth, SUBLANES, 3 * D_MODEL), F32),
        compiler_params=_params(2),
        name="mods",
    )(rows, w_mod, b_mod.reshape(depth, 1, 3 * D_MODEL))


def _chunk_rows(x, offset):
    return jnp.concatenate(
        [jnp.broadcast_to(x[c * GLA_CHUNK + offset:c * GLA_CHUNK + offset + 1, :], (GLA_CHUNK, x.shape[1]))
         for c in range(CHUNKS)], axis=0)


def _even_in_kernel(ctx_ref, x_ref, mod_ref, g_ref, b2_ref, cw_ref,
                    lower_in, upper_in, wqk_in, wv_in, wga_in, wa_in, w2_in, wcb_in, wcc_in, wcx_in, wgb_in,
                    of_ref, qib_ref, kib_ref, keb_ref, decb_ref, v_ref, sga_ref, y_ref,
                    lower_ref, upper_ref, wqk_ref, wv_ref, wga_ref, wa_ref, w2_ref, wcb_ref, wcc_ref, wcx_ref,
                    wgb_ref, qif_ref, kif_ref, kef_ref, decf_ref, st_ref):
    t = pl.program_id(1)

    @pl.when(t == 0)
    def _():
        st_ref[...] = jnp.zeros_like(st_ref)
        for src, dst in ((lower_in, lower_ref), (upper_in, upper_ref), (wqk_in, wqk_ref), (wv_in, wv_ref),
                         (wga_in, wga_ref), (wa_in, wa_ref), (w2_in, w2_ref), (wcb_in, wcb_ref),
                         (wcc_in, wcc_ref), (wcx_in, wcx_ref), (wgb_in, wgb_ref)):
            dst[...] = src[...]

    xt = jnp.where(t == 0, ctx_ref[...], x_ref[...])
    h = _modulated_norm(xt, g_ref[...], mod_ref[0:1, :], mod_ref[1:2, :])
    hb = h.astype(BF16)

    a_lr = _dot(hb, wa_ref[...])
    z = _dot(a_lr.astype(BF16), w2_ref[...]) + b2_ref[...]
    qk = _dot(hb, wqk_ref[...])
    v_ref[...] = _dot(hb, wv_ref[...]).astype(BF16)

    lg = _log_sigmoid(z) * (1.0 / GLA_TAU)
    lgb = lg.astype(BF16)
    cs_f = _dot(lower_ref[...], lgb[:, :GLA_QK])
    cs_b = _dot(upper_ref[...], lgb[:, GLA_QK:])
    tot_f = _chunk_rows(cs_f, GLA_CHUNK - 1)
    tot_b = _chunk_rows(cs_b, 0)
    c_c = _dot(hb, wcc_ref[...])

    q = qk[:, :GLA_QK] * (GLA_DK ** -0.5)
    k = qk[:, GLA_QK:]
    qif_ref[...] = (q * jnp.exp(cs_f)).astype(BF16)
    kif_ref[...] = (k * jnp.exp(-cs_f)).astype(BF16)
    kef_ref[...] = (k * jnp.exp(tot_f - cs_f)).astype(BF16)
    qib_ref[...] = (q * jnp.exp(cs_b)).astype(BF16)
    kib_ref[...] = (k * jnp.exp(-cs_b)).astype(BF16)
    keb_ref[...] = (k * jnp.exp(tot_b - cs_b)).astype(BF16)
    chunk_row = lambda x, c: x[c * GLA_CHUNK:c * GLA_CHUNK + 1, :]
    padding = [jnp.zeros((DEC_ROWS - CHUNKS, GLA_QK), F32)]
    decf_ref[...] = jnp.concatenate([jnp.exp(chunk_row(tot_f, c)) for c in range(CHUNKS)] + padding, axis=0)
    decb_ref[...] = jnp.concatenate([jnp.exp(chunk_row(tot_b, c)) for c in range(CHUNKS)] + padding, axis=0)

    def emit(rows, vcols, o):
        of_ref[rows, vcols] = o.astype(BF16)

    mask = _gla_mask(reverse=False)
    gla = (qif_ref, kif_ref, kef_ref, v_ref)
    close = lambda hd, opened: _gla_head_close(hd, opened, qif_ref, v_ref, decf_ref, st_ref, emit, reverse=False)
    opened = _gla_head_open(0, *gla, mask)
    zz = c_c * _dot(hb, wcx_ref[...])
    close(0, opened)
    opened = _gla_head_open(1, *gla, mask)
    c_b = _dot(hb, wcb_ref[...])
    close(1, opened)
    opened = _gla_head_open(2, *gla, mask)
    g_b = _dot(hb, wgb_ref[...])
    close(2, opened)
    opened = _gla_head_open(3, *gla, mask)
    sga_ref[...] = _silu(_dot(hb, wga_ref[...])).astype(BF16)
    close(3, opened)

    pos = lax.broadcasted_iota(jnp.int32, (TM, SC_WIDTH), 0)
    row_len = jnp.where(t == 0, TM, GRID_W)
    in_row = pos & (row_len - 1)
    z_prev = jnp.where(in_row != 0, pltpu.roll(zz, 1, 0), 0.0)
    z_next = jnp.where(in_row != row_len - 1, pltpu.roll(zz, TM - 1, 0), 0.0)
    zc = cw_ref[0:1, :] * z_prev + cw_ref[1:2, :] * zz + cw_ref[2:3, :] * z_next
    y_ref[...] = (c_b * zc * _silu(g_b)).astype(BF16)


def _chunk_triangles():
    r = lax.broadcasted_iota(jnp.int32, (TM, TM), 0)
    c = lax.broadcasted_iota(jnp.int32, (TM, TM), 1)
    same = (r // GLA_CHUNK) == (c // GLA_CHUNK)
    return (same & (c <= r)).astype(BF16), (same & (c >= r)).astype(BF16)


def _ctx_spec():
    return pl.BlockSpec((None, TM, D_MODEL), lambda b, t: (b, 0, 0))


def _latent_spec(tile, park):
    return pl.BlockSpec((None, TM, D_MODEL),
                        lambda b, t: (b, jnp.where(tile(t) == 0, park, tile(t) - 1), 0))


def _even_in(ctx, x, mods, rows, mats, nt):
    bsz = x.shape[0]
    tok = lambda n: pl.BlockSpec((None, TM, n), lambda b, t: (b, t, 0))
    dec = pl.BlockSpec((None, None, DEC_ROWS, GLA_QK), lambda b, t: (b, t, 0, 0))
    tok_shape = lambda n: jax.ShapeDtypeStruct((bsz, nt * TM, n), BF16)
    dec_shape = jax.ShapeDtypeStruct((bsz, nt, DEC_ROWS, GLA_QK), F32)
    mats = _chunk_triangles() + tuple(mats)
    consts = tuple(rows) + mats
    return pl.pallas_call(
        _even_in_kernel,
        grid=(bsz, nt),
        in_specs=[
            _ctx_spec(), _latent_spec(lambda t: t, 0), _mod_spec(0, bsz, lambda t: t),
        ] + [_const_spec(w) for w in consts],
        out_specs=[tok(GLA_V), tok(GLA_QK), tok(GLA_QK), tok(GLA_QK), dec,
                   tok(GLA_V), tok(GLA_V), tok(SC_WIDTH)],
        out_shape=[tok_shape(GLA_V), tok_shape(GLA_QK), tok_shape(GLA_QK), tok_shape(GLA_QK), dec_shape,
                   tok_shape(GLA_V), tok_shape(GLA_V), tok_shape(SC_WIDTH)],
        scratch_shapes=[pltpu.VMEM(m.shape, BF16) for m in mats]
        + [pltpu.VMEM((TM, GLA_QK), BF16)] * 3
        + [pltpu.VMEM((DEC_ROWS, GLA_QK), F32), pltpu.VMEM((GLA_HEADS, GLA_DK, GLA_DV), F32)],
        compiler_params=_params(2),
        name="even_in",
    )(ctx, x, mods, *[_operand(w) for w in consts])


def _chunk_slice(ch):
    return slice(ch * GLA_CHUNK, (ch + 1) * GLA_CHUNK)


def _gla_mask(reverse):
    r = lax.broadcasted_iota(jnp.int32, (TM, TM), 0)
    c = lax.broadcasted_iota(jnp.int32, (TM, TM), 1)
    same_chunk = (r // GLA_CHUNK) == (c // GLA_CHUNK)
    return same_chunk & ((c >= r) if reverse else (c <= r))


def _gla_head_open(hd, qi_ref, ki_ref, ke_ref, v_ref, mask):
    kcols = slice(hd * GLA_DK, (hd + 1) * GLA_DK)
    vcols = slice(hd * GLA_DV, (hd + 1) * GLA_DV)
    incr = [_dot_tn(ke_ref[_chunk_slice(ch), kcols], v_ref[_chunk_slice(ch), vcols]) for ch in range(CHUNKS)]
    scores = jnp.where(mask, _dot_nt(qi_ref[:, kcols], ki_ref[:, kcols]), 0.0).astype(BF16)
    return incr, scores


def _gla_head_close(hd, opened, qi_ref, v_ref, dec_ref, st_ref, emit, reverse):
    incr, scores = opened
    kcols = slice(hd * GLA_DK, (hd + 1) * GLA_DK)
    vcols = slice(hd * GLA_DV, (hd + 1) * GLA_DV)
    pad = jnp.zeros((GLA_DK - DEC_ROWS, GLA_DK), F32)
    dec_cols = jnp.concatenate([dec_ref[:, kcols], pad], axis=0).T
    state = st_ref[hd]
    before = {}
    for ch in (range(CHUNKS - 1, -1, -1) if reverse else range(CHUNKS)):
        before[ch] = state.astype(BF16)
        state = dec_cols[:, ch:ch + 1] * state + incr[ch]
    st_ref[hd] = state
    intra = _dot(scores, v_ref[:, vcols])
    for ch in range(CHUNKS):
        rows = _chunk_slice(ch)
        emit(rows, vcols, intra[rows, :] + _dot(qi_ref[rows, kcols], before[ch]))


def _gla_tile(qi_ref, ki_ref, ke_ref, v_ref, dec_ref, st_ref, emit, reverse):
    mask = _gla_mask(reverse)
    for hd in range(GLA_HEADS):
        opened = _gla_head_open(hd, qi_ref, ki_ref, ke_ref, v_ref, mask)
        _gla_head_close(hd, opened, qi_ref, v_ref, dec_ref, st_ref, emit, reverse)


def _gla_bwd_kernel(qi_ref, ki_ref, ke_ref, v_ref, dec_ref, of_ref, sga_ref, y_ref, ctx_ref, x_ref,
                    mod_ref, gg_ref, wo_ref, wy_ref, out_ref, st_ref, ob_ref):
    j = pl.program_id(1)

    @pl.when(j == 0)
    def _():
        st_ref[...] = jnp.zeros_like(st_ref)

    proj_y = _dot(y_ref[...], wy_ref[...])

    def emit(rows, vcols, o):
        ob_ref[rows, vcols] = o

    _gla_tile(qi_ref, ki_ref, ke_ref, v_ref, dec_ref, st_ref, emit, reverse=True)

    heads = []
    for hd in range(GLA_HEADS):
        vcols = slice(hd * GLA_DV, (hd + 1) * GLA_DV)
        o = of_ref[:, vcols].astype(F32) + ob_ref[:, vcols]
        ms = jnp.mean(o * o, axis=-1, keepdims=True)
        on = o * lax.rsqrt(ms + EPS) * gg_ref[...]
        heads.append((on * sga_ref[:, vcols].astype(F32)).astype(BF16))
    inner = jnp.concatenate(heads, axis=-1)
    proj = _dot(inner, wo_ref[...]) + proj_y
    out_ref[...] = jnp.where(j == 0, ctx_ref[...], x_ref[...]) + mod_ref[2:3, :] * proj


def _bwd_tile(nt):
    return lambda j: jnp.where(j == 0, 0, nt - j)


def _gla_bwd(qi, ki, ke, v, dec, o_f, sga, y, ctx, x, mods, gg, wo, wy, nt):
    bsz = qi.shape[0]
    tile = _bwd_tile(nt)
    tok = lambda n: pl.BlockSpec((None, TM, n), lambda b, j: (b, tile(j), 0))
    return pl.pallas_call(
        _gla_bwd_kernel,
        grid=(bsz, nt),
        in_specs=[tok(GLA_QK), tok(GLA_QK), tok(GLA_QK), tok(GLA_V),
                  pl.BlockSpec((None, None, DEC_ROWS, GLA_QK), lambda b, j: (b, tile(j), 0, 0)),
                  tok(GLA_V), tok(GLA_V), tok(SC_WIDTH), _ctx_spec(), _latent_spec(tile, nt - 2),
                  _mod_spec(0, bsz, tile), _const_spec(gg), _const_spec(wo), _const_spec(wy)],
        out_specs=tok(D_MODEL),
        out_shape=jax.ShapeDtypeStruct((bsz, nt * TM, D_MODEL), F32),
        scratch_shapes=[pltpu.VMEM((GLA_HEADS, GLA_DK, GLA_DV), F32),
                        pltpu.VMEM((TM, GLA_V), F32)],
        compiler_params=_params(2),
        name="gla_bwd",
    )(qi, ki, ke, v, dec, o_f, sga, y, ctx, x, mods, gg, _operand(wo), _operand(wy))


def _to_segment_major(src_ref, stage_ref, width):
    slabs = width // LANES
    for n in range(slabs):
        for s in range(SUBLANES):
            stage_ref[n, s * SEG_PITCH:s * SEG_PITCH + SEG_LEN, :] = (
                src_ref[s * SEG_LEN:(s + 1) * SEG_LEN, n * LANES:(n + 1) * LANES])
    groups = [
        jnp.concatenate([stage_ref[n, pl.ds(j, SUBLANES, stride=SEG_PITCH), :] for n in range(slabs)], axis=1)
        for j in range(SEG_LEN)]
    return jnp.concatenate(groups, axis=0)


def _from_segment_major(val, stage_ref, dst_ref, width):
    slabs = width // LANES
    for j in range(SEG_LEN):
        for n in range(slabs):
            stage_ref[n, pl.ds(j, SUBLANES, stride=SEG_PITCH), :] = (
                val[j * SUBLANES:(j + 1) * SUBLANES, n * LANES:(n + 1) * LANES])
    for n in range(slabs):
        for s in range(SUBLANES):
            dst_ref[s * SEG_LEN:(s + 1) * SEG_LEN, n * LANES:(n + 1) * LANES] = (
                stage_ref[n, s * SEG_PITCH:s * SEG_PITCH + SEG_LEN, :])


def _rg_conv(xr, halo_ref, rows_ref, cols, reverse):
    row = lax.broadcasted_iota(jnp.int32, (SUBLANES, xr.shape[1]), 0)
    edge = []
    for g in range(RG_CONV - 1):
        grp = slice(g * SUBLANES, (g + 1) * SUBLANES)
        if reverse:
            edge.append(jnp.where(row == SUBLANES - 1, pltpu.roll(halo_ref[grp, cols], SUBLANES - 1, 0),
                                  pltpu.roll(xr[grp, :], SUBLANES - 1, 0)))
        else:
            cur = xr[TM - (RG_CONV - 1 - g) * SUBLANES:TM - (RG_CONV - 2 - g) * SUBLANES, :]
            edge.append(jnp.where(row == 0, pltpu.roll(halo_ref[grp, cols], 1, 0), pltpu.roll(cur, 1, 0)))
    tap_w = lambda j: rows_ref[ROW_CONV_W + j:ROW_CONV_W + j + 1, cols]
    xc = rows_ref[ROW_CONV_B:ROW_CONV_B + 1, cols] + tap_w(RG_CONV - 1) * xr
    for k in range(1, RG_CONV):
        if reverse:
            tap = jnp.concatenate([xr[k * SUBLANES:, :]] + edge[:k], axis=0)
        else:
            tap = jnp.concatenate(edge[RG_CONV - 1 - k:] + [xr[:TM - k * SUBLANES, :]], axis=0)
        xc = xc + tap_w(RG_CONV - 1 - k) * tap
    halo_ref[:, cols] = xr[:HALO_ROWS, :] if reverse else xr[TM - HALO_ROWS:, :]
    return xc


def _rg_setup(lam_ref, ba_ref, bx_ref, cb_ref, cw_ref, rows_ref, copies):
    rows_ref[ROW_DECAY:ROW_DECAY + 1, :] = (-0.5 * RG_C * LOG2_E) * _softplus(-lam_ref[...])
    rows_ref[ROW_BIAS_A:ROW_BIAS_A + 1, :] = 0.5 * ba_ref[...]
    rows_ref[ROW_BIAS_X:ROW_BIAS_X + 1, :] = 0.5 * bx_ref[...]
    rows_ref[ROW_CONV_B:ROW_CONV_B + 1, :] = 0.5 * cb_ref[...]
    rows_ref[ROW_CONV_W:ROW_CONV_W + RG_CONV, :] = 0.5 * cw_ref[...]
    for src_ref, dst_ref in copies:
        dst_ref[...] = src_ref[...]


def _rg_gates_scan(xc, first_block, wg_ref, rows_ref, carry_ref, reverse):
    row = lax.broadcasted_iota(jnp.int32, (SUBLANES, RG_BLOCK_W), 0)
    shift = SUBLANES - 1 if reverse else 1
    steps = range(SEG_LEN - 1, -1, -1) if reverse else range(SEG_LEN)
    out = []
    for i in range(xc.shape[1] // RG_BLOCK_W):
        n = first_block + i
        cols = slice(n * RG_BLOCK_W, (n + 1) * RG_BLOCK_W)
        xb = xc[:, i * RG_BLOCK_W:(i + 1) * RG_BLOCK_W]
        gates = _dot(xb.astype(BF16), wg_ref[n])
        th_r = jnp.tanh(gates[:, :RG_BLOCK_W] + rows_ref[ROW_BIAS_A:ROW_BIAS_A + 1, cols])
        th_i = jnp.tanh(gates[:, RG_BLOCK_W:] + rows_ref[ROW_BIAS_X:ROW_BIAS_X + 1, cols])
        half_decay = rows_ref[ROW_DECAY:ROW_DECAY + 1, cols]
        a = jnp.exp2(th_r * half_decay + half_decay)
        y = 1.0 - a * a
        root = y * lax.rsqrt(jnp.maximum(y, F32_TINY))
        u = root * ((th_i + 1.0) * xb)

        h = jnp.zeros((SUBLANES, RG_BLOCK_W), F32)
        prod = jnp.ones((SUBLANES, RG_BLOCK_W), F32)
        local, prods = {}, {}
        for j in steps:
            grp = slice(j * SUBLANES, (j + 1) * SUBLANES)
            h = a[grp, :] * h + u[grp, :]
            prod = prod * a[grp, :]
            local[j], prods[j] = h, prod

        entering = carry_ref[:, cols]
        for s in range(SUBLANES - 1):
            nxt = pltpu.roll(prod * entering + h, shift, 0)
            target = SUBLANES - 2 - s if reverse else s + 1
            entering = jnp.where(row == target, nxt, entering)
        carry_ref[:, cols] = pltpu.roll(prod * entering + h, shift, 0)

        out.append(jnp.concatenate([local[j] + prods[j] * entering for j in range(SEG_LEN)], axis=0))
    return jnp.concatenate(out, axis=1)


def _odd_fwd_kernel(xx_ref, mod_ref, g_ref, wx_ref, cw_ref, cb_ref, wg_ref, ba_ref, bx_ref, lam_ref,
                    xr_ref, hf_ref, stage_ref, halo_ref, carry_ref, rows_ref, wx_s, wg_s):
    t = pl.program_id(1)

    @pl.when(t == 0)
    def _():
        carry_ref[...] = jnp.zeros_like(carry_ref)
        _rg_setup(lam_ref, ba_ref, bx_ref, cb_ref, cw_ref, rows_ref, ((wx_ref, wx_s), (wg_ref, wg_s)))

    @pl.when(t <= 1)
    def _():
        halo_ref[...] = jnp.zeros_like(halo_ref)

    x = _to_segment_major(xx_ref, stage_ref, D_MODEL)
    hb = _modulated_norm(x, g_ref[...], mod_ref[0:1, :], mod_ref[1:2, :]).astype(BF16)
    chunks = RG_WIDTH // RG_CHUNK_W
    project = lambda m: _dot(hb, wx_s[:, m * RG_CHUNK_W:(m + 1) * RG_CHUNK_W])
    xr_next = project(0)
    for m in range(chunks):
        cols = slice(m * RG_CHUNK_W, (m + 1) * RG_CHUNK_W)
        xr = xr_next
        if m + 1 < chunks:
            xr_next = project(m + 1)
        xr_ref[:, cols] = xr.astype(BF16)
        xc = _rg_conv(xr, halo_ref, rows_ref, cols, reverse=False)
        hf_ref[:, cols] = _rg_gates_scan(xc, m * (RG_CHUNK_W // RG_BLOCK_W), wg_s, rows_ref, carry_ref,
                                         reverse=False).astype(BF16)


def _odd_fwd(xx, mods, g, wx, cw, cb, wg, ba, bx, lam, nt):
    bsz = xx.shape[0]
    tok = lambda n: pl.BlockSpec((None, TM, n), lambda b, t: (b, t, 0))
    consts = (g, wx, cw, cb, wg, ba, bx, lam)
    wide = jax.ShapeDtypeStruct((bsz, nt * TM, RG_WIDTH), BF16)
    return pl.pallas_call(
        _odd_fwd_kernel,
        grid=(bsz, nt),
        in_specs=[tok(D_MODEL), _mod_spec(1, bsz, lambda t: t)] + [_const_spec(w) for w in consts],
        out_specs=[tok(RG_WIDTH), tok(RG_WIDTH)],
        out_shape=[wide, wide],
        scratch_shapes=[pltpu.VMEM((D_MODEL // LANES, SUBLANES * SEG_PITCH, LANES), F32),
                        pltpu.VMEM((HALO_ROWS, RG_WIDTH), F32),
                        pltpu.VMEM((SUBLANES, RG_WIDTH), F32),
                        pltpu.VMEM((RG_ROWS, RG_WIDTH), F32),
                        pltpu.VMEM(wx.shape, BF16),
                        pltpu.VMEM(wg.shape, BF16)],
        compiler_params=_params(2),
        name="odd_fwd",
    )(xx, mods, *[_operand(w) for w in consts])


def _odd_bwd_kernel(xx_ref, mod_ref, g_ref, xr_ref, hf_ref, wgate_ref, cw_ref, cb_ref, wg_ref, ba_ref,
                    bx_ref, lam_ref, wout_ref, fg_ref, out_ref,
                    stage_ref, halo_ref, yy_ref, carry_ref, rows_ref, wgate_s, wg_s, wout_s):
    j = pl.program_id(1)

    @pl.when(j == 0)
    def _():
        carry_ref[...] = jnp.zeros_like(carry_ref)
        _rg_setup(lam_ref, ba_ref, bx_ref, cb_ref, cw_ref, rows_ref,
                  ((wgate_ref, wgate_s), (wg_ref, wg_s), (wout_ref, wout_s)))

    @pl.when(j <= 1)
    def _():
        halo_ref[...] = jnp.zeros_like(halo_ref)

    x = _to_segment_major(xx_ref, stage_ref, D_MODEL)
    hb = _modulated_norm(x, g_ref[...], mod_ref[0:1, :], mod_ref[1:2, :]).astype(BF16)
    for m in range(RG_WIDTH // RG_CHUNK_W):
        cols = slice(m * RG_CHUNK_W, (m + 1) * RG_CHUNK_W)
        gate = _dot(hb, wgate_s[:, cols])
        xc = _rg_conv(xr_ref[:, cols].astype(F32), halo_ref, rows_ref, cols, reverse=True)
        h_bwd = _rg_gates_scan(xc, m * (RG_CHUNK_W // RG_BLOCK_W), wg_s, rows_ref, carry_ref, reverse=True)
        yy_ref[:, cols] = ((hf_ref[:, cols].astype(F32) + h_bwd) * _silu(gate)).astype(BF16)
    xn = x + mod_ref[2:3, :] * _dot(yy_ref[...], wout_s[...])
    ms = jnp.mean(xn * xn, axis=-1, keepdims=True)
    _from_segment_major(xn * lax.rsqrt(ms + EPS) * fg_ref[...], stage_ref, out_ref, D_MODEL)


def _odd_bwd(xx, mods, g, xr, hf, wgate, cw, cb, wg, ba, bx, lam, wout, fg, nt):
    bsz = xx.shape[0]
    tile = _bwd_tile(nt)
    tok = lambda n: pl.BlockSpec((None, TM, n), lambda b, j: (b, tile(j), 0))
    consts = (wgate, cw, cb, wg, ba, bx, lam, wout, fg)
    out_block = lambda b, j: (b, jnp.where(j == 0, nt - 2, nt - 1 - j), 0)
    return pl.pallas_call(
        _odd_bwd_kernel,
        grid=(bsz, nt),
        in_specs=[tok(D_MODEL), _mod_spec(1, bsz, tile), _const_spec(g), tok(RG_WIDTH), tok(RG_WIDTH)]
        + [_const_spec(w) for w in consts],
        out_specs=pl.BlockSpec((None, TM, D_MODEL), out_block),
        out_shape=jax.ShapeDtypeStruct((bsz, (nt - 1) * TM, D_MODEL), F32),
        scratch_shapes=[pltpu.VMEM((D_MODEL // LANES, SUBLANES * SEG_PITCH, LANES), F32),
                        pltpu.VMEM((HALO_ROWS, RG_WIDTH), F32),
                        pltpu.VMEM((TM, RG_WIDTH), BF16),
                        pltpu.VMEM((SUBLANES, RG_WIDTH), F32),
                        pltpu.VMEM((RG_ROWS, RG_WIDTH), F32),
                        pltpu.VMEM(wgate.shape, BF16),
                        pltpu.VMEM(wg.shape, BF16),
                        pltpu.VMEM(wout.shape, BF16)],
        compiler_params=_params(2),
        name="odd_bwd",
    )(xx, mods, g, xr, hf, *[_operand(w) for w in consts])


def _gate_weights(w_a, w_x):
    return jnp.concatenate([w_a, w_x], axis=-1).astype(BF16)


def kernel(x, c, ctx, c_ctx, norm_g, w_mod, b_mod, e_w_in, e_w_a2, e_b_a2, e_gla_g, e_conv_w, e_w_out,
           o_w_in, o_conv_w, o_conv_b, o_w_a, o_b_a, o_w_x, o_b_x, o_lam, o_w_out, final_g):
    bsz, seq, _ = x.shape
    assert ctx.shape[1] == TM and seq % TM == 0 and bsz < SUBLANES
    assert w_mod.shape[0] == 2
    nt = 1 + seq // TM

    rows = jnp.concatenate([c, c_ctx[None, :], jnp.zeros((SUBLANES - bsz - 1, D_MODEL), F32)], axis=0)
    mods = _mods(rows, w_mod, b_mod).reshape(w_mod.shape[0], SUBLANES, 3, D_MODEL)

    w_in = e_w_in[0]
    w_in_bf16 = w_in.astype(BF16)
    square = (D_MODEL, D_MODEL)
    offs = {}
    off = 0
    for name, size in (("q", GLA_QK), ("k", GLA_QK), ("v", GLA_V), ("ga", GLA_V), ("af", GLA_LOWRANK),
                       ("ab", GLA_LOWRANK), ("cb", SC_WIDTH), ("cc", SC_WIDTH), ("cx", SC_WIDTH),
                       ("gb", SC_WIDTH)):
        offs[name] = (off, off + size)
        off += size
    cols = lambda lo, hi: w_in[:, lo:hi].astype(BF16)
    assert offs["q"][0] == 0 and offs["v"][0] == D_MODEL and offs["ga"] == (2 * D_MODEL, 3 * D_MODEL)
    wqk, wv, wga = (_Window(w_in_bf16, square, (0, i)) for i in range(3))
    wa = jnp.pad(w_in[:, offs["af"][0]:offs["ab"][1]], ((0, 0), (0, LANES - 2 * GLA_LOWRANK))).astype(BF16)
    w2 = jnp.zeros((LANES, 2 * GLA_QK), F32)
    w2 = w2.at[:GLA_LOWRANK, :GLA_QK].set(e_w_a2[0, 0])
    w2 = w2.at[GLA_LOWRANK:2 * GLA_LOWRANK, GLA_QK:].set(e_w_a2[0, 1]).astype(BF16)
    b2 = e_b_a2[0].reshape(1, 2 * GLA_QK)
    mats = (wqk, wv, wga, wa, w2, cols(*offs["cb"]), cols(*offs["cc"]), cols(*offs["cx"]), cols(*offs["gb"]))
    (o_f, qi_b, ki_b, ke_b, dec_b, v, sga, y) = _even_in(
        ctx, x, mods, (norm_g[0:1], b2, e_conv_w[0]), mats, nt)
    w_out = e_w_out[0].astype(BF16)
    xx = _gla_bwd(qi_b, ki_b, ke_b, v, dec_b, o_f, sga, y, ctx, x, mods, e_gla_g[0:1],
                  _Window(w_out, square, (0, 0)), _Window(w_out, square, (1, 0)), nt)

    w_in = o_w_in[0].astype(BF16)
    half = (D_MODEL, RG_WIDTH)
    wide = lambda a: a.reshape(1, RG_WIDTH)
    xr, h_f = _odd_fwd(xx, mods, norm_g[1:2], _Window(w_in, half, (0, 0)), o_conv_w[0, 0],
                       wide(o_conv_b[0, 0]), _gate_weights(o_w_a[0, 0], o_w_x[0, 0]),
                       wide(o_b_a[0, 0]), wide(o_b_x[0, 0]), wide(o_lam[0, 0]), nt)
    return _odd_bwd(xx, mods, norm_g[1:2], xr, h_f, _Window(w_in, half, (0, 1)), o_conv_w[0, 1],
                    wide(o_conv_b[0, 1]), _gate_weights(o_w_a[0, 1], o_w_x[0, 1]),
                    wide(o_b_a[0, 1]), wide(o_b_x[0, 1]), wide(o_lam[0, 1]), o_w_out[0].astype(BF16),
                    final_g.reshape(1, D_MODEL), nt)
```

```python
import functools

import jax
import jax.numpy as jnp
from jax import lax
from jax.experimental import pallas as pl
from jax.experimental.pallas import tpu as pltpu

F32 = jnp.float32
BF16 = jnp.bfloat16

D_MODEL = 1024
EPS = 1e-6
GLA_HEADS = 4
GLA_DK = 128
GLA_DV = 256
GLA_QK = GLA_HEADS * GLA_DK
GLA_V = GLA_HEADS * GLA_DV
GLA_LOWRANK = 16
GLA_TAU = 16.0
GLA_CHUNK = 64
GRID_W = 64
SC_WIDTH = D_MODEL
RG_WIDTH = 2 * D_MODEL
RG_BLOCKS = 16
RG_BLOCK_W = 128
RG_C = 8.0
RG_CONV = 4

TM = 256
CHUNKS = TM // GLA_CHUNK
SUBLANES = 8
LANES = 128
VMEM_LIMIT = 56 * 1024 * 1024
SEG_LEN = TM // SUBLANES
SEG_PITCH = SEG_LEN + SUBLANES
HALO_ROWS = (RG_CONV - 1) * SUBLANES
RG_CHUNK_W = 256
ROW_DECAY, ROW_BIAS_A, ROW_BIAS_X, ROW_CONV_B, ROW_CONV_W = 0, 1, 2, 3, 4
RG_ROWS = ROW_CONV_W + RG_CONV
LOG2_E = 1.4426950408889634
F32_TINY = 1.1754943508222875e-38


def _silu(x):
    return x * jax.nn.sigmoid(x)


def _log_sigmoid(z):
    return jnp.minimum(z, 0.0) - jnp.log(1.0 + jnp.exp(-jnp.abs(z)))


def _softplus(z):
    return jnp.maximum(z, 0.0) + jnp.log1p(jnp.exp(-jnp.abs(z)))


def _dot(a, b):
    return jnp.dot(a, b, preferred_element_type=F32)


def _dot_nt(a, b):
    return lax.dot_general(a, b, (((1,), (1,)), ((), ())), preferred_element_type=F32)


def _dot_tn(a, b):
    return lax.dot_general(a, b, (((0,), (0,)), ((), ())), preferred_element_type=F32)


def _modulated_norm(x, g, shift, scale):
    ms = jnp.mean(x * x, axis=-1, keepdims=True)
    return (x * lax.rsqrt(ms + EPS) * g) * (1.0 + scale) + shift


def _params(n_axes):
    return pltpu.CompilerParams(
        dimension_semantics=("arbitrary",) * n_axes,
        vmem_limit_bytes=VMEM_LIMIT)


class _Window:
    def __init__(self, array, block_shape, block_index):
        self.array, self.shape, self.index = array, tuple(block_shape), tuple(block_index)


def _operand(w):
    return w.array if isinstance(w, _Window) else w


def _const_spec(w):
    index = w.index if isinstance(w, _Window) else (0,) * len(w.shape)
    return pl.BlockSpec(w.shape, lambda *_: index, pipeline_mode=pl.Buffered(1))


def _mod_spec(layer, bsz, tile):
    return pl.BlockSpec((None, None, 3, D_MODEL),
                        lambda b, t: (layer, jnp.where(tile(t) == 0, bsz, b), 0, 0))


def _mods_kernel(s_ref, w_ref, b_ref, o_ref):
    s = _silu(s_ref[...])
    o_ref[0] = jnp.dot(s, w_ref[0], preferred_element_type=F32,
                       precision=lax.Precision.HIGHEST) + b_ref[0]


def _mods(rows, w_mod, b_mod):
    depth = w_mod.shape[0]
    nb = 3
    return pl.pallas_call(
        _mods_kernel,
        grid=(depth, nb),
        in_specs=[
            pl.BlockSpec((SUBLANES, D_MODEL), lambda l, j: (0, 0)),
            pl.BlockSpec((1, D_MODEL, D_MODEL), lambda l, j: (l, 0, j)),
            pl.BlockSpec((1, 1, D_MODEL), lambda l, j: (l, 0, j)),
        ],
        out_specs=pl.BlockSpec((1, SUBLANES, D_MODEL), lambda l, j: (l, 0, j)),
        out_shape=jax.ShapeDtypeStruct((depth, SUBLANES, 3 * D_MODEL), F32),
        compiler_params=_params(2),
        name="mods",
    )(rows, w_mod, b_mod.reshape(depth, 1, 3 * D_MODEL))


def _chunk_rows(x, offset):
    return jnp.concatenate(
        [jnp.broadcast_to(x[c * GLA_CHUNK + offset:c * GLA_CHUNK + offset + 1, :], (GLA_CHUNK, x.shape[1]))
         for c in range(CHUNKS)], axis=0)


def _even_in_kernel(ctx_ref, x_ref, mod_ref, g_ref, b2_ref, cw_ref,
                    lower_in, upper_in, wqk_in, wv_in, wga_in, wa_in, w2_in, wcb_in, wcc_in, wcx_in, wgb_in,
                    of_ref, qib_ref, kib_ref, keb_ref, decb_ref, v_ref, sga_ref, y_ref,
                    lower_ref, upper_ref, wqk_ref, wv_ref, wga_ref, wa_ref, w2_ref, wcb_ref, wcc_ref, wcx_ref,
                    wgb_ref, qif_ref, kif_ref, kef_ref, decf_ref, st_ref):
    t = pl.program_id(1)

    @pl.when(t == 0)
    def _():
        st_ref[...] = jnp.zeros_like(st_ref)

    @pl.when(_first_step())
    def _():
        for src, dst in ((lower_in, lower_ref), (upper_in, upper_ref), (wqk_in, wqk_ref), (wv_in, wv_ref),
                         (wga_in, wga_ref), (wa_in, wa_ref), (w2_in, w2_ref), (wcb_in, wcb_ref),
                         (wcc_in, wcc_ref), (wcx_in, wcx_ref), (wgb_in, wgb_ref)):
            dst[...] = src[...]

    xt = jnp.where(t == 0, ctx_ref[...], x_ref[...])
    h = _modulated_norm(xt, g_ref[...], mod_ref[0:1, :], mod_ref[1:2, :])
    hb = h.astype(BF16)

    a_lr = _dot(hb, wa_ref[...])
    z = _dot(a_lr.astype(BF16), w2_ref[...]) + b2_ref[...]
    qk = _dot(hb, wqk_ref[...])
    v_ref[...] = _dot(hb, wv_ref[...]).astype(BF16)

    lg = _log_sigmoid(z) * (1.0 / GLA_TAU)
    lgb = lg.astype(BF16)
    cs_f = _dot(lower_ref[...], lgb[:, :GLA_QK])
    cs_b = _dot(upper_ref[...], lgb[:, GLA_QK:])
    tot_f = _chunk_rows(cs_f, GLA_CHUNK - 1)
    tot_b = _chunk_rows(cs_b, 0)
    c_c = _dot(hb, wcc_ref[...])

    q = qk[:, :GLA_QK] * (GLA_DK ** -0.5)
    k = qk[:, GLA_QK:]
    qif_ref[...] = (q * jnp.exp(cs_f)).astype(BF16)
    kif_ref[...] = (k * jnp.exp(-cs_f)).astype(BF16)
    kef_ref[...] = (k * jnp.exp(tot_f - cs_f)).astype(BF16)
    qib_ref[...] = (q * jnp.exp(cs_b)).astype(BF16)
    kib_ref[...] = (k * jnp.exp(-cs_b)).astype(BF16)
    keb_ref[...] = (k * jnp.exp(tot_b - cs_b)).astype(BF16)
    for c in range(CHUNKS):
        r = c * GLA_CHUNK
        decf_ref[c:c + 1, :] = jnp.exp(tot_f[r:r + 1, :])
        decb_ref[c:c + 1, :] = jnp.exp(tot_b[r:r + 1, :])

    def emit(rows, vcols, o):
        of_ref[rows, vcols] = o.astype(BF16)

    mask = _gla_mask(reverse=False)
    gla = (qif_ref, kif_ref, kef_ref, v_ref)
    close = lambda hd, opened: _gla_head_close(hd, opened, qif_ref, v_ref, decf_ref, st_ref, emit, reverse=False)
    opened = _gla_head_open(0, *gla, mask)
    zz = c_c * _dot(hb, wcx_ref[...])
    close(0, opened)
    opened = _gla_head_open(1, *gla, mask)
    c_b = _dot(hb, wcb_ref[...])
    close(1, opened)
    opened = _gla_head_open(2, *gla, mask)
    g_b = _dot(hb, wgb_ref[...])
    close(2, opened)
    opened = _gla_head_open(3, *gla, mask)
    sga_ref[...] = _silu(_dot(hb, wga_ref[...])).astype(BF16)
    close(3, opened)

    pos = lax.broadcasted_iota(jnp.int32, (TM, SC_WIDTH), 0)
    row_len = jnp.where(t == 0, TM, GRID_W)
    in_row = pos & (row_len - 1)
    z_prev = jnp.where(in_row != 0, pltpu.roll(zz, 1, 0), 0.0)
    z_next = jnp.where(in_row != row_len - 1, pltpu.roll(zz, TM - 1, 0), 0.0)
    zc = cw_ref[0:1, :] * z_prev + cw_ref[1:2, :] * zz + cw_ref[2:3, :] * z_next
    y_ref[...] = (c_b * zc * _silu(g_b)).astype(BF16)


def _chunk_triangles():
    r = lax.broadcasted_iota(jnp.int32, (TM, TM), 0)
    c = lax.broadcasted_iota(jnp.int32, (TM, TM), 1)
    same = (r // GLA_CHUNK) == (c // GLA_CHUNK)
    return (same & (c <= r)).astype(BF16), (same & (c >= r)).astype(BF16)


def _ctx_spec():
    return pl.BlockSpec((None, TM, D_MODEL), lambda b, t: (b, 0, 0))


def _latent_spec(tile, park):
    return pl.BlockSpec((None, TM, D_MODEL),
                        lambda b, t: (b, jnp.where(tile(t) == 0, park, tile(t) - 1), 0))


def _even_in(ctx, x, mods, rows, mats, nt):
    bsz = x.shape[0]
    tok = lambda n: pl.BlockSpec((None, TM, n), lambda b, t: (b, t, 0))
    dec = pl.BlockSpec((None, None, CHUNKS, GLA_QK), lambda b, t: (b, t, 0, 0))
    tok_shape = lambda n: jax.ShapeDtypeStruct((bsz, nt * TM, n), BF16)
    dec_shape = jax.ShapeDtypeStruct((bsz, nt, CHUNKS, GLA_QK), F32)
    mats = _chunk_triangles() + tuple(mats)
    consts = tuple(rows) + mats
    return pl.pallas_call(
        _even_in_kernel,
        grid=(bsz, nt),
        in_specs=[
            _ctx_spec(), _latent_spec(lambda t: t, 0), _mod_spec(0, bsz, lambda t: t),
        ] + [_const_spec(w) for w in consts],
        out_specs=[tok(GLA_V), tok(GLA_QK), tok(GLA_QK), tok(GLA_QK), dec,
                   tok(GLA_V), tok(GLA_V), tok(SC_WIDTH)],
        out_shape=[tok_shape(GLA_V), tok_shape(GLA_QK), tok_shape(GLA_QK), tok_shape(GLA_QK), dec_shape,
                   tok_shape(GLA_V), tok_shape(GLA_V), tok_shape(SC_WIDTH)],
        scratch_shapes=[pltpu.VMEM(m.shape, BF16) for m in mats]
        + [pltpu.VMEM((TM, GLA_QK), BF16)] * 3
        + [pltpu.VMEM((CHUNKS, GLA_QK), F32), pltpu.VMEM((GLA_HEADS, GLA_DV, GLA_DK), F32)],
        compiler_params=_params(2),
        name="even_in",
    )(ctx, x, mods, *[_operand(w) for w in consts])


def _chunk_slice(ch):
    return slice(ch * GLA_CHUNK, (ch + 1) * GLA_CHUNK)


def _gla_mask(reverse):
    r = lax.broadcasted_iota(jnp.int32, (TM, TM), 0)
    c = lax.broadcasted_iota(jnp.int32, (TM, TM), 1)
    same_chunk = (r // GLA_CHUNK) == (c // GLA_CHUNK)
    return same_chunk & ((c >= r) if reverse else (c <= r))


def _gla_head_open(hd, qi_ref, ki_ref, ke_ref, v_ref, mask):
    kcols = slice(hd * GLA_DK, (hd + 1) * GLA_DK)
    vcols = slice(hd * GLA_DV, (hd + 1) * GLA_DV)
    incr = [_dot_tn(v_ref[_chunk_slice(ch), vcols], ke_ref[_chunk_slice(ch), kcols]) for ch in range(CHUNKS)]
    scores = jnp.where(mask, _dot_nt(qi_ref[:, kcols], ki_ref[:, kcols]), 0.0).astype(BF16)
    return incr, scores


def _gla_head_close(hd, opened, qi_ref, v_ref, dec_ref, st_ref, emit, reverse):
    incr, scores = opened
    kcols = slice(hd * GLA_DK, (hd + 1) * GLA_DK)
    vcols = slice(hd * GLA_DV, (hd + 1) * GLA_DV)
    state = st_ref[hd]
    before = {}
    for ch in (range(CHUNKS - 1, -1, -1) if reverse else range(CHUNKS)):
        before[ch] = state.astype(BF16)
        state = dec_ref[ch:ch + 1, kcols] * state + incr[ch]
    st_ref[hd] = state
    intra = _dot(scores, v_ref[:, vcols])
    for ch in range(CHUNKS):
        rows = _chunk_slice(ch)
        emit(rows, vcols, intra[rows, :] + _dot_nt(qi_ref[rows, kcols], before[ch]))


def _gla_tile(qi_ref, ki_ref, ke_ref, v_ref, dec_ref, st_ref, emit, reverse):
    mask = _gla_mask(reverse)
    for hd in range(GLA_HEADS):
        opened = _gla_head_open(hd, qi_ref, ki_ref, ke_ref, v_ref, mask)
        _gla_head_close(hd, opened, qi_ref, v_ref, dec_ref, st_ref, emit, reverse)


def _gla_bwd_kernel(qi_ref, ki_ref, ke_ref, v_ref, dec_ref, of_ref, sga_ref, y_ref, ctx_ref, x_ref,
                    mod_ref, gg_ref, wo_in, wy_in, out_ref, st_ref, ob_ref, wo_ref, wy_ref):
    j = pl.program_id(1)

    @pl.when(_first_step())
    def _():
        wo_ref[...] = wo_in[...].astype(BF16)
        wy_ref[...] = wy_in[...].astype(BF16)

    @pl.when(j == 0)
    def _():
        st_ref[...] = jnp.zeros_like(st_ref)

    proj_y = _dot(y_ref[...], wy_ref[...])

    def emit(rows, vcols, o):
        ob_ref[rows, vcols] = o

    _gla_tile(qi_ref, ki_ref, ke_ref, v_ref, dec_ref, st_ref, emit, reverse=True)

    heads = []
    for hd in range(GLA_HEADS):
        vcols = slice(hd * GLA_DV, (hd + 1) * GLA_DV)
        o = of_ref[:, vcols].astype(F32) + ob_ref[:, vcols]
        ms = jnp.mean(o * o, axis=-1, keepdims=True)
        on = o * lax.rsqrt(ms + EPS) * gg_ref[...]
        heads.append((on * sga_ref[:, vcols].astype(F32)).astype(BF16))
    inner = jnp.concatenate(heads, axis=-1)
    proj = _dot(inner, wo_ref[...]) + proj_y
    out_ref[...] = jnp.where(j == 0, ctx_ref[...], x_ref[...]) + mod_ref[2:3, :] * proj


def _bwd_tile(nt):
    return lambda j: jnp.where(j == 0, 0, nt - j)


def _gla_bwd(qi, ki, ke, v, dec, o_f, sga, y, ctx, x, mods, gg, wo, wy, nt):
    bsz = qi.shape[0]
    tile = _bwd_tile(nt)
    tok = lambda n: pl.BlockSpec((None, TM, n), lambda b, j: (b, tile(j), 0))
    return pl.pallas_call(
        _gla_bwd_kernel,
        grid=(bsz, nt),
        in_specs=[tok(GLA_QK), tok(GLA_QK), tok(GLA_QK), tok(GLA_V),
                  pl.BlockSpec((None, None, CHUNKS, GLA_QK), lambda b, j: (b, tile(j), 0, 0)),
                  tok(GLA_V), tok(GLA_V), tok(SC_WIDTH), _ctx_spec(), _latent_spec(tile, nt - 2),
                  _mod_spec(0, bsz, tile), _const_spec(gg), _const_spec(wo), _const_spec(wy)],
        out_specs=tok(D_MODEL),
        out_shape=jax.ShapeDtypeStruct((bsz, nt * TM, D_MODEL), F32),
        scratch_shapes=[pltpu.VMEM((GLA_HEADS, GLA_DV, GLA_DK), F32),
                        pltpu.VMEM((TM, GLA_V), F32),
                        pltpu.VMEM(wo.shape, BF16),
                        pltpu.VMEM(wy.shape, BF16)],
        compiler_params=_params(2),
        name="gla_bwd",
    )(qi, ki, ke, v, dec, o_f, sga, y, ctx, x, mods, gg, _operand(wo), _operand(wy))


def _to_segment_major(src_ref, stage_ref, width):
    slabs = width // LANES
    for n in range(slabs):
        for s in range(SUBLANES):
            stage_ref[n, s * SEG_PITCH:s * SEG_PITCH + SEG_LEN, :] = (
                src_ref[s * SEG_LEN:(s + 1) * SEG_LEN, n * LANES:(n + 1) * LANES])
    groups = [
        jnp.concatenate([stage_ref[n, pl.ds(j, SUBLANES, stride=SEG_PITCH), :] for n in range(slabs)], axis=1)
        for j in range(SEG_LEN)]
    return jnp.concatenate(groups, axis=0)


def _from_segment_major(val, stage_ref, dst_ref, width):
    slabs = width // LANES
    for j in range(SEG_LEN):
        for n in range(slabs):
            stage_ref[n, pl.ds(j, SUBLANES, stride=SEG_PITCH), :] = (
                val[j * SUBLANES:(j + 1) * SUBLANES, n * LANES:(n + 1) * LANES])
    for n in range(slabs):
        for s in range(SUBLANES):
            dst_ref[s * SEG_LEN:(s + 1) * SEG_LEN, n * LANES:(n + 1) * LANES] = (
                stage_ref[n, s * SEG_PITCH:s * SEG_PITCH + SEG_LEN, :])


def _rg_conv(xr, halo_ref, rows_ref, cols, reverse):
    row = lax.broadcasted_iota(jnp.int32, (SUBLANES, xr.shape[1]), 0)
    edge = []
    for g in range(RG_CONV - 1):
        grp = slice(g * SUBLANES, (g + 1) * SUBLANES)
        if reverse:
            edge.append(jnp.where(row == SUBLANES - 1, pltpu.roll(halo_ref[grp, cols], SUBLANES - 1, 0),
                                  pltpu.roll(xr[grp, :], SUBLANES - 1, 0)))
        else:
            cur = xr[TM - (RG_CONV - 1 - g) * SUBLANES:TM - (RG_CONV - 2 - g) * SUBLANES, :]
            edge.append(jnp.where(row == 0, pltpu.roll(halo_ref[grp, cols], 1, 0), pltpu.roll(cur, 1, 0)))
    tap_w = lambda j: rows_ref[ROW_CONV_W + j:ROW_CONV_W + j + 1, cols]
    xc = rows_ref[ROW_CONV_B:ROW_CONV_B + 1, cols] + tap_w(RG_CONV - 1) * xr
    for k in range(1, RG_CONV):
        if reverse:
            tap = jnp.concatenate([xr[k * SUBLANES:, :]] + edge[:k], axis=0)
        else:
            tap = jnp.concatenate(edge[RG_CONV - 1 - k:] + [xr[:TM - k * SUBLANES, :]], axis=0)
        xc = xc + tap_w(RG_CONV - 1 - k) * tap
    halo_ref[:, cols] = xr[:HALO_ROWS, :] if reverse else xr[TM - HALO_ROWS:, :]
    return xc


def _first_step():
    return (pl.program_id(0) == 0) & (pl.program_id(1) == 0)


def _rg_setup(lam_ref, ba_ref, bx_ref, cb_ref, cw_ref, wa_ref, wx_ref, rows_ref, wg_s, copies):
    wg_s[:, :, :RG_BLOCK_W] = wa_ref[...].astype(BF16)
    wg_s[:, :, RG_BLOCK_W:] = wx_ref[...].astype(BF16)
    rows_ref[ROW_DECAY:ROW_DECAY + 1, :] = (-0.5 * RG_C * LOG2_E) * _softplus(-lam_ref[...])
    rows_ref[ROW_BIAS_A:ROW_BIAS_A + 1, :] = 0.5 * ba_ref[...]
    rows_ref[ROW_BIAS_X:ROW_BIAS_X + 1, :] = 0.5 * bx_ref[...]
    rows_ref[ROW_CONV_B:ROW_CONV_B + 1, :] = 0.5 * cb_ref[...]
    rows_ref[ROW_CONV_W:ROW_CONV_W + RG_CONV, :] = 0.5 * cw_ref[...]
    for src_ref, dst_ref in copies:
        dst_ref[...] = src_ref[...].astype(BF16)


def _rg_gates_scan(xc, first_block, wg_ref, rows_ref, carry_ref, reverse):
    row = lax.broadcasted_iota(jnp.int32, (SUBLANES, RG_BLOCK_W), 0)
    shift = SUBLANES - 1 if reverse else 1
    steps = range(SEG_LEN - 1, -1, -1) if reverse else range(SEG_LEN)
    out = []
    for i in range(xc.shape[1] // RG_BLOCK_W):
        n = first_block + i
        cols = slice(n * RG_BLOCK_W, (n + 1) * RG_BLOCK_W)
        xb = xc[:, i * RG_BLOCK_W:(i + 1) * RG_BLOCK_W]
        gates = _dot(xb.astype(BF16), wg_ref[n])
        th_r = jnp.tanh(gates[:, :RG_BLOCK_W] + rows_ref[ROW_BIAS_A:ROW_BIAS_A + 1, cols])
        th_i = jnp.tanh(gates[:, RG_BLOCK_W:] + rows_ref[ROW_BIAS_X:ROW_BIAS_X + 1, cols])
        half_decay = rows_ref[ROW_DECAY:ROW_DECAY + 1, cols]
        a = jnp.exp2(th_r * half_decay + half_decay)
        y = 1.0 - a * a
        root = y * lax.rsqrt(jnp.maximum(y, F32_TINY))
        u = root * ((th_i + 1.0) * xb)

        h = jnp.zeros((SUBLANES, RG_BLOCK_W), F32)
        prod = jnp.ones((SUBLANES, RG_BLOCK_W), F32)
        local, prods = {}, {}
        for j in steps:
            grp = slice(j * SUBLANES, (j + 1) * SUBLANES)
            h = a[grp, :] * h + u[grp, :]
            prod = prod * a[grp, :]
            local[j], prods[j] = h, prod

        entering = carry_ref[:, cols]
        for s in range(SUBLANES - 1):
            nxt = pltpu.roll(prod * entering + h, shift, 0)
            target = SUBLANES - 2 - s if reverse else s + 1
            entering = jnp.where(row == target, nxt, entering)
        carry_ref[:, cols] = pltpu.roll(prod * entering + h, shift, 0)

        out.append(jnp.concatenate([local[j] + prods[j] * entering for j in range(SEG_LEN)], axis=0))
    return jnp.concatenate(out, axis=1)


def _odd_fwd_kernel(xx_ref, mod_ref, g_ref, wx_ref, cw_ref, cb_ref, ga_ref, gx_ref, ba_ref, bx_ref, lam_ref,
                    xr_ref, hf_ref, stage_ref, halo_ref, carry_ref, rows_ref, wx_s, wg_s):
    t = pl.program_id(1)

    @pl.when(_first_step())
    def _():
        _rg_setup(lam_ref, ba_ref, bx_ref, cb_ref, cw_ref, ga_ref, gx_ref, rows_ref, wg_s, ((wx_ref, wx_s),))

    @pl.when(t == 0)
    def _():
        carry_ref[...] = jnp.zeros_like(carry_ref)

    @pl.when(t <= 1)
    def _():
        halo_ref[...] = jnp.zeros_like(halo_ref)

    x = _to_segment_major(xx_ref, stage_ref, D_MODEL)
    hb = _modulated_norm(x, g_ref[...], mod_ref[0:1, :], mod_ref[1:2, :]).astype(BF16)
    chunks = RG_WIDTH // RG_CHUNK_W
    project = lambda m: _dot(hb, wx_s[:, m * RG_CHUNK_W:(m + 1) * RG_CHUNK_W])
    xr_next = project(0)
    for m in range(chunks):
        cols = slice(m * RG_CHUNK_W, (m + 1) * RG_CHUNK_W)
        xr = xr_next
        if m + 1 < chunks:
            xr_next = project(m + 1)
        xr_ref[:, cols] = xr.astype(BF16)
        xc = _rg_conv(xr, halo_ref, rows_ref, cols, reverse=False)
        hf_ref[:, cols] = _rg_gates_scan(xc, m * (RG_CHUNK_W // RG_BLOCK_W), wg_s, rows_ref, carry_ref,
                                         reverse=False).astype(BF16)


def _odd_fwd(xx, mods, g, wx, cw, cb, ga, gx, ba, bx, lam, nt):
    bsz = xx.shape[0]
    tok = lambda n: pl.BlockSpec((None, TM, n), lambda b, t: (b, t, 0))
    consts = (g, wx, cw, cb, ga, gx, ba, bx, lam)
    gate_scratch = pltpu.VMEM((RG_BLOCKS, RG_BLOCK_W, 2 * RG_BLOCK_W), BF16)
    wide = jax.ShapeDtypeStruct((bsz, nt * TM, RG_WIDTH), BF16)
    return pl.pallas_call(
        _odd_fwd_kernel,
        grid=(bsz, nt),
        in_specs=[tok(D_MODEL), _mod_spec(1, bsz, lambda t: t)] + [_const_spec(w) for w in consts],
        out_specs=[tok(RG_WIDTH), tok(RG_WIDTH)],
        out_shape=[wide, wide],
        scratch_shapes=[pltpu.VMEM((D_MODEL // LANES, SUBLANES * SEG_PITCH, LANES), F32),
                        pltpu.VMEM((HALO_ROWS, RG_WIDTH), F32),
                        pltpu.VMEM((SUBLANES, RG_WIDTH), F32),
                        pltpu.VMEM((RG_ROWS, RG_WIDTH), F32),
                        pltpu.VMEM(wx.shape, BF16),
                        gate_scratch],
        compiler_params=_params(2),
        name="odd_fwd",
    )(xx, mods, *[_operand(w) for w in consts])


def _odd_bwd_kernel(xx_ref, mod_ref, g_ref, xr_ref, hf_ref, wgate_ref, cw_ref, cb_ref, ga_ref, gx_ref, ba_ref,
                    bx_ref, lam_ref, wout_ref, fg_ref, out_ref,
                    stage_ref, halo_ref, yy_ref, carry_ref, rows_ref, wgate_s, wg_s, wout_s):
    j = pl.program_id(1)

    @pl.when(_first_step())
    def _():
        _rg_setup(lam_ref, ba_ref, bx_ref, cb_ref, cw_ref, ga_ref, gx_ref, rows_ref, wg_s,
                  ((wgate_ref, wgate_s), (wout_ref, wout_s)))

    @pl.when(j == 0)
    def _():
        carry_ref[...] = jnp.zeros_like(carry_ref)

    @pl.when(j <= 1)
    def _():
        halo_ref[...] = jnp.zeros_like(halo_ref)

    x = _to_segment_major(xx_ref, stage_ref, D_MODEL)
    hb = _modulated_norm(x, g_ref[...], mod_ref[0:1, :], mod_ref[1:2, :]).astype(BF16)
    for m in range(RG_WIDTH // RG_CHUNK_W):
        cols = slice(m * RG_CHUNK_W, (m + 1) * RG_CHUNK_W)
        gate = _dot(hb, wgate_s[:, cols])
        xc = _rg_conv(xr_ref[:, cols].astype(F32), halo_ref, rows_ref, cols, reverse=True)
        h_bwd = _rg_gates_scan(xc, m * (RG_CHUNK_W // RG_BLOCK_W), wg_s, rows_ref, carry_ref, reverse=True)
        yy_ref[:, cols] = ((hf_ref[:, cols].astype(F32) + h_bwd) * _silu(gate)).astype(BF16)
    xn = x + mod_ref[2:3, :] * _dot(yy_ref[...], wout_s[...])
    ms = jnp.mean(xn * xn, axis=-1, keepdims=True)
    _from_segment_major(xn * lax.rsqrt(ms + EPS) * fg_ref[...], stage_ref, out_ref, D_MODEL)


def _odd_bwd(xx, mods, g, xr, hf, wgate, cw, cb, ga, gx, ba, bx, lam, wout, fg, nt):
    bsz = xx.shape[0]
    tile = _bwd_tile(nt)
    tok = lambda n: pl.BlockSpec((None, TM, n), lambda b, j: (b, tile(j), 0))
    consts = (wgate, cw, cb, ga, gx, ba, bx, lam, wout, fg)
    gate_scratch = pltpu.VMEM((RG_BLOCKS, RG_BLOCK_W, 2 * RG_BLOCK_W), BF16)
    out_block = lambda b, j: (b, jnp.where(j == 0, nt - 2, nt - 1 - j), 0)
    return pl.pallas_call(
        _odd_bwd_kernel,
        grid=(bsz, nt),
        in_specs=[tok(D_MODEL), _mod_spec(1, bsz, tile), _const_spec(g), tok(RG_WIDTH), tok(RG_WIDTH)]
        + [_const_spec(w) for w in consts],
        out_specs=pl.BlockSpec((None, TM, D_MODEL), out_block),
        out_shape=jax.ShapeDtypeStruct((bsz, (nt - 1) * TM, D_MODEL), F32),
        scratch_shapes=[pltpu.VMEM((D_MODEL // LANES, SUBLANES * SEG_PITCH, LANES), F32),
                        pltpu.VMEM((HALO_ROWS, RG_WIDTH), F32),
                        pltpu.VMEM((TM, RG_WIDTH), BF16),
                        pltpu.VMEM((SUBLANES, RG_WIDTH), F32),
                        pltpu.VMEM((RG_ROWS, RG_WIDTH), F32),
                        pltpu.VMEM(wgate.shape, BF16),
                        gate_scratch,
                        pltpu.VMEM(wout.shape, BF16)],
        compiler_params=_params(2),
        name="odd_bwd",
    )(xx, mods, g, xr, hf, *[_operand(w) for w in consts])


def kernel(x, c, ctx, c_ctx, norm_g, w_mod, b_mod, e_w_in, e_w_a2, e_b_a2, e_gla_g, e_conv_w, e_w_out,
           o_w_in, o_conv_w, o_conv_b, o_w_a, o_b_a, o_w_x, o_b_x, o_lam, o_w_out, final_g):
    bsz, seq, _ = x.shape
    assert ctx.shape[1] == TM and seq % TM == 0 and bsz < SUBLANES
    assert w_mod.shape[0] == 2
    nt = 1 + seq // TM

    rows = jnp.concatenate([c, c_ctx[None, :], jnp.zeros((SUBLANES - bsz - 1, D_MODEL), F32)], axis=0)
    mods = _mods(rows, w_mod, b_mod).reshape(w_mod.shape[0], SUBLANES, 3, D_MODEL)

    w_in = e_w_in[0]
    w_in_bf16 = w_in.astype(BF16)
    square = (D_MODEL, D_MODEL)
    offs = {}
    off = 0
    for name, size in (("q", GLA_QK), ("k", GLA_QK), ("v", GLA_V), ("ga", GLA_V), ("af", GLA_LOWRANK),
                       ("ab", GLA_LOWRANK), ("cb", SC_WIDTH), ("cc", SC_WIDTH), ("cx", SC_WIDTH),
                       ("gb", SC_WIDTH)):
        offs[name] = (off, off + size)
        off += size
    cols = lambda lo, hi: w_in[:, lo:hi].astype(BF16)
    assert offs["q"][0] == 0 and offs["v"][0] == D_MODEL and offs["ga"] == (2 * D_MODEL, 3 * D_MODEL)
    wqk, wv, wga = (_Window(w_in_bf16, square, (0, i)) for i in range(3))
    wa = jnp.pad(w_in[:, offs["af"][0]:offs["ab"][1]], ((0, 0), (0, LANES - 2 * GLA_LOWRANK))).astype(BF16)
    w2 = jnp.zeros((LANES, 2 * GLA_QK), F32)
    w2 = w2.at[:GLA_LOWRANK, :GLA_QK].set(e_w_a2[0, 0])
    w2 = w2.at[GLA_LOWRANK:2 * GLA_LOWRANK, GLA_QK:].set(e_w_a2[0, 1]).astype(BF16)
    b2 = e_b_a2[0].reshape(1, 2 * GLA_QK)
    mats = (wqk, wv, wga, wa, w2, cols(*offs["cb"]), cols(*offs["cc"]), cols(*offs["cx"]), cols(*offs["gb"]))
    (o_f, qi_b, ki_b, ke_b, dec_b, v, sga, y) = _even_in(
        ctx, x, mods, (norm_g[0:1], b2, e_conv_w[0]), mats, nt)
    w_out = e_w_out[0]
    xx = _gla_bwd(qi_b, ki_b, ke_b, v, dec_b, o_f, sga, y, ctx, x, mods, e_gla_g[0:1],
                  _Window(w_out, square, (0, 0)), _Window(w_out, square, (1, 0)), nt)

    w_in = o_w_in[0]
    half = (D_MODEL, RG_WIDTH)
    wide = lambda a: a.reshape(1, RG_WIDTH)
    xr, h_f = _odd_fwd(xx, mods, norm_g[1:2], _Window(w_in, half, (0, 0)), o_conv_w[0, 0],
                       wide(o_conv_b[0, 0]), o_w_a[0, 0], o_w_x[0, 0],
                       wide(o_b_a[0, 0]), wide(o_b_x[0, 0]), wide(o_lam[0, 0]), nt)
    return _odd_bwd(xx, mods, norm_g[1:2], xr, h_f, _Window(w_in, half, (0, 1)), o_conv_w[0, 1],
                    wide(o_conv_b[0, 1]), o_w_a[0, 1], o_w_x[0, 1],
                    wide(o_b_a[0, 1]), wide(o_b_x[0, 1]), wide(o_lam[0, 1]), o_w_out[0],
                    final_g.reshape(1, D_MODEL), nt)
```

```python
import functools

import jax
import jax.numpy as jnp
from jax import lax
from jax.experimental import pallas as pl
from jax.experimental.pallas import tpu as pltpu

F32 = jnp.float32
BF16 = jnp.bfloat16

D_MODEL = 1024
EPS = 1e-6
GLA_HEADS = 4
GLA_DK = 128
GLA_DV = 256
GLA_QK = GLA_HEADS * GLA_DK
GLA_V = GLA_HEADS * GLA_DV
GLA_LOWRANK = 16
GLA_TAU = 16.0
GLA_CHUNK = 64
GRID_W = 64
SC_WIDTH = D_MODEL
RG_WIDTH = 2 * D_MODEL
RG_BLOCKS = 16
RG_BLOCK_W = 128
RG_C = 8.0
RG_CONV = 4

TM = 256
CHUNKS = TM // GLA_CHUNK
SUBLANES = 8
LANES = 128
VMEM_LIMIT = 56 * 1024 * 1024
SEG_LEN = TM // SUBLANES
SEG_PITCH = SEG_LEN + SUBLANES
HALO_ROWS = (RG_CONV - 1) * SUBLANES
RG_CHUNK_W = 256
ROW_DECAY, ROW_BIAS_A, ROW_BIAS_X, ROW_CONV_B, ROW_CONV_W = 0, 1, 2, 3, 4
RG_ROWS = ROW_CONV_W + RG_CONV
LOG2_E = 1.4426950408889634
F32_TINY = 1.1754943508222875e-38


def _silu(x):
    return x * jax.nn.sigmoid(x)


def _log_sigmoid(z):
    return jnp.minimum(z, 0.0) - jnp.log(1.0 + jnp.exp(-jnp.abs(z)))


def _softplus(z):
    return jnp.maximum(z, 0.0) + jnp.log1p(jnp.exp(-jnp.abs(z)))


def _dot(a, b):
    return jnp.dot(a, b, preferred_element_type=F32)


def _dot_nt(a, b):
    return lax.dot_general(a, b, (((1,), (1,)), ((), ())), preferred_element_type=F32)


def _dot_tn(a, b):
    return lax.dot_general(a, b, (((0,), (0,)), ((), ())), preferred_element_type=F32)


def _modulated_norm(x, g, shift, scale):
    ms = jnp.mean(x * x, axis=-1, keepdims=True)
    return (x * lax.rsqrt(ms + EPS) * g) * (1.0 + scale) + shift


def _params(n_axes):
    return pltpu.CompilerParams(
        dimension_semantics=("arbitrary",) * n_axes,
        vmem_limit_bytes=VMEM_LIMIT)


class _Window:
    def __init__(self, array, block_shape, block_index):
        self.array, self.shape, self.index = array, tuple(block_shape), tuple(block_index)


def _operand(w):
    return w.array if isinstance(w, _Window) else w


def _const_spec(w):
    index = w.index if isinstance(w, _Window) else (0,) * len(w.shape)
    return pl.BlockSpec(w.shape, lambda *_: index, pipeline_mode=pl.Buffered(1))


def _mod_spec(layer, bsz, tile):
    return pl.BlockSpec((None, None, 3, D_MODEL),
                        lambda b, t: (layer, jnp.where(tile(t) == 0, bsz, b), 0, 0))


def _mods_kernel(s_ref, w_ref, b_ref, o_ref):
    s = _silu(s_ref[...])
    o_ref[0] = jnp.dot(s, w_ref[0], preferred_element_type=F32,
                       precision=lax.Precision.HIGHEST) + b_ref[0]


def _mods(rows, w_mod, b_mod):
    depth = w_mod.shape[0]
    nb = 3
    return pl.pallas_call(
        _mods_kernel,
        grid=(depth, nb),
        in_specs=[
            pl.BlockSpec((SUBLANES, D_MODEL), lambda l, j: (0, 0)),
            pl.BlockSpec((1, D_MODEL, D_MODEL), lambda l, j: (l, 0, j)),
            pl.BlockSpec((1, 1, D_MODEL), lambda l, j: (l, 0, j)),
        ],
        out_specs=pl.BlockSpec((1, SUBLANES, D_MODEL), lambda l, j: (l, 0, j)),
        out_shape=jax.ShapeDtypeStruct((depth, SUBLANES, 3 * D_MODEL), F32),
        compiler_params=_params(2),
        name="mods",
    )(rows, w_mod, b_mod.reshape(depth, 1, 3 * D_MODEL))


def _chunk_rows(x, offset):
    return jnp.concatenate(
        [jnp.broadcast_to(x[c * GLA_CHUNK + offset:c * GLA_CHUNK + offset + 1, :], (GLA_CHUNK, x.shape[1]))
         for c in range(CHUNKS)], axis=0)


def _even_in_kernel(ctx_ref, x_ref, mod_ref, g_ref, b2_ref, cw_ref,
                    lower_in, upper_in, wqk_in, wv_in, wga_in, wa_in, w2_in, wcb_in, wcc_in, wcx_in, wgb_in,
                    of_ref, qib_ref, kib_ref, keb_ref, decb_ref, v_ref, sga_ref, y_ref,
                    lower_ref, upper_ref, wqk_ref, wv_ref, wga_ref, wa_ref, w2_ref, wcb_ref, wcc_ref, wcx_ref,
                    wgb_ref, qif_ref, kif_ref, kef_ref, decf_ref, st_ref):
    t = pl.program_id(1)

    @pl.when(t == 0)
    def _():
        st_ref[...] = jnp.zeros_like(st_ref)

    @pl.when(_first_step())
    def _():
        for src, dst in ((lower_in, lower_ref), (upper_in, upper_ref), (wqk_in, wqk_ref), (wv_in, wv_ref),
                         (wga_in, wga_ref), (wa_in, wa_ref), (w2_in, w2_ref), (wcb_in, wcb_ref),
                         (wcc_in, wcc_ref), (wcx_in, wcx_ref), (wgb_in, wgb_ref)):
            dst[...] = src[...].astype(BF16)

    xt = jnp.where(t == 0, ctx_ref[...], x_ref[...])
    h = _modulated_norm(xt, g_ref[...], mod_ref[0:1, :], mod_ref[1:2, :])
    hb = h.astype(BF16)

    a_lr = _dot(hb, wa_ref[...])
    z = _dot(a_lr.astype(BF16), w2_ref[...]) + b2_ref[...]
    qk = _dot(hb, wqk_ref[...])
    v_ref[...] = _dot(hb, wv_ref[...]).astype(BF16)

    lg = _log_sigmoid(z) * (1.0 / GLA_TAU)
    lgb = lg.astype(BF16)
    cs_f = _dot(lower_ref[...], lgb[:, :GLA_QK])
    cs_b = _dot(upper_ref[...], lgb[:, GLA_QK:])
    tot_f = _chunk_rows(cs_f, GLA_CHUNK - 1)
    tot_b = _chunk_rows(cs_b, 0)
    c_c = _dot(hb, wcc_ref[...])

    q = qk[:, :GLA_QK] * (GLA_DK ** -0.5)
    k = qk[:, GLA_QK:]
    qif_ref[...] = (q * jnp.exp(cs_f)).astype(BF16)
    kif_ref[...] = (k * jnp.exp(-cs_f)).astype(BF16)
    kef_ref[...] = (k * jnp.exp(tot_f - cs_f)).astype(BF16)
    qib_ref[...] = (q * jnp.exp(cs_b)).astype(BF16)
    kib_ref[...] = (k * jnp.exp(-cs_b)).astype(BF16)
    keb_ref[...] = (k * jnp.exp(tot_b - cs_b)).astype(BF16)
    for c in range(CHUNKS):
        r = c * GLA_CHUNK
        decf_ref[c:c + 1, :] = jnp.exp(tot_f[r:r + 1, :])
        decb_ref[c:c + 1, :] = jnp.exp(tot_b[r:r + 1, :])

    def emit(rows, vcols, o):
        of_ref[rows, vcols] = o.astype(BF16)

    mask = _gla_mask(reverse=False)
    gla = (qif_ref, kif_ref, kef_ref, v_ref)
    close = lambda hd, opened: _gla_head_close(hd, opened, qif_ref, v_ref, decf_ref, st_ref, emit, reverse=False)
    opened = _gla_head_open(0, *gla, mask)
    zz = c_c * _dot(hb, wcx_ref[...])
    close(0, opened)
    opened = _gla_head_open(1, *gla, mask)
    c_b = _dot(hb, wcb_ref[...])
    close(1, opened)
    opened = _gla_head_open(2, *gla, mask)
    g_b = _dot(hb, wgb_ref[...])
    close(2, opened)
    opened = _gla_head_open(3, *gla, mask)
    sga_ref[...] = _silu(_dot(hb, wga_ref[...])).astype(BF16)
    close(3, opened)

    pos = lax.broadcasted_iota(jnp.int32, (TM, SC_WIDTH), 0)
    row_len = jnp.where(t == 0, TM, GRID_W)
    in_row = pos & (row_len - 1)
    z_prev = jnp.where(in_row != 0, pltpu.roll(zz, 1, 0), 0.0)
    z_next = jnp.where(in_row != row_len - 1, pltpu.roll(zz, TM - 1, 0), 0.0)
    zc = cw_ref[0:1, :] * z_prev + cw_ref[1:2, :] * zz + cw_ref[2:3, :] * z_next
    y_ref[...] = (c_b * zc * _silu(g_b)).astype(BF16)


def _chunk_triangles():
    r = lax.broadcasted_iota(jnp.int32, (TM, TM), 0)
    c = lax.broadcasted_iota(jnp.int32, (TM, TM), 1)
    same = (r // GLA_CHUNK) == (c // GLA_CHUNK)
    return (same & (c <= r)).astype(BF16), (same & (c >= r)).astype(BF16)


def _ctx_spec():
    return pl.BlockSpec((None, TM, D_MODEL), lambda b, t: (b, 0, 0))


def _latent_spec(tile, park):
    return pl.BlockSpec((None, TM, D_MODEL),
                        lambda b, t: (b, jnp.where(tile(t) == 0, park, tile(t) - 1), 0))


def _even_in(ctx, x, mods, rows, mats, nt):
    bsz = x.shape[0]
    tok = lambda n: pl.BlockSpec((None, TM, n), lambda b, t: (b, t, 0))
    dec = pl.BlockSpec((None, None, CHUNKS, GLA_QK), lambda b, t: (b, t, 0, 0))
    tok_shape = lambda n: jax.ShapeDtypeStruct((bsz, nt * TM, n), BF16)
    dec_shape = jax.ShapeDtypeStruct((bsz, nt, CHUNKS, GLA_QK), F32)
    mats = _chunk_triangles() + tuple(mats)
    consts = tuple(rows) + mats
    return pl.pallas_call(
        _even_in_kernel,
        grid=(bsz, nt),
        in_specs=[
            _ctx_spec(), _latent_spec(lambda t: t, 0), _mod_spec(0, bsz, lambda t: t),
        ] + [_const_spec(w) for w in consts],
        out_specs=[tok(GLA_V), tok(GLA_QK), tok(GLA_QK), tok(GLA_QK), dec,
                   tok(GLA_V), tok(GLA_V), tok(SC_WIDTH)],
        out_shape=[tok_shape(GLA_V), tok_shape(GLA_QK), tok_shape(GLA_QK), tok_shape(GLA_QK), dec_shape,
                   tok_shape(GLA_V), tok_shape(GLA_V), tok_shape(SC_WIDTH)],
        scratch_shapes=[pltpu.VMEM(m.shape, BF16) for m in mats]
        + [pltpu.VMEM((TM, GLA_QK), BF16)] * 3
        + [pltpu.VMEM((CHUNKS, GLA_QK), F32), pltpu.VMEM((GLA_HEADS, GLA_DV, GLA_DK), F32)],
        compiler_params=_params(2),
        name="even_in",
    )(ctx, x, mods, *[_operand(w) for w in consts])


def _chunk_slice(ch):
    return slice(ch * GLA_CHUNK, (ch + 1) * GLA_CHUNK)


def _gla_mask(reverse):
    r = lax.broadcasted_iota(jnp.int32, (TM, TM), 0)
    c = lax.broadcasted_iota(jnp.int32, (TM, TM), 1)
    same_chunk = (r // GLA_CHUNK) == (c // GLA_CHUNK)
    return same_chunk & ((c >= r) if reverse else (c <= r))


def _gla_head_open(hd, qi_ref, ki_ref, ke_ref, v_ref, mask):
    kcols = slice(hd * GLA_DK, (hd + 1) * GLA_DK)
    vcols = slice(hd * GLA_DV, (hd + 1) * GLA_DV)
    incr = [_dot_tn(v_ref[_chunk_slice(ch), vcols], ke_ref[_chunk_slice(ch), kcols]) for ch in range(CHUNKS)]
    scores = jnp.where(mask, _dot_nt(qi_ref[:, kcols], ki_ref[:, kcols]), 0.0).astype(BF16)
    return incr, scores


def _gla_head_close(hd, opened, qi_ref, v_ref, dec_ref, st_ref, emit, reverse):
    incr, scores = opened
    kcols = slice(hd * GLA_DK, (hd + 1) * GLA_DK)
    vcols = slice(hd * GLA_DV, (hd + 1) * GLA_DV)
    state = st_ref[hd]
    before = {}
    for ch in (range(CHUNKS - 1, -1, -1) if reverse else range(CHUNKS)):
        before[ch] = state.astype(BF16)
        state = dec_ref[ch:ch + 1, kcols] * state + incr[ch]
    st_ref[hd] = state
    intra = _dot(scores, v_ref[:, vcols])
    for ch in range(CHUNKS):
        rows = _chunk_slice(ch)
        emit(rows, vcols, intra[rows, :] + _dot_nt(qi_ref[rows, kcols], before[ch]))


def _gla_tile(qi_ref, ki_ref, ke_ref, v_ref, dec_ref, st_ref, emit, reverse):
    mask = _gla_mask(reverse)
    for hd in range(GLA_HEADS):
        opened = _gla_head_open(hd, qi_ref, ki_ref, ke_ref, v_ref, mask)
        _gla_head_close(hd, opened, qi_ref, v_ref, dec_ref, st_ref, emit, reverse)


def _gla_bwd_kernel(qi_ref, ki_ref, ke_ref, v_ref, dec_ref, of_ref, sga_ref, y_ref, ctx_ref, x_ref,
                    mod_ref, gg_ref, wo_in, wy_in, out_ref, st_ref, ob_ref, wo_ref, wy_ref):
    j = pl.program_id(1)

    @pl.when(_first_step())
    def _():
        wo_ref[...] = wo_in[...].astype(BF16)
        wy_ref[...] = wy_in[...].astype(BF16)

    @pl.when(j == 0)
    def _():
        st_ref[...] = jnp.zeros_like(st_ref)

    proj_y = _dot(y_ref[...], wy_ref[...])

    def emit(rows, vcols, o):
        ob_ref[rows, vcols] = o

    _gla_tile(qi_ref, ki_ref, ke_ref, v_ref, dec_ref, st_ref, emit, reverse=True)

    heads = []
    for hd in range(GLA_HEADS):
        vcols = slice(hd * GLA_DV, (hd + 1) * GLA_DV)
        o = of_ref[:, vcols].astype(F32) + ob_ref[:, vcols]
        ms = jnp.mean(o * o, axis=-1, keepdims=True)
        on = o * lax.rsqrt(ms + EPS) * gg_ref[...]
        heads.append((on * sga_ref[:, vcols].astype(F32)).astype(BF16))
    inner = jnp.concatenate(heads, axis=-1)
    proj = _dot(inner, wo_ref[...]) + proj_y
    out_ref[...] = jnp.where(j == 0, ctx_ref[...], x_ref[...]) + mod_ref[2:3, :] * proj


def _bwd_tile(nt):
    return lambda j: jnp.where(j == 0, 0, nt - j)


def _gla_bwd(qi, ki, ke, v, dec, o_f, sga, y, ctx, x, mods, gg, wo, wy, nt):
    bsz = qi.shape[0]
    tile = _bwd_tile(nt)
    tok = lambda n: pl.BlockSpec((None, TM, n), lambda b, j: (b, tile(j), 0))
    return pl.pallas_call(
        _gla_bwd_kernel,
        grid=(bsz, nt),
        in_specs=[tok(GLA_QK), tok(GLA_QK), tok(GLA_QK), tok(GLA_V),
                  pl.BlockSpec((None, None, CHUNKS, GLA_QK), lambda b, j: (b, tile(j), 0, 0)),
                  tok(GLA_V), tok(GLA_V), tok(SC_WIDTH), _ctx_spec(), _latent_spec(tile, nt - 2),
                  _mod_spec(0, bsz, tile), _const_spec(gg), _const_spec(wo), _const_spec(wy)],
        out_specs=tok(D_MODEL),
        out_shape=jax.ShapeDtypeStruct((bsz, nt * TM, D_MODEL), F32),
        scratch_shapes=[pltpu.VMEM((GLA_HEADS, GLA_DV, GLA_DK), F32),
                        pltpu.VMEM((TM, GLA_V), F32),
                        pltpu.VMEM(wo.shape, BF16),
                        pltpu.VMEM(wy.shape, BF16)],
        compiler_params=_params(2),
        name="gla_bwd",
    )(qi, ki, ke, v, dec, o_f, sga, y, ctx, x, mods, gg, _operand(wo), _operand(wy))


def _to_segment_major(src_ref, stage_ref, width):
    slabs = width // LANES
    for n in range(slabs):
        for s in range(SUBLANES):
            stage_ref[n, s * SEG_PITCH:s * SEG_PITCH + SEG_LEN, :] = (
                src_ref[s * SEG_LEN:(s + 1) * SEG_LEN, n * LANES:(n + 1) * LANES])
    groups = [
        jnp.concatenate([stage_ref[n, pl.ds(j, SUBLANES, stride=SEG_PITCH), :] for n in range(slabs)], axis=1)
        for j in range(SEG_LEN)]
    return jnp.concatenate(groups, axis=0)


def _from_segment_major(val, stage_ref, dst_ref, width):
    slabs = width // LANES
    for j in range(SEG_LEN):
        for n in range(slabs):
            stage_ref[n, pl.ds(j, SUBLANES, stride=SEG_PITCH), :] = (
                val[j * SUBLANES:(j + 1) * SUBLANES, n * LANES:(n + 1) * LANES])
    for n in range(slabs):
        for s in range(SUBLANES):
            dst_ref[s * SEG_LEN:(s + 1) * SEG_LEN, n * LANES:(n + 1) * LANES] = (
                stage_ref[n, s * SEG_PITCH:s * SEG_PITCH + SEG_LEN, :])


def _rg_conv(xr, halo_ref, rows_ref, cols, reverse):
    row = lax.broadcasted_iota(jnp.int32, (SUBLANES, xr.shape[1]), 0)
    edge = []
    for g in range(RG_CONV - 1):
        grp = slice(g * SUBLANES, (g + 1) * SUBLANES)
        if reverse:
            edge.append(jnp.where(row == SUBLANES - 1, pltpu.roll(halo_ref[grp, cols], SUBLANES - 1, 0),
                                  pltpu.roll(xr[grp, :], SUBLANES - 1, 0)))
        else:
            cur = xr[TM - (RG_CONV - 1 - g) * SUBLANES:TM - (RG_CONV - 2 - g) * SUBLANES, :]
            edge.append(jnp.where(row == 0, pltpu.roll(halo_ref[grp, cols], 1, 0), pltpu.roll(cur, 1, 0)))
    tap_w = lambda j: rows_ref[ROW_CONV_W + j:ROW_CONV_W + j + 1, cols]
    xc = rows_ref[ROW_CONV_B:ROW_CONV_B + 1, cols] + tap_w(RG_CONV - 1) * xr
    for k in range(1, RG_CONV):
        if reverse:
            tap = jnp.concatenate([xr[k * SUBLANES:, :]] + edge[:k], axis=0)
        else:
            tap = jnp.concatenate(edge[RG_CONV - 1 - k:] + [xr[:TM - k * SUBLANES, :]], axis=0)
        xc = xc + tap_w(RG_CONV - 1 - k) * tap
    halo_ref[:, cols] = xr[:HALO_ROWS, :] if reverse else xr[TM - HALO_ROWS:, :]
    return xc


def _first_step():
    return (pl.program_id(0) == 0) & (pl.program_id(1) == 0)


def _rg_setup(lam_ref, ba_ref, bx_ref, cb_ref, cw_ref, wa_ref, wx_ref, rows_ref, wg_s, copies):
    wg_s[:, :, :RG_BLOCK_W] = wa_ref[...].astype(BF16)
    wg_s[:, :, RG_BLOCK_W:] = wx_ref[...].astype(BF16)
    rows_ref[ROW_DECAY:ROW_DECAY + 1, :] = (-0.5 * RG_C * LOG2_E) * _softplus(-lam_ref[...])
    rows_ref[ROW_BIAS_A:ROW_BIAS_A + 1, :] = 0.5 * ba_ref[...]
    rows_ref[ROW_BIAS_X:ROW_BIAS_X + 1, :] = 0.5 * bx_ref[...]
    rows_ref[ROW_CONV_B:ROW_CONV_B + 1, :] = 0.5 * cb_ref[...]
    rows_ref[ROW_CONV_W:ROW_CONV_W + RG_CONV, :] = 0.5 * cw_ref[...]
    for src_ref, dst_ref in copies:
        dst_ref[...] = src_ref[...].astype(BF16)


def _rg_gates_scan(xc, first_block, wg_ref, rows_ref, carry_ref, reverse):
    row = lax.broadcasted_iota(jnp.int32, (SUBLANES, RG_BLOCK_W), 0)
    shift = SUBLANES - 1 if reverse else 1
    steps = range(SEG_LEN - 1, -1, -1) if reverse else range(SEG_LEN)
    out = []
    for i in range(xc.shape[1] // RG_BLOCK_W):
        n = first_block + i
        cols = slice(n * RG_BLOCK_W, (n + 1) * RG_BLOCK_W)
        xb = xc[:, i * RG_BLOCK_W:(i + 1) * RG_BLOCK_W]
        gates = _dot(xb.astype(BF16), wg_ref[n])
        th_r = jnp.tanh(gates[:, :RG_BLOCK_W] + rows_ref[ROW_BIAS_A:ROW_BIAS_A + 1, cols])
        th_i = jnp.tanh(gates[:, RG_BLOCK_W:] + rows_ref[ROW_BIAS_X:ROW_BIAS_X + 1, cols])
        half_decay = rows_ref[ROW_DECAY:ROW_DECAY + 1, cols]
        a = jnp.exp2(th_r * half_decay + half_decay)
        y = 1.0 - a * a
        root = y * lax.rsqrt(jnp.maximum(y, F32_TINY))
        u = root * ((th_i + 1.0) * xb)

        h = jnp.zeros((SUBLANES, RG_BLOCK_W), F32)
        prod = jnp.ones((SUBLANES, RG_BLOCK_W), F32)
        local, prods = {}, {}
        for j in steps:
            grp = slice(j * SUBLANES, (j + 1) * SUBLANES)
            h = a[grp, :] * h + u[grp, :]
            prod = prod * a[grp, :]
            local[j], prods[j] = h, prod

        entering = carry_ref[:, cols]
        for s in range(SUBLANES - 1):
            nxt = pltpu.roll(prod * entering + h, shift, 0)
            target = SUBLANES - 2 - s if reverse else s + 1
            entering = jnp.where(row == target, nxt, entering)
        carry_ref[:, cols] = pltpu.roll(prod * entering + h, shift, 0)

        out.append(jnp.concatenate([local[j] + prods[j] * entering for j in range(SEG_LEN)], axis=0))
    return jnp.concatenate(out, axis=1)


def _odd_fwd_kernel(xx_ref, mod_ref, g_ref, wx_ref, cw_ref, cb_ref, ga_ref, gx_ref, ba_ref, bx_ref, lam_ref,
                    xr_ref, hf_ref, stage_ref, halo_ref, carry_ref, rows_ref, wx_s, wg_s):
    t = pl.program_id(1)

    @pl.when(_first_step())
    def _():
        _rg_setup(lam_ref, ba_ref, bx_ref, cb_ref, cw_ref, ga_ref, gx_ref, rows_ref, wg_s, ((wx_ref, wx_s),))

    @pl.when(t == 0)
    def _():
        carry_ref[...] = jnp.zeros_like(carry_ref)

    @pl.when(t <= 1)
    def _():
        halo_ref[...] = jnp.zeros_like(halo_ref)

    x = _to_segment_major(xx_ref, stage_ref, D_MODEL)
    hb = _modulated_norm(x, g_ref[...], mod_ref[0:1, :], mod_ref[1:2, :]).astype(BF16)
    chunks = RG_WIDTH // RG_CHUNK_W
    project = lambda m: _dot(hb, wx_s[:, m * RG_CHUNK_W:(m + 1) * RG_CHUNK_W])
    xr_next = project(0)
    for m in range(chunks):
        cols = slice(m * RG_CHUNK_W, (m + 1) * RG_CHUNK_W)
        xr = xr_next
        if m + 1 < chunks:
            xr_next = project(m + 1)
        xr_ref[:, cols] = xr.astype(BF16)
        xc = _rg_conv(xr, halo_ref, rows_ref, cols, reverse=False)
        hf_ref[:, cols] = _rg_gates_scan(xc, m * (RG_CHUNK_W // RG_BLOCK_W), wg_s, rows_ref, carry_ref,
                                         reverse=False).astype(BF16)


def _odd_fwd(xx, mods, g, wx, cw, cb, ga, gx, ba, bx, lam, nt):
    bsz = xx.shape[0]
    tok = lambda n: pl.BlockSpec((None, TM, n), lambda b, t: (b, t, 0))
    consts = (g, wx, cw, cb, ga, gx, ba, bx, lam)
    gate_scratch = pltpu.VMEM((RG_BLOCKS, RG_BLOCK_W, 2 * RG_BLOCK_W), BF16)
    wide = jax.ShapeDtypeStruct((bsz, nt * TM, RG_WIDTH), BF16)
    return pl.pallas_call(
        _odd_fwd_kernel,
        grid=(bsz, nt),
        in_specs=[tok(D_MODEL), _mod_spec(1, bsz, lambda t: t)] + [_const_spec(w) for w in consts],
        out_specs=[tok(RG_WIDTH), tok(RG_WIDTH)],
        out_shape=[wide, wide],
        scratch_shapes=[pltpu.VMEM((D_MODEL // LANES, SUBLANES * SEG_PITCH, LANES), F32),
                        pltpu.VMEM((HALO_ROWS, RG_WIDTH), F32),
                        pltpu.VMEM((SUBLANES, RG_WIDTH), F32),
                        pltpu.VMEM((RG_ROWS, RG_WIDTH), F32),
                        pltpu.VMEM(wx.shape, BF16),
                        gate_scratch],
        compiler_params=_params(2),
        name="odd_fwd",
    )(xx, mods, *[_operand(w) for w in consts])


def _odd_bwd_kernel(xx_ref, mod_ref, g_ref, xr_ref, hf_ref, wgate_ref, cw_ref, cb_ref, ga_ref, gx_ref, ba_ref,
                    bx_ref, lam_ref, wout_ref, fg_ref, out_ref,
                    stage_ref, halo_ref, yy_ref, carry_ref, rows_ref, wgate_s, wg_s, wout_s):
    j = pl.program_id(1)

    @pl.when(_first_step())
    def _():
        _rg_setup(lam_ref, ba_ref, bx_ref, cb_ref, cw_ref, ga_ref, gx_ref, rows_ref, wg_s,
                  ((wgate_ref, wgate_s), (wout_ref, wout_s)))

    @pl.when(j == 0)
    def _():
        carry_ref[...] = jnp.zeros_like(carry_ref)

    @pl.when(j <= 1)
    def _():
        halo_ref[...] = jnp.zeros_like(halo_ref)

    x = _to_segment_major(xx_ref, stage_ref, D_MODEL)
    hb = _modulated_norm(x, g_ref[...], mod_ref[0:1, :], mod_ref[1:2, :]).astype(BF16)
    for m in range(RG_WIDTH // RG_CHUNK_W):
        cols = slice(m * RG_CHUNK_W, (m + 1) * RG_CHUNK_W)
        gate = _dot(hb, wgate_s[:, cols])
        xc = _rg_conv(xr_ref[:, cols].astype(F32), halo_ref, rows_ref, cols, reverse=True)
        h_bwd = _rg_gates_scan(xc, m * (RG_CHUNK_W // RG_BLOCK_W), wg_s, rows_ref, carry_ref, reverse=True)
        yy_ref[:, cols] = ((hf_ref[:, cols].astype(F32) + h_bwd) * _silu(gate)).astype(BF16)
    xn = x + mod_ref[2:3, :] * _dot(yy_ref[...], wout_s[...])
    ms = jnp.mean(xn * xn, axis=-1, keepdims=True)
    _from_segment_major(xn * lax.rsqrt(ms + EPS) * fg_ref[...], stage_ref, out_ref, D_MODEL)


def _odd_bwd(xx, mods, g, xr, hf, wgate, cw, cb, ga, gx, ba, bx, lam, wout, fg, nt):
    bsz = xx.shape[0]
    tile = _bwd_tile(nt)
    tok = lambda n: pl.BlockSpec((None, TM, n), lambda b, j: (b, tile(j), 0))
    consts = (wgate, cw, cb, ga, gx, ba, bx, lam, wout, fg)
    gate_scratch = pltpu.VMEM((RG_BLOCKS, RG_BLOCK_W, 2 * RG_BLOCK_W), BF16)
    out_block = lambda b, j: (b, jnp.where(j == 0, nt - 2, nt - 1 - j), 0)
    return pl.pallas_call(
        _odd_bwd_kernel,
        grid=(bsz, nt),
        in_specs=[tok(D_MODEL), _mod_spec(1, bsz, tile), _const_spec(g), tok(RG_WIDTH), tok(RG_WIDTH)]
        + [_const_spec(w) for w in consts],
        out_specs=pl.BlockSpec((None, TM, D_MODEL), out_block),
        out_shape=jax.ShapeDtypeStruct((bsz, (nt - 1) * TM, D_MODEL), F32),
        scratch_shapes=[pltpu.VMEM((D_MODEL // LANES, SUBLANES * SEG_PITCH, LANES), F32),
                        pltpu.VMEM((HALO_ROWS, RG_WIDTH), F32),
                        pltpu.VMEM((TM, RG_WIDTH), BF16),
                        pltpu.VMEM((SUBLANES, RG_WIDTH), F32),
                        pltpu.VMEM((RG_ROWS, RG_WIDTH), F32),
                        pltpu.VMEM(wgate.shape, BF16),
                        gate_scratch,
                        pltpu.VMEM(wout.shape, BF16)],
        compiler_params=_params(2),
        name="odd_bwd",
    )(xx, mods, g, xr, hf, *[_operand(w) for w in consts])


def kernel(x, c, ctx, c_ctx, norm_g, w_mod, b_mod, e_w_in, e_w_a2, e_b_a2, e_gla_g, e_conv_w, e_w_out,
           o_w_in, o_conv_w, o_conv_b, o_w_a, o_b_a, o_w_x, o_b_x, o_lam, o_w_out, final_g):
    bsz, seq, _ = x.shape
    assert ctx.shape[1] == TM and seq % TM == 0 and bsz < SUBLANES
    assert w_mod.shape[0] == 2
    nt = 1 + seq // TM

    rows = jnp.concatenate([c, c_ctx[None, :], jnp.zeros((SUBLANES - bsz - 1, D_MODEL), F32)], axis=0)
    mods = _mods(rows, w_mod, b_mod).reshape(w_mod.shape[0], SUBLANES, 3, D_MODEL)

    w_in = e_w_in[0]
    square = (D_MODEL, D_MODEL)
    offs = {}
    off = 0
    for name, size in (("q", GLA_QK), ("k", GLA_QK), ("v", GLA_V), ("ga", GLA_V), ("af", GLA_LOWRANK),
                       ("ab", GLA_LOWRANK), ("cb", SC_WIDTH), ("cc", SC_WIDTH), ("cx", SC_WIDTH),
                       ("gb", SC_WIDTH)):
        offs[name] = (off, off + size)
        off += size
    cols = lambda lo, hi: w_in[:, lo:hi].astype(BF16)
    assert offs["q"][0] == 0 and offs["v"][0] == D_MODEL and offs["ga"] == (2 * D_MODEL, 3 * D_MODEL)
    wqk, wv, wga = (_Window(w_in, square, (0, i)) for i in range(3))
    wa = jnp.pad(w_in[:, offs["af"][0]:offs["ab"][1]], ((0, 0), (0, LANES - 2 * GLA_LOWRANK))).astype(BF16)
    w2 = jnp.zeros((LANES, 2 * GLA_QK), F32)
    w2 = w2.at[:GLA_LOWRANK, :GLA_QK].set(e_w_a2[0, 0])
    w2 = w2.at[GLA_LOWRANK:2 * GLA_LOWRANK, GLA_QK:].set(e_w_a2[0, 1]).astype(BF16)
    b2 = e_b_a2[0].reshape(1, 2 * GLA_QK)
    mats = (wqk, wv, wga, wa, w2, cols(*offs["cb"]), cols(*offs["cc"]), cols(*offs["cx"]), cols(*offs["gb"]))
    (o_f, qi_b, ki_b, ke_b, dec_b, v, sga, y) = _even_in(
        ctx, x, mods, (norm_g[0:1], b2, e_conv_w[0]), mats, nt)
    w_out = e_w_out[0]
    xx = _gla_bwd(qi_b, ki_b, ke_b, v, dec_b, o_f, sga, y, ctx, x, mods, e_gla_g[0:1],
                  _Window(w_out, square, (0, 0)), _Window(w_out, square, (1, 0)), nt)

    w_in = o_w_in[0]
    half = (D_MODEL, RG_WIDTH)
    wide = lambda a: a.reshape(1, RG_WIDTH)
    xr, h_f = _odd_fwd(xx, mods, norm_g[1:2], _Window(w_in, half, (0, 0)), o_conv_w[0, 0],
                       wide(o_conv_b[0, 0]), o_w_a[0, 0], o_w_x[0, 0],
                       wide(o_b_a[0, 0]), wide(o_b_x[0, 0]), wide(o_lam[0, 0]), nt)
    return _odd_bwd(xx, mods, norm_g[1:2], xr, h_f, _Window(w_in, half, (0, 1)), o_conv_w[0, 1],
                    wide(o_conv_b[0, 1]), o_w_a[0, 1], o_w_x[0, 1],
                    wide(o_b_a[0, 1]), wide(o_b_x[0, 1]), wide(o_lam[0, 1]), o_w_out[0],
                    final_g.reshape(1, D_MODEL), nt)
```

```python
import functools

import jax
import jax.numpy as jnp
from jax import lax
from jax.experimental import pallas as pl
from jax.experimental.pallas import tpu as pltpu

F32 = jnp.float32
BF16 = jnp.bfloat16

D_MODEL = 1024
EPS = 1e-6
GLA_HEADS = 4
GLA_DK = 128
GLA_DV = 256
GLA_QK = GLA_HEADS * GLA_DK
GLA_V = GLA_HEADS * GLA_DV
GLA_LOWRANK = 16
GLA_TAU = 16.0
GLA_CHUNK = 64
GRID_W = 64
SC_WIDTH = D_MODEL
RG_WIDTH = 2 * D_MODEL
RG_BLOCKS = 16
RG_BLOCK_W = 128
RG_C = 8.0
RG_CONV = 4

TM = 256
CHUNKS = TM // GLA_CHUNK
SUBLANES = 8
LANES = 128
VMEM_LIMIT = 56 * 1024 * 1024
SEG_LEN = TM // SUBLANES
SEG_PITCH = SEG_LEN + SUBLANES
HALO_ROWS = (RG_CONV - 1) * SUBLANES
RG_CHUNK_W = 256
ROW_DECAY, ROW_BIAS_A, ROW_BIAS_X, ROW_CONV_B, ROW_CONV_W = 0, 1, 2, 3, 4
RG_ROWS = ROW_CONV_W + RG_CONV
LOG2_E = 1.4426950408889634
F32_TINY = 1.1754943508222875e-38


def _silu(x):
    return x * jax.nn.sigmoid(x)


def _log_sigmoid(z):
    return jnp.minimum(z, 0.0) - jnp.log(1.0 + jnp.exp(-jnp.abs(z)))


def _softplus(z):
    return jnp.maximum(z, 0.0) + jnp.log1p(jnp.exp(-jnp.abs(z)))


def _dot(a, b):
    return jnp.dot(a, b, preferred_element_type=F32)


def _dot_nt(a, b):
    return lax.dot_general(a, b, (((1,), (1,)), ((), ())), preferred_element_type=F32)


def _dot_tn(a, b):
    return lax.dot_general(a, b, (((0,), (0,)), ((), ())), preferred_element_type=F32)


def _modulated_norm(x, g, shift, scale):
    ms = jnp.mean(x * x, axis=-1, keepdims=True)
    return (x * lax.rsqrt(ms + EPS) * g) * (1.0 + scale) + shift


def _params(n_axes):
    return pltpu.CompilerParams(
        dimension_semantics=("arbitrary",) * n_axes,
        vmem_limit_bytes=VMEM_LIMIT)


class _Window:
    def __init__(self, array, block_shape, block_index):
        self.array, self.shape, self.index = array, tuple(block_shape), tuple(block_index)


def _operand(w):
    return w.array if isinstance(w, _Window) else w


def _const_spec(w):
    index = w.index if isinstance(w, _Window) else (0,) * len(w.shape)
    return pl.BlockSpec(w.shape, lambda *_: index, pipeline_mode=pl.Buffered(1))


def _mod_spec(layer, bsz, tile):
    return pl.BlockSpec((None, None, 3, D_MODEL),
                        lambda b, t: (layer, jnp.where(tile(t) == 0, bsz, b), 0, 0))


def _mods_kernel(s_ref, w_ref, b_ref, o_ref):
    s = _silu(s_ref[...])
    o_ref[0] = jnp.dot(s, w_ref[0], preferred_element_type=F32,
                       precision=lax.Precision.HIGHEST) + b_ref[0]


def _mods(rows, w_mod, b_mod):
    depth = w_mod.shape[0]
    nb = 3
    return pl.pallas_call(
        _mods_kernel,
        grid=(depth, nb),
        in_specs=[
            pl.BlockSpec((SUBLANES, D_MODEL), lambda l, j: (0, 0)),
            pl.BlockSpec((1, D_MODEL, D_MODEL), lambda l, j: (l, 0, j)),
            pl.BlockSpec((1, 1, D_MODEL), lambda l, j: (l, 0, j)),
        ],
        out_specs=pl.BlockSpec((1, SUBLANES, D_MODEL), lambda l, j: (l, 0, j)),
        out_shape=jax.ShapeDtypeStruct((depth, SUBLANES, 3 * D_MODEL), F32),
        compiler_params=_params(2),
        name="mods",
    )(rows, w_mod, b_mod.reshape(depth, 1, 3 * D_MODEL))


def _chunk_rows(x, offset):
    return jnp.concatenate(
        [jnp.broadcast_to(x[c * GLA_CHUNK + offset:c * GLA_CHUNK + offset + 1, :], (GLA_CHUNK, x.shape[1]))
         for c in range(CHUNKS)], axis=0)


def _even_in_kernel(ctx_ref, x_ref, mod_ref, g_ref, b2_ref, cw_ref,
                    lower_in, upper_in, wqk_in, wv_in, wga_in, wa_in, w2_in, wcb_in, wcc_in, wcx_in, wgb_in,
                    of_ref, qib_ref, kib_ref, keb_ref, decb_ref, v_ref, sga_ref, y_ref,
                    lower_ref, upper_ref, wqk_ref, wv_ref, wga_ref, wa_ref, w2_ref, wcb_ref, wcc_ref, wcx_ref,
                    wgb_ref, qif_ref, kif_ref, kef_ref, decf_ref, st_ref):
    t = pl.program_id(1)

    @pl.when(t == 0)
    def _():
        st_ref[...] = jnp.zeros_like(st_ref)

    @pl.when(_first_step())
    def _():
        for src, dst in ((lower_in, lower_ref), (upper_in, upper_ref), (wqk_in, wqk_ref), (wv_in, wv_ref),
                         (wga_in, wga_ref), (wa_in, wa_ref), (w2_in, w2_ref), (wcb_in, wcb_ref),
                         (wcc_in, wcc_ref), (wcx_in, wcx_ref), (wgb_in, wgb_ref)):
            dst[...] = src[...]

    xt = jnp.where(t == 0, ctx_ref[...], x_ref[...])
    h = _modulated_norm(xt, g_ref[...], mod_ref[0:1, :], mod_ref[1:2, :])
    hb = h.astype(BF16)

    a_lr = _dot(hb, wa_ref[...])
    z = _dot(a_lr.astype(BF16), w2_ref[...]) + b2_ref[...]
    qk = _dot(hb, wqk_ref[...])
    v_ref[...] = _dot(hb, wv_ref[...]).astype(BF16)

    lg = _log_sigmoid(z) * (1.0 / GLA_TAU)
    lgb = lg.astype(BF16)
    cs_f = _dot(lower_ref[...], lgb[:, :GLA_QK])
    cs_b = _dot(upper_ref[...], lgb[:, GLA_QK:])
    tot_f = _chunk_rows(cs_f, GLA_CHUNK - 1)
    tot_b = _chunk_rows(cs_b, 0)
    c_c = _dot(hb, wcc_ref[...])

    q = qk[:, :GLA_QK] * (GLA_DK ** -0.5)
    k = qk[:, GLA_QK:]
    qif_ref[...] = (q * jnp.exp(cs_f)).astype(BF16)
    kif_ref[...] = (k * jnp.exp(-cs_f)).astype(BF16)
    kef_ref[...] = (k * jnp.exp(tot_f - cs_f)).astype(BF16)
    qib_ref[...] = (q * jnp.exp(cs_b)).astype(BF16)
    kib_ref[...] = (k * jnp.exp(-cs_b)).astype(BF16)
    keb_ref[...] = (k * jnp.exp(tot_b - cs_b)).astype(BF16)
    for c in range(CHUNKS):
        r = c * GLA_CHUNK
        decf_ref[c:c + 1, :] = jnp.exp(tot_f[r:r + 1, :])
        decb_ref[c:c + 1, :] = jnp.exp(tot_b[r:r + 1, :])

    def emit(rows, vcols, o):
        of_ref[rows, vcols] = o.astype(BF16)

    mask = _gla_mask(reverse=False)
    gla = (qif_ref, kif_ref, kef_ref, v_ref)
    close = lambda hd, opened: _gla_head_close(hd, opened, qif_ref, v_ref, decf_ref, st_ref, emit, reverse=False)
    opened = _gla_head_open(0, *gla, mask)
    zz = c_c * _dot(hb, wcx_ref[...])
    close(0, opened)
    opened = _gla_head_open(1, *gla, mask)
    c_b = _dot(hb, wcb_ref[...])
    close(1, opened)
    opened = _gla_head_open(2, *gla, mask)
    g_b = _dot(hb, wgb_ref[...])
    close(2, opened)
    opened = _gla_head_open(3, *gla, mask)
    sga_ref[...] = _silu(_dot(hb, wga_ref[...])).astype(BF16)
    close(3, opened)

    pos = lax.broadcasted_iota(jnp.int32, (TM, SC_WIDTH), 0)
    row_len = jnp.where(t == 0, TM, GRID_W)
    in_row = pos & (row_len - 1)
    z_prev = jnp.where(in_row != 0, pltpu.roll(zz, 1, 0), 0.0)
    z_next = jnp.where(in_row != row_len - 1, pltpu.roll(zz, TM - 1, 0), 0.0)
    zc = cw_ref[0:1, :] * z_prev + cw_ref[1:2, :] * zz + cw_ref[2:3, :] * z_next
    y_ref[...] = (c_b * zc * _silu(g_b)).astype(BF16)


def _chunk_triangles():
    r = lax.broadcasted_iota(jnp.int32, (TM, TM), 0)
    c = lax.broadcasted_iota(jnp.int32, (TM, TM), 1)
    same = (r // GLA_CHUNK) == (c // GLA_CHUNK)
    return (same & (c <= r)).astype(BF16), (same & (c >= r)).astype(BF16)


def _ctx_spec():
    return pl.BlockSpec((None, TM, D_MODEL), lambda b, t: (b, 0, 0))


def _latent_spec(tile, park):
    return pl.BlockSpec((None, TM, D_MODEL),
                        lambda b, t: (b, jnp.where(tile(t) == 0, park, tile(t) - 1), 0))


def _even_in(ctx, x, mods, rows, mats, nt):
    bsz = x.shape[0]
    tok = lambda n: pl.BlockSpec((None, TM, n), lambda b, t: (b, t, 0))
    dec = pl.BlockSpec((None, None, CHUNKS, GLA_QK), lambda b, t: (b, t, 0, 0))
    tok_shape = lambda n: jax.ShapeDtypeStruct((bsz, nt * TM, n), BF16)
    dec_shape = jax.ShapeDtypeStruct((bsz, nt, CHUNKS, GLA_QK), F32)
    mats = _chunk_triangles() + tuple(mats)
    consts = tuple(rows) + mats
    return pl.pallas_call(
        _even_in_kernel,
        grid=(bsz, nt),
        in_specs=[
            _ctx_spec(), _latent_spec(lambda t: t, 0), _mod_spec(0, bsz, lambda t: t),
        ] + [_const_spec(w) for w in consts],
        out_specs=[tok(GLA_V), tok(GLA_QK), tok(GLA_QK), tok(GLA_QK), dec,
                   tok(GLA_V), tok(GLA_V), tok(SC_WIDTH)],
        out_shape=[tok_shape(GLA_V), tok_shape(GLA_QK), tok_shape(GLA_QK), tok_shape(GLA_QK), dec_shape,
                   tok_shape(GLA_V), tok_shape(GLA_V), tok_shape(SC_WIDTH)],
        scratch_shapes=[pltpu.VMEM(m.shape, BF16) for m in mats]
        + [pltpu.VMEM((TM, GLA_QK), BF16)] * 3
        + [pltpu.VMEM((CHUNKS, GLA_QK), F32), pltpu.VMEM((GLA_HEADS, GLA_DV, GLA_DK), F32)],
        compiler_params=_params(2),
        name="even_in",
    )(ctx, x, mods, *[_operand(w) for w in consts])


def _chunk_slice(ch):
    return slice(ch * GLA_CHUNK, (ch + 1) * GLA_CHUNK)


def _gla_mask(reverse):
    r = lax.broadcasted_iota(jnp.int32, (TM, TM), 0)
    c = lax.broadcasted_iota(jnp.int32, (TM, TM), 1)
    same_chunk = (r // GLA_CHUNK) == (c // GLA_CHUNK)
    return same_chunk & ((c >= r) if reverse else (c <= r))


def _gla_head_open(hd, qi_ref, ki_ref, ke_ref, v_ref, mask):
    kcols = slice(hd * GLA_DK, (hd + 1) * GLA_DK)
    vcols = slice(hd * GLA_DV, (hd + 1) * GLA_DV)
    incr = [_dot_tn(v_ref[_chunk_slice(ch), vcols], ke_ref[_chunk_slice(ch), kcols]) for ch in range(CHUNKS)]
    scores = jnp.where(mask, _dot_nt(qi_ref[:, kcols], ki_ref[:, kcols]), 0.0).astype(BF16)
    return incr, scores


def _gla_head_close(hd, opened, qi_ref, v_ref, dec_ref, st_ref, emit, reverse):
    incr, scores = opened
    kcols = slice(hd * GLA_DK, (hd + 1) * GLA_DK)
    vcols = slice(hd * GLA_DV, (hd + 1) * GLA_DV)
    state = st_ref[hd]
    before = {}
    for ch in (range(CHUNKS - 1, -1, -1) if reverse else range(CHUNKS)):
        before[ch] = state.astype(BF16)
        state = dec_ref[ch:ch + 1, kcols] * state + incr[ch]
    st_ref[hd] = state
    intra = _dot(scores, v_ref[:, vcols])
    for ch in range(CHUNKS):
        rows = _chunk_slice(ch)
        emit(rows, vcols, intra[rows, :] + _dot_nt(qi_ref[rows, kcols], before[ch]))


def _gla_tile(qi_ref, ki_ref, ke_ref, v_ref, dec_ref, st_ref, emit, reverse):
    mask = _gla_mask(reverse)
    opened = [_gla_head_open(hd, qi_ref, ki_ref, ke_ref, v_ref, mask) for hd in range(GLA_HEADS)]
    for hd in range(GLA_HEADS):
        _gla_head_close(hd, opened[hd], qi_ref, v_ref, dec_ref, st_ref, emit, reverse)


def _gla_bwd_kernel(qi_ref, ki_ref, ke_ref, v_ref, dec_ref, of_ref, sga_ref, y_ref, ctx_ref, x_ref,
                    mod_ref, gg_ref, wo_in, wy_in, out_ref, st_ref, ob_ref, wo_ref, wy_ref, stage_ref):
    j = pl.program_id(1)

    @pl.when(_first_step())
    def _():
        wo_ref[...] = wo_in[...].astype(BF16)
        wy_ref[...] = wy_in[...].astype(BF16)

    @pl.when(j == 0)
    def _():
        st_ref[...] = jnp.zeros_like(st_ref)

    proj_y = _dot(y_ref[...], wy_ref[...])

    def emit(rows, vcols, o):
        ob_ref[rows, vcols] = o

    _gla_tile(qi_ref, ki_ref, ke_ref, v_ref, dec_ref, st_ref, emit, reverse=True)

    heads = []
    for hd in range(GLA_HEADS):
        vcols = slice(hd * GLA_DV, (hd + 1) * GLA_DV)
        o = of_ref[:, vcols].astype(F32) + ob_ref[:, vcols]
        ms = jnp.mean(o * o, axis=-1, keepdims=True)
        on = o * lax.rsqrt(ms + EPS) * gg_ref[...]
        heads.append((on * sga_ref[:, vcols].astype(F32)).astype(BF16))
    inner = jnp.concatenate(heads, axis=-1)
    proj = _dot(inner, wo_ref[...]) + proj_y
    ob_ref[...] = jnp.where(j == 0, ctx_ref[...], x_ref[...]) + mod_ref[2:3, :] * proj
    out_ref[...] = _to_segment_major(ob_ref, stage_ref, D_MODEL)


def _bwd_tile(nt):
    return lambda j: jnp.where(j == 0, 0, nt - j)


def _gla_bwd(qi, ki, ke, v, dec, o_f, sga, y, ctx, x, mods, gg, wo, wy, nt):
    assert GLA_V == D_MODEL
    bsz = qi.shape[0]
    tile = _bwd_tile(nt)
    tok = lambda n: pl.BlockSpec((None, TM, n), lambda b, j: (b, tile(j), 0))
    return pl.pallas_call(
        _gla_bwd_kernel,
        grid=(bsz, nt),
        in_specs=[tok(GLA_QK), tok(GLA_QK), tok(GLA_QK), tok(GLA_V),
                  pl.BlockSpec((None, None, CHUNKS, GLA_QK), lambda b, j: (b, tile(j), 0, 0)),
                  tok(GLA_V), tok(GLA_V), tok(SC_WIDTH), _ctx_spec(), _latent_spec(tile, nt - 2),
                  _mod_spec(0, bsz, tile), _const_spec(gg), _const_spec(wo), _const_spec(wy)],
        out_specs=tok(D_MODEL),
        out_shape=jax.ShapeDtypeStruct((bsz, nt * TM, D_MODEL), F32),
        scratch_shapes=[pltpu.VMEM((GLA_HEADS, GLA_DV, GLA_DK), F32),
                        pltpu.VMEM((TM, GLA_V), F32),
                        pltpu.VMEM(wo.shape, BF16),
                        pltpu.VMEM(wy.shape, BF16),
                        pltpu.VMEM((D_MODEL // LANES, SUBLANES * SEG_PITCH, LANES), F32)],
        compiler_params=_params(2),
        name="gla_bwd",
    )(qi, ki, ke, v, dec, o_f, sga, y, ctx, x, mods, gg, _operand(wo), _operand(wy))


def _to_segment_major(src_ref, stage_ref, width):
    slabs = width // LANES
    for n in range(slabs):
        for s in range(SUBLANES):
            stage_ref[n, s * SEG_PITCH:s * SEG_PITCH + SEG_LEN, :] = (
                src_ref[s * SEG_LEN:(s + 1) * SEG_LEN, n * LANES:(n + 1) * LANES])
    groups = [
        jnp.concatenate([stage_ref[n, pl.ds(j, SUBLANES, stride=SEG_PITCH), :] for n in range(slabs)], axis=1)
        for j in range(SEG_LEN)]
    return jnp.concatenate(groups, axis=0)


def _from_segment_major(val, stage_ref, dst_ref, width):
    slabs = width // LANES
    for j in range(SEG_LEN):
        for n in range(slabs):
            stage_ref[n, pl.ds(j, SUBLANES, stride=SEG_PITCH), :] = (
                val[j * SUBLANES:(j + 1) * SUBLANES, n * LANES:(n + 1) * LANES])
    for n in range(slabs):
        for s in range(SUBLANES):
            dst_ref[s * SEG_LEN:(s + 1) * SEG_LEN, n * LANES:(n + 1) * LANES] = (
                stage_ref[n, s * SEG_PITCH:s * SEG_PITCH + SEG_LEN, :])


def _rg_conv(xr, halo_ref, rows_ref, cols, reverse):
    row = lax.broadcasted_iota(jnp.int32, (SUBLANES, xr.shape[1]), 0)
    edge = []
    for g in range(RG_CONV - 1):
        grp = slice(g * SUBLANES, (g + 1) * SUBLANES)
        if reverse:
            edge.append(jnp.where(row == SUBLANES - 1, pltpu.roll(halo_ref[grp, cols], SUBLANES - 1, 0),
                                  pltpu.roll(xr[grp, :], SUBLANES - 1, 0)))
        else:
            cur = xr[TM - (RG_CONV - 1 - g) * SUBLANES:TM - (RG_CONV - 2 - g) * SUBLANES, :]
            edge.append(jnp.where(row == 0, pltpu.roll(halo_ref[grp, cols], 1, 0), pltpu.roll(cur, 1, 0)))
    tap_w = lambda j: rows_ref[ROW_CONV_W + j:ROW_CONV_W + j + 1, cols]
    xc = rows_ref[ROW_CONV_B:ROW_CONV_B + 1, cols] + tap_w(RG_CONV - 1) * xr
    for k in range(1, RG_CONV):
        if reverse:
            tap = jnp.concatenate([xr[k * SUBLANES:, :]] + edge[:k], axis=0)
        else:
            tap = jnp.concatenate(edge[RG_CONV - 1 - k:] + [xr[:TM - k * SUBLANES, :]], axis=0)
        xc = xc + tap_w(RG_CONV - 1 - k) * tap
    halo_ref[:, cols] = xr[:HALO_ROWS, :] if reverse else xr[TM - HALO_ROWS:, :]
    return xc


def _first_step():
    return (pl.program_id(0) == 0) & (pl.program_id(1) == 0)


def _rg_setup(lam_ref, ba_ref, bx_ref, cb_ref, cw_ref, wa_ref, wx_ref, rows_ref, wg_s, copies):
    wg_s[:, :, :RG_BLOCK_W] = wa_ref[...].astype(BF16)
    wg_s[:, :, RG_BLOCK_W:] = wx_ref[...].astype(BF16)
    rows_ref[ROW_DECAY:ROW_DECAY + 1, :] = (-0.5 * RG_C * LOG2_E) * _softplus(-lam_ref[...])
    rows_ref[ROW_BIAS_A:ROW_BIAS_A + 1, :] = 0.5 * ba_ref[...]
    rows_ref[ROW_BIAS_X:ROW_BIAS_X + 1, :] = 0.5 * bx_ref[...]
    rows_ref[ROW_CONV_B:ROW_CONV_B + 1, :] = 0.5 * cb_ref[...]
    rows_ref[ROW_CONV_W:ROW_CONV_W + RG_CONV, :] = 0.5 * cw_ref[...]
    for src_ref, dst_ref in copies:
        dst_ref[...] = src_ref[...].astype(BF16)


def _rg_gates_scan(xc, first_block, wg_ref, rows_ref, carry_ref, reverse):
    row = lax.broadcasted_iota(jnp.int32, (SUBLANES, RG_BLOCK_W), 0)
    shift = SUBLANES - 1 if reverse else 1
    steps = range(SEG_LEN - 1, -1, -1) if reverse else range(SEG_LEN)
    out = []
    for i in range(xc.shape[1] // RG_BLOCK_W):
        n = first_block + i
        cols = slice(n * RG_BLOCK_W, (n + 1) * RG_BLOCK_W)
        xb = xc[:, i * RG_BLOCK_W:(i + 1) * RG_BLOCK_W]
        gates = _dot(xb.astype(BF16), wg_ref[n])
        th_r = jnp.tanh(gates[:, :RG_BLOCK_W] + rows_ref[ROW_BIAS_A:ROW_BIAS_A + 1, cols])
        th_i = jnp.tanh(gates[:, RG_BLOCK_W:] + rows_ref[ROW_BIAS_X:ROW_BIAS_X + 1, cols])
        half_decay = rows_ref[ROW_DECAY:ROW_DECAY + 1, cols]
        a = jnp.exp2(th_r * half_decay + half_decay)
        y = 1.0 - a * a
        root = y * lax.rsqrt(jnp.maximum(y, F32_TINY))
        u = root * ((th_i + 1.0) * xb)

        h = jnp.zeros((SUBLANES, RG_BLOCK_W), F32)
        prod = jnp.ones((SUBLANES, RG_BLOCK_W), F32)
        local, prods = {}, {}
        for j in steps:
            grp = slice(j * SUBLANES, (j + 1) * SUBLANES)
            h = a[grp, :] * h + u[grp, :]
            prod = prod * a[grp, :]
            local[j], prods[j] = h, prod

        entering = carry_ref[:, cols]
        for s in range(SUBLANES - 1):
            nxt = pltpu.roll(prod * entering + h, shift, 0)
            target = SUBLANES - 2 - s if reverse else s + 1
            entering = jnp.where(row == target, nxt, entering)
        carry_ref[:, cols] = pltpu.roll(prod * entering + h, shift, 0)

        out.append(jnp.concatenate([local[j] + prods[j] * entering for j in range(SEG_LEN)], axis=0))
    return jnp.concatenate(out, axis=1)


def _odd_fwd_kernel(xx_ref, mod_ref, g_ref, wx_ref, cw_ref, cb_ref, ga_ref, gx_ref, ba_ref, bx_ref, lam_ref,
                    xr_ref, hf_ref, hn_ref, halo_ref, carry_ref, rows_ref, wx_s, wg_s):
    t = pl.program_id(1)

    @pl.when(_first_step())
    def _():
        _rg_setup(lam_ref, ba_ref, bx_ref, cb_ref, cw_ref, ga_ref, gx_ref, rows_ref, wg_s, ((wx_ref, wx_s),))

    @pl.when(t == 0)
    def _():
        carry_ref[...] = jnp.zeros_like(carry_ref)

    @pl.when(t <= 1)
    def _():
        halo_ref[...] = jnp.zeros_like(halo_ref)

    hb = _modulated_norm(xx_ref[...], g_ref[...], mod_ref[0:1, :], mod_ref[1:2, :]).astype(BF16)
    hn_ref[...] = hb
    chunks = RG_WIDTH // RG_CHUNK_W
    project = lambda m: _dot(hb, wx_s[:, m * RG_CHUNK_W:(m + 1) * RG_CHUNK_W])
    xr_next = project(0)
    for m in range(chunks):
        cols = slice(m * RG_CHUNK_W, (m + 1) * RG_CHUNK_W)
        xr = xr_next
        if m + 1 < chunks:
            xr_next = project(m + 1)
        xr_ref[:, cols] = xr.astype(BF16)
        xc = _rg_conv(xr, halo_ref, rows_ref, cols, reverse=False)
        hf_ref[:, cols] = _rg_gates_scan(xc, m * (RG_CHUNK_W // RG_BLOCK_W), wg_s, rows_ref, carry_ref,
                                         reverse=False).astype(BF16)


def _odd_fwd(xx, mods, g, wx, cw, cb, ga, gx, ba, bx, lam, nt):
    bsz = xx.shape[0]
    tok = lambda n: pl.BlockSpec((None, TM, n), lambda b, t: (b, t, 0))
    consts = (g, wx, cw, cb, ga, gx, ba, bx, lam)
    gate_scratch = pltpu.VMEM((RG_BLOCKS, RG_BLOCK_W, 2 * RG_BLOCK_W), BF16)
    wide = jax.ShapeDtypeStruct((bsz, nt * TM, RG_WIDTH), BF16)
    return pl.pallas_call(
        _odd_fwd_kernel,
        grid=(bsz, nt),
        in_specs=[tok(D_MODEL), _mod_spec(1, bsz, lambda t: t)] + [_const_spec(w) for w in consts],
        out_specs=[tok(RG_WIDTH), tok(RG_WIDTH), tok(D_MODEL)],
        out_shape=[wide, wide, jax.ShapeDtypeStruct((bsz, nt * TM, D_MODEL), BF16)],
        scratch_shapes=[pltpu.VMEM((HALO_ROWS, RG_WIDTH), F32),
                        pltpu.VMEM((SUBLANES, RG_WIDTH), F32),
                        pltpu.VMEM((RG_ROWS, RG_WIDTH), F32),
                        pltpu.VMEM(wx.shape, BF16),
                        gate_scratch],
        compiler_params=_params(2),
        name="odd_fwd",
    )(xx, mods, *[_operand(w) for w in consts])


def _odd_bwd_kernel(xx_ref, mod_ref, hn_ref, xr_ref, hf_ref, wgate_ref, cw_ref, cb_ref, ga_ref, gx_ref, ba_ref,
                    bx_ref, lam_ref, wout_ref, fg_ref, out_ref,
                    stage_ref, halo_ref, yy_ref, carry_ref, rows_ref, wgate_s, wg_s, wout_s):
    j = pl.program_id(1)

    @pl.when(_first_step())
    def _():
        _rg_setup(lam_ref, ba_ref, bx_ref, cb_ref, cw_ref, ga_ref, gx_ref, rows_ref, wg_s,
                  ((wgate_ref, wgate_s), (wout_ref, wout_s)))

    @pl.when(j == 0)
    def _():
        carry_ref[...] = jnp.zeros_like(carry_ref)

    @pl.when(j <= 1)
    def _():
        halo_ref[...] = jnp.zeros_like(halo_ref)

    hb = hn_ref[...]
    for m in range(RG_WIDTH // RG_CHUNK_W):
        cols = slice(m * RG_CHUNK_W, (m + 1) * RG_CHUNK_W)
        gate = _dot(hb, wgate_s[:, cols])
        xc = _rg_conv(xr_ref[:, cols].astype(F32), halo_ref, rows_ref, cols, reverse=True)
        h_bwd = _rg_gates_scan(xc, m * (RG_CHUNK_W // RG_BLOCK_W), wg_s, rows_ref, carry_ref, reverse=True)
        yy_ref[:, cols] = ((hf_ref[:, cols].astype(F32) + h_bwd) * _silu(gate)).astype(BF16)
    xn = xx_ref[...] + mod_ref[2:3, :] * _dot(yy_ref[...], wout_s[...])
    ms = jnp.mean(xn * xn, axis=-1, keepdims=True)
    _from_segment_major(xn * lax.rsqrt(ms + EPS) * fg_ref[...], stage_ref, out_ref, D_MODEL)


def _odd_bwd(xx, mods, hn, xr, hf, wgate, cw, cb, ga, gx, ba, bx, lam, wout, fg, nt):
    bsz = xx.shape[0]
    tile = _bwd_tile(nt)
    tok = lambda n: pl.BlockSpec((None, TM, n), lambda b, j: (b, tile(j), 0))
    consts = (wgate, cw, cb, ga, gx, ba, bx, lam, wout, fg)
    gate_scratch = pltpu.VMEM((RG_BLOCKS, RG_BLOCK_W, 2 * RG_BLOCK_W), BF16)
    out_block = lambda b, j: (b, jnp.where(j == 0, nt - 2, nt - 1 - j), 0)
    return pl.pallas_call(
        _odd_bwd_kernel,
        grid=(bsz, nt),
        in_specs=[tok(D_MODEL), _mod_spec(1, bsz, tile), tok(D_MODEL), tok(RG_WIDTH), tok(RG_WIDTH)]
        + [_const_spec(w) for w in consts],
        out_specs=pl.BlockSpec((None, TM, D_MODEL), out_block),
        out_shape=jax.ShapeDtypeStruct((bsz, (nt - 1) * TM, D_MODEL), F32),
        scratch_shapes=[pltpu.VMEM((D_MODEL // LANES, SUBLANES * SEG_PITCH, LANES), F32),
                        pltpu.VMEM((HALO_ROWS, RG_WIDTH), F32),
                        pltpu.VMEM((TM, RG_WIDTH), BF16),
                        pltpu.VMEM((SUBLANES, RG_WIDTH), F32),
                        pltpu.VMEM((RG_ROWS, RG_WIDTH), F32),
                        pltpu.VMEM(wgate.shape, BF16),
                        gate_scratch,
                        pltpu.VMEM(wout.shape, BF16)],
        compiler_params=_params(2),
        name="odd_bwd",
    )(xx, mods, hn, xr, hf, *[_operand(w) for w in consts])


def kernel(x, c, ctx, c_ctx, norm_g, w_mod, b_mod, e_w_in, e_w_a2, e_b_a2, e_gla_g, e_conv_w, e_w_out,
           o_w_in, o_conv_w, o_conv_b, o_w_a, o_b_a, o_w_x, o_b_x, o_lam, o_w_out, final_g):
    bsz, seq, _ = x.shape
    assert ctx.shape[1] == TM and seq % TM == 0 and bsz < SUBLANES
    assert w_mod.shape[0] == 2
    nt = 1 + seq // TM

    rows = jnp.concatenate([c, c_ctx[None, :], jnp.zeros((SUBLANES - bsz - 1, D_MODEL), F32)], axis=0)
    mods = _mods(rows, w_mod, b_mod).reshape(w_mod.shape[0], SUBLANES, 3, D_MODEL)

    w_in = e_w_in[0]
    w_in_bf16 = w_in.astype(BF16)
    square = (D_MODEL, D_MODEL)
    offs = {}
    off = 0
    for name, size in (("q", GLA_QK), ("k", GLA_QK), ("v", GLA_V), ("ga", GLA_V), ("af", GLA_LOWRANK),
                       ("ab", GLA_LOWRANK), ("cb", SC_WIDTH), ("cc", SC_WIDTH), ("cx", SC_WIDTH),
                       ("gb", SC_WIDTH)):
        offs[name] = (off, off + size)
        off += size
    cols = lambda lo, hi: w_in[:, lo:hi].astype(BF16)
    assert offs["q"][0] == 0 and offs["v"][0] == D_MODEL and offs["ga"] == (2 * D_MODEL, 3 * D_MODEL)
    wqk, wv, wga = (_Window(w_in_bf16, square, (0, i)) for i in range(3))
    wa = jnp.pad(w_in[:, offs["af"][0]:offs["ab"][1]], ((0, 0), (0, LANES - 2 * GLA_LOWRANK))).astype(BF16)
    w2 = jnp.zeros((LANES, 2 * GLA_QK), F32)
    w2 = w2.at[:GLA_LOWRANK, :GLA_QK].set(e_w_a2[0, 0])
    w2 = w2.at[GLA_LOWRANK:2 * GLA_LOWRANK, GLA_QK:].set(e_w_a2[0, 1]).astype(BF16)
    b2 = e_b_a2[0].reshape(1, 2 * GLA_QK)
    mats = (wqk, wv, wga, wa, w2, cols(*offs["cb"]), cols(*offs["cc"]), cols(*offs["cx"]), cols(*offs["gb"]))
    (o_f, qi_b, ki_b, ke_b, dec_b, v, sga, y) = _even_in(
        ctx, x, mods, (norm_g[0:1], b2, e_conv_w[0]), mats, nt)
    w_out = e_w_out[0]
    xx = _gla_bwd(qi_b, ki_b, ke_b, v, dec_b, o_f, sga, y, ctx, x, mods, e_gla_g[0:1],
                  _Window(w_out, square, (0, 0)), _Window(w_out, square, (1, 0)), nt)

    w_in = o_w_in[0]
    half = (D_MODEL, RG_WIDTH)
    wide = lambda a: a.reshape(1, RG_WIDTH)
    xr, h_f, hn = _odd_fwd(xx, mods, norm_g[1:2], _Window(w_in, half, (0, 0)), o_conv_w[0, 0],
                           wide(o_conv_b[0, 0]), o_w_a[0, 0], o_w_x[0, 0],
                           wide(o_b_a[0, 0]), wide(o_b_x[0, 0]), wide(o_lam[0, 0]), nt)
    return _odd_bwd(xx, mods, hn, xr, h_f, _Window(w_in, half, (0, 1)), o_conv_w[0, 1],
                    wide(o_conv_b[0, 1]), o_w_a[0, 1], o_w_x[0, 1],
                    wide(o_b_a[0, 1]), wide(o_b_x[0, 1]), wide(o_lam[0, 1]), o_w_out[0],
                    final_g.reshape(1, D_MODEL), nt)
```

```python
import jax
import jax.numpy as jnp
from jax import lax
from jax.experimental import pallas as pl
from jax.experimental.pallas import tpu as pltpu

F32 = jnp.float32
BF16 = jnp.bfloat16

D_MODEL = 1024
EPS = 1e-6
GLA_HEADS = 4
GLA_DK = 128
GLA_DV = 256
GLA_QK = GLA_HEADS * GLA_DK
GLA_V = GLA_HEADS * GLA_DV
GLA_LOWRANK = 16
GLA_TAU = 16.0
GLA_CHUNK = 64
GRID_W = 64
SC_WIDTH = D_MODEL
RG_WIDTH = 2 * D_MODEL
RG_BLOCKS = 16
RG_BLOCK_W = 128
RG_C = 8.0
RG_CONV = 4

TM = 256
CHUNKS = TM // GLA_CHUNK
SUBLANES = 8
LANES = 128
VMEM_LIMIT = 56 * 1024 * 1024
SEG_LEN = TM // SUBLANES
SEG_PITCH = SEG_LEN + SUBLANES
HALO_ROWS = (RG_CONV - 1) * SUBLANES
RG_CHUNK_W = 256
ROW_DECAY, ROW_BIAS_A, ROW_BIAS_X, ROW_CONV_B, ROW_CONV_W = 0, 1, 2, 3, 4
RG_ROWS = ROW_CONV_W + RG_CONV
LOG2_E = 1.4426950408889634
F32_TINY = 1.1754943508222875e-38


def _silu(x):
    return x * jax.nn.sigmoid(x)


def _log_sigmoid(z):
    return jnp.minimum(z, 0.0) - jnp.log(1.0 + jnp.exp(-jnp.abs(z)))


def _softplus(z):
    return jnp.maximum(z, 0.0) + jnp.log1p(jnp.exp(-jnp.abs(z)))


def _dot(a, b):
    return jnp.dot(a, b, preferred_element_type=F32)


def _dot_nt(a, b):
    return lax.dot_general(a, b, (((1,), (1,)), ((), ())), preferred_element_type=F32)


def _dot_tn(a, b):
    return lax.dot_general(a, b, (((0,), (0,)), ((), ())), preferred_element_type=F32)


def _modulated_norm(x, g, shift, scale):
    ms = jnp.mean(x * x, axis=-1, keepdims=True)
    return (x * lax.rsqrt(ms + EPS) * g) * (1.0 + scale) + shift


def _params(n_axes):
    return pltpu.CompilerParams(
        dimension_semantics=("arbitrary",) * n_axes,
        vmem_limit_bytes=VMEM_LIMIT)


def _first_step():
    return (pl.program_id(0) == 0) & (pl.program_id(1) == 0)


class _Window:
    def __init__(self, array, block_shape, block_index):
        self.array, self.shape, self.index = array, tuple(block_shape), tuple(block_index)


def _operand(w):
    return w.array if isinstance(w, _Window) else w


def _const_spec(w):
    index = w.index if isinstance(w, _Window) else (0,) * len(w.shape)
    return pl.BlockSpec(w.shape, lambda *_: index, pipeline_mode=pl.Buffered(1))


def _mod_spec(layer, bsz, tile):
    return pl.BlockSpec((None, None, 3, D_MODEL),
                        lambda b, t: (layer, jnp.where(tile(t) == 0, bsz, b), 0, 0))


def _mods_kernel(c_ref, cctx_ref, w_ref, b_ref, o_ref, s_ref):
    bsz = c_ref.shape[0]
    s_ref[...] = jnp.zeros_like(s_ref)
    s_ref[0:bsz, :] = c_ref[...]
    s_ref[bsz:bsz + 1, :] = cctx_ref[...]
    s = _silu(s_ref[...])
    o_ref[0] = jnp.dot(s, w_ref[0], preferred_element_type=F32,
                       precision=lax.Precision.HIGHEST) + b_ref[0]


def _mods(c, c_ctx, w_mod, b_mod):
    depth = w_mod.shape[0]
    nb = 3
    return pl.pallas_call(
        _mods_kernel,
        grid=(depth, nb),
        in_specs=[
            pl.BlockSpec(c.shape, lambda l, j: (0, 0)),
            pl.BlockSpec((1, D_MODEL), lambda l, j: (0, 0)),
            pl.BlockSpec((1, D_MODEL, D_MODEL), lambda l, j: (l, 0, j)),
            pl.BlockSpec((1, 1, D_MODEL), lambda l, j: (l, 0, j)),
        ],
        out_specs=pl.BlockSpec((1, SUBLANES, D_MODEL), lambda l, j: (l, 0, j)),
        out_shape=jax.ShapeDtypeStruct((depth, SUBLANES, 3 * D_MODEL), F32),
        scratch_shapes=[pltpu.VMEM((SUBLANES, D_MODEL), F32)],
        compiler_params=_params(2),
        name="mods",
    )(c, c_ctx.reshape(1, D_MODEL), w_mod, b_mod.reshape(depth, 1, 3 * D_MODEL))


EVEN_WIDE = 2 * GLA_QK + 2 * GLA_V
EVEN_LOWRANK = 2 * GLA_LOWRANK
EVEN_PACKED = EVEN_WIDE + 4 * SC_WIDTH
PACK_ROWS = 128


def _pack_in_proj_kernel(w_ref, main_ref, lowrank_ref):
    tail = EVEN_WIDE + EVEN_LOWRANK
    main_ref[:, :EVEN_WIDE] = w_ref[:, :EVEN_WIDE].astype(BF16)
    main_ref[:, EVEN_WIDE:] = w_ref[:, tail:].astype(BF16)
    lowrank_ref[...] = jnp.concatenate(
        [w_ref[:, EVEN_WIDE:tail], jnp.zeros((PACK_ROWS, LANES - EVEN_LOWRANK), F32)], axis=1).astype(BF16)


def _pack_in_proj(w_in):
    rows, width = w_in.shape
    assert width == EVEN_PACKED + EVEN_LOWRANK and rows % PACK_ROWS == 0
    return pl.pallas_call(
        _pack_in_proj_kernel,
        grid=(rows // PACK_ROWS,),
        in_specs=[pl.BlockSpec((PACK_ROWS, width), lambda i: (i, 0))],
        out_specs=[pl.BlockSpec((PACK_ROWS, EVEN_PACKED), lambda i: (i, 0)),
                   pl.BlockSpec((PACK_ROWS, LANES), lambda i: (i, 0))],
        out_shape=[jax.ShapeDtypeStruct((rows, EVEN_PACKED), BF16), jax.ShapeDtypeStruct((rows, LANES), BF16)],
        compiler_params=_params(1),
        name="pack_in_proj",
    )(w_in)

def _chunk_rows(x, offset):
    return jnp.concatenate(
        [jnp.broadcast_to(x[c * GLA_CHUNK + offset:c * GLA_CHUNK + offset + 1, :], (GLA_CHUNK, x.shape[1]))
         for c in range(CHUNKS)], axis=0)


def _even_in_kernel(ctx_ref, x_ref, mod_ref, g_ref, b2_ref, cw_ref, wa2_ref,
                    wqk_in, wv_in, wga_in, wa_in, wcb_in, wcc_in, wcx_in, wgb_in,
                    of_ref, qib_ref, kib_ref, keb_ref, decb_ref, v_ref, sga_ref, y_ref,
                    wqk_ref, wv_ref, wga_ref, wa_ref, wcb_ref, wcc_ref, wcx_ref, wgb_ref,
                    lower_ref, upper_ref, w2_ref, qif_ref, kif_ref, kef_ref, decf_ref, st_ref):
    t = pl.program_id(1)

    @pl.when(t == 0)
    def _():
        st_ref[...] = jnp.zeros_like(st_ref)

    @pl.when(_first_step())
    def _():
        for src, dst in ((wqk_in, wqk_ref), (wv_in, wv_ref), (wga_in, wga_ref), (wa_in, wa_ref),
                         (wcb_in, wcb_ref), (wcc_in, wcc_ref), (wcx_in, wcx_ref), (wgb_in, wgb_ref)):
            dst[...] = src[...]
        r = lax.broadcasted_iota(jnp.int32, (TM, TM), 0)
        c = lax.broadcasted_iota(jnp.int32, (TM, TM), 1)
        same = (r // GLA_CHUNK) == (c // GLA_CHUNK)
        lower_ref[...] = jnp.where(same & (c <= r), 1.0, 0.0).astype(BF16)
        upper_ref[...] = jnp.where(same & (c >= r), 1.0, 0.0).astype(BF16)
        w2_ref[...] = jnp.zeros_like(w2_ref)
        w2_ref[0:GLA_LOWRANK, 0:GLA_QK] = wa2_ref[0].astype(BF16)
        w2_ref[GLA_LOWRANK:2 * GLA_LOWRANK, GLA_QK:] = wa2_ref[1].astype(BF16)

    xt = jnp.where(t == 0, ctx_ref[...], x_ref[...])
    h = _modulated_norm(xt, g_ref[...], mod_ref[0:1, :], mod_ref[1:2, :])
    hb = h.astype(BF16)

    a_lr = _dot(hb, wa_ref[...])
    z = _dot(a_lr.astype(BF16), w2_ref[...]) + b2_ref[...]
    qk = _dot(hb, wqk_ref[...])
    v_ref[...] = _dot(hb, wv_ref[...]).astype(BF16)

    lg = _log_sigmoid(z) * (1.0 / GLA_TAU)
    lgb = lg.astype(BF16)
    cs_f = _dot(lower_ref[...], lgb[:, :GLA_QK])
    cs_b = _dot(upper_ref[...], lgb[:, GLA_QK:])
    tot_f = _chunk_rows(cs_f, GLA_CHUNK - 1)
    tot_b = _chunk_rows(cs_b, 0)
    c_c = _dot(hb, wcc_ref[...])

    q = qk[:, :GLA_QK] * (GLA_DK ** -0.5)
    k = qk[:, GLA_QK:]
    qif_ref[...] = (q * jnp.exp(cs_f)).astype(BF16)
    kif_ref[...] = (k * jnp.exp(-cs_f)).astype(BF16)
    kef_ref[...] = (k * jnp.exp(tot_f - cs_f)).astype(BF16)
    qib_ref[...] = (q * jnp.exp(cs_b)).astype(BF16)
    kib_ref[...] = (k * jnp.exp(-cs_b)).astype(BF16)
    keb_ref[...] = (k * jnp.exp(tot_b - cs_b)).astype(BF16)
    for c in range(CHUNKS):
        r = c * GLA_CHUNK
        decf_ref[c:c + 1, :] = jnp.exp(tot_f[r:r + 1, :])
        decb_ref[c:c + 1, :] = jnp.exp(tot_b[r:r + 1, :])

    def emit(rows, vcols, o):
        of_ref[rows, vcols] = o.astype(BF16)

    mask = _gla_mask(reverse=False)
    gla = (qif_ref, kif_ref, kef_ref, v_ref)
    close = lambda hd, opened: _gla_head_close(hd, opened, qif_ref, v_ref, decf_ref, st_ref, emit, reverse=False)
    opened = _gla_head_open(0, *gla, mask)
    zz = c_c * _dot(hb, wcx_ref[...])
    close(0, opened)
    opened = _gla_head_open(1, *gla, mask)
    c_b = _dot(hb, wcb_ref[...])
    close(1, opened)
    opened = _gla_head_open(2, *gla, mask)
    g_b = _dot(hb, wgb_ref[...])
    close(2, opened)
    opened = _gla_head_open(3, *gla, mask)
    sga_ref[...] = _silu(_dot(hb, wga_ref[...])).astype(BF16)
    close(3, opened)

    pos = lax.broadcasted_iota(jnp.int32, (TM, SC_WIDTH), 0)
    row_len = jnp.where(t == 0, TM, GRID_W)
    in_row = pos & (row_len - 1)
    z_prev = jnp.where(in_row != 0, pltpu.roll(zz, 1, 0), 0.0)
    z_next = jnp.where(in_row != row_len - 1, pltpu.roll(zz, TM - 1, 0), 0.0)
    zc = cw_ref[0:1, :] * z_prev + cw_ref[1:2, :] * zz + cw_ref[2:3, :] * z_next
    y_ref[...] = (c_b * zc * _silu(g_b)).astype(BF16)


def _ctx_spec():
    return pl.BlockSpec((None, TM, D_MODEL), lambda b, t: (b, 0, 0))


def _latent_spec(tile, park):
    return pl.BlockSpec((None, TM, D_MODEL),
                        lambda b, t: (b, jnp.where(tile(t) == 0, park, tile(t) - 1), 0))


def _even_in(ctx, x, mods, rows, mats, nt):
    bsz = x.shape[0]
    tok = lambda n: pl.BlockSpec((None, TM, n), lambda b, t: (b, t, 0))
    dec = pl.BlockSpec((None, None, CHUNKS, GLA_QK), lambda b, t: (b, t, 0, 0))
    tok_shape = lambda n: jax.ShapeDtypeStruct((bsz, nt * TM, n), BF16)
    dec_shape = jax.ShapeDtypeStruct((bsz, nt, CHUNKS, GLA_QK), F32)
    consts = tuple(rows) + tuple(mats)
    return pl.pallas_call(
        _even_in_kernel,
        grid=(bsz, nt),
        in_specs=[
            _ctx_spec(), _latent_spec(lambda t: t, 0), _mod_spec(0, bsz, lambda t: t),
        ] + [_const_spec(w) for w in consts],
        out_specs=[tok(GLA_V), tok(GLA_QK), tok(GLA_QK), tok(GLA_QK), dec,
                   tok(GLA_V), tok(GLA_V), tok(SC_WIDTH)],
        out_shape=[tok_shape(GLA_V), tok_shape(GLA_QK), tok_shape(GLA_QK), tok_shape(GLA_QK), dec_shape,
                   tok_shape(GLA_V), tok_shape(GLA_V), tok_shape(SC_WIDTH)],
        scratch_shapes=[pltpu.VMEM(m.shape, BF16) for m in mats]
        + [pltpu.VMEM((TM, TM), BF16)] * 2 + [pltpu.VMEM((LANES, 2 * GLA_QK), BF16)]
        + [pltpu.VMEM((TM, GLA_QK), BF16)] * 3
        + [pltpu.VMEM((CHUNKS, GLA_QK), F32), pltpu.VMEM((GLA_HEADS, GLA_DV, GLA_DK), F32)],
        compiler_params=_params(2),
        name="even_in",
    )(ctx, x, mods, *[_operand(w) for w in consts])


def _chunk_slice(ch):
    return slice(ch * GLA_CHUNK, (ch + 1) * GLA_CHUNK)


def _gla_mask(reverse):
    r = lax.broadcasted_iota(jnp.int32, (TM, TM), 0)
    c = lax.broadcasted_iota(jnp.int32, (TM, TM), 1)
    same_chunk = (r // GLA_CHUNK) == (c // GLA_CHUNK)
    return same_chunk & ((c >= r) if reverse else (c <= r))


def _gla_head_open(hd, qi_ref, ki_ref, ke_ref, v_ref, mask):
    kcols = slice(hd * GLA_DK, (hd + 1) * GLA_DK)
    vcols = slice(hd * GLA_DV, (hd + 1) * GLA_DV)
    incr = [_dot_tn(v_ref[_chunk_slice(ch), vcols], ke_ref[_chunk_slice(ch), kcols]) for ch in range(CHUNKS)]
    scores = jnp.where(mask, _dot_nt(qi_ref[:, kcols], ki_ref[:, kcols]), 0.0).astype(BF16)
    return incr, scores


def _gla_head_close(hd, opened, qi_ref, v_ref, dec_ref, st_ref, emit, reverse):
    incr, scores = opened
    kcols = slice(hd * GLA_DK, (hd + 1) * GLA_DK)
    vcols = slice(hd * GLA_DV, (hd + 1) * GLA_DV)
    state = st_ref[hd]
    before = {}
    for ch in (range(CHUNKS - 1, -1, -1) if reverse else range(CHUNKS)):
        before[ch] = state.astype(BF16)
        state = dec_ref[ch:ch + 1, kcols] * state + incr[ch]
    st_ref[hd] = state
    intra = _dot(scores, v_ref[:, vcols])
    for ch in range(CHUNKS):
        rows = _chunk_slice(ch)
        emit(rows, vcols, intra[rows, :] + _dot_nt(qi_ref[rows, kcols], before[ch]))


def _gla_tile(qi_ref, ki_ref, ke_ref, v_ref, dec_ref, st_ref, emit, reverse):
    mask = _gla_mask(reverse)
    opened = [_gla_head_open(hd, qi_ref, ki_ref, ke_ref, v_ref, mask) for hd in range(GLA_HEADS)]
    for hd in range(GLA_HEADS):
        _gla_head_close(hd, opened[hd], qi_ref, v_ref, dec_ref, st_ref, emit, reverse)


def _gla_bwd_kernel(qi_ref, ki_ref, ke_ref, v_ref, dec_ref, of_ref, sga_ref, y_ref, ctx_ref, x_ref,
                    mod_ref, gg_ref, wo_in, wy_in, out_ref, st_ref, ob_ref, wo_ref, wy_ref, stage_ref):
    j = pl.program_id(1)

    @pl.when(_first_step())
    def _():
        wo_ref[...] = wo_in[...].astype(BF16)
        wy_ref[...] = wy_in[...].astype(BF16)

    @pl.when(j == 0)
    def _():
        st_ref[...] = jnp.zeros_like(st_ref)

    proj_y = _dot(y_ref[...], wy_ref[...])

    def emit(rows, vcols, o):
        ob_ref[rows, vcols] = o

    _gla_tile(qi_ref, ki_ref, ke_ref, v_ref, dec_ref, st_ref, emit, reverse=True)

    heads = []
    for hd in range(GLA_HEADS):
        vcols = slice(hd * GLA_DV, (hd + 1) * GLA_DV)
        o = of_ref[:, vcols].astype(F32) + ob_ref[:, vcols]
        ms = jnp.mean(o * o, axis=-1, keepdims=True)
        on = o * lax.rsqrt(ms + EPS) * gg_ref[...]
        heads.append((on * sga_ref[:, vcols].astype(F32)).astype(BF16))
    inner = jnp.concatenate(heads, axis=-1)
    proj = _dot(inner, wo_ref[...]) + proj_y
    ob_ref[...] = jnp.where(j == 0, ctx_ref[...], x_ref[...]) + mod_ref[2:3, :] * proj
    out_ref[...] = _to_segment_major(ob_ref, stage_ref, D_MODEL)


def _bwd_tile(nt):
    return lambda j: jnp.where(j == 0, 0, nt - j)


def _gla_bwd(qi, ki, ke, v, dec, o_f, sga, y, ctx, x, mods, gg, wo, wy, nt):
    assert GLA_V == D_MODEL
    bsz = qi.shape[0]
    tile = _bwd_tile(nt)
    tok = lambda n: pl.BlockSpec((None, TM, n), lambda b, j: (b, tile(j), 0))
    return pl.pallas_call(
        _gla_bwd_kernel,
        grid=(bsz, nt),
        in_specs=[tok(GLA_QK), tok(GLA_QK), tok(GLA_QK), tok(GLA_V),
                  pl.BlockSpec((None, None, CHUNKS, GLA_QK), lambda b, j: (b, tile(j), 0, 0)),
                  tok(GLA_V), tok(GLA_V), tok(SC_WIDTH), _ctx_spec(), _latent_spec(tile, nt - 2),
                  _mod_spec(0, bsz, tile), _const_spec(gg), _const_spec(wo), _const_spec(wy)],
        out_specs=tok(D_MODEL),
        out_shape=jax.ShapeDtypeStruct((bsz, nt * TM, D_MODEL), F32),
        scratch_shapes=[pltpu.VMEM((GLA_HEADS, GLA_DV, GLA_DK), F32),
                        pltpu.VMEM((TM, GLA_V), F32),
                        pltpu.VMEM(wo.shape, BF16),
                        pltpu.VMEM(wy.shape, BF16),
                        pltpu.VMEM((D_MODEL // LANES, SUBLANES * SEG_PITCH, LANES), F32)],
        compiler_params=_params(2),
        name="gla_bwd",
    )(qi, ki, ke, v, dec, o_f, sga, y, ctx, x, mods, gg, _operand(wo), _operand(wy))


def _to_segment_major(src_ref, stage_ref, width):
    slabs = width // LANES
    for n in range(slabs):
        for s in range(SUBLANES):
            stage_ref[n, s * SEG_PITCH:s * SEG_PITCH + SEG_LEN, :] = (
                src_ref[s * SEG_LEN:(s + 1) * SEG_LEN, n * LANES:(n + 1) * LANES])
    groups = [
        jnp.concatenate([stage_ref[n, pl.ds(j, SUBLANES, stride=SEG_PITCH), :] for n in range(slabs)], axis=1)
        for j in range(SEG_LEN)]
    return jnp.concatenate(groups, axis=0)


def _from_segment_major(val, stage_ref, dst_ref, width):
    slabs = width // LANES
    for j in range(SEG_LEN):
        for n in range(slabs):
            stage_ref[n, pl.ds(j, SUBLANES, stride=SEG_PITCH), :] = (
                val[j * SUBLANES:(j + 1) * SUBLANES, n * LANES:(n + 1) * LANES])
    for n in range(slabs):
        for s in range(SUBLANES):
            dst_ref[s * SEG_LEN:(s + 1) * SEG_LEN, n * LANES:(n + 1) * LANES] = (
                stage_ref[n, s * SEG_PITCH:s * SEG_PITCH + SEG_LEN, :])


def _rg_conv(xr, halo_ref, rows_ref, cols, reverse):
    row = lax.broadcasted_iota(jnp.int32, (SUBLANES, xr.shape[1]), 0)
    edge = []
    for g in range(RG_CONV - 1):
        grp = slice(g * SUBLANES, (g + 1) * SUBLANES)
        if reverse:
            edge.append(jnp.where(row == SUBLANES - 1, pltpu.roll(halo_ref[grp, cols], SUBLANES - 1, 0),
                                  pltpu.roll(xr[grp, :], SUBLANES - 1, 0)))
        else:
            cur = xr[TM - (RG_CONV - 1 - g) * SUBLANES:TM - (RG_CONV - 2 - g) * SUBLANES, :]
            edge.append(jnp.where(row == 0, pltpu.roll(halo_ref[grp, cols], 1, 0), pltpu.roll(cur, 1, 0)))
    tap_w = lambda j: rows_ref[ROW_CONV_W + j:ROW_CONV_W + j + 1, cols]
    xc = rows_ref[ROW_CONV_B:ROW_CONV_B + 1, cols] + tap_w(RG_CONV - 1) * xr
    for k in range(1, RG_CONV):
        if reverse:
            tap = jnp.concatenate([xr[k * SUBLANES:, :]] + edge[:k], axis=0)
        else:
            tap = jnp.concatenate(edge[RG_CONV - 1 - k:] + [xr[:TM - k * SUBLANES, :]], axis=0)
        xc = xc + tap_w(RG_CONV - 1 - k) * tap
    halo_ref[:, cols] = xr[:HALO_ROWS, :] if reverse else xr[TM - HALO_ROWS:, :]
    return xc


def _rg_setup(lam_ref, ba_ref, bx_ref, cb_ref, cw_ref, wa_ref, wx_ref, rows_ref, wg_s, copies):
    wg_s[:, :, :RG_BLOCK_W] = wa_ref[...].astype(BF16)
    wg_s[:, :, RG_BLOCK_W:] = wx_ref[...].astype(BF16)
    rows_ref[ROW_DECAY:ROW_DECAY + 1, :] = (-0.5 * RG_C * LOG2_E) * _softplus(-lam_ref[...])
    rows_ref[ROW_BIAS_A:ROW_BIAS_A + 1, :] = 0.5 * ba_ref[...]
    rows_ref[ROW_BIAS_X:ROW_BIAS_X + 1, :] = 0.5 * bx_ref[...]
    rows_ref[ROW_CONV_B:ROW_CONV_B + 1, :] = 0.5 * cb_ref[...]
    rows_ref[ROW_CONV_W:ROW_CONV_W + RG_CONV, :] = 0.5 * cw_ref[...]
    for src_ref, dst_ref in copies:
        dst_ref[...] = src_ref[...].astype(BF16)


def _rg_gates_scan(xc, first_block, wg_ref, rows_ref, carry_ref, reverse):
    row = lax.broadcasted_iota(jnp.int32, (SUBLANES, RG_BLOCK_W), 0)
    shift = SUBLANES - 1 if reverse else 1
    steps = range(SEG_LEN - 1, -1, -1) if reverse else range(SEG_LEN)
    out = []
    for i in range(xc.shape[1] // RG_BLOCK_W):
        n = first_block + i
        cols = slice(n * RG_BLOCK_W, (n + 1) * RG_BLOCK_W)
        xb = xc[:, i * RG_BLOCK_W:(i + 1) * RG_BLOCK_W]
        gates = _dot(xb.astype(BF16), wg_ref[n])
        th_r = jnp.tanh(gates[:, :RG_BLOCK_W] + rows_ref[ROW_BIAS_A:ROW_BIAS_A + 1, cols])
        th_i = jnp.tanh(gates[:, RG_BLOCK_W:] + rows_ref[ROW_BIAS_X:ROW_BIAS_X + 1, cols])
        half_decay = rows_ref[ROW_DECAY:ROW_DECAY + 1, cols]
        a = jnp.exp2(th_r * half_decay + half_decay)
        y = 1.0 - a * a
        root = y * lax.rsqrt(jnp.maximum(y, F32_TINY))
        u = root * ((th_i + 1.0) * xb)

        h = jnp.zeros((SUBLANES, RG_BLOCK_W), F32)
        prod = jnp.ones((SUBLANES, RG_BLOCK_W), F32)
        local, prods = {}, {}
        for j in steps:
            grp = slice(j * SUBLANES, (j + 1) * SUBLANES)
            h = a[grp, :] * h + u[grp, :]
            prod = prod * a[grp, :]
            local[j], prods[j] = h, prod

        entering = carry_ref[:, cols]
        for s in range(SUBLANES - 1):
            nxt = pltpu.roll(prod * entering + h, shift, 0)
            target = SUBLANES - 2 - s if reverse else s + 1
            entering = jnp.where(row == target, nxt, entering)
        carry_ref[:, cols] = pltpu.roll(prod * entering + h, shift, 0)

        out.append(jnp.concatenate([local[j] + prods[j] * entering for j in range(SEG_LEN)], axis=0))
    return jnp.concatenate(out, axis=1)


def _odd_fwd_kernel(xx_ref, mod_ref, g_ref, wx_ref, cw_ref, cb_ref, ga_ref, gx_ref, ba_ref, bx_ref, lam_ref,
                    xr_ref, hf_ref, hn_ref, halo_ref, carry_ref, rows_ref, wx_s, wg_s):
    t = pl.program_id(1)

    @pl.when(_first_step())
    def _():
        _rg_setup(lam_ref, ba_ref, bx_ref, cb_ref, cw_ref, ga_ref, gx_ref, rows_ref, wg_s, ((wx_ref, wx_s),))

    @pl.when(t == 0)
    def _():
        carry_ref[...] = jnp.zeros_like(carry_ref)

    @pl.when(t <= 1)
    def _():
        halo_ref[...] = jnp.zeros_like(halo_ref)

    hb = _modulated_norm(xx_ref[...], g_ref[...], mod_ref[0:1, :], mod_ref[1:2, :]).astype(BF16)
    hn_ref[...] = hb
    chunks = RG_WIDTH // RG_CHUNK_W
    project = lambda m: _dot(hb, wx_s[:, m * RG_CHUNK_W:(m + 1) * RG_CHUNK_W])
    xr_next = project(0)
    for m in range(chunks):
        cols = slice(m * RG_CHUNK_W, (m + 1) * RG_CHUNK_W)
        xr = xr_next
        if m + 1 < chunks:
            xr_next = project(m + 1)
        xr_ref[:, cols] = xr.astype(BF16)
        xc = _rg_conv(xr, halo_ref, rows_ref, cols, reverse=False)
        hf_ref[:, cols] = _rg_gates_scan(xc, m * (RG_CHUNK_W // RG_BLOCK_W), wg_s, rows_ref, carry_ref,
                                         reverse=False).astype(BF16)


def _odd_fwd(xx, mods, g, wx, cw, cb, ga, gx, ba, bx, lam, nt):
    bsz = xx.shape[0]
    tok = lambda n: pl.BlockSpec((None, TM, n), lambda b, t: (b, t, 0))
    consts = (g, wx, cw, cb, ga, gx, ba, bx, lam)
    gate_scratch = pltpu.VMEM((RG_BLOCKS, RG_BLOCK_W, 2 * RG_BLOCK_W), BF16)
    wide = jax.ShapeDtypeStruct((bsz, nt * TM, RG_WIDTH), BF16)
    return pl.pallas_call(
        _odd_fwd_kernel,
        grid=(bsz, nt),
        in_specs=[tok(D_MODEL), _mod_spec(1, bsz, lambda t: t)] + [_const_spec(w) for w in consts],
        out_specs=[tok(RG_WIDTH), tok(RG_WIDTH), tok(D_MODEL)],
        out_shape=[wide, wide, jax.ShapeDtypeStruct((bsz, nt * TM, D_MODEL), BF16)],
        scratch_shapes=[pltpu.VMEM((HALO_ROWS, RG_WIDTH), F32),
                        pltpu.VMEM((SUBLANES, RG_WIDTH), F32),
                        pltpu.VMEM((RG_ROWS, RG_WIDTH), F32),
                        pltpu.VMEM(wx.shape, BF16),
                        gate_scratch],
        compiler_params=_params(2),
        name="odd_fwd",
    )(xx, mods, *[_operand(w) for w in consts])


def _odd_bwd_kernel(xx_ref, mod_ref, hn_ref, xr_ref, hf_ref, wgate_ref, cw_ref, cb_ref, ga_ref, gx_ref, ba_ref,
                    bx_ref, lam_ref, wout_ref, fg_ref, out_ref,
                    stage_ref, halo_ref, yy_ref, carry_ref, rows_ref, wgate_s, wg_s, wout_s):
    j = pl.program_id(1)

    @pl.when(_first_step())
    def _():
        _rg_setup(lam_ref, ba_ref, bx_ref, cb_ref, cw_ref, ga_ref, gx_ref, rows_ref, wg_s,
                  ((wgate_ref, wgate_s), (wout_ref, wout_s)))

    @pl.when(j == 0)
    def _():
        carry_ref[...] = jnp.zeros_like(carry_ref)

    @pl.when(j <= 1)
    def _():
        halo_ref[...] = jnp.zeros_like(halo_ref)

    hb = hn_ref[...]
    for m in range(RG_WIDTH // RG_CHUNK_W):
        cols = slice(m * RG_CHUNK_W, (m + 1) * RG_CHUNK_W)
        gate = _dot(hb, wgate_s[:, cols])
        xc = _rg_conv(xr_ref[:, cols].astype(F32), halo_ref, rows_ref, cols, reverse=True)
        h_bwd = _rg_gates_scan(xc, m * (RG_CHUNK_W // RG_BLOCK_W), wg_s, rows_ref, carry_ref, reverse=True)
        yy_ref[:, cols] = ((hf_ref[:, cols].astype(F32) + h_bwd) * _silu(gate)).astype(BF16)
    xn = xx_ref[...] + mod_ref[2:3, :] * _dot(yy_ref[...], wout_s[...])
    ms = jnp.mean(xn * xn, axis=-1, keepdims=True)
    _from_segment_major(xn * lax.rsqrt(ms + EPS) * fg_ref[...], stage_ref, out_ref, D_MODEL)


def _odd_bwd(xx, mods, hn, xr, hf, wgate, cw, cb, ga, gx, ba, bx, lam, wout, fg, nt):
    bsz = xx.shape[0]
    tile = _bwd_tile(nt)
    tok = lambda n: pl.BlockSpec((None, TM, n), lambda b, j: (b, tile(j), 0))
    consts = (wgate, cw, cb, ga, gx, ba, bx, lam, wout, fg)
    gate_scratch = pltpu.VMEM((RG_BLOCKS, RG_BLOCK_W, 2 * RG_BLOCK_W), BF16)
    out_block = lambda b, j: (b, jnp.where(j == 0, nt - 2, nt - 1 - j), 0)
    return pl.pallas_call(
        _odd_bwd_kernel,
        grid=(bsz, nt),
        in_specs=[tok(D_MODEL), _mod_spec(1, bsz, tile), tok(D_MODEL), tok(RG_WIDTH), tok(RG_WIDTH)]
        + [_const_spec(w) for w in consts],
        out_specs=pl.BlockSpec((None, TM, D_MODEL), out_block),
        out_shape=jax.ShapeDtypeStruct((bsz, (nt - 1) * TM, D_MODEL), F32),
        scratch_shapes=[pltpu.VMEM((D_MODEL // LANES, SUBLANES * SEG_PITCH, LANES), F32),
                        pltpu.VMEM((HALO_ROWS, RG_WIDTH), F32),
                        pltpu.VMEM((TM, RG_WIDTH), BF16),
                        pltpu.VMEM((SUBLANES, RG_WIDTH), F32),
                        pltpu.VMEM((RG_ROWS, RG_WIDTH), F32),
                        pltpu.VMEM(wgate.shape, BF16),
                        gate_scratch,
                        pltpu.VMEM(wout.shape, BF16)],
        compiler_params=_params(2),
        name="odd_bwd",
    )(xx, mods, hn, xr, hf, *[_operand(w) for w in consts])


def kernel(x, c, ctx, c_ctx, norm_g, w_mod, b_mod, e_w_in, e_w_a2, e_b_a2, e_gla_g, e_conv_w, e_w_out,
           o_w_in, o_conv_w, o_conv_b, o_w_a, o_b_a, o_w_x, o_b_x, o_lam, o_w_out, final_g):
    bsz, seq, _ = x.shape
    assert ctx.shape[1] == TM and seq % TM == 0 and bsz < SUBLANES
    assert w_mod.shape[0] == 2
    nt = 1 + seq // TM

    mods = _mods(c, c_ctx, w_mod, b_mod).reshape(w_mod.shape[0], SUBLANES, 3, D_MODEL)

    w_main, wa = _pack_in_proj(e_w_in[0])
    square = (D_MODEL, D_MODEL)
    assert 2 * GLA_QK == GLA_V == SC_WIDTH == D_MODEL
    wqk, wv, wga, wcb, wcc, wcx, wgb = (_Window(w_main, square, (0, i)) for i in range(EVEN_PACKED // D_MODEL))
    b2 = e_b_a2[0].reshape(1, 2 * GLA_QK)
    mats = (wqk, wv, wga, wa, wcb, wcc, wcx, wgb)
    (o_f, qi_b, ki_b, ke_b, dec_b, v, sga, y) = _even_in(
        ctx, x, mods, (norm_g[0:1], b2, e_conv_w[0], e_w_a2[0]), mats, nt)
    w_out = e_w_out[0]
    xx = _gla_bwd(qi_b, ki_b, ke_b, v, dec_b, o_f, sga, y, ctx, x, mods, e_gla_g[0:1],
                  _Window(w_out, square, (0, 0)), _Window(w_out, square, (1, 0)), nt)

    w_in = o_w_in[0]
    half = (D_MODEL, RG_WIDTH)
    wide = lambda a: a.reshape(1, RG_WIDTH)
    xr, h_f, hn = _odd_fwd(xx, mods, norm_g[1:2], _Window(w_in, half, (0, 0)), o_conv_w[0, 0],
                           wide(o_conv_b[0, 0]), o_w_a[0, 0], o_w_x[0, 0],
                           wide(o_b_a[0, 0]), wide(o_b_x[0, 0]), wide(o_lam[0, 0]), nt)
    return _odd_bwd(xx, mods, hn, xr, h_f, _Window(w_in, half, (0, 1)), o_conv_w[0, 1],
                    wide(o_conv_b[0, 1]), o_w_a[0, 1], o_w_x[0, 1],
                    wide(o_b_a[0, 1]), wide(o_b_x[0, 1]), wide(o_lam[0, 1]), o_w_out[0],
                    final_g.reshape(1, D_MODEL), nt)
```

```python
import jax
import jax.numpy as jnp
from jax import lax
from jax.experimental import pallas as pl
from jax.experimental.pallas import tpu as pltpu

F32 = jnp.float32
BF16 = jnp.bfloat16

D_MODEL = 1024
EPS = 1e-6
GLA_HEADS = 4
GLA_DK = 128
GLA_DV = 256
GLA_QK = GLA_HEADS * GLA_DK
GLA_V = GLA_HEADS * GLA_DV
GLA_LOWRANK = 16
GLA_TAU = 16.0
GLA_CHUNK = 64
GRID_W = 64
SC_WIDTH = D_MODEL
RG_WIDTH = 2 * D_MODEL
RG_BLOCKS = 16
RG_BLOCK_W = 128
RG_C = 8.0
RG_CONV = 4

TM = 256
CHUNKS = TM // GLA_CHUNK
SUBLANES = 8
LANES = 128
VMEM_LIMIT = 56 * 1024 * 1024
SEG_LEN = TM // SUBLANES
SEG_PITCH = SEG_LEN + SUBLANES
HALO_ROWS = (RG_CONV - 1) * SUBLANES
RG_CHUNK_W = 256
ROW_DECAY, ROW_BIAS_A, ROW_BIAS_X, ROW_CONV_B, ROW_CONV_W = 0, 1, 2, 3, 4
RG_ROWS = ROW_CONV_W + RG_CONV
LOG2_E = 1.4426950408889634
F32_TINY = 1.1754943508222875e-38


def _silu(x):
    return x * jax.nn.sigmoid(x)


def _log_sigmoid(z):
    return jnp.minimum(z, 0.0) - jnp.log(1.0 + jnp.exp(-jnp.abs(z)))


def _softplus(z):
    return jnp.maximum(z, 0.0) + jnp.log1p(jnp.exp(-jnp.abs(z)))


def _dot(a, b):
    return jnp.dot(a, b, preferred_element_type=F32)


def _dot_nt(a, b):
    return lax.dot_general(a, b, (((1,), (1,)), ((), ())), preferred_element_type=F32)


def _dot_tn(a, b):
    return lax.dot_general(a, b, (((0,), (0,)), ((), ())), preferred_element_type=F32)


def _modulated_norm(x, g, shift, scale):
    ms = jnp.mean(x * x, axis=-1, keepdims=True)
    return (x * lax.rsqrt(ms + EPS) * g) * (1.0 + scale) + shift


def _params(n_axes):
    return pltpu.CompilerParams(
        dimension_semantics=("arbitrary",) * n_axes,
        vmem_limit_bytes=VMEM_LIMIT)


def _first_step():
    return (pl.program_id(0) == 0) & (pl.program_id(1) == 0)


class _Window:
    def __init__(self, array, block_shape, block_index):
        self.array, self.shape, self.index = array, tuple(block_shape), tuple(block_index)


def _operand(w):
    return w.array if isinstance(w, _Window) else w


def _const_spec(w):
    index = w.index if isinstance(w, _Window) else (0,) * len(w.shape)
    return pl.BlockSpec(w.shape, lambda *_: index, pipeline_mode=pl.Buffered(1))


def _mod_spec(layer, bsz, tile):
    return pl.BlockSpec((None, None, 3, D_MODEL),
                        lambda b, t: (layer, jnp.where(tile(t) == 0, bsz, b), 0, 0))


def _mods_kernel(c_ref, cctx_ref, w_ref, b_ref, o_ref, s_ref):
    bsz = c_ref.shape[0]
    s_ref[...] = jnp.zeros_like(s_ref)
    s_ref[0:bsz, :] = c_ref[...]
    s_ref[bsz:bsz + 1, :] = cctx_ref[...]
    s = _silu(s_ref[...])
    o_ref[0] = jnp.dot(s, w_ref[0], preferred_element_type=F32,
                       precision=lax.Precision.HIGHEST) + b_ref[0]


def _mods(c, c_ctx, w_mod, b_mod):
    depth = w_mod.shape[0]
    nb = 3
    return pl.pallas_call(
        _mods_kernel,
        grid=(depth, nb),
        in_specs=[
            pl.BlockSpec(c.shape, lambda l, j: (0, 0)),
            pl.BlockSpec((1, D_MODEL), lambda l, j: (0, 0)),
            pl.BlockSpec((1, D_MODEL, D_MODEL), lambda l, j: (l, 0, j)),
            pl.BlockSpec((1, 1, D_MODEL), lambda l, j: (l, 0, j)),
        ],
        out_specs=pl.BlockSpec((1, SUBLANES, D_MODEL), lambda l, j: (l, 0, j)),
        out_shape=jax.ShapeDtypeStruct((depth, SUBLANES, 3 * D_MODEL), F32),
        scratch_shapes=[pltpu.VMEM((SUBLANES, D_MODEL), F32)],
        compiler_params=_params(2),
        name="mods",
    )(c, c_ctx.reshape(1, D_MODEL), w_mod, b_mod.reshape(depth, 1, 3 * D_MODEL))


def _chunk_rows(x, offset):
    return jnp.concatenate(
        [jnp.broadcast_to(x[c * GLA_CHUNK + offset:c * GLA_CHUNK + offset + 1, :], (GLA_CHUNK, x.shape[1]))
         for c in range(CHUNKS)], axis=0)


def _even_in_kernel(ctx_ref, x_ref, mod_ref, g_ref, b2_ref, cw_ref, wa2_ref,
                    wqk_in, wv_in, wga_in, wa_in, wcb_in, wcc_in, wcx_in, wgb_in,
                    of_ref, qib_ref, kib_ref, keb_ref, decb_ref, v_ref, sga_ref, y_ref,
                    wqk_ref, wv_ref, wga_ref, wa_ref, wcb_ref, wcc_ref, wcx_ref, wgb_ref,
                    lower_ref, upper_ref, w2_ref, qif_ref, kif_ref, kef_ref, decf_ref, st_ref):
    t = pl.program_id(1)

    @pl.when(t == 0)
    def _():
        st_ref[...] = jnp.zeros_like(st_ref)

    @pl.when(_first_step())
    def _():
        for src, dst in ((wqk_in, wqk_ref), (wv_in, wv_ref), (wga_in, wga_ref), (wa_in, wa_ref),
                         (wcb_in, wcb_ref), (wcc_in, wcc_ref), (wcx_in, wcx_ref), (wgb_in, wgb_ref)):
            dst[...] = src[...]
        r = lax.broadcasted_iota(jnp.int32, (TM, TM), 0)
        c = lax.broadcasted_iota(jnp.int32, (TM, TM), 1)
        same = (r // GLA_CHUNK) == (c // GLA_CHUNK)
        lower_ref[...] = jnp.where(same & (c <= r), 1.0, 0.0).astype(BF16)
        upper_ref[...] = jnp.where(same & (c >= r), 1.0, 0.0).astype(BF16)
        w2_ref[...] = jnp.zeros_like(w2_ref)
        w2_ref[0:GLA_LOWRANK, 0:GLA_QK] = wa2_ref[0].astype(BF16)
        w2_ref[GLA_LOWRANK:2 * GLA_LOWRANK, GLA_QK:] = wa2_ref[1].astype(BF16)

    xt = jnp.where(t == 0, ctx_ref[...], x_ref[...])
    h = _modulated_norm(xt, g_ref[...], mod_ref[0:1, :], mod_ref[1:2, :])
    hb = h.astype(BF16)

    a_lr = _dot(hb, wa_ref[...])
    z = _dot(a_lr.astype(BF16), w2_ref[...]) + b2_ref[...]
    qk = _dot(hb, wqk_ref[...])
    v_ref[...] = _dot(hb, wv_ref[...]).astype(BF16)

    lg = _log_sigmoid(z) * (1.0 / GLA_TAU)
    lgb = lg.astype(BF16)
    cs_f = _dot(lower_ref[...], lgb[:, :GLA_QK])
    cs_b = _dot(upper_ref[...], lgb[:, GLA_QK:])
    tot_f = _chunk_rows(cs_f, GLA_CHUNK - 1)
    tot_b = _chunk_rows(cs_b, 0)
    c_c = _dot(hb, wcc_ref[...])

    q = qk[:, :GLA_QK] * (GLA_DK ** -0.5)
    k = qk[:, GLA_QK:]
    qif_ref[...] = (q * jnp.exp(cs_f)).astype(BF16)
    kif_ref[...] = (k * jnp.exp(-cs_f)).astype(BF16)
    kef_ref[...] = (k * jnp.exp(tot_f - cs_f)).astype(BF16)
    qib_ref[...] = (q * jnp.exp(cs_b)).astype(BF16)
    kib_ref[...] = (k * jnp.exp(-cs_b)).astype(BF16)
    keb_ref[...] = (k * jnp.exp(tot_b - cs_b)).astype(BF16)
    for c in range(CHUNKS):
        r = c * GLA_CHUNK
        decf_ref[c:c + 1, :] = jnp.exp(tot_f[r:r + 1, :])
        decb_ref[c:c + 1, :] = jnp.exp(tot_b[r:r + 1, :])

    def emit(rows, vcols, o):
        of_ref[rows, vcols] = o.astype(BF16)

    mask = _gla_mask(reverse=False)
    gla = (qif_ref, kif_ref, kef_ref, v_ref)
    close = lambda hd, opened: _gla_head_close(hd, opened, qif_ref, v_ref, decf_ref, st_ref, emit, reverse=False)
    opened = _gla_head_open(0, *gla, mask)
    zz = c_c * _dot(hb, wcx_ref[...])
    close(0, opened)
    opened = _gla_head_open(1, *gla, mask)
    c_b = _dot(hb, wcb_ref[...])
    close(1, opened)
    opened = _gla_head_open(2, *gla, mask)
    g_b = _dot(hb, wgb_ref[...])
    close(2, opened)
    opened = _gla_head_open(3, *gla, mask)
    sga_ref[...] = _silu(_dot(hb, wga_ref[...])).astype(BF16)
    close(3, opened)

    pos = lax.broadcasted_iota(jnp.int32, (TM, SC_WIDTH), 0)
    row_len = jnp.where(t == 0, TM, GRID_W)
    in_row = pos & (row_len - 1)
    z_prev = jnp.where(in_row != 0, pltpu.roll(zz, 1, 0), 0.0)
    z_next = jnp.where(in_row != row_len - 1, pltpu.roll(zz, TM - 1, 0), 0.0)
    zc = cw_ref[0:1, :] * z_prev + cw_ref[1:2, :] * zz + cw_ref[2:3, :] * z_next
    y_ref[...] = (c_b * zc * _silu(g_b)).astype(BF16)


def _ctx_spec():
    return pl.BlockSpec((None, TM, D_MODEL), lambda b, t: (b, 0, 0))


def _latent_spec(tile, park):
    return pl.BlockSpec((None, TM, D_MODEL),
                        lambda b, t: (b, jnp.where(tile(t) == 0, park, tile(t) - 1), 0))


def _even_in(ctx, x, mods, rows, mats, nt):
    bsz = x.shape[0]
    tok = lambda n: pl.BlockSpec((None, TM, n), lambda b, t: (b, t, 0))
    dec = pl.BlockSpec((None, None, CHUNKS, GLA_QK), lambda b, t: (b, t, 0, 0))
    tok_shape = lambda n: jax.ShapeDtypeStruct((bsz, nt * TM, n), BF16)
    dec_shape = jax.ShapeDtypeStruct((bsz, nt, CHUNKS, GLA_QK), F32)
    consts = tuple(rows) + tuple(mats)
    return pl.pallas_call(
        _even_in_kernel,
        grid=(bsz, nt),
        in_specs=[
            _ctx_spec(), _latent_spec(lambda t: t, 0), _mod_spec(0, bsz, lambda t: t),
        ] + [_const_spec(w) for w in consts],
        out_specs=[tok(GLA_V), tok(GLA_QK), tok(GLA_QK), tok(GLA_QK), dec,
                   tok(GLA_V), tok(GLA_V), tok(SC_WIDTH)],
        out_shape=[tok_shape(GLA_V), tok_shape(GLA_QK), tok_shape(GLA_QK), tok_shape(GLA_QK), dec_shape,
                   tok_shape(GLA_V), tok_shape(GLA_V), tok_shape(SC_WIDTH)],
        scratch_shapes=[pltpu.VMEM(m.shape, BF16) for m in mats]
        + [pltpu.VMEM((TM, TM), BF16)] * 2 + [pltpu.VMEM((LANES, 2 * GLA_QK), BF16)]
        + [pltpu.VMEM((TM, GLA_QK), BF16)] * 3
        + [pltpu.VMEM((CHUNKS, GLA_QK), F32), pltpu.VMEM((GLA_HEADS, GLA_DV, GLA_DK), F32)],
        compiler_params=_params(2),
        name="even_in",
    )(ctx, x, mods, *[_operand(w) for w in consts])


def _chunk_slice(ch):
    return slice(ch * GLA_CHUNK, (ch + 1) * GLA_CHUNK)


def _gla_mask(reverse):
    r = lax.broadcasted_iota(jnp.int32, (TM, TM), 0)
    c = lax.broadcasted_iota(jnp.int32, (TM, TM), 1)
    same_chunk = (r // GLA_CHUNK) == (c // GLA_CHUNK)
    return same_chunk & ((c >= r) if reverse else (c <= r))


def _gla_head_open(hd, qi_ref, ki_ref, ke_ref, v_ref, mask):
    kcols = slice(hd * GLA_DK, (hd + 1) * GLA_DK)
    vcols = slice(hd * GLA_DV, (hd + 1) * GLA_DV)
    incr = [_dot_tn(v_ref[_chunk_slice(ch), vcols], ke_ref[_chunk_slice(ch), kcols]) for ch in range(CHUNKS)]
    scores = jnp.where(mask, _dot_nt(qi_ref[:, kcols], ki_ref[:, kcols]), 0.0).astype(BF16)
    return incr, scores


def _gla_head_close(hd, opened, qi_ref, v_ref, dec_ref, st_ref, emit, reverse):
    incr, scores = opened
    kcols = slice(hd * GLA_DK, (hd + 1) * GLA_DK)
    vcols = slice(hd * GLA_DV, (hd + 1) * GLA_DV)
    state = st_ref[hd]
    before = {}
    for ch in (range(CHUNKS - 1, -1, -1) if reverse else range(CHUNKS)):
        before[ch] = state.astype(BF16)
        state = dec_ref[ch:ch + 1, kcols] * state + incr[ch]
    st_ref[hd] = state
    intra = _dot(scores, v_ref[:, vcols])
    for ch in range(CHUNKS):
        rows = _chunk_slice(ch)
        emit(rows, vcols, intra[rows, :] + _dot_nt(qi_ref[rows, kcols], before[ch]))


def _gla_tile(qi_ref, ki_ref, ke_ref, v_ref, dec_ref, st_ref, emit, reverse):
    mask = _gla_mask(reverse)
    opened = [_gla_head_open(hd, qi_ref, ki_ref, ke_ref, v_ref, mask) for hd in range(GLA_HEADS)]
    for hd in range(GLA_HEADS):
        _gla_head_close(hd, opened[hd], qi_ref, v_ref, dec_ref, st_ref, emit, reverse)


def _gla_bwd_kernel(qi_ref, ki_ref, ke_ref, v_ref, dec_ref, of_ref, sga_ref, y_ref, ctx_ref, x_ref,
                    mod_ref, gg_ref, wo_in, wy_in, out_ref, st_ref, ob_ref, wo_ref, wy_ref, stage_ref):
    j = pl.program_id(1)

    @pl.when(_first_step())
    def _():
        wo_ref[...] = wo_in[...].astype(BF16)
        wy_ref[...] = wy_in[...].astype(BF16)

    @pl.when(j == 0)
    def _():
        st_ref[...] = jnp.zeros_like(st_ref)

    proj_y = _dot(y_ref[...], wy_ref[...])

    def emit(rows, vcols, o):
        ob_ref[rows, vcols] = o

    _gla_tile(qi_ref, ki_ref, ke_ref, v_ref, dec_ref, st_ref, emit, reverse=True)

    heads = []
    for hd in range(GLA_HEADS):
        vcols = slice(hd * GLA_DV, (hd + 1) * GLA_DV)
        o = of_ref[:, vcols].astype(F32) + ob_ref[:, vcols]
        ms = jnp.mean(o * o, axis=-1, keepdims=True)
        on = o * lax.rsqrt(ms + EPS) * gg_ref[...]
        heads.append((on * sga_ref[:, vcols].astype(F32)).astype(BF16))
    inner = jnp.concatenate(heads, axis=-1)
    proj = _dot(inner, wo_ref[...]) + proj_y
    ob_ref[...] = jnp.where(j == 0, ctx_ref[...], x_ref[...]) + mod_ref[2:3, :] * proj
    out_ref[...] = _to_segment_major(ob_ref, stage_ref, D_MODEL)


def _bwd_tile(nt):
    return lambda j: jnp.where(j == 0, 0, nt - j)


def _gla_bwd(qi, ki, ke, v, dec, o_f, sga, y, ctx, x, mods, gg, wo, wy, nt):
    assert GLA_V == D_MODEL
    bsz = qi.shape[0]
    tile = _bwd_tile(nt)
    tok = lambda n: pl.BlockSpec((None, TM, n), lambda b, j: (b, tile(j), 0))
    return pl.pallas_call(
        _gla_bwd_kernel,
        grid=(bsz, nt),
        in_specs=[tok(GLA_QK), tok(GLA_QK), tok(GLA_QK), tok(GLA_V),
                  pl.BlockSpec((None, None, CHUNKS, GLA_QK), lambda b, j: (b, tile(j), 0, 0)),
                  tok(GLA_V), tok(GLA_V), tok(SC_WIDTH), _ctx_spec(), _latent_spec(tile, nt - 2),
                  _mod_spec(0, bsz, tile), _const_spec(gg), _const_spec(wo), _const_spec(wy)],
        out_specs=tok(D_MODEL),
        out_shape=jax.ShapeDtypeStruct((bsz, nt * TM, D_MODEL), F32),
        scratch_shapes=[pltpu.VMEM((GLA_HEADS, GLA_DV, GLA_DK), F32),
                        pltpu.VMEM((TM, GLA_V), F32),
                        pltpu.VMEM(wo.shape, BF16),
                        pltpu.VMEM(wy.shape, BF16),
                        pltpu.VMEM((D_MODEL // LANES, SUBLANES * SEG_PITCH, LANES), F32)],
        compiler_params=_params(2),
        name="gla_bwd",
    )(qi, ki, ke, v, dec, o_f, sga, y, ctx, x, mods, gg, _operand(wo), _operand(wy))


def _to_segment_major(src_ref, stage_ref, width):
    slabs = width // LANES
    for n in range(slabs):
        for s in range(SUBLANES):
            stage_ref[n, s * SEG_PITCH:s * SEG_PITCH + SEG_LEN, :] = (
                src_ref[s * SEG_LEN:(s + 1) * SEG_LEN, n * LANES:(n + 1) * LANES])
    groups = [
        jnp.concatenate([stage_ref[n, pl.ds(j, SUBLANES, stride=SEG_PITCH), :] for n in range(slabs)], axis=1)
        for j in range(SEG_LEN)]
    return jnp.concatenate(groups, axis=0)


def _from_segment_major(val, stage_ref, dst_ref, width):
    slabs = width // LANES
    for j in range(SEG_LEN):
        for n in range(slabs):
            stage_ref[n, pl.ds(j, SUBLANES, stride=SEG_PITCH), :] = (
                val[j * SUBLANES:(j + 1) * SUBLANES, n * LANES:(n + 1) * LANES])
    for n in range(slabs):
        for s in range(SUBLANES):
            dst_ref[s * SEG_LEN:(s + 1) * SEG_LEN, n * LANES:(n + 1) * LANES] = (
                stage_ref[n, s * SEG_PITCH:s * SEG_PITCH + SEG_LEN, :])


def _rg_conv(xr, halo_ref, rows_ref, cols, reverse):
    row = lax.broadcasted_iota(jnp.int32, (SUBLANES, xr.shape[1]), 0)
    edge = []
    for g in range(RG_CONV - 1):
        grp = slice(g * SUBLANES, (g + 1) * SUBLANES)
        if reverse:
            edge.append(jnp.where(row == SUBLANES - 1, pltpu.roll(halo_ref[grp, cols], SUBLANES - 1, 0),
                                  pltpu.roll(xr[grp, :], SUBLANES - 1, 0)))
        else:
            cur = xr[TM - (RG_CONV - 1 - g) * SUBLANES:TM - (RG_CONV - 2 - g) * SUBLANES, :]
            edge.append(jnp.where(row == 0, pltpu.roll(halo_ref[grp, cols], 1, 0), pltpu.roll(cur, 1, 0)))
    tap_w = lambda j: rows_ref[ROW_CONV_W + j:ROW_CONV_W + j + 1, cols]
    xc = rows_ref[ROW_CONV_B:ROW_CONV_B + 1, cols] + tap_w(RG_CONV - 1) * xr
    for k in range(1, RG_CONV):
        if reverse:
            tap = jnp.concatenate([xr[k * SUBLANES:, :]] + edge[:k], axis=0)
        else:
            tap = jnp.concatenate(edge[RG_CONV - 1 - k:] + [xr[:TM - k * SUBLANES, :]], axis=0)
        xc = xc + tap_w(RG_CONV - 1 - k) * tap
    halo_ref[:, cols] = xr[:HALO_ROWS, :] if reverse else xr[TM - HALO_ROWS:, :]
    return xc


def _rg_setup(lam_ref, ba_ref, bx_ref, cb_ref, cw_ref, wa_ref, wx_ref, rows_ref, wg_s, copies):
    wg_s[:, :, :RG_BLOCK_W] = wa_ref[...].astype(BF16)
    wg_s[:, :, RG_BLOCK_W:] = wx_ref[...].astype(BF16)
    rows_ref[ROW_DECAY:ROW_DECAY + 1, :] = (-0.5 * RG_C * LOG2_E) * _softplus(-lam_ref[...])
    rows_ref[ROW_BIAS_A:ROW_BIAS_A + 1, :] = 0.5 * ba_ref[...]
    rows_ref[ROW_BIAS_X:ROW_BIAS_X + 1, :] = 0.5 * bx_ref[...]
    rows_ref[ROW_CONV_B:ROW_CONV_B + 1, :] = 0.5 * cb_ref[...]
    rows_ref[ROW_CONV_W:ROW_CONV_W + RG_CONV, :] = 0.5 * cw_ref[...]
    for src_ref, dst_ref in copies:
        dst_ref[...] = src_ref[...].astype(BF16)


def _rg_gates_scan(xc, first_block, wg_ref, rows_ref, carry_ref, reverse):
    row = lax.broadcasted_iota(jnp.int32, (SUBLANES, RG_BLOCK_W), 0)
    shift = SUBLANES - 1 if reverse else 1
    steps = range(SEG_LEN - 1, -1, -1) if reverse else range(SEG_LEN)
    out = []
    for i in range(xc.shape[1] // RG_BLOCK_W):
        n = first_block + i
        cols = slice(n * RG_BLOCK_W, (n + 1) * RG_BLOCK_W)
        xb = xc[:, i * RG_BLOCK_W:(i + 1) * RG_BLOCK_W]
        gates = _dot(xb.astype(BF16), wg_ref[n])
        th_r = jnp.tanh(gates[:, :RG_BLOCK_W] + rows_ref[ROW_BIAS_A:ROW_BIAS_A + 1, cols])
        th_i = jnp.tanh(gates[:, RG_BLOCK_W:] + rows_ref[ROW_BIAS_X:ROW_BIAS_X + 1, cols])
        half_decay = rows_ref[ROW_DECAY:ROW_DECAY + 1, cols]
        a = jnp.exp2(th_r * half_decay + half_decay)
        y = 1.0 - a * a
        root = y * lax.rsqrt(jnp.maximum(y, F32_TINY))
        u = root * ((th_i + 1.0) * xb)

        h = jnp.zeros((SUBLANES, RG_BLOCK_W), F32)
        prod = jnp.ones((SUBLANES, RG_BLOCK_W), F32)
        local, prods = {}, {}
        for j in steps:
            grp = slice(j * SUBLANES, (j + 1) * SUBLANES)
            h = a[grp, :] * h + u[grp, :]
            prod = prod * a[grp, :]
            local[j], prods[j] = h, prod

        entering = carry_ref[:, cols]
        for s in range(SUBLANES - 1):
            nxt = pltpu.roll(prod * entering + h, shift, 0)
            target = SUBLANES - 2 - s if reverse else s + 1
            entering = jnp.where(row == target, nxt, entering)
        carry_ref[:, cols] = pltpu.roll(prod * entering + h, shift, 0)

        out.append(jnp.concatenate([local[j] + prods[j] * entering for j in range(SEG_LEN)], axis=0))
    return jnp.concatenate(out, axis=1)


def _odd_fwd_kernel(xx_ref, mod_ref, g_ref, wx_ref, cw_ref, cb_ref, ga_ref, gx_ref, ba_ref, bx_ref, lam_ref,
                    xr_ref, hf_ref, hn_ref, halo_ref, carry_ref, rows_ref, wx_s, wg_s):
    t = pl.program_id(1)

    @pl.when(_first_step())
    def _():
        _rg_setup(lam_ref, ba_ref, bx_ref, cb_ref, cw_ref, ga_ref, gx_ref, rows_ref, wg_s, ((wx_ref, wx_s),))

    @pl.when(t == 0)
    def _():
        carry_ref[...] = jnp.zeros_like(carry_ref)

    @pl.when(t <= 1)
    def _():
        halo_ref[...] = jnp.zeros_like(halo_ref)

    hb = _modulated_norm(xx_ref[...], g_ref[...], mod_ref[0:1, :], mod_ref[1:2, :]).astype(BF16)
    hn_ref[...] = hb
    chunks = RG_WIDTH // RG_CHUNK_W
    project = lambda m: _dot(hb, wx_s[:, m * RG_CHUNK_W:(m + 1) * RG_CHUNK_W])
    xr_next = project(0)
    for m in range(chunks):
        cols = slice(m * RG_CHUNK_W, (m + 1) * RG_CHUNK_W)
        xr = xr_next
        if m + 1 < chunks:
            xr_next = project(m + 1)
        xr_ref[:, cols] = xr.astype(BF16)
        xc = _rg_conv(xr, halo_ref, rows_ref, cols, reverse=False)
        hf_ref[:, cols] = _rg_gates_scan(xc, m * (RG_CHUNK_W // RG_BLOCK_W), wg_s, rows_ref, carry_ref,
                                         reverse=False).astype(BF16)


def _odd_fwd(xx, mods, g, wx, cw, cb, ga, gx, ba, bx, lam, nt):
    bsz = xx.shape[0]
    tok = lambda n: pl.BlockSpec((None, TM, n), lambda b, t: (b, t, 0))
    consts = (g, wx, cw, cb, ga, gx, ba, bx, lam)
    gate_scratch = pltpu.VMEM((RG_BLOCKS, RG_BLOCK_W, 2 * RG_BLOCK_W), BF16)
    wide = jax.ShapeDtypeStruct((bsz, nt * TM, RG_WIDTH), BF16)
    return pl.pallas_call(
        _odd_fwd_kernel,
        grid=(bsz, nt),
        in_specs=[tok(D_MODEL), _mod_spec(1, bsz, lambda t: t)] + [_const_spec(w) for w in consts],
        out_specs=[tok(RG_WIDTH), tok(RG_WIDTH), tok(D_MODEL)],
        out_shape=[wide, wide, jax.ShapeDtypeStruct((bsz, nt * TM, D_MODEL), BF16)],
        scratch_shapes=[pltpu.VMEM((HALO_ROWS, RG_WIDTH), F32),
                        pltpu.VMEM((SUBLANES, RG_WIDTH), F32),
                        pltpu.VMEM((RG_ROWS, RG_WIDTH), F32),
                        pltpu.VMEM(wx.shape, BF16),
                        gate_scratch],
        compiler_params=_params(2),
        name="odd_fwd",
    )(xx, mods, *[_operand(w) for w in consts])


def _odd_bwd_kernel(xx_ref, mod_ref, hn_ref, xr_ref, hf_ref, wgate_ref, cw_ref, cb_ref, ga_ref, gx_ref, ba_ref,
                    bx_ref, lam_ref, wout_ref, fg_ref, out_ref,
                    stage_ref, halo_ref, yy_ref, carry_ref, rows_ref, wgate_s, wg_s, wout_s):
    j = pl.program_id(1)

    @pl.when(_first_step())
    def _():
        _rg_setup(lam_ref, ba_ref, bx_ref, cb_ref, cw_ref, ga_ref, gx_ref, rows_ref, wg_s,
                  ((wgate_ref, wgate_s), (wout_ref, wout_s)))

    @pl.when(j == 0)
    def _():
        carry_ref[...] = jnp.zeros_like(carry_ref)

    @pl.when(j <= 1)
    def _():
        halo_ref[...] = jnp.zeros_like(halo_ref)

    def tile(with_output):
        hb = hn_ref[...]
        for m in range(RG_WIDTH // RG_CHUNK_W):
            cols = slice(m * RG_CHUNK_W, (m + 1) * RG_CHUNK_W)
            if with_output:
                gate = _dot(hb, wgate_s[:, cols])
            xc = _rg_conv(xr_ref[:, cols].astype(F32), halo_ref, rows_ref, cols, reverse=True)
            h_bwd = _rg_gates_scan(xc, m * (RG_CHUNK_W // RG_BLOCK_W), wg_s, rows_ref, carry_ref,
                                   reverse=True)
            if with_output:
                yy_ref[:, cols] = ((hf_ref[:, cols].astype(F32) + h_bwd) * _silu(gate)).astype(BF16)
        if with_output:
            xn = xx_ref[...] + mod_ref[2:3, :] * _dot(yy_ref[...], wout_s[...])
            ms = jnp.mean(xn * xn, axis=-1, keepdims=True)
            _from_segment_major(xn * lax.rsqrt(ms + EPS) * fg_ref[...], stage_ref, out_ref, D_MODEL)

    pl.when(j == 0)(lambda: tile(with_output=False))
    pl.when(j > 0)(lambda: tile(with_output=True))


def _odd_bwd(xx, mods, hn, xr, hf, wgate, cw, cb, ga, gx, ba, bx, lam, wout, fg, nt):
    bsz = xx.shape[0]
    tile = _bwd_tile(nt)
    tok = lambda n: pl.BlockSpec((None, TM, n), lambda b, j: (b, tile(j), 0))
    consts = (wgate, cw, cb, ga, gx, ba, bx, lam, wout, fg)
    gate_scratch = pltpu.VMEM((RG_BLOCKS, RG_BLOCK_W, 2 * RG_BLOCK_W), BF16)
    out_block = lambda b, j: (b, jnp.where(j == 0, nt - 2, nt - 1 - j), 0)
    return pl.pallas_call(
        _odd_bwd_kernel,
        grid=(bsz, nt),
        in_specs=[tok(D_MODEL), _mod_spec(1, bsz, tile), tok(D_MODEL), tok(RG_WIDTH), tok(RG_WIDTH)]
        + [_const_spec(w) for w in consts],
        out_specs=pl.BlockSpec((None, TM, D_MODEL), out_block),
        out_shape=jax.ShapeDtypeStruct((bsz, (nt - 1) * TM, D_MODEL), F32),
        scratch_shapes=[pltpu.VMEM((D_MODEL // LANES, SUBLANES * SEG_PITCH, LANES), F32),
                        pltpu.VMEM((HALO_ROWS, RG_WIDTH), F32),
                        pltpu.VMEM((TM, RG_WIDTH), BF16),
                        pltpu.VMEM((SUBLANES, RG_WIDTH), F32),
                        pltpu.VMEM((RG_ROWS, RG_WIDTH), F32),
                        pltpu.VMEM(wgate.shape, BF16),
                        gate_scratch,
                        pltpu.VMEM(wout.shape, BF16)],
        compiler_params=_params(2),
        name="odd_bwd",
    )(xx, mods, hn, xr, hf, *[_operand(w) for w in consts])


def kernel(x, c, ctx, c_ctx, norm_g, w_mod, b_mod, e_w_in, e_w_a2, e_b_a2, e_gla_g, e_conv_w, e_w_out,
           o_w_in, o_conv_w, o_conv_b, o_w_a, o_b_a, o_w_x, o_b_x, o_lam, o_w_out, final_g):
    bsz, seq, _ = x.shape
    assert ctx.shape[1] == TM and seq % TM == 0 and bsz < SUBLANES
    assert w_mod.shape[0] == 2
    nt = 1 + seq // TM

    mods = _mods(c, c_ctx, w_mod, b_mod).reshape(w_mod.shape[0], SUBLANES, 3, D_MODEL)

    w_in = e_w_in[0]
    w_in_bf16 = w_in.astype(BF16)
    square = (D_MODEL, D_MODEL)
    offs = {}
    off = 0
    for name, size in (("q", GLA_QK), ("k", GLA_QK), ("v", GLA_V), ("ga", GLA_V), ("af", GLA_LOWRANK),
                       ("ab", GLA_LOWRANK), ("cb", SC_WIDTH), ("cc", SC_WIDTH), ("cx", SC_WIDTH),
                       ("gb", SC_WIDTH)):
        offs[name] = (off, off + size)
        off += size
    cols = lambda lo, hi: w_in[:, lo:hi].astype(BF16)
    assert offs["q"][0] == 0 and offs["v"][0] == D_MODEL and offs["ga"] == (2 * D_MODEL, 3 * D_MODEL)
    wqk, wv, wga = (_Window(w_in_bf16, square, (0, i)) for i in range(3))
    wa = jnp.pad(w_in[:, offs["af"][0]:offs["ab"][1]], ((0, 0), (0, LANES - 2 * GLA_LOWRANK))).astype(BF16)
    b2 = e_b_a2[0].reshape(1, 2 * GLA_QK)
    mats = (wqk, wv, wga, wa, cols(*offs["cb"]), cols(*offs["cc"]), cols(*offs["cx"]), cols(*offs["gb"]))
    (o_f, qi_b, ki_b, ke_b, dec_b, v, sga, y) = _even_in(
        ctx, x, mods, (norm_g[0:1], b2, e_conv_w[0], e_w_a2[0]), mats, nt)
    w_out = e_w_out[0]
    xx = _gla_bwd(qi_b, ki_b, ke_b, v, dec_b, o_f, sga, y, ctx, x, mods, e_gla_g[0:1],
                  _Window(w_out, square, (0, 0)), _Window(w_out, square, (1, 0)), nt)

    w_in = o_w_in[0]
    half = (D_MODEL, RG_WIDTH)
    wide = lambda a: a.reshape(1, RG_WIDTH)
    xr, h_f, hn = _odd_fwd(xx, mods, norm_g[1:2], _Window(w_in, half, (0, 0)), o_conv_w[0, 0],
                           wide(o_conv_b[0, 0]), o_w_a[0, 0], o_w_x[0, 0],
                           wide(o_b_a[0, 0]), wide(o_b_x[0, 0]), wide(o_lam[0, 0]), nt)
    return _odd_bwd(xx, mods, hn, xr, h_f, _Window(w_in, half, (0, 1)), o_conv_w[0, 1],
                    wide(o_conv_b[0, 1]), o_w_a[0, 1], o_w_x[0, 1],
                    wide(o_b_a[0, 1]), wide(o_b_x[0, 1]), wide(o_lam[0, 1]), o_w_out[0],
                    final_g.reshape(1, D_MODEL), nt)
```

```python
import jax
import jax.numpy as jnp
from jax import lax
from jax.experimental import pallas as pl
from jax.experimental.pallas import tpu as pltpu

F32 = jnp.float32
BF16 = jnp.bfloat16

D_MODEL = 1024
EPS = 1e-6
GLA_HEADS = 4
GLA_DK = 128
GLA_DV = 256
GLA_QK = GLA_HEADS * GLA_DK
GLA_V = GLA_HEADS * GLA_DV
GLA_LOWRANK = 16
GLA_TAU = 16.0
GLA_CHUNK = 64
GRID_W = 64
SC_WIDTH = D_MODEL
RG_WIDTH = 2 * D_MODEL
RG_BLOCKS = 16
RG_BLOCK_W = 128
RG_C = 8.0
RG_CONV = 4

TM = 256
CHUNKS = TM // GLA_CHUNK
GLA_PAIR = 2
SUBLANES = 8
LANES = 128
VMEM_LIMIT = 56 * 1024 * 1024
SEG_LEN = TM // SUBLANES
SEG_PITCH = SEG_LEN + SUBLANES
HALO_ROWS = (RG_CONV - 1) * SUBLANES
RG_CHUNK_W = 256
ROW_DECAY, ROW_BIAS_A, ROW_BIAS_X, ROW_CONV_B, ROW_CONV_W = 0, 1, 2, 3, 4
RG_ROWS = ROW_CONV_W + RG_CONV
LOG2_E = 1.4426950408889634
F32_TINY = 1.1754943508222875e-38


def _silu(x):
    return x * jax.nn.sigmoid(x)


def _log_sigmoid(z):
    return jnp.minimum(z, 0.0) - jnp.log(1.0 + jnp.exp(-jnp.abs(z)))


def _softplus(z):
    return jnp.maximum(z, 0.0) + jnp.log1p(jnp.exp(-jnp.abs(z)))


def _dot(a, b):
    return jnp.dot(a, b, preferred_element_type=F32)


def _dot_nt(a, b):
    return lax.dot_general(a, b, (((1,), (1,)), ((), ())), preferred_element_type=F32)


def _dot_tn(a, b):
    return lax.dot_general(a, b, (((0,), (0,)), ((), ())), preferred_element_type=F32)


def _modulated_norm(x, g, shift, scale):
    ms = jnp.mean(x * x, axis=-1, keepdims=True)
    return (x * lax.rsqrt(ms + EPS) * g) * (1.0 + scale) + shift


def _params(n_axes):
    return pltpu.CompilerParams(
        dimension_semantics=("arbitrary",) * n_axes,
        vmem_limit_bytes=VMEM_LIMIT)


def _first_step():
    return (pl.program_id(0) == 0) & (pl.program_id(1) == 0)


class _Window:
    def __init__(self, array, block_shape, block_index):
        self.array, self.shape, self.index = array, tuple(block_shape), tuple(block_index)


def _operand(w):
    return w.array if isinstance(w, _Window) else w


def _const_spec(w):
    index = w.index if isinstance(w, _Window) else (0,) * len(w.shape)
    return pl.BlockSpec(w.shape, lambda *_: index, pipeline_mode=pl.Buffered(1))


def _mod_spec(layer, bsz, tile):
    return pl.BlockSpec((None, None, 3, D_MODEL),
                        lambda b, t: (layer, jnp.where(tile(t) == 0, bsz, b), 0, 0))


def _mods_kernel(c_ref, cctx_ref, w_ref, b_ref, o_ref, s_ref):
    bsz = c_ref.shape[0]
    s_ref[...] = jnp.zeros_like(s_ref)
    s_ref[0:bsz, :] = c_ref[...]
    s_ref[bsz:bsz + 1, :] = cctx_ref[...]
    s = _silu(s_ref[...])
    o_ref[0] = jnp.dot(s, w_ref[0], preferred_element_type=F32,
                       precision=lax.Precision.HIGHEST) + b_ref[0]


def _mods(c, c_ctx, w_mod, b_mod):
    depth = w_mod.shape[0]
    nb = 3
    return pl.pallas_call(
        _mods_kernel,
        grid=(depth, nb),
        in_specs=[
            pl.BlockSpec(c.shape, lambda l, j: (0, 0)),
            pl.BlockSpec((1, D_MODEL), lambda l, j: (0, 0)),
            pl.BlockSpec((1, D_MODEL, D_MODEL), lambda l, j: (l, 0, j)),
            pl.BlockSpec((1, 1, D_MODEL), lambda l, j: (l, 0, j)),
        ],
        out_specs=pl.BlockSpec((1, SUBLANES, D_MODEL), lambda l, j: (l, 0, j)),
        out_shape=jax.ShapeDtypeStruct((depth, SUBLANES, 3 * D_MODEL), F32),
        scratch_shapes=[pltpu.VMEM((SUBLANES, D_MODEL), F32)],
        compiler_params=_params(2),
        name="mods",
    )(c, c_ctx.reshape(1, D_MODEL), w_mod, b_mod.reshape(depth, 1, 3 * D_MODEL))


def _chunk_rows(x, offset):
    return jnp.concatenate(
        [jnp.broadcast_to(x[c * GLA_CHUNK + offset:c * GLA_CHUNK + offset + 1, :], (GLA_CHUNK, x.shape[1]))
         for c in range(CHUNKS)], axis=0)


def _even_in_kernel(ctx_ref, x_ref, mod_ref, g_ref, b2_ref, cw_ref, wa2_ref,
                    wqk_in, wv_in, wga_in, wa_in, wcb_in, wcc_in, wcx_in, wgb_in,
                    of_ref, qib_ref, kib_ref, keb_ref, decb_ref, v_ref, sga_ref, y_ref,
                    wqk_ref, wv_ref, wga_ref, wa_ref, wcb_ref, wcc_ref, wcx_ref, wgb_ref,
                    lower_ref, upper_ref, w2_ref, qif_ref, kif_ref, kef_ref, decf_ref, st_ref):
    t = pl.program_id(1)

    @pl.when(t == 0)
    def _():
        st_ref[...] = jnp.zeros_like(st_ref)

    @pl.when(_first_step())
    def _():
        for src, dst in ((wqk_in, wqk_ref), (wv_in, wv_ref), (wga_in, wga_ref), (wa_in, wa_ref),
                         (wcb_in, wcb_ref), (wcc_in, wcc_ref), (wcx_in, wcx_ref), (wgb_in, wgb_ref)):
            dst[...] = src[...]
        r = lax.broadcasted_iota(jnp.int32, (TM, TM), 0)
        c = lax.broadcasted_iota(jnp.int32, (TM, TM), 1)
        same = (r // GLA_CHUNK) == (c // GLA_CHUNK)
        lower_ref[...] = jnp.where(same & (c <= r), 1.0, 0.0).astype(BF16)
        upper_ref[...] = jnp.where(same & (c >= r), 1.0, 0.0).astype(BF16)
        w2_ref[...] = jnp.zeros_like(w2_ref)
        w2_ref[0:GLA_LOWRANK, 0:GLA_QK] = wa2_ref[0].astype(BF16)
        w2_ref[GLA_LOWRANK:2 * GLA_LOWRANK, GLA_QK:] = wa2_ref[1].astype(BF16)

    xt = jnp.where(t == 0, ctx_ref[...], x_ref[...])
    h = _modulated_norm(xt, g_ref[...], mod_ref[0:1, :], mod_ref[1:2, :])
    hb = h.astype(BF16)

    a_lr = _dot(hb, wa_ref[...])
    z = _dot(a_lr.astype(BF16), w2_ref[...]) + b2_ref[...]
    qk = _dot(hb, wqk_ref[...])
    v_ref[...] = _dot(hb, wv_ref[...]).astype(BF16)

    lg = _log_sigmoid(z) * (1.0 / GLA_TAU)
    lgb = lg.astype(BF16)
    cs_f = _dot(lower_ref[...], lgb[:, :GLA_QK])
    cs_b = _dot(upper_ref[...], lgb[:, GLA_QK:])
    tot_f = _chunk_rows(cs_f, GLA_CHUNK - 1)
    tot_b = _chunk_rows(cs_b, 0)
    c_c = _dot(hb, wcc_ref[...])

    q = qk[:, :GLA_QK] * (GLA_DK ** -0.5)
    k = qk[:, GLA_QK:]
    qif_ref[...] = (q * jnp.exp(cs_f)).astype(BF16)
    kif_ref[...] = (k * jnp.exp(-cs_f)).astype(BF16)
    kef_ref[...] = (k * jnp.exp(tot_f - cs_f)).astype(BF16)
    qib_ref[...] = (q * jnp.exp(cs_b)).astype(BF16)
    kib_ref[...] = (k * jnp.exp(-cs_b)).astype(BF16)
    keb_ref[...] = (k * jnp.exp(tot_b - cs_b)).astype(BF16)
    for c in range(CHUNKS):
        r = c * GLA_CHUNK
        decf_ref[c:c + 1, :] = jnp.exp(tot_f[r:r + 1, :])
        decb_ref[c:c + 1, :] = jnp.exp(tot_b[r:r + 1, :])

    def emit(rows, vcols, o):
        of_ref[rows, vcols] = o.astype(BF16)

    mask = _gla_mask(reverse=False)
    gla = (qif_ref, kif_ref, kef_ref, v_ref)
    close = lambda hd, opened: _gla_head_close(hd, opened, qif_ref, v_ref, decf_ref, st_ref, emit, reverse=False)
    opened = _gla_head_open(0, *gla, mask)
    zz = c_c * _dot(hb, wcx_ref[...])
    close(0, opened)
    opened = _gla_head_open(1, *gla, mask)
    c_b = _dot(hb, wcb_ref[...])
    close(1, opened)
    opened = _gla_head_open(2, *gla, mask)
    g_b = _dot(hb, wgb_ref[...])
    close(2, opened)
    opened = _gla_head_open(3, *gla, mask)
    sga_ref[...] = _silu(_dot(hb, wga_ref[...])).astype(BF16)
    close(3, opened)

    pos = lax.broadcasted_iota(jnp.int32, (TM, SC_WIDTH), 0)
    row_len = jnp.where(t == 0, TM, GRID_W)
    in_row = pos & (row_len - 1)
    z_prev = jnp.where(in_row != 0, pltpu.roll(zz, 1, 0), 0.0)
    z_next = jnp.where(in_row != row_len - 1, pltpu.roll(zz, TM - 1, 0), 0.0)
    zc = cw_ref[0:1, :] * z_prev + cw_ref[1:2, :] * zz + cw_ref[2:3, :] * z_next
    y_ref[...] = (c_b * zc * _silu(g_b)).astype(BF16)


def _ctx_spec():
    return pl.BlockSpec((None, TM, D_MODEL), lambda b, t: (b, 0, 0))


def _latent_spec(tile, park):
    return pl.BlockSpec((None, TM, D_MODEL),
                        lambda b, t: (b, jnp.where(tile(t) == 0, park, tile(t) - 1), 0))


def _even_in(ctx, x, mods, rows, mats, nt):
    bsz = x.shape[0]
    tok = lambda n: pl.BlockSpec((None, TM, n), lambda b, t: (b, t, 0))
    dec = pl.BlockSpec((None, None, CHUNKS, GLA_QK), lambda b, t: (b, t, 0, 0))
    tok_shape = lambda n: jax.ShapeDtypeStruct((bsz, nt * TM, n), BF16)
    dec_shape = jax.ShapeDtypeStruct((bsz, nt, CHUNKS, GLA_QK), F32)
    consts = tuple(rows) + tuple(mats)
    return pl.pallas_call(
        _even_in_kernel,
        grid=(bsz, nt),
        in_specs=[
            _ctx_spec(), _latent_spec(lambda t: t, 0), _mod_spec(0, bsz, lambda t: t),
        ] + [_const_spec(w) for w in consts],
        out_specs=[tok(GLA_V), tok(GLA_QK), tok(GLA_QK), tok(GLA_QK), dec,
                   tok(GLA_V), tok(GLA_V), tok(SC_WIDTH)],
        out_shape=[tok_shape(GLA_V), tok_shape(GLA_QK), tok_shape(GLA_QK), tok_shape(GLA_QK), dec_shape,
                   tok_shape(GLA_V), tok_shape(GLA_V), tok_shape(SC_WIDTH)],
        scratch_shapes=[pltpu.VMEM(m.shape, BF16) for m in mats]
        + [pltpu.VMEM((TM, TM), BF16)] * 2 + [pltpu.VMEM((LANES, 2 * GLA_QK), BF16)]
        + [pltpu.VMEM((TM, GLA_QK), BF16)] * 3
        + [pltpu.VMEM((CHUNKS, GLA_QK), F32), pltpu.VMEM((GLA_HEADS, GLA_DV, GLA_DK), F32)],
        compiler_params=_params(2),
        name="even_in",
    )(ctx, x, mods, *[_operand(w) for w in consts])


def _chunk_slice(ch):
    return slice(ch * GLA_CHUNK, (ch + 1) * GLA_CHUNK)


def _gla_mask(reverse):
    r = lax.broadcasted_iota(jnp.int32, (TM, TM), 0)
    c = lax.broadcasted_iota(jnp.int32, (TM, TM), 1)
    same_chunk = (r // GLA_CHUNK) == (c // GLA_CHUNK)
    return same_chunk & ((c >= r) if reverse else (c <= r))


def _gla_head_open(hd, qi_ref, ki_ref, ke_ref, v_ref, mask):
    kcols = slice(hd * GLA_DK, (hd + 1) * GLA_DK)
    vcols = slice(hd * GLA_DV, (hd + 1) * GLA_DV)
    incr = [_dot_tn(v_ref[_chunk_slice(ch), vcols], ke_ref[_chunk_slice(ch), kcols]) for ch in range(CHUNKS)]
    scores = jnp.where(mask, _dot_nt(qi_ref[:, kcols], ki_ref[:, kcols]), 0.0).astype(BF16)
    return incr, scores


def _gla_head_close(hd, opened, qi_ref, v_ref, dec_ref, st_ref, emit, reverse):
    incr, scores = opened
    kcols = slice(hd * GLA_DK, (hd + 1) * GLA_DK)
    vcols = slice(hd * GLA_DV, (hd + 1) * GLA_DV)
    state = st_ref[hd]
    before = {}
    for ch in (range(CHUNKS - 1, -1, -1) if reverse else range(CHUNKS)):
        before[ch] = state.astype(BF16)
        state = dec_ref[ch:ch + 1, kcols] * state + incr[ch]
    st_ref[hd] = state
    intra = _dot(scores, v_ref[:, vcols])
    for ch in range(CHUNKS):
        rows = _chunk_slice(ch)
        emit(rows, vcols, intra[rows, :] + _dot_nt(qi_ref[rows, kcols], before[ch]))


def _gla_tile(qi_ref, ki_ref, ke_ref, v_ref, dec_ref, st_ref, emit, reverse):
    mask = _gla_mask(reverse)
    opened = [_gla_head_open(hd, qi_ref, ki_ref, ke_ref, v_ref, mask) for hd in range(GLA_HEADS)]
    for hd in range(GLA_HEADS):
        _gla_head_close(hd, opened[hd], qi_ref, v_ref, dec_ref, st_ref, emit, reverse)


def _gla_bwd_kernel(qi_ref, ki_ref, ke_ref, v_ref, dec_ref, of_ref, sga_ref, y_ref, ctx_ref, x_ref,
                    modc_ref, modl_ref, gg_ref, wo_in, wy_in, out_ref, st_ref, ob_ref, wo_ref, wy_ref, stage_ref):
    j = pl.program_id(1)
    samples = range(GLA_PAIR)

    @pl.when(_first_step())
    def _():
        wo_ref[...] = wo_in[...].astype(BF16)
        wy_ref[...] = wy_in[...].astype(BF16)

    @pl.when(j == 0)
    def _():
        st_ref[...] = jnp.zeros_like(st_ref)

    proj_y = _dot(y_ref[...].reshape(GLA_PAIR * TM, SC_WIDTH), wy_ref[...])

    def emit_to(r):
        def emit(rows, vcols, o):
            ob_ref[r, rows, vcols] = o
        return emit

    mask = _gla_mask(reverse=True)
    opened = [[_gla_head_open(hd, qi_ref.at[r], ki_ref.at[r], ke_ref.at[r], v_ref.at[r], mask)
               for hd in range(GLA_HEADS)] for r in samples]
    for hd in range(GLA_HEADS):
        for r in samples:
            _gla_head_close(hd, opened[r][hd], qi_ref.at[r], v_ref.at[r], dec_ref.at[r], st_ref.at[r],
                            emit_to(r), reverse=True)

    inner = []
    for r in samples:
        for hd in range(GLA_HEADS):
            vcols = slice(hd * GLA_DV, (hd + 1) * GLA_DV)
            o = of_ref[r, :, vcols].astype(F32) + ob_ref[r, :, vcols]
            ms = jnp.mean(o * o, axis=-1, keepdims=True)
            on = o * lax.rsqrt(ms + EPS) * gg_ref[...]
            inner.append((on * sga_ref[r, :, vcols].astype(F32)).astype(BF16))
    inner = jnp.concatenate([jnp.concatenate(inner[r * GLA_HEADS:(r + 1) * GLA_HEADS], axis=-1)
                             for r in samples], axis=0)
    proj = _dot(inner, wo_ref[...]) + proj_y
    for r in samples:
        gate = jnp.where(j == 0, modc_ref[2:3, :], modl_ref[r, 2:3, :])
        ob_ref[r] = jnp.where(j == 0, ctx_ref[r], x_ref[r]) + gate * proj[r * TM:(r + 1) * TM, :]
        out_ref[r] = _to_segment_major(ob_ref.at[r], stage_ref, D_MODEL)


def _bwd_tile(nt):
    return lambda j: jnp.where(j == 0, 0, nt - j)


def _gla_bwd(qi, ki, ke, v, dec, o_f, sga, y, ctx, x, mods, gg, wo, wy, nt):
    assert GLA_V == D_MODEL
    bsz = qi.shape[0]
    assert bsz % GLA_PAIR == 0
    tile = _bwd_tile(nt)
    tok = lambda n: pl.BlockSpec((GLA_PAIR, TM, n), lambda p, j: (p, tile(j), 0))
    latent_block = lambda p, j: (p, jnp.where(j == 0, nt - 2, tile(j) - 1), 0)
    return pl.pallas_call(
        _gla_bwd_kernel,
        grid=(bsz // GLA_PAIR, nt),
        in_specs=[tok(GLA_QK), tok(GLA_QK), tok(GLA_QK), tok(GLA_V),
                  pl.BlockSpec((GLA_PAIR, None, CHUNKS, GLA_QK), lambda p, j: (p, tile(j), 0, 0)),
                  tok(GLA_V), tok(GLA_V), tok(SC_WIDTH),
                  pl.BlockSpec((GLA_PAIR, TM, D_MODEL), lambda p, j: (p, 0, 0)),
                  pl.BlockSpec((GLA_PAIR, TM, D_MODEL), latent_block),
                  pl.BlockSpec((None, None, 3, D_MODEL), lambda p, j: (0, bsz, 0, 0)),
                  pl.BlockSpec((None, GLA_PAIR, 3, D_MODEL), lambda p, j: (0, p, 0, 0)),
                  _const_spec(gg), _const_spec(wo), _const_spec(wy)],
        out_specs=tok(D_MODEL),
        out_shape=jax.ShapeDtypeStruct((bsz, nt * TM, D_MODEL), F32),
        scratch_shapes=[pltpu.VMEM((GLA_PAIR, GLA_HEADS, GLA_DV, GLA_DK), F32),
                        pltpu.VMEM((GLA_PAIR, TM, GLA_V), F32),
                        pltpu.VMEM(wo.shape, BF16),
                        pltpu.VMEM(wy.shape, BF16),
                        pltpu.VMEM((D_MODEL // LANES, SUBLANES * SEG_PITCH, LANES), F32)],
        compiler_params=_params(2),
        name="gla_bwd",
    )(qi, ki, ke, v, dec, o_f, sga, y, ctx, x, mods, mods, gg, _operand(wo), _operand(wy))


def _to_segment_major(src_ref, stage_ref, width):
    slabs = width // LANES
    for n in range(slabs):
        for s in range(SUBLANES):
            stage_ref[n, s * SEG_PITCH:s * SEG_PITCH + SEG_LEN, :] = (
                src_ref[s * SEG_LEN:(s + 1) * SEG_LEN, n * LANES:(n + 1) * LANES])
    groups = [
        jnp.concatenate([stage_ref[n, pl.ds(j, SUBLANES, stride=SEG_PITCH), :] for n in range(slabs)], axis=1)
        for j in range(SEG_LEN)]
    return jnp.concatenate(groups, axis=0)


def _from_segment_major(val, stage_ref, dst_ref, width):
    slabs = width // LANES
    for j in range(SEG_LEN):
        for n in range(slabs):
            stage_ref[n, pl.ds(j, SUBLANES, stride=SEG_PITCH), :] = (
                val[j * SUBLANES:(j + 1) * SUBLANES, n * LANES:(n + 1) * LANES])
    for n in range(slabs):
        for s in range(SUBLANES):
            dst_ref[s * SEG_LEN:(s + 1) * SEG_LEN, n * LANES:(n + 1) * LANES] = (
                stage_ref[n, s * SEG_PITCH:s * SEG_PITCH + SEG_LEN, :])


def _rg_conv(xr, halo_ref, rows_ref, cols, reverse):
    row = lax.broadcasted_iota(jnp.int32, (SUBLANES, xr.shape[1]), 0)
    edge = []
    for g in range(RG_CONV - 1):
        grp = slice(g * SUBLANES, (g + 1) * SUBLANES)
        if reverse:
            edge.append(jnp.where(row == SUBLANES - 1, pltpu.roll(halo_ref[grp, cols], SUBLANES - 1, 0),
                                  pltpu.roll(xr[grp, :], SUBLANES - 1, 0)))
        else:
            cur = xr[TM - (RG_CONV - 1 - g) * SUBLANES:TM - (RG_CONV - 2 - g) * SUBLANES, :]
            edge.append(jnp.where(row == 0, pltpu.roll(halo_ref[grp, cols], 1, 0), pltpu.roll(cur, 1, 0)))
    tap_w = lambda j: rows_ref[ROW_CONV_W + j:ROW_CONV_W + j + 1, cols]
    xc = rows_ref[ROW_CONV_B:ROW_CONV_B + 1, cols] + tap_w(RG_CONV - 1) * xr
    for k in range(1, RG_CONV):
        if reverse:
            tap = jnp.concatenate([xr[k * SUBLANES:, :]] + edge[:k], axis=0)
        else:
            tap = jnp.concatenate(edge[RG_CONV - 1 - k:] + [xr[:TM - k * SUBLANES, :]], axis=0)
        xc = xc + tap_w(RG_CONV - 1 - k) * tap
    halo_ref[:, cols] = xr[:HALO_ROWS, :] if reverse else xr[TM - HALO_ROWS:, :]
    return xc


def _rg_setup(lam_ref, ba_ref, bx_ref, cb_ref, cw_ref, wa_ref, wx_ref, rows_ref, wg_s, copies):
    wg_s[:, :, :RG_BLOCK_W] = wa_ref[...].astype(BF16)
    wg_s[:, :, RG_BLOCK_W:] = wx_ref[...].astype(BF16)
    rows_ref[ROW_DECAY:ROW_DECAY + 1, :] = (-0.5 * RG_C * LOG2_E) * _softplus(-lam_ref[...])
    rows_ref[ROW_BIAS_A:ROW_BIAS_A + 1, :] = 0.5 * ba_ref[...]
    rows_ref[ROW_BIAS_X:ROW_BIAS_X + 1, :] = 0.5 * bx_ref[...]
    rows_ref[ROW_CONV_B:ROW_CONV_B + 1, :] = 0.5 * cb_ref[...]
    rows_ref[ROW_CONV_W:ROW_CONV_W + RG_CONV, :] = 0.5 * cw_ref[...]
    for src_ref, dst_ref in copies:
        dst_ref[...] = src_ref[...].astype(BF16)


def _rg_gates_scan(xc, first_block, wg_ref, rows_ref, carry_ref, reverse):
    row = lax.broadcasted_iota(jnp.int32, (SUBLANES, RG_BLOCK_W), 0)
    shift = SUBLANES - 1 if reverse else 1
    steps = range(SEG_LEN - 1, -1, -1) if reverse else range(SEG_LEN)
    out = []
    for i in range(xc.shape[1] // RG_BLOCK_W):
        n = first_block + i
        cols = slice(n * RG_BLOCK_W, (n + 1) * RG_BLOCK_W)
        xb = xc[:, i * RG_BLOCK_W:(i + 1) * RG_BLOCK_W]
        gates = _dot(xb.astype(BF16), wg_ref[n])
        th_r = jnp.tanh(gates[:, :RG_BLOCK_W] + rows_ref[ROW_BIAS_A:ROW_BIAS_A + 1, cols])
        th_i = jnp.tanh(gates[:, RG_BLOCK_W:] + rows_ref[ROW_BIAS_X:ROW_BIAS_X + 1, cols])
        half_decay = rows_ref[ROW_DECAY:ROW_DECAY + 1, cols]
        a = jnp.exp2(th_r * half_decay + half_decay)
        y = 1.0 - a * a
        root = y * lax.rsqrt(jnp.maximum(y, F32_TINY))
        u = root * ((th_i + 1.0) * xb)

        h = jnp.zeros((SUBLANES, RG_BLOCK_W), F32)
        prod = jnp.ones((SUBLANES, RG_BLOCK_W), F32)
        local, prods = {}, {}
        for j in steps:
            grp = slice(j * SUBLANES, (j + 1) * SUBLANES)
            h = a[grp, :] * h + u[grp, :]
            prod = prod * a[grp, :]
            local[j], prods[j] = h, prod

        entering = carry_ref[:, cols]
        for s in range(SUBLANES - 1):
            nxt = pltpu.roll(prod * entering + h, shift, 0)
            target = SUBLANES - 2 - s if reverse else s + 1
            entering = jnp.where(row == target, nxt, entering)
        carry_ref[:, cols] = pltpu.roll(prod * entering + h, shift, 0)

        out.append(jnp.concatenate([local[j] + prods[j] * entering for j in range(SEG_LEN)], axis=0))
    return jnp.concatenate(out, axis=1)


def _odd_fwd_kernel(xx_ref, mod_ref, g_ref, wx_ref, cw_ref, cb_ref, ga_ref, gx_ref, ba_ref, bx_ref, lam_ref,
                    xr_ref, hf_ref, hn_ref, halo_ref, carry_ref, rows_ref, wx_s, wg_s):
    t = pl.program_id(1)

    @pl.when(_first_step())
    def _():
        _rg_setup(lam_ref, ba_ref, bx_ref, cb_ref, cw_ref, ga_ref, gx_ref, rows_ref, wg_s, ((wx_ref, wx_s),))

    @pl.when(t == 0)
    def _():
        carry_ref[...] = jnp.zeros_like(carry_ref)

    @pl.when(t <= 1)
    def _():
        halo_ref[...] = jnp.zeros_like(halo_ref)

    hb = _modulated_norm(xx_ref[...], g_ref[...], mod_ref[0:1, :], mod_ref[1:2, :]).astype(BF16)
    hn_ref[...] = hb
    chunks = RG_WIDTH // RG_CHUNK_W
    project = lambda m: _dot(hb, wx_s[:, m * RG_CHUNK_W:(m + 1) * RG_CHUNK_W])
    xr_next = project(0)
    for m in range(chunks):
        cols = slice(m * RG_CHUNK_W, (m + 1) * RG_CHUNK_W)
        xr = xr_next
        if m + 1 < chunks:
            xr_next = project(m + 1)
        xr_ref[:, cols] = xr.astype(BF16)
        xc = _rg_conv(xr, halo_ref, rows_ref, cols, reverse=False)
        hf_ref[:, cols] = _rg_gates_scan(xc, m * (RG_CHUNK_W // RG_BLOCK_W), wg_s, rows_ref, carry_ref,
                                         reverse=False).astype(BF16)


def _odd_fwd(xx, mods, g, wx, cw, cb, ga, gx, ba, bx, lam, nt):
    bsz = xx.shape[0]
    tok = lambda n: pl.BlockSpec((None, TM, n), lambda b, t: (b, t, 0))
    consts = (g, wx, cw, cb, ga, gx, ba, bx, lam)
    gate_scratch = pltpu.VMEM((RG_BLOCKS, RG_BLOCK_W, 2 * RG_BLOCK_W), BF16)
    wide = jax.ShapeDtypeStruct((bsz, nt * TM, RG_WIDTH), BF16)
    return pl.pallas_call(
        _odd_fwd_kernel,
        grid=(bsz, nt),
        in_specs=[tok(D_MODEL), _mod_spec(1, bsz, lambda t: t)] + [_const_spec(w) for w in consts],
        out_specs=[tok(RG_WIDTH), tok(RG_WIDTH), tok(D_MODEL)],
        out_shape=[wide, wide, jax.ShapeDtypeStruct((bsz, nt * TM, D_MODEL), BF16)],
        scratch_shapes=[pltpu.VMEM((HALO_ROWS, RG_WIDTH), F32),
                        pltpu.VMEM((SUBLANES, RG_WIDTH), F32),
                        pltpu.VMEM((RG_ROWS, RG_WIDTH), F32),
                        pltpu.VMEM(wx.shape, BF16),
                        gate_scratch],
        compiler_params=_params(2),
        name="odd_fwd",
    )(xx, mods, *[_operand(w) for w in consts])


def _odd_bwd_kernel(xx_ref, mod_ref, hn_ref, xr_ref, hf_ref, wgate_ref, cw_ref, cb_ref, ga_ref, gx_ref, ba_ref,
                    bx_ref, lam_ref, wout_ref, fg_ref, out_ref,
                    stage_ref, halo_ref, yy_ref, carry_ref, rows_ref, wgate_s, wg_s, wout_s):
    j = pl.program_id(1)

    @pl.when(_first_step())
    def _():
        _rg_setup(lam_ref, ba_ref, bx_ref, cb_ref, cw_ref, ga_ref, gx_ref, rows_ref, wg_s,
                  ((wgate_ref, wgate_s), (wout_ref, wout_s)))

    @pl.when(j == 0)
    def _():
        carry_ref[...] = jnp.zeros_like(carry_ref)

    @pl.when(j <= 1)
    def _():
        halo_ref[...] = jnp.zeros_like(halo_ref)

    def tile(with_output):
        hb = hn_ref[...]
        for m in range(RG_WIDTH // RG_CHUNK_W):
            cols = slice(m * RG_CHUNK_W, (m + 1) * RG_CHUNK_W)
            if with_output:
                gate = _dot(hb, wgate_s[:, cols])
            xc = _rg_conv(xr_ref[:, cols].astype(F32), halo_ref, rows_ref, cols, reverse=True)
            h_bwd = _rg_gates_scan(xc, m * (RG_CHUNK_W // RG_BLOCK_W), wg_s, rows_ref, carry_ref,
                                   reverse=True)
            if with_output:
                yy_ref[:, cols] = ((hf_ref[:, cols].astype(F32) + h_bwd) * _silu(gate)).astype(BF16)
        if with_output:
            xn = xx_ref[...] + mod_ref[2:3, :] * _dot(yy_ref[...], wout_s[...])
            ms = jnp.mean(xn * xn, axis=-1, keepdims=True)
            _from_segment_major(xn * lax.rsqrt(ms + EPS) * fg_ref[...], stage_ref, out_ref, D_MODEL)

    pl.when(j == 0)(lambda: tile(with_output=False))
    pl.when(j > 0)(lambda: tile(with_output=True))


def _odd_bwd(xx, mods, hn, xr, hf, wgate, cw, cb, ga, gx, ba, bx, lam, wout, fg, nt):
    bsz = xx.shape[0]
    tile = _bwd_tile(nt)
    tok = lambda n: pl.BlockSpec((None, TM, n), lambda b, j: (b, tile(j), 0))
    consts = (wgate, cw, cb, ga, gx, ba, bx, lam, wout, fg)
    gate_scratch = pltpu.VMEM((RG_BLOCKS, RG_BLOCK_W, 2 * RG_BLOCK_W), BF16)
    out_block = lambda b, j: (b, jnp.where(j == 0, nt - 2, nt - 1 - j), 0)
    return pl.pallas_call(
        _odd_bwd_kernel,
        grid=(bsz, nt),
        in_specs=[tok(D_MODEL), _mod_spec(1, bsz, tile), tok(D_MODEL), tok(RG_WIDTH), tok(RG_WIDTH)]
        + [_const_spec(w) for w in consts],
        out_specs=pl.BlockSpec((None, TM, D_MODEL), out_block),
        out_shape=jax.ShapeDtypeStruct((bsz, (nt - 1) * TM, D_MODEL), F32),
        scratch_shapes=[pltpu.VMEM((D_MODEL // LANES, SUBLANES * SEG_PITCH, LANES), F32),
                        pltpu.VMEM((HALO_ROWS, RG_WIDTH), F32),
                        pltpu.VMEM((TM, RG_WIDTH), BF16),
                        pltpu.VMEM((SUBLANES, RG_WIDTH), F32),
                        pltpu.VMEM((RG_ROWS, RG_WIDTH), F32),
                        pltpu.VMEM(wgate.shape, BF16),
                        gate_scratch,
                        pltpu.VMEM(wout.shape, BF16)],
        compiler_params=_params(2),
        name="odd_bwd",
    )(xx, mods, hn, xr, hf, *[_operand(w) for w in consts])


def kernel(x, c, ctx, c_ctx, norm_g, w_mod, b_mod, e_w_in, e_w_a2, e_b_a2, e_gla_g, e_conv_w, e_w_out,
           o_w_in, o_conv_w, o_conv_b, o_w_a, o_b_a, o_w_x, o_b_x, o_lam, o_w_out, final_g):
    bsz, seq, _ = x.shape
    assert ctx.shape[1] == TM and seq % TM == 0 and bsz < SUBLANES
    assert w_mod.shape[0] == 2
    nt = 1 + seq // TM

    mods = _mods(c, c_ctx, w_mod, b_mod).reshape(w_mod.shape[0], SUBLANES, 3, D_MODEL)

    w_in = e_w_in[0]
    w_in_bf16 = w_in.astype(BF16)
    square = (D_MODEL, D_MODEL)
    offs = {}
    off = 0
    for name, size in (("q", GLA_QK), ("k", GLA_QK), ("v", GLA_V), ("ga", GLA_V), ("af", GLA_LOWRANK),
                       ("ab", GLA_LOWRANK), ("cb", SC_WIDTH), ("cc", SC_WIDTH), ("cx", SC_WIDTH),
                       ("gb", SC_WIDTH)):
        offs[name] = (off, off + size)
        off += size
    cols = lambda lo, hi: w_in[:, lo:hi].astype(BF16)
    assert offs["q"][0] == 0 and offs["v"][0] == D_MODEL and offs["ga"] == (2 * D_MODEL, 3 * D_MODEL)
    wqk, wv, wga = (_Window(w_in_bf16, square, (0, i)) for i in range(3))
    wa = jnp.pad(w_in[:, offs["af"][0]:offs["ab"][1]], ((0, 0), (0, LANES - 2 * GLA_LOWRANK))).astype(BF16)
    b2 = e_b_a2[0].reshape(1, 2 * GLA_QK)
    mats = (wqk, wv, wga, wa, cols(*offs["cb"]), cols(*offs["cc"]), cols(*offs["cx"]), cols(*offs["gb"]))
    (o_f, qi_b, ki_b, ke_b, dec_b, v, sga, y) = _even_in(
        ctx, x, mods, (norm_g[0:1], b2, e_conv_w[0], e_w_a2[0]), mats, nt)
    w_out = e_w_out[0]
    xx = _gla_bwd(qi_b, ki_b, ke_b, v, dec_b, o_f, sga, y, ctx, x, mods, e_gla_g[0:1],
                  _Window(w_out, square, (0, 0)), _Window(w_out, square, (1, 0)), nt)

    w_in = o_w_in[0]
    half = (D_MODEL, RG_WIDTH)
    wide = lambda a: a.reshape(1, RG_WIDTH)
    xr, h_f, hn = _odd_fwd(xx, mods, norm_g[1:2], _Window(w_in, half, (0, 0)), o_conv_w[0, 0],
                           wide(o_conv_b[0, 0]), o_w_a[0, 0], o_w_x[0, 0],
                           wide(o_b_a[0, 0]), wide(o_b_x[0, 0]), wide(o_lam[0, 0]), nt)
    return _odd_bwd(xx, mods, hn, xr, h_f, _Window(w_in, half, (0, 1)), o_conv_w[0, 1],
                    wide(o_conv_b[0, 1]), o_w_a[0, 1], o_w_x[0, 1],
                    wide(o_b_a[0, 1]), wide(o_b_x[0, 1]), wide(o_lam[0, 1]), o_w_out[0],
                    final_g.reshape(1, D_MODEL), nt)
```

```python
import jax
import jax.numpy as jnp
from jax import lax
from jax.experimental import pallas as pl
from jax.experimental.pallas import tpu as pltpu

F32 = jnp.float32
BF16 = jnp.bfloat16

D_MODEL = 1024
EPS = 1e-6
GLA_HEADS = 4
GLA_DK = 128
GLA_DV = 256
GLA_QK = GLA_HEADS * GLA_DK
GLA_V = GLA_HEADS * GLA_DV
GLA_LOWRANK = 16
GLA_TAU = 16.0
GLA_CHUNK = 64
GRID_W = 64
SC_WIDTH = D_MODEL
RG_WIDTH = 2 * D_MODEL
RG_BLOCKS = 16
RG_BLOCK_W = 128
RG_C = 8.0
RG_CONV = 4

TM = 256
CHUNKS = TM // GLA_CHUNK
GLA_PAIR = 2
RG_PAIR = 2
SUBLANES = 8
LANES = 128
VMEM_LIMIT = 56 * 1024 * 1024
SEG_LEN = TM // SUBLANES
SEG_PITCH = SEG_LEN + SUBLANES
HALO_ROWS = (RG_CONV - 1) * SUBLANES
RG_CHUNK_W = 256
ROW_DECAY, ROW_BIAS_A, ROW_BIAS_X, ROW_CONV_B, ROW_CONV_W = 0, 1, 2, 3, 4
RG_ROWS = ROW_CONV_W + RG_CONV
LOG2_E = 1.4426950408889634
F32_TINY = 1.1754943508222875e-38


def _silu(x):
    return x * jax.nn.sigmoid(x)


def _log_sigmoid(z):
    return jnp.minimum(z, 0.0) - jnp.log(1.0 + jnp.exp(-jnp.abs(z)))


def _softplus(z):
    return jnp.maximum(z, 0.0) + jnp.log1p(jnp.exp(-jnp.abs(z)))


def _dot(a, b):
    return jnp.dot(a, b, preferred_element_type=F32)


def _dot_nt(a, b):
    return lax.dot_general(a, b, (((1,), (1,)), ((), ())), preferred_element_type=F32)


def _dot_tn(a, b):
    return lax.dot_general(a, b, (((0,), (0,)), ((), ())), preferred_element_type=F32)


def _modulated_norm(x, g, shift, scale):
    ms = jnp.mean(x * x, axis=-1, keepdims=True)
    return (x * lax.rsqrt(ms + EPS) * g) * (1.0 + scale) + shift


def _params(n_axes):
    return pltpu.CompilerParams(
        dimension_semantics=("arbitrary",) * n_axes,
        vmem_limit_bytes=VMEM_LIMIT)


def _first_step():
    return (pl.program_id(0) == 0) & (pl.program_id(1) == 0)


class _Window:
    def __init__(self, array, block_shape, block_index):
        self.array, self.shape, self.index = array, tuple(block_shape), tuple(block_index)


def _operand(w):
    return w.array if isinstance(w, _Window) else w


def _const_spec(w):
    index = w.index if isinstance(w, _Window) else (0,) * len(w.shape)
    return pl.BlockSpec(w.shape, lambda *_: index, pipeline_mode=pl.Buffered(1))


def _mod_spec(layer, bsz, tile):
    return pl.BlockSpec((None, None, 3, D_MODEL),
                        lambda b, t: (layer, jnp.where(tile(t) == 0, bsz, b), 0, 0))


def _mods_kernel(c_ref, cctx_ref, w_ref, b_ref, o_ref, s_ref):
    bsz = c_ref.shape[0]
    s_ref[...] = jnp.zeros_like(s_ref)
    s_ref[0:bsz, :] = c_ref[...]
    s_ref[bsz:bsz + 1, :] = cctx_ref[...]
    s = _silu(s_ref[...])
    o_ref[0] = jnp.dot(s, w_ref[0], preferred_element_type=F32,
                       precision=lax.Precision.HIGHEST) + b_ref[0]


def _mods(c, c_ctx, w_mod, b_mod):
    depth = w_mod.shape[0]
    nb = 3
    return pl.pallas_call(
        _mods_kernel,
        grid=(depth, nb),
        in_specs=[
            pl.BlockSpec(c.shape, lambda l, j: (0, 0)),
            pl.BlockSpec((1, D_MODEL), lambda l, j: (0, 0)),
            pl.BlockSpec((1, D_MODEL, D_MODEL), lambda l, j: (l, 0, j)),
            pl.BlockSpec((1, 1, D_MODEL), lambda l, j: (l, 0, j)),
        ],
        out_specs=pl.BlockSpec((1, SUBLANES, D_MODEL), lambda l, j: (l, 0, j)),
        out_shape=jax.ShapeDtypeStruct((depth, SUBLANES, 3 * D_MODEL), F32),
        scratch_shapes=[pltpu.VMEM((SUBLANES, D_MODEL), F32)],
        compiler_params=_params(2),
        name="mods",
    )(c, c_ctx.reshape(1, D_MODEL), w_mod, b_mod.reshape(depth, 1, 3 * D_MODEL))


def _chunk_rows(x, offset):
    return jnp.concatenate(
        [jnp.broadcast_to(x[c * GLA_CHUNK + offset:c * GLA_CHUNK + offset + 1, :], (GLA_CHUNK, x.shape[1]))
         for c in range(CHUNKS)], axis=0)


def _even_in_kernel(ctx_ref, x_ref, mod_ref, g_ref, b2_ref, cw_ref, wa2_ref,
                    wqk_in, wv_in, wga_in, wa_in, wcb_in, wcc_in, wcx_in, wgb_in,
                    of_ref, qib_ref, kib_ref, keb_ref, decb_ref, v_ref, sga_ref, y_ref,
                    wqk_ref, wv_ref, wga_ref, wa_ref, wcb_ref, wcc_ref, wcx_ref, wgb_ref,
                    lower_ref, upper_ref, w2_ref, qif_ref, kif_ref, kef_ref, decf_ref, st_ref):
    t = pl.program_id(1)

    @pl.when(t == 0)
    def _():
        st_ref[...] = jnp.zeros_like(st_ref)

    @pl.when(_first_step())
    def _():
        for src, dst in ((wqk_in, wqk_ref), (wv_in, wv_ref), (wga_in, wga_ref), (wa_in, wa_ref),
                         (wcb_in, wcb_ref), (wcc_in, wcc_ref), (wcx_in, wcx_ref), (wgb_in, wgb_ref)):
            dst[...] = src[...]
        r = lax.broadcasted_iota(jnp.int32, (TM, TM), 0)
        c = lax.broadcasted_iota(jnp.int32, (TM, TM), 1)
        same = (r // GLA_CHUNK) == (c // GLA_CHUNK)
        lower_ref[...] = jnp.where(same & (c <= r), 1.0, 0.0).astype(BF16)
        upper_ref[...] = jnp.where(same & (c >= r), 1.0, 0.0).astype(BF16)
        w2_ref[...] = jnp.zeros_like(w2_ref)
        w2_ref[0:GLA_LOWRANK, 0:GLA_QK] = wa2_ref[0].astype(BF16)
        w2_ref[GLA_LOWRANK:2 * GLA_LOWRANK, GLA_QK:] = wa2_ref[1].astype(BF16)

    xt = jnp.where(t == 0, ctx_ref[...], x_ref[...])
    h = _modulated_norm(xt, g_ref[...], mod_ref[0:1, :], mod_ref[1:2, :])
    hb = h.astype(BF16)

    a_lr = _dot(hb, wa_ref[...])
    z = _dot(a_lr.astype(BF16), w2_ref[...]) + b2_ref[...]
    qk = _dot(hb, wqk_ref[...])
    v_ref[...] = _dot(hb, wv_ref[...]).astype(BF16)

    lg = _log_sigmoid(z) * (1.0 / GLA_TAU)
    lgb = lg.astype(BF16)
    cs_f = _dot(lower_ref[...], lgb[:, :GLA_QK])
    cs_b = _dot(upper_ref[...], lgb[:, GLA_QK:])
    tot_f = _chunk_rows(cs_f, GLA_CHUNK - 1)
    tot_b = _chunk_rows(cs_b, 0)
    c_c = _dot(hb, wcc_ref[...])

    q = qk[:, :GLA_QK] * (GLA_DK ** -0.5)
    k = qk[:, GLA_QK:]
    qif_ref[...] = (q * jnp.exp(cs_f)).astype(BF16)
    kif_ref[...] = (k * jnp.exp(-cs_f)).astype(BF16)
    kef_ref[...] = (k * jnp.exp(tot_f - cs_f)).astype(BF16)
    qib_ref[...] = (q * jnp.exp(cs_b)).astype(BF16)
    kib_ref[...] = (k * jnp.exp(-cs_b)).astype(BF16)
    keb_ref[...] = (k * jnp.exp(tot_b - cs_b)).astype(BF16)
    for c in range(CHUNKS):
        r = c * GLA_CHUNK
        decf_ref[c:c + 1, :] = jnp.exp(tot_f[r:r + 1, :])
        decb_ref[c:c + 1, :] = jnp.exp(tot_b[r:r + 1, :])

    def emit(rows, vcols, o):
        of_ref[rows, vcols] = o.astype(BF16)

    mask = _gla_mask(reverse=False)
    gla = (qif_ref, kif_ref, kef_ref, v_ref)
    close = lambda hd, opened: _gla_head_close(hd, opened, qif_ref, v_ref, decf_ref, st_ref, emit, reverse=False)
    opened = _gla_head_open(0, *gla, mask)
    zz = c_c * _dot(hb, wcx_ref[...])
    close(0, opened)
    opened = _gla_head_open(1, *gla, mask)
    c_b = _dot(hb, wcb_ref[...])
    close(1, opened)
    opened = _gla_head_open(2, *gla, mask)
    g_b = _dot(hb, wgb_ref[...])
    close(2, opened)
    opened = _gla_head_open(3, *gla, mask)
    sga_ref[...] = _silu(_dot(hb, wga_ref[...])).astype(BF16)
    close(3, opened)

    pos = lax.broadcasted_iota(jnp.int32, (TM, SC_WIDTH), 0)
    row_len = jnp.where(t == 0, TM, GRID_W)
    in_row = pos & (row_len - 1)
    z_prev = jnp.where(in_row != 0, pltpu.roll(zz, 1, 0), 0.0)
    z_next = jnp.where(in_row != row_len - 1, pltpu.roll(zz, TM - 1, 0), 0.0)
    zc = cw_ref[0:1, :] * z_prev + cw_ref[1:2, :] * zz + cw_ref[2:3, :] * z_next
    y_ref[...] = (c_b * zc * _silu(g_b)).astype(BF16)


def _ctx_spec():
    return pl.BlockSpec((None, TM, D_MODEL), lambda b, t: (b, 0, 0))


def _latent_spec(tile, park):
    return pl.BlockSpec((None, TM, D_MODEL),
                        lambda b, t: (b, jnp.where(tile(t) == 0, park, tile(t) - 1), 0))


def _even_in(ctx, x, mods, rows, mats, nt):
    bsz = x.shape[0]
    tok = lambda n: pl.BlockSpec((None, TM, n), lambda b, t: (b, t, 0))
    dec = pl.BlockSpec((None, None, CHUNKS, GLA_QK), lambda b, t: (b, t, 0, 0))
    tok_shape = lambda n: jax.ShapeDtypeStruct((bsz, nt * TM, n), BF16)
    dec_shape = jax.ShapeDtypeStruct((bsz, nt, CHUNKS, GLA_QK), F32)
    consts = tuple(rows) + tuple(mats)
    return pl.pallas_call(
        _even_in_kernel,
        grid=(bsz, nt),
        in_specs=[
            _ctx_spec(), _latent_spec(lambda t: t, 0), _mod_spec(0, bsz, lambda t: t),
        ] + [_const_spec(w) for w in consts],
        out_specs=[tok(GLA_V), tok(GLA_QK), tok(GLA_QK), tok(GLA_QK), dec,
                   tok(GLA_V), tok(GLA_V), tok(SC_WIDTH)],
        out_shape=[tok_shape(GLA_V), tok_shape(GLA_QK), tok_shape(GLA_QK), tok_shape(GLA_QK), dec_shape,
                   tok_shape(GLA_V), tok_shape(GLA_V), tok_shape(SC_WIDTH)],
        scratch_shapes=[pltpu.VMEM(m.shape, BF16) for m in mats]
        + [pltpu.VMEM((TM, TM), BF16)] * 2 + [pltpu.VMEM((LANES, 2 * GLA_QK), BF16)]
        + [pltpu.VMEM((TM, GLA_QK), BF16)] * 3
        + [pltpu.VMEM((CHUNKS, GLA_QK), F32), pltpu.VMEM((GLA_HEADS, GLA_DV, GLA_DK), F32)],
        compiler_params=_params(2),
        name="even_in",
    )(ctx, x, mods, *[_operand(w) for w in consts])


def _chunk_slice(ch):
    return slice(ch * GLA_CHUNK, (ch + 1) * GLA_CHUNK)


def _gla_mask(reverse):
    r = lax.broadcasted_iota(jnp.int32, (TM, TM), 0)
    c = lax.broadcasted_iota(jnp.int32, (TM, TM), 1)
    same_chunk = (r // GLA_CHUNK) == (c // GLA_CHUNK)
    return same_chunk & ((c >= r) if reverse else (c <= r))


def _gla_head_open(hd, qi_ref, ki_ref, ke_ref, v_ref, mask):
    kcols = slice(hd * GLA_DK, (hd + 1) * GLA_DK)
    vcols = slice(hd * GLA_DV, (hd + 1) * GLA_DV)
    incr = [_dot_tn(v_ref[_chunk_slice(ch), vcols], ke_ref[_chunk_slice(ch), kcols]) for ch in range(CHUNKS)]
    scores = jnp.where(mask, _dot_nt(qi_ref[:, kcols], ki_ref[:, kcols]), 0.0).astype(BF16)
    return incr, scores


def _gla_head_close(hd, opened, qi_ref, v_ref, dec_ref, st_ref, emit, reverse):
    incr, scores = opened
    kcols = slice(hd * GLA_DK, (hd + 1) * GLA_DK)
    vcols = slice(hd * GLA_DV, (hd + 1) * GLA_DV)
    state = st_ref[hd]
    before = {}
    for ch in (range(CHUNKS - 1, -1, -1) if reverse else range(CHUNKS)):
        before[ch] = state.astype(BF16)
        state = dec_ref[ch:ch + 1, kcols] * state + incr[ch]
    st_ref[hd] = state
    intra = _dot(scores, v_ref[:, vcols])
    for ch in range(CHUNKS):
        rows = _chunk_slice(ch)
        emit(rows, vcols, intra[rows, :] + _dot_nt(qi_ref[rows, kcols], before[ch]))


def _gla_tile(qi_ref, ki_ref, ke_ref, v_ref, dec_ref, st_ref, emit, reverse):
    mask = _gla_mask(reverse)
    opened = [_gla_head_open(hd, qi_ref, ki_ref, ke_ref, v_ref, mask) for hd in range(GLA_HEADS)]
    for hd in range(GLA_HEADS):
        _gla_head_close(hd, opened[hd], qi_ref, v_ref, dec_ref, st_ref, emit, reverse)


def _gla_bwd_kernel(qi_ref, ki_ref, ke_ref, v_ref, dec_ref, of_ref, sga_ref, y_ref, ctx_ref, x_ref,
                    modc_ref, modl_ref, gg_ref, wo_in, wy_in, out_ref, st_ref, ob_ref, wo_ref, wy_ref, stage_ref):
    j = pl.program_id(1)
    samples = range(GLA_PAIR)

    @pl.when(_first_step())
    def _():
        wo_ref[...] = wo_in[...].astype(BF16)
        wy_ref[...] = wy_in[...].astype(BF16)

    @pl.when(j == 0)
    def _():
        st_ref[...] = jnp.zeros_like(st_ref)

    proj_y = _dot(y_ref[...].reshape(GLA_PAIR * TM, SC_WIDTH), wy_ref[...])

    def emit_to(r):
        def emit(rows, vcols, o):
            ob_ref[r, rows, vcols] = o
        return emit

    mask = _gla_mask(reverse=True)
    opened = [[_gla_head_open(hd, qi_ref.at[r], ki_ref.at[r], ke_ref.at[r], v_ref.at[r], mask)
               for hd in range(GLA_HEADS)] for r in samples]
    for hd in range(GLA_HEADS):
        for r in samples:
            _gla_head_close(hd, opened[r][hd], qi_ref.at[r], v_ref.at[r], dec_ref.at[r], st_ref.at[r],
                            emit_to(r), reverse=True)

    inner = []
    for r in samples:
        for hd in range(GLA_HEADS):
            vcols = slice(hd * GLA_DV, (hd + 1) * GLA_DV)
            o = of_ref[r, :, vcols].astype(F32) + ob_ref[r, :, vcols]
            ms = jnp.mean(o * o, axis=-1, keepdims=True)
            on = o * lax.rsqrt(ms + EPS) * gg_ref[...]
            inner.append((on * sga_ref[r, :, vcols].astype(F32)).astype(BF16))
    inner = jnp.concatenate([jnp.concatenate(inner[r * GLA_HEADS:(r + 1) * GLA_HEADS], axis=-1)
                             for r in samples], axis=0)
    proj = _dot(inner, wo_ref[...]) + proj_y
    for r in samples:
        gate = jnp.where(j == 0, modc_ref[2:3, :], modl_ref[r, 2:3, :])
        ob_ref[r] = jnp.where(j == 0, ctx_ref[r], x_ref[r]) + gate * proj[r * TM:(r + 1) * TM, :]
        out_ref[r] = _to_segment_major(ob_ref.at[r], stage_ref, D_MODEL)


def _bwd_tile(nt):
    return lambda j: jnp.where(j == 0, 0, nt - j)


def _gla_bwd(qi, ki, ke, v, dec, o_f, sga, y, ctx, x, mods, gg, wo, wy, nt):
    assert GLA_V == D_MODEL
    bsz = qi.shape[0]
    assert bsz % GLA_PAIR == 0
    tile = _bwd_tile(nt)
    tok = lambda n: pl.BlockSpec((GLA_PAIR, TM, n), lambda p, j: (p, tile(j), 0))
    latent_block = lambda p, j: (p, jnp.where(j == 0, nt - 2, tile(j) - 1), 0)
    return pl.pallas_call(
        _gla_bwd_kernel,
        grid=(bsz // GLA_PAIR, nt),
        in_specs=[tok(GLA_QK), tok(GLA_QK), tok(GLA_QK), tok(GLA_V),
                  pl.BlockSpec((GLA_PAIR, None, CHUNKS, GLA_QK), lambda p, j: (p, tile(j), 0, 0)),
                  tok(GLA_V), tok(GLA_V), tok(SC_WIDTH),
                  pl.BlockSpec((GLA_PAIR, TM, D_MODEL), lambda p, j: (p, 0, 0)),
                  pl.BlockSpec((GLA_PAIR, TM, D_MODEL), latent_block),
                  pl.BlockSpec((None, None, 3, D_MODEL), lambda p, j: (0, bsz, 0, 0)),
                  pl.BlockSpec((None, GLA_PAIR, 3, D_MODEL), lambda p, j: (0, p, 0, 0)),
                  _const_spec(gg), _const_spec(wo), _const_spec(wy)],
        out_specs=tok(D_MODEL),
        out_shape=jax.ShapeDtypeStruct((bsz, nt * TM, D_MODEL), F32),
        scratch_shapes=[pltpu.VMEM((GLA_PAIR, GLA_HEADS, GLA_DV, GLA_DK), F32),
                        pltpu.VMEM((GLA_PAIR, TM, GLA_V), F32),
                        pltpu.VMEM(wo.shape, BF16),
                        pltpu.VMEM(wy.shape, BF16),
                        pltpu.VMEM((D_MODEL // LANES, SUBLANES * SEG_PITCH, LANES), F32)],
        compiler_params=_params(2),
        name="gla_bwd",
    )(qi, ki, ke, v, dec, o_f, sga, y, ctx, x, mods, mods, gg, _operand(wo), _operand(wy))


def _to_segment_major(src_ref, stage_ref, width):
    slabs = width // LANES
    for n in range(slabs):
        for s in range(SUBLANES):
            stage_ref[n, s * SEG_PITCH:s * SEG_PITCH + SEG_LEN, :] = (
                src_ref[s * SEG_LEN:(s + 1) * SEG_LEN, n * LANES:(n + 1) * LANES])
    groups = [
        jnp.concatenate([stage_ref[n, pl.ds(j, SUBLANES, stride=SEG_PITCH), :] for n in range(slabs)], axis=1)
        for j in range(SEG_LEN)]
    return jnp.concatenate(groups, axis=0)


def _from_segment_major(val, stage_ref, dst_ref, width):
    slabs = width // LANES
    for j in range(SEG_LEN):
        for n in range(slabs):
            stage_ref[n, pl.ds(j, SUBLANES, stride=SEG_PITCH), :] = (
                val[j * SUBLANES:(j + 1) * SUBLANES, n * LANES:(n + 1) * LANES])
    for n in range(slabs):
        for s in range(SUBLANES):
            dst_ref[s * SEG_LEN:(s + 1) * SEG_LEN, n * LANES:(n + 1) * LANES] = (
                stage_ref[n, s * SEG_PITCH:s * SEG_PITCH + SEG_LEN, :])


def _rg_conv(xr, halo_ref, rows_ref, cols, reverse):
    row = lax.broadcasted_iota(jnp.int32, (SUBLANES, xr.shape[1]), 0)
    edge = []
    for g in range(RG_CONV - 1):
        grp = slice(g * SUBLANES, (g + 1) * SUBLANES)
        if reverse:
            edge.append(jnp.where(row == SUBLANES - 1, pltpu.roll(halo_ref[grp, cols], SUBLANES - 1, 0),
                                  pltpu.roll(xr[grp, :], SUBLANES - 1, 0)))
        else:
            cur = xr[TM - (RG_CONV - 1 - g) * SUBLANES:TM - (RG_CONV - 2 - g) * SUBLANES, :]
            edge.append(jnp.where(row == 0, pltpu.roll(halo_ref[grp, cols], 1, 0), pltpu.roll(cur, 1, 0)))
    tap_w = lambda j: rows_ref[ROW_CONV_W + j:ROW_CONV_W + j + 1, cols]
    xc = rows_ref[ROW_CONV_B:ROW_CONV_B + 1, cols] + tap_w(RG_CONV - 1) * xr
    for k in range(1, RG_CONV):
        if reverse:
            tap = jnp.concatenate([xr[k * SUBLANES:, :]] + edge[:k], axis=0)
        else:
            tap = jnp.concatenate(edge[RG_CONV - 1 - k:] + [xr[:TM - k * SUBLANES, :]], axis=0)
        xc = xc + tap_w(RG_CONV - 1 - k) * tap
    halo_ref[:, cols] = xr[:HALO_ROWS, :] if reverse else xr[TM - HALO_ROWS:, :]
    return xc


def _rg_setup(lam_ref, ba_ref, bx_ref, cb_ref, cw_ref, wa_ref, wx_ref, rows_ref, wg_s, copies):
    wg_s[:, :, :RG_BLOCK_W] = wa_ref[...].astype(BF16)
    wg_s[:, :, RG_BLOCK_W:] = wx_ref[...].astype(BF16)
    rows_ref[ROW_DECAY:ROW_DECAY + 1, :] = (-0.5 * RG_C * LOG2_E) * _softplus(-lam_ref[...])
    rows_ref[ROW_BIAS_A:ROW_BIAS_A + 1, :] = 0.5 * ba_ref[...]
    rows_ref[ROW_BIAS_X:ROW_BIAS_X + 1, :] = 0.5 * bx_ref[...]
    rows_ref[ROW_CONV_B:ROW_CONV_B + 1, :] = 0.5 * cb_ref[...]
    rows_ref[ROW_CONV_W:ROW_CONV_W + RG_CONV, :] = 0.5 * cw_ref[...]
    for src_ref, dst_ref in copies:
        dst_ref[...] = src_ref[...].astype(BF16)


def _rg_gates_scan(xc, first_block, wg_ref, rows_ref, carry_ref, reverse):
    row = lax.broadcasted_iota(jnp.int32, (SUBLANES, RG_BLOCK_W), 0)
    shift = SUBLANES - 1 if reverse else 1
    steps = range(SEG_LEN - 1, -1, -1) if reverse else range(SEG_LEN)
    out = []
    for i in range(xc.shape[1] // RG_BLOCK_W):
        n = first_block + i
        cols = slice(n * RG_BLOCK_W, (n + 1) * RG_BLOCK_W)
        xb = xc[:, i * RG_BLOCK_W:(i + 1) * RG_BLOCK_W]
        gates = _dot(xb.astype(BF16), wg_ref[n])
        th_r = jnp.tanh(gates[:, :RG_BLOCK_W] + rows_ref[ROW_BIAS_A:ROW_BIAS_A + 1, cols])
        th_i = jnp.tanh(gates[:, RG_BLOCK_W:] + rows_ref[ROW_BIAS_X:ROW_BIAS_X + 1, cols])
        half_decay = rows_ref[ROW_DECAY:ROW_DECAY + 1, cols]
        a = jnp.exp2(th_r * half_decay + half_decay)
        y = 1.0 - a * a
        root = y * lax.rsqrt(jnp.maximum(y, F32_TINY))
        u = root * ((th_i + 1.0) * xb)

        h = jnp.zeros((SUBLANES, RG_BLOCK_W), F32)
        prod = jnp.ones((SUBLANES, RG_BLOCK_W), F32)
        local, prods = {}, {}
        for j in steps:
            grp = slice(j * SUBLANES, (j + 1) * SUBLANES)
            h = a[grp, :] * h + u[grp, :]
            prod = prod * a[grp, :]
            local[j], prods[j] = h, prod

        entering = carry_ref[:, cols]
        for s in range(SUBLANES - 1):
            nxt = pltpu.roll(prod * entering + h, shift, 0)
            target = SUBLANES - 2 - s if reverse else s + 1
            entering = jnp.where(row == target, nxt, entering)
        carry_ref[:, cols] = pltpu.roll(prod * entering + h, shift, 0)

        out.append(jnp.concatenate([local[j] + prods[j] * entering for j in range(SEG_LEN)], axis=0))
    return jnp.concatenate(out, axis=1)


def _odd_fwd_kernel(xx_ref, modc_ref, modl_ref, g_ref, wx_ref, cw_ref, cb_ref, ga_ref, gx_ref, ba_ref, bx_ref,
                    lam_ref, xr_ref, hf_ref, hn_ref, halo_ref, carry_ref, rows_ref, wx_s, wg_s):
    t = pl.program_id(1)

    @pl.when(_first_step())
    def _():
        _rg_setup(lam_ref, ba_ref, bx_ref, cb_ref, cw_ref, ga_ref, gx_ref, rows_ref, wg_s, ((wx_ref, wx_s),))

    @pl.when(t == 0)
    def _():
        carry_ref[...] = jnp.zeros_like(carry_ref)

    @pl.when(t <= 1)
    def _():
        halo_ref[...] = jnp.zeros_like(halo_ref)

    samples = range(RG_PAIR)
    for r in samples:
        shift = jnp.where(t == 0, modc_ref[0:1, :], modl_ref[r, 0:1, :])
        scale = jnp.where(t == 0, modc_ref[1:2, :], modl_ref[r, 1:2, :])
        hn_ref[r] = _modulated_norm(xx_ref[r], g_ref[...], shift, scale).astype(BF16)
    hb = hn_ref[...].reshape(RG_PAIR * TM, D_MODEL)
    chunks = RG_WIDTH // RG_CHUNK_W
    project = lambda m: _dot(hb, wx_s[:, m * RG_CHUNK_W:(m + 1) * RG_CHUNK_W])
    xr_next = project(0)
    for m in range(chunks):
        cols = slice(m * RG_CHUNK_W, (m + 1) * RG_CHUNK_W)
        xr_pair = xr_next
        if m + 1 < chunks:
            xr_next = project(m + 1)
        for r in samples:
            xr = xr_pair[r * TM:(r + 1) * TM, :]
            xr_ref[r, :, cols] = xr.astype(BF16)
            xc = _rg_conv(xr, halo_ref.at[r], rows_ref, cols, reverse=False)
            hf_ref[r, :, cols] = _rg_gates_scan(xc, m * (RG_CHUNK_W // RG_BLOCK_W), wg_s, rows_ref,
                                                carry_ref.at[r], reverse=False).astype(BF16)


def _odd_fwd(xx, mods, g, wx, cw, cb, ga, gx, ba, bx, lam, nt):
    bsz = xx.shape[0]
    assert bsz % RG_PAIR == 0
    tok = lambda n: pl.BlockSpec((RG_PAIR, TM, n), lambda p, t: (p, t, 0))
    consts = (g, wx, cw, cb, ga, gx, ba, bx, lam)
    gate_scratch = pltpu.VMEM((RG_BLOCKS, RG_BLOCK_W, 2 * RG_BLOCK_W), BF16)
    wide = jax.ShapeDtypeStruct((bsz, nt * TM, RG_WIDTH), BF16)
    return pl.pallas_call(
        _odd_fwd_kernel,
        grid=(bsz // RG_PAIR, nt),
        in_specs=[tok(D_MODEL),
                  pl.BlockSpec((None, None, 3, D_MODEL), lambda p, t: (1, bsz, 0, 0)),
                  pl.BlockSpec((None, RG_PAIR, 3, D_MODEL), lambda p, t: (1, p, 0, 0))]
        + [_const_spec(w) for w in consts],
        out_specs=[tok(RG_WIDTH), tok(RG_WIDTH), tok(D_MODEL)],
        out_shape=[wide, wide, jax.ShapeDtypeStruct((bsz, nt * TM, D_MODEL), BF16)],
        scratch_shapes=[pltpu.VMEM((RG_PAIR, HALO_ROWS, RG_WIDTH), F32),
                        pltpu.VMEM((RG_PAIR, SUBLANES, RG_WIDTH), F32),
                        pltpu.VMEM((RG_ROWS, RG_WIDTH), F32),
                        pltpu.VMEM(wx.shape, BF16),
                        gate_scratch],
        compiler_params=_params(2),
        name="odd_fwd",
    )(xx, mods, mods, *[_operand(w) for w in consts])


def _odd_bwd_kernel(xx_ref, mod_ref, hn_ref, xr_ref, hf_ref, wgate_ref, cw_ref, cb_ref, ga_ref, gx_ref, ba_ref,
                    bx_ref, lam_ref, wout_ref, fg_ref, out_ref,
                    stage_ref, halo_ref, yy_ref, carry_ref, rows_ref, wgate_s, wg_s, wout_s):
    j = pl.program_id(1)

    @pl.when(_first_step())
    def _():
        _rg_setup(lam_ref, ba_ref, bx_ref, cb_ref, cw_ref, ga_ref, gx_ref, rows_ref, wg_s,
                  ((wgate_ref, wgate_s), (wout_ref, wout_s)))

    @pl.when(j == 0)
    def _():
        carry_ref[...] = jnp.zeros_like(carry_ref)

    @pl.when(j <= 1)
    def _():
        halo_ref[...] = jnp.zeros_like(halo_ref)

    def tile(with_output):
        hb = hn_ref[...]
        for m in range(RG_WIDTH // RG_CHUNK_W):
            cols = slice(m * RG_CHUNK_W, (m + 1) * RG_CHUNK_W)
            if with_output:
                gate = _dot(hb, wgate_s[:, cols])
            xc = _rg_conv(xr_ref[:, cols].astype(F32), halo_ref, rows_ref, cols, reverse=True)
            h_bwd = _rg_gates_scan(xc, m * (RG_CHUNK_W // RG_BLOCK_W), wg_s, rows_ref, carry_ref,
                                   reverse=True)
            if with_output:
                yy_ref[:, cols] = ((hf_ref[:, cols].astype(F32) + h_bwd) * _silu(gate)).astype(BF16)
        if with_output:
            xn = xx_ref[...] + mod_ref[2:3, :] * _dot(yy_ref[...], wout_s[...])
            ms = jnp.mean(xn * xn, axis=-1, keepdims=True)
            _from_segment_major(xn * lax.rsqrt(ms + EPS) * fg_ref[...], stage_ref, out_ref, D_MODEL)

    pl.when(j == 0)(lambda: tile(with_output=False))
    pl.when(j > 0)(lambda: tile(with_output=True))


def _odd_bwd(xx, mods, hn, xr, hf, wgate, cw, cb, ga, gx, ba, bx, lam, wout, fg, nt):
    bsz = xx.shape[0]
    tile = _bwd_tile(nt)
    tok = lambda n: pl.BlockSpec((None, TM, n), lambda b, j: (b, tile(j), 0))
    consts = (wgate, cw, cb, ga, gx, ba, bx, lam, wout, fg)
    gate_scratch = pltpu.VMEM((RG_BLOCKS, RG_BLOCK_W, 2 * RG_BLOCK_W), BF16)
    out_block = lambda b, j: (b, jnp.where(j == 0, nt - 2, nt - 1 - j), 0)
    return pl.pallas_call(
        _odd_bwd_kernel,
        grid=(bsz, nt),
        in_specs=[tok(D_MODEL), _mod_spec(1, bsz, tile), tok(D_MODEL), tok(RG_WIDTH), tok(RG_WIDTH)]
        + [_const_spec(w) for w in consts],
        out_specs=pl.BlockSpec((None, TM, D_MODEL), out_block),
        out_shape=jax.ShapeDtypeStruct((bsz, (nt - 1) * TM, D_MODEL), F32),
        scratch_shapes=[pltpu.VMEM((D_MODEL // LANES, SUBLANES * SEG_PITCH, LANES), F32),
                        pltpu.VMEM((HALO_ROWS, RG_WIDTH), F32),
                        pltpu.VMEM((TM, RG_WIDTH), BF16),
                        pltpu.VMEM((SUBLANES, RG_WIDTH), F32),
                        pltpu.VMEM((RG_ROWS, RG_WIDTH), F32),
                        pltpu.VMEM(wgate.shape, BF16),
                        gate_scratch,
                        pltpu.VMEM(wout.shape, BF16)],
        compiler_params=_params(2),
        name="odd_bwd",
    )(xx, mods, hn, xr, hf, *[_operand(w) for w in consts])


def kernel(x, c, ctx, c_ctx, norm_g, w_mod, b_mod, e_w_in, e_w_a2, e_b_a2, e_gla_g, e_conv_w, e_w_out,
           o_w_in, o_conv_w, o_conv_b, o_w_a, o_b_a, o_w_x, o_b_x, o_lam, o_w_out, final_g):
    bsz, seq, _ = x.shape
    assert ctx.shape[1] == TM and seq % TM == 0 and bsz < SUBLANES
    assert w_mod.shape[0] == 2
    nt = 1 + seq // TM

    mods = _mods(c, c_ctx, w_mod, b_mod).reshape(w_mod.shape[0], SUBLANES, 3, D_MODEL)

    w_in = e_w_in[0]
    w_in_bf16 = w_in.astype(BF16)
    square = (D_MODEL, D_MODEL)
    offs = {}
    off = 0
    for name, size in (("q", GLA_QK), ("k", GLA_QK), ("v", GLA_V), ("ga", GLA_V), ("af", GLA_LOWRANK),
                       ("ab", GLA_LOWRANK), ("cb", SC_WIDTH), ("cc", SC_WIDTH), ("cx", SC_WIDTH),
                       ("gb", SC_WIDTH)):
        offs[name] = (off, off + size)
        off += size
    cols = lambda lo, hi: w_in[:, lo:hi].astype(BF16)
    assert offs["q"][0] == 0 and offs["v"][0] == D_MODEL and offs["ga"] == (2 * D_MODEL, 3 * D_MODEL)
    wqk, wv, wga = (_Window(w_in_bf16, square, (0, i)) for i in range(3))
    wa = jnp.pad(w_in[:, offs["af"][0]:offs["ab"][1]], ((0, 0), (0, LANES - 2 * GLA_LOWRANK))).astype(BF16)
    b2 = e_b_a2[0].reshape(1, 2 * GLA_QK)
    mats = (wqk, wv, wga, wa, cols(*offs["cb"]), cols(*offs["cc"]), cols(*offs["cx"]), cols(*offs["gb"]))
    (o_f, qi_b, ki_b, ke_b, dec_b, v, sga, y) = _even_in(
        ctx, x, mods, (norm_g[0:1], b2, e_conv_w[0], e_w_a2[0]), mats, nt)
    w_out = e_w_out[0]
    xx = _gla_bwd(qi_b, ki_b, ke_b, v, dec_b, o_f, sga, y, ctx, x, mods, e_gla_g[0:1],
                  _Window(w_out, square, (0, 0)), _Window(w_out, square, (1, 0)), nt)

    w_in = o_w_in[0]
    half = (D_MODEL, RG_WIDTH)
    wide = lambda a: a.reshape(1, RG_WIDTH)
    xr, h_f, hn = _odd_fwd(xx, mods, norm_g[1:2], _Window(w_in, half, (0, 0)), o_conv_w[0, 0],
                           wide(o_conv_b[0, 0]), o_w_a[0, 0], o_w_x[0, 0],
                           wide(o_b_a[0, 0]), wide(o_b_x[0, 0]), wide(o_lam[0, 0]), nt)
    return _odd_bwd(xx, mods, hn, xr, h_f, _Window(w_in, half, (0, 1)), o_conv_w[0, 1],
                    wide(o_conv_b[0, 1]), o_w_a[0, 1], o_w_x[0, 1],
                    wide(o_b_a[0, 1]), wide(o_b_x[0, 1]), wide(o_lam[0, 1]), o_w_out[0],
                    final_g.reshape(1, D_MODEL), nt)
```

```python
import jax
import jax.numpy as jnp
from jax import lax
from jax.experimental import pallas as pl
from jax.experimental.pallas import tpu as pltpu

F32 = jnp.float32
BF16 = jnp.bfloat16

D_MODEL = 1024
EPS = 1e-6
GLA_HEADS = 4
GLA_DK = 128
GLA_DV = 256
GLA_QK = GLA_HEADS * GLA_DK
GLA_V = GLA_HEADS * GLA_DV
GLA_LOWRANK = 16
GLA_TAU = 16.0
GLA_CHUNK = 64
GRID_W = 64
SC_WIDTH = D_MODEL
RG_WIDTH = 2 * D_MODEL
RG_BLOCKS = 16
RG_BLOCK_W = 128
RG_C = 8.0
RG_CONV = 4

TM = 256
CHUNKS = TM // GLA_CHUNK
GLA_PAIR = 2
RG_PAIR = 2
EVEN_PAIR = 2
SUBLANES = 8
LANES = 128
VMEM_LIMIT = 56 * 1024 * 1024
SEG_LEN = TM // SUBLANES
SEG_PITCH = SEG_LEN + SUBLANES
HALO_ROWS = (RG_CONV - 1) * SUBLANES
RG_CHUNK_W = 256
ROW_DECAY, ROW_BIAS_A, ROW_BIAS_X, ROW_CONV_B, ROW_CONV_W = 0, 1, 2, 3, 4
RG_ROWS = ROW_CONV_W + RG_CONV
LOG2_E = 1.4426950408889634
F32_TINY = 1.1754943508222875e-38


def _silu(x):
    return x * jax.nn.sigmoid(x)


def _log_sigmoid(z):
    return jnp.minimum(z, 0.0) - jnp.log(1.0 + jnp.exp(-jnp.abs(z)))


def _softplus(z):
    return jnp.maximum(z, 0.0) + jnp.log1p(jnp.exp(-jnp.abs(z)))


def _dot(a, b):
    return jnp.dot(a, b, preferred_element_type=F32)


def _dot_nt(a, b):
    return lax.dot_general(a, b, (((1,), (1,)), ((), ())), preferred_element_type=F32)


def _dot_tn(a, b):
    return lax.dot_general(a, b, (((0,), (0,)), ((), ())), preferred_element_type=F32)


def _modulated_norm(x, g, shift, scale):
    ms = jnp.mean(x * x, axis=-1, keepdims=True)
    return (x * lax.rsqrt(ms + EPS) * g) * (1.0 + scale) + shift


def _params(n_axes):
    return pltpu.CompilerParams(
        dimension_semantics=("arbitrary",) * n_axes,
        vmem_limit_bytes=VMEM_LIMIT)


def _first_step():
    return (pl.program_id(0) == 0) & (pl.program_id(1) == 0)


class _Window:
    def __init__(self, array, block_shape, block_index):
        self.array, self.shape, self.index = array, tuple(block_shape), tuple(block_index)


def _operand(w):
    return w.array if isinstance(w, _Window) else w


def _const_spec(w):
    index = w.index if isinstance(w, _Window) else (0,) * len(w.shape)
    return pl.BlockSpec(w.shape, lambda *_: index, pipeline_mode=pl.Buffered(1))


def _mod_spec(layer, bsz, tile):
    return pl.BlockSpec((None, None, 3, D_MODEL),
                        lambda b, t: (layer, jnp.where(tile(t) == 0, bsz, b), 0, 0))


def _mods_kernel(c_ref, cctx_ref, w_ref, b_ref, o_ref, s_ref):
    bsz = c_ref.shape[0]
    s_ref[...] = jnp.zeros_like(s_ref)
    s_ref[0:bsz, :] = c_ref[...]
    s_ref[bsz:bsz + 1, :] = cctx_ref[...]
    s = _silu(s_ref[...])
    o_ref[0] = jnp.dot(s, w_ref[0], preferred_element_type=F32,
                       precision=lax.Precision.HIGHEST) + b_ref[0]


def _mods(c, c_ctx, w_mod, b_mod):
    depth = w_mod.shape[0]
    nb = 3
    return pl.pallas_call(
        _mods_kernel,
        grid=(depth, nb),
        in_specs=[
            pl.BlockSpec(c.shape, lambda l, j: (0, 0)),
            pl.BlockSpec((1, D_MODEL), lambda l, j: (0, 0)),
            pl.BlockSpec((1, D_MODEL, D_MODEL), lambda l, j: (l, 0, j)),
            pl.BlockSpec((1, 1, D_MODEL), lambda l, j: (l, 0, j)),
        ],
        out_specs=pl.BlockSpec((1, SUBLANES, D_MODEL), lambda l, j: (l, 0, j)),
        out_shape=jax.ShapeDtypeStruct((depth, SUBLANES, 3 * D_MODEL), F32),
        scratch_shapes=[pltpu.VMEM((SUBLANES, D_MODEL), F32)],
        compiler_params=_params(2),
        name="mods",
    )(c, c_ctx.reshape(1, D_MODEL), w_mod, b_mod.reshape(depth, 1, 3 * D_MODEL))


def _chunk_rows(x, offset):
    return jnp.concatenate(
        [jnp.broadcast_to(x[c * GLA_CHUNK + offset:c * GLA_CHUNK + offset + 1, :], (GLA_CHUNK, x.shape[1]))
         for c in range(x.shape[0] // GLA_CHUNK)], axis=0)


def _even_in_kernel(ctx_ref, x_ref, modc_ref, modl_ref, g_ref, b2_ref, cw_ref, wa2_ref,
                    wqk_ref, wv_ref, wga_ref, wa_ref, wcb_ref, wcc_ref, wcx_ref, wgb_ref,
                    of_ref, qib_ref, kib_ref, keb_ref, decb_ref, v_ref, sga_ref, y_ref,
                    lower_ref, upper_ref, w2_ref, qif_ref, kif_ref, kef_ref, decf_ref, st_ref):
    t = pl.program_id(1)
    samples = range(EVEN_PAIR)
    rows_of = lambda r: slice(r * TM, (r + 1) * TM)
    split = lambda val: val.reshape(EVEN_PAIR, TM, val.shape[-1])

    @pl.when(t == 0)
    def _():
        st_ref[...] = jnp.zeros_like(st_ref)

    @pl.when(_first_step())
    def _():
        r = lax.broadcasted_iota(jnp.int32, (TM, TM), 0)
        c = lax.broadcasted_iota(jnp.int32, (TM, TM), 1)
        same = (r // GLA_CHUNK) == (c // GLA_CHUNK)
        lower_ref[...] = jnp.where(same & (c <= r), 1.0, 0.0).astype(BF16)
        upper_ref[...] = jnp.where(same & (c >= r), 1.0, 0.0).astype(BF16)
        w2_ref[...] = jnp.zeros_like(w2_ref)
        w2_ref[0:GLA_LOWRANK, 0:GLA_QK] = wa2_ref[0].astype(BF16)
        w2_ref[GLA_LOWRANK:2 * GLA_LOWRANK, GLA_QK:] = wa2_ref[1].astype(BF16)

    normed = []
    for r in samples:
        xt = jnp.where(t == 0, ctx_ref[r], x_ref[r])
        shift = jnp.where(t == 0, modc_ref[0:1, :], modl_ref[r, 0:1, :])
        scale = jnp.where(t == 0, modc_ref[1:2, :], modl_ref[r, 1:2, :])
        normed.append(_modulated_norm(xt, g_ref[...], shift, scale).astype(BF16))
    hb = jnp.concatenate(normed, axis=0)

    a_lr = _dot(hb, wa_ref[...])
    z = _dot(a_lr.astype(BF16), w2_ref[...]) + b2_ref[...]
    qk = _dot(hb, wqk_ref[...])
    v_ref[...] = split(_dot(hb, wv_ref[...]).astype(BF16))

    lg = _log_sigmoid(z) * (1.0 / GLA_TAU)
    lgb = lg.astype(BF16)
    cs_f = jnp.concatenate([_dot(lower_ref[...], lgb[rows_of(r), :GLA_QK]) for r in samples], axis=0)
    cs_b = jnp.concatenate([_dot(upper_ref[...], lgb[rows_of(r), GLA_QK:]) for r in samples], axis=0)
    tot_f = _chunk_rows(cs_f, GLA_CHUNK - 1)
    tot_b = _chunk_rows(cs_b, 0)
    c_c = _dot(hb, wcc_ref[...])

    q = qk[:, :GLA_QK] * (GLA_DK ** -0.5)
    k = qk[:, GLA_QK:]
    qif_ref[...] = split((q * jnp.exp(cs_f)).astype(BF16))
    kif_ref[...] = split((k * jnp.exp(-cs_f)).astype(BF16))
    kef_ref[...] = split((k * jnp.exp(tot_f - cs_f)).astype(BF16))
    qib_ref[...] = split((q * jnp.exp(cs_b)).astype(BF16))
    kib_ref[...] = split((k * jnp.exp(-cs_b)).astype(BF16))
    keb_ref[...] = split((k * jnp.exp(tot_b - cs_b)).astype(BF16))
    for r in samples:
        for c in range(CHUNKS):
            row = r * TM + c * GLA_CHUNK
            decf_ref[r, c:c + 1, :] = jnp.exp(tot_f[row:row + 1, :])
            decb_ref[r, c:c + 1, :] = jnp.exp(tot_b[row:row + 1, :])

    def emit_to(r):
        def emit(rows, vcols, o):
            of_ref[r, rows, vcols] = o.astype(BF16)
        return emit

    mask = _gla_mask(reverse=False)
    open_head = lambda hd: [_gla_head_open(hd, qif_ref.at[r], kif_ref.at[r], kef_ref.at[r], v_ref.at[r], mask)
                            for r in samples]

    def close_head(hd, opened):
        for r in samples:
            _gla_head_close(hd, opened[r], qif_ref.at[r], v_ref.at[r], decf_ref.at[r], st_ref.at[r],
                            emit_to(r), reverse=False)

    opened = open_head(0)
    zz = c_c * _dot(hb, wcx_ref[...])
    close_head(0, opened)
    opened = open_head(1)
    c_b = _dot(hb, wcb_ref[...])
    close_head(1, opened)
    opened = open_head(2)
    g_b = _dot(hb, wgb_ref[...])
    close_head(2, opened)
    opened = open_head(3)
    sga_ref[...] = split(_silu(_dot(hb, wga_ref[...])).astype(BF16))
    close_head(3, opened)

    step_rows = EVEN_PAIR * TM
    pos = lax.broadcasted_iota(jnp.int32, (step_rows, SC_WIDTH), 0)
    row_len = jnp.where(t == 0, TM, GRID_W)
    in_row = pos & (row_len - 1)
    z_prev = jnp.where(in_row != 0, pltpu.roll(zz, 1, 0), 0.0)
    z_next = jnp.where(in_row != row_len - 1, pltpu.roll(zz, step_rows - 1, 0), 0.0)
    zc = cw_ref[0:1, :] * z_prev + cw_ref[1:2, :] * zz + cw_ref[2:3, :] * z_next
    y_ref[...] = split((c_b * zc * _silu(g_b)).astype(BF16))


def _even_in(ctx, x, mods, rows, mats, nt):
    bsz = x.shape[0]
    assert bsz % EVEN_PAIR == 0
    tok = lambda n: pl.BlockSpec((EVEN_PAIR, TM, n), lambda p, t: (p, t, 0))
    dec = pl.BlockSpec((EVEN_PAIR, None, CHUNKS, GLA_QK), lambda p, t: (p, t, 0, 0))
    tok_shape = lambda n: jax.ShapeDtypeStruct((bsz, nt * TM, n), BF16)
    dec_shape = jax.ShapeDtypeStruct((bsz, nt, CHUNKS, GLA_QK), F32)
    consts = tuple(rows) + tuple(mats)
    return pl.pallas_call(
        _even_in_kernel,
        grid=(bsz // EVEN_PAIR, nt),
        in_specs=[
            pl.BlockSpec((EVEN_PAIR, TM, D_MODEL), lambda p, t: (p, 0, 0)),
            pl.BlockSpec((EVEN_PAIR, TM, D_MODEL), lambda p, t: (p, jnp.maximum(t - 1, 0), 0)),
            pl.BlockSpec((None, None, 3, D_MODEL), lambda p, t: (0, bsz, 0, 0)),
            pl.BlockSpec((None, EVEN_PAIR, 3, D_MODEL), lambda p, t: (0, p, 0, 0)),
        ] + [_const_spec(w) for w in consts],
        out_specs=[tok(GLA_V), tok(GLA_QK), tok(GLA_QK), tok(GLA_QK), dec,
                   tok(GLA_V), tok(GLA_V), tok(SC_WIDTH)],
        out_shape=[tok_shape(GLA_V), tok_shape(GLA_QK), tok_shape(GLA_QK), tok_shape(GLA_QK), dec_shape,
                   tok_shape(GLA_V), tok_shape(GLA_V), tok_shape(SC_WIDTH)],
        scratch_shapes=[pltpu.VMEM((TM, TM), BF16)] * 2 + [pltpu.VMEM((LANES, 2 * GLA_QK), BF16)]
        + [pltpu.VMEM((EVEN_PAIR, TM, GLA_QK), BF16)] * 3
        + [pltpu.VMEM((EVEN_PAIR, CHUNKS, GLA_QK), F32),
           pltpu.VMEM((EVEN_PAIR, GLA_HEADS, GLA_DV, GLA_DK), F32)],
        compiler_params=_params(2),
        name="even_in",
    )(ctx, x, mods, mods, *[_operand(w) for w in consts])


def _chunk_slice(ch):
    return slice(ch * GLA_CHUNK, (ch + 1) * GLA_CHUNK)


def _gla_mask(reverse):
    r = lax.broadcasted_iota(jnp.int32, (TM, TM), 0)
    c = lax.broadcasted_iota(jnp.int32, (TM, TM), 1)
    same_chunk = (r // GLA_CHUNK) == (c // GLA_CHUNK)
    return same_chunk & ((c >= r) if reverse else (c <= r))


def _gla_head_open(hd, qi_ref, ki_ref, ke_ref, v_ref, mask):
    kcols = slice(hd * GLA_DK, (hd + 1) * GLA_DK)
    vcols = slice(hd * GLA_DV, (hd + 1) * GLA_DV)
    incr = [_dot_tn(v_ref[_chunk_slice(ch), vcols], ke_ref[_chunk_slice(ch), kcols]) for ch in range(CHUNKS)]
    scores = jnp.where(mask, _dot_nt(qi_ref[:, kcols], ki_ref[:, kcols]), 0.0).astype(BF16)
    return incr, scores


def _gla_head_close(hd, opened, qi_ref, v_ref, dec_ref, st_ref, emit, reverse):
    incr, scores = opened
    kcols = slice(hd * GLA_DK, (hd + 1) * GLA_DK)
    vcols = slice(hd * GLA_DV, (hd + 1) * GLA_DV)
    state = st_ref[hd]
    before = {}
    for ch in (range(CHUNKS - 1, -1, -1) if reverse else range(CHUNKS)):
        before[ch] = state.astype(BF16)
        state = dec_ref[ch:ch + 1, kcols] * state + incr[ch]
    st_ref[hd] = state
    intra = _dot(scores, v_ref[:, vcols])
    for ch in range(CHUNKS):
        rows = _chunk_slice(ch)
        emit(rows, vcols, intra[rows, :] + _dot_nt(qi_ref[rows, kcols], before[ch]))


def _gla_tile(qi_ref, ki_ref, ke_ref, v_ref, dec_ref, st_ref, emit, reverse):
    mask = _gla_mask(reverse)
    opened = [_gla_head_open(hd, qi_ref, ki_ref, ke_ref, v_ref, mask) for hd in range(GLA_HEADS)]
    for hd in range(GLA_HEADS):
        _gla_head_close(hd, opened[hd], qi_ref, v_ref, dec_ref, st_ref, emit, reverse)


def _gla_bwd_kernel(qi_ref, ki_ref, ke_ref, v_ref, dec_ref, of_ref, sga_ref, y_ref, ctx_ref, x_ref,
                    modc_ref, modl_ref, gg_ref, wo_in, wy_in, out_ref, st_ref, ob_ref, wo_ref, wy_ref, stage_ref):
    j = pl.program_id(1)
    samples = range(GLA_PAIR)

    @pl.when(_first_step())
    def _():
        wo_ref[...] = wo_in[...].astype(BF16)
        wy_ref[...] = wy_in[...].astype(BF16)

    @pl.when(j == 0)
    def _():
        st_ref[...] = jnp.zeros_like(st_ref)

    proj_y = _dot(y_ref[...].reshape(GLA_PAIR * TM, SC_WIDTH), wy_ref[...])

    def emit_to(r):
        def emit(rows, vcols, o):
            ob_ref[r, rows, vcols] = o
        return emit

    mask = _gla_mask(reverse=True)
    opened = [[_gla_head_open(hd, qi_ref.at[r], ki_ref.at[r], ke_ref.at[r], v_ref.at[r], mask)
               for hd in range(GLA_HEADS)] for r in samples]
    for hd in range(GLA_HEADS):
        for r in samples:
            _gla_head_close(hd, opened[r][hd], qi_ref.at[r], v_ref.at[r], dec_ref.at[r], st_ref.at[r],
                            emit_to(r), reverse=True)

    inner = []
    for r in samples:
        for hd in range(GLA_HEADS):
            vcols = slice(hd * GLA_DV, (hd + 1) * GLA_DV)
            o = of_ref[r, :, vcols].astype(F32) + ob_ref[r, :, vcols]
            ms = jnp.mean(o * o, axis=-1, keepdims=True)
            on = o * lax.rsqrt(ms + EPS) * gg_ref[...]
            inner.append((on * sga_ref[r, :, vcols].astype(F32)).astype(BF16))
    inner = jnp.concatenate([jnp.concatenate(inner[r * GLA_HEADS:(r + 1) * GLA_HEADS], axis=-1)
                             for r in samples], axis=0)
    proj = _dot(inner, wo_ref[...]) + proj_y
    for r in samples:
        gate = jnp.where(j == 0, modc_ref[2:3, :], modl_ref[r, 2:3, :])
        ob_ref[r] = jnp.where(j == 0, ctx_ref[r], x_ref[r]) + gate * proj[r * TM:(r + 1) * TM, :]
        out_ref[r] = _to_segment_major(ob_ref.at[r], stage_ref, D_MODEL)


def _bwd_tile(nt):
    return lambda j: jnp.where(j == 0, 0, nt - j)


def _gla_bwd(qi, ki, ke, v, dec, o_f, sga, y, ctx, x, mods, gg, wo, wy, nt):
    assert GLA_V == D_MODEL
    bsz = qi.shape[0]
    assert bsz % GLA_PAIR == 0
    tile = _bwd_tile(nt)
    tok = lambda n: pl.BlockSpec((GLA_PAIR, TM, n), lambda p, j: (p, tile(j), 0))
    latent_block = lambda p, j: (p, jnp.where(j == 0, nt - 2, tile(j) - 1), 0)
    return pl.pallas_call(
        _gla_bwd_kernel,
        grid=(bsz // GLA_PAIR, nt),
        in_specs=[tok(GLA_QK), tok(GLA_QK), tok(GLA_QK), tok(GLA_V),
                  pl.BlockSpec((GLA_PAIR, None, CHUNKS, GLA_QK), lambda p, j: (p, tile(j), 0, 0)),
                  tok(GLA_V), tok(GLA_V), tok(SC_WIDTH),
                  pl.BlockSpec((GLA_PAIR, TM, D_MODEL), lambda p, j: (p, 0, 0)),
                  pl.BlockSpec((GLA_PAIR, TM, D_MODEL), latent_block),
                  pl.BlockSpec((None, None, 3, D_MODEL), lambda p, j: (0, bsz, 0, 0)),
                  pl.BlockSpec((None, GLA_PAIR, 3, D_MODEL), lambda p, j: (0, p, 0, 0)),
                  _const_spec(gg), _const_spec(wo), _const_spec(wy)],
        out_specs=tok(D_MODEL),
        out_shape=jax.ShapeDtypeStruct((bsz, nt * TM, D_MODEL), F32),
        scratch_shapes=[pltpu.VMEM((GLA_PAIR, GLA_HEADS, GLA_DV, GLA_DK), F32),
                        pltpu.VMEM((GLA_PAIR, TM, GLA_V), F32),
                        pltpu.VMEM(wo.shape, BF16),
                        pltpu.VMEM(wy.shape, BF16),
                        pltpu.VMEM((D_MODEL // LANES, SUBLANES * SEG_PITCH, LANES), F32)],
        compiler_params=_params(2),
        name="gla_bwd",
    )(qi, ki, ke, v, dec, o_f, sga, y, ctx, x, mods, mods, gg, _operand(wo), _operand(wy))


def _to_segment_major(src_ref, stage_ref, width):
    slabs = width // LANES
    for n in range(slabs):
        for s in range(SUBLANES):
            stage_ref[n, s * SEG_PITCH:s * SEG_PITCH + SEG_LEN, :] = (
                src_ref[s * SEG_LEN:(s + 1) * SEG_LEN, n * LANES:(n + 1) * LANES])
    groups = [
        jnp.concatenate([stage_ref[n, pl.ds(j, SUBLANES, stride=SEG_PITCH), :] for n in range(slabs)], axis=1)
        for j in range(SEG_LEN)]
    return jnp.concatenate(groups, axis=0)


def _from_segment_major(val, stage_ref, dst_ref, width):
    slabs = width // LANES
    for j in range(SEG_LEN):
        for n in range(slabs):
            stage_ref[n, pl.ds(j, SUBLANES, stride=SEG_PITCH), :] = (
                val[j * SUBLANES:(j + 1) * SUBLANES, n * LANES:(n + 1) * LANES])
    for n in range(slabs):
        for s in range(SUBLANES):
            dst_ref[s * SEG_LEN:(s + 1) * SEG_LEN, n * LANES:(n + 1) * LANES] = (
                stage_ref[n, s * SEG_PITCH:s * SEG_PITCH + SEG_LEN, :])


def _rg_conv(xr, halo_ref, rows_ref, cols, reverse):
    row = lax.broadcasted_iota(jnp.int32, (SUBLANES, xr.shape[1]), 0)
    edge = []
    for g in range(RG_CONV - 1):
        grp = slice(g * SUBLANES, (g + 1) * SUBLANES)
        if reverse:
            edge.append(jnp.where(row == SUBLANES - 1, pltpu.roll(halo_ref[grp, cols], SUBLANES - 1, 0),
                                  pltpu.roll(xr[grp, :], SUBLANES - 1, 0)))
        else:
            cur = xr[TM - (RG_CONV - 1 - g) * SUBLANES:TM - (RG_CONV - 2 - g) * SUBLANES, :]
            edge.append(jnp.where(row == 0, pltpu.roll(halo_ref[grp, cols], 1, 0), pltpu.roll(cur, 1, 0)))
    tap_w = lambda j: rows_ref[ROW_CONV_W + j:ROW_CONV_W + j + 1, cols]
    xc = rows_ref[ROW_CONV_B:ROW_CONV_B + 1, cols] + tap_w(RG_CONV - 1) * xr
    for k in range(1, RG_CONV):
        if reverse:
            tap = jnp.concatenate([xr[k * SUBLANES:, :]] + edge[:k], axis=0)
        else:
            tap = jnp.concatenate(edge[RG_CONV - 1 - k:] + [xr[:TM - k * SUBLANES, :]], axis=0)
        xc = xc + tap_w(RG_CONV - 1 - k) * tap
    halo_ref[:, cols] = xr[:HALO_ROWS, :] if reverse else xr[TM - HALO_ROWS:, :]
    return xc


def _rg_setup(lam_ref, ba_ref, bx_ref, cb_ref, cw_ref, wa_ref, wx_ref, rows_ref, wg_s, copies):
    wg_s[:, :, :RG_BLOCK_W] = wa_ref[...].astype(BF16)
    wg_s[:, :, RG_BLOCK_W:] = wx_ref[...].astype(BF16)
    rows_ref[ROW_DECAY:ROW_DECAY + 1, :] = (-0.5 * RG_C * LOG2_E) * _softplus(-lam_ref[...])
    rows_ref[ROW_BIAS_A:ROW_BIAS_A + 1, :] = 0.5 * ba_ref[...]
    rows_ref[ROW_BIAS_X:ROW_BIAS_X + 1, :] = 0.5 * bx_ref[...]
    rows_ref[ROW_CONV_B:ROW_CONV_B + 1, :] = 0.5 * cb_ref[...]
    rows_ref[ROW_CONV_W:ROW_CONV_W + RG_CONV, :] = 0.5 * cw_ref[...]
    for src_ref, dst_ref in copies:
        dst_ref[...] = src_ref[...].astype(BF16)


def _rg_gates_scan(xc, first_block, wg_ref, rows_ref, carry_ref, reverse):
    row = lax.broadcasted_iota(jnp.int32, (SUBLANES, RG_BLOCK_W), 0)
    shift = SUBLANES - 1 if reverse else 1
    steps = range(SEG_LEN - 1, -1, -1) if reverse else range(SEG_LEN)
    out = []
    for i in range(xc.shape[1] // RG_BLOCK_W):
        n = first_block + i
        cols = slice(n * RG_BLOCK_W, (n + 1) * RG_BLOCK_W)
        xb = xc[:, i * RG_BLOCK_W:(i + 1) * RG_BLOCK_W]
        gates = _dot(xb.astype(BF16), wg_ref[n])
        th_r = jnp.tanh(gates[:, :RG_BLOCK_W] + rows_ref[ROW_BIAS_A:ROW_BIAS_A + 1, cols])
        th_i = jnp.tanh(gates[:, RG_BLOCK_W:] + rows_ref[ROW_BIAS_X:ROW_BIAS_X + 1, cols])
        half_decay = rows_ref[ROW_DECAY:ROW_DECAY + 1, cols]
        a = jnp.exp2(th_r * half_decay + half_decay)
        y = 1.0 - a * a
        root = y * lax.rsqrt(jnp.maximum(y, F32_TINY))
        u = root * ((th_i + 1.0) * xb)

        h = jnp.zeros((SUBLANES, RG_BLOCK_W), F32)
        prod = jnp.ones((SUBLANES, RG_BLOCK_W), F32)
        local, prods = {}, {}
        for j in steps:
            grp = slice(j * SUBLANES, (j + 1) * SUBLANES)
            h = a[grp, :] * h + u[grp, :]
            prod = prod * a[grp, :]
            local[j], prods[j] = h, prod

        entering = carry_ref[:, cols]
        for s in range(SUBLANES - 1):
            nxt = pltpu.roll(prod * entering + h, shift, 0)
            target = SUBLANES - 2 - s if reverse else s + 1
            entering = jnp.where(row == target, nxt, entering)
        carry_ref[:, cols] = pltpu.roll(prod * entering + h, shift, 0)

        out.append(jnp.concatenate([local[j] + prods[j] * entering for j in range(SEG_LEN)], axis=0))
    return jnp.concatenate(out, axis=1)


def _odd_fwd_kernel(xx_ref, modc_ref, modl_ref, g_ref, wx_ref, cw_ref, cb_ref, ga_ref, gx_ref, ba_ref, bx_ref,
                    lam_ref, xr_ref, hf_ref, hn_ref, halo_ref, carry_ref, rows_ref, wx_s, wg_s):
    t = pl.program_id(1)

    @pl.when(_first_step())
    def _():
        _rg_setup(lam_ref, ba_ref, bx_ref, cb_ref, cw_ref, ga_ref, gx_ref, rows_ref, wg_s, ((wx_ref, wx_s),))

    @pl.when(t == 0)
    def _():
        carry_ref[...] = jnp.zeros_like(carry_ref)

    @pl.when(t <= 1)
    def _():
        halo_ref[...] = jnp.zeros_like(halo_ref)

    samples = range(RG_PAIR)
    for r in samples:
        shift = jnp.where(t == 0, modc_ref[0:1, :], modl_ref[r, 0:1, :])
        scale = jnp.where(t == 0, modc_ref[1:2, :], modl_ref[r, 1:2, :])
        hn_ref[r] = _modulated_norm(xx_ref[r], g_ref[...], shift, scale).astype(BF16)
    hb = hn_ref[...].reshape(RG_PAIR * TM, D_MODEL)
    chunks = RG_WIDTH // RG_CHUNK_W
    project = lambda m: _dot(hb, wx_s[:, m * RG_CHUNK_W:(m + 1) * RG_CHUNK_W])
    xr_next = project(0)
    for m in range(chunks):
        cols = slice(m * RG_CHUNK_W, (m + 1) * RG_CHUNK_W)
        xr_pair = xr_next
        if m + 1 < chunks:
            xr_next = project(m + 1)
        for r in samples:
            xr = xr_pair[r * TM:(r + 1) * TM, :]
            xr_ref[r, :, cols] = xr.astype(BF16)
            xc = _rg_conv(xr, halo_ref.at[r], rows_ref, cols, reverse=False)
            hf_ref[r, :, cols] = _rg_gates_scan(xc, m * (RG_CHUNK_W // RG_BLOCK_W), wg_s, rows_ref,
                                                carry_ref.at[r], reverse=False).astype(BF16)


def _odd_fwd(xx, mods, g, wx, cw, cb, ga, gx, ba, bx, lam, nt):
    bsz = xx.shape[0]
    assert bsz % RG_PAIR == 0
    tok = lambda n: pl.BlockSpec((RG_PAIR, TM, n), lambda p, t: (p, t, 0))
    consts = (g, wx, cw, cb, ga, gx, ba, bx, lam)
    gate_scratch = pltpu.VMEM((RG_BLOCKS, RG_BLOCK_W, 2 * RG_BLOCK_W), BF16)
    wide = jax.ShapeDtypeStruct((bsz, nt * TM, RG_WIDTH), BF16)
    return pl.pallas_call(
        _odd_fwd_kernel,
        grid=(bsz // RG_PAIR, nt),
        in_specs=[tok(D_MODEL),
                  pl.BlockSpec((None, None, 3, D_MODEL), lambda p, t: (1, bsz, 0, 0)),
                  pl.BlockSpec((None, RG_PAIR, 3, D_MODEL), lambda p, t: (1, p, 0, 0))]
        + [_const_spec(w) for w in consts],
        out_specs=[tok(RG_WIDTH), tok(RG_WIDTH), tok(D_MODEL)],
        out_shape=[wide, wide, jax.ShapeDtypeStruct((bsz, nt * TM, D_MODEL), BF16)],
        scratch_shapes=[pltpu.VMEM((RG_PAIR, HALO_ROWS, RG_WIDTH), F32),
                        pltpu.VMEM((RG_PAIR, SUBLANES, RG_WIDTH), F32),
                        pltpu.VMEM((RG_ROWS, RG_WIDTH), F32),
                        pltpu.VMEM(wx.shape, BF16),
                        gate_scratch],
        compiler_params=_params(2),
        name="odd_fwd",
    )(xx, mods, mods, *[_operand(w) for w in consts])


def _odd_bwd_kernel(xx_ref, mod_ref, hn_ref, xr_ref, hf_ref, wgate_ref, cw_ref, cb_ref, ga_ref, gx_ref, ba_ref,
                    bx_ref, lam_ref, wout_ref, fg_ref, out_ref,
                    stage_ref, halo_ref, yy_ref, carry_ref, rows_ref, wgate_s, wg_s, wout_s):
    j = pl.program_id(1)

    @pl.when(_first_step())
    def _():
        _rg_setup(lam_ref, ba_ref, bx_ref, cb_ref, cw_ref, ga_ref, gx_ref, rows_ref, wg_s,
                  ((wgate_ref, wgate_s), (wout_ref, wout_s)))

    @pl.when(j == 0)
    def _():
        carry_ref[...] = jnp.zeros_like(carry_ref)

    @pl.when(j <= 1)
    def _():
        halo_ref[...] = jnp.zeros_like(halo_ref)

    def tile(with_output):
        hb = hn_ref[...]
        for m in range(RG_WIDTH // RG_CHUNK_W):
            cols = slice(m * RG_CHUNK_W, (m + 1) * RG_CHUNK_W)
            if with_output:
                gate = _dot(hb, wgate_s[:, cols])
            xc = _rg_conv(xr_ref[:, cols].astype(F32), halo_ref, rows_ref, cols, reverse=True)
            h_bwd = _rg_gates_scan(xc, m * (RG_CHUNK_W // RG_BLOCK_W), wg_s, rows_ref, carry_ref,
                                   reverse=True)
            if with_output:
                yy_ref[:, cols] = ((hf_ref[:, cols].astype(F32) + h_bwd) * _silu(gate)).astype(BF16)
        if with_output:
            xn = xx_ref[...] + mod_ref[2:3, :] * _dot(yy_ref[...], wout_s[...])
            ms = jnp.mean(xn * xn, axis=-1, keepdims=True)
            _from_segment_major(xn * lax.rsqrt(ms + EPS) * fg_ref[...], stage_ref, out_ref, D_MODEL)

    pl.when(j == 0)(lambda: tile(with_output=False))
    pl.when(j > 0)(lambda: tile(with_output=True))


def _odd_bwd(xx, mods, hn, xr, hf, wgate, cw, cb, ga, gx, ba, bx, lam, wout, fg, nt):
    bsz = xx.shape[0]
    tile = _bwd_tile(nt)
    tok = lambda n: pl.BlockSpec((None, TM, n), lambda b, j: (b, tile(j), 0))
    consts = (wgate, cw, cb, ga, gx, ba, bx, lam, wout, fg)
    gate_scratch = pltpu.VMEM((RG_BLOCKS, RG_BLOCK_W, 2 * RG_BLOCK_W), BF16)
    out_block = lambda b, j: (b, jnp.where(j == 0, nt - 2, nt - 1 - j), 0)
    return pl.pallas_call(
        _odd_bwd_kernel,
        grid=(bsz, nt),
        in_specs=[tok(D_MODEL), _mod_spec(1, bsz, tile), tok(D_MODEL), tok(RG_WIDTH), tok(RG_WIDTH)]
        + [_const_spec(w) for w in consts],
        out_specs=pl.BlockSpec((None, TM, D_MODEL), out_block),
        out_shape=jax.ShapeDtypeStruct((bsz, (nt - 1) * TM, D_MODEL), F32),
        scratch_shapes=[pltpu.VMEM((D_MODEL // LANES, SUBLANES * SEG_PITCH, LANES), F32),
                        pltpu.VMEM((HALO_ROWS, RG_WIDTH), F32),
                        pltpu.VMEM((TM, RG_WIDTH), BF16),
                        pltpu.VMEM((SUBLANES, RG_WIDTH), F32),
                        pltpu.VMEM((RG_ROWS, RG_WIDTH), F32),
                        pltpu.VMEM(wgate.shape, BF16),
                        gate_scratch,
                        pltpu.VMEM(wout.shape, BF16)],
        compiler_params=_params(2),
        name="odd_bwd",
    )(xx, mods, hn, xr, hf, *[_operand(w) for w in consts])


def kernel(x, c, ctx, c_ctx, norm_g, w_mod, b_mod, e_w_in, e_w_a2, e_b_a2, e_gla_g, e_conv_w, e_w_out,
           o_w_in, o_conv_w, o_conv_b, o_w_a, o_b_a, o_w_x, o_b_x, o_lam, o_w_out, final_g):
    bsz, seq, _ = x.shape
    assert ctx.shape[1] == TM and seq % TM == 0 and bsz < SUBLANES
    assert w_mod.shape[0] == 2
    nt = 1 + seq // TM

    mods = _mods(c, c_ctx, w_mod, b_mod).reshape(w_mod.shape[0], SUBLANES, 3, D_MODEL)

    w_in = e_w_in[0]
    w_in_bf16 = w_in.astype(BF16)
    square = (D_MODEL, D_MODEL)
    offs = {}
    off = 0
    for name, size in (("q", GLA_QK), ("k", GLA_QK), ("v", GLA_V), ("ga", GLA_V), ("af", GLA_LOWRANK),
                       ("ab", GLA_LOWRANK), ("cb", SC_WIDTH), ("cc", SC_WIDTH), ("cx", SC_WIDTH),
                       ("gb", SC_WIDTH)):
        offs[name] = (off, off + size)
        off += size
    cols = lambda lo, hi: w_in[:, lo:hi].astype(BF16)
    assert offs["q"][0] == 0 and offs["v"][0] == D_MODEL and offs["ga"] == (2 * D_MODEL, 3 * D_MODEL)
    wqk, wv, wga = (_Window(w_in_bf16, square, (0, i)) for i in range(3))
    wa = jnp.pad(w_in[:, offs["af"][0]:offs["ab"][1]], ((0, 0), (0, LANES - 2 * GLA_LOWRANK))).astype(BF16)
    b2 = e_b_a2[0].reshape(1, 2 * GLA_QK)
    mats = (wqk, wv, wga, wa, cols(*offs["cb"]), cols(*offs["cc"]), cols(*offs["cx"]), cols(*offs["gb"]))
    (o_f, qi_b, ki_b, ke_b, dec_b, v, sga, y) = _even_in(
        ctx, x, mods, (norm_g[0:1], b2, e_conv_w[0], e_w_a2[0]), mats, nt)
    w_out = e_w_out[0]
    xx = _gla_bwd(qi_b, ki_b, ke_b, v, dec_b, o_f, sga, y, ctx, x, mods, e_gla_g[0:1],
                  _Window(w_out, square, (0, 0)), _Window(w_out, square, (1, 0)), nt)

    w_in = o_w_in[0]
    half = (D_MODEL, RG_WIDTH)
    wide = lambda a: a.reshape(1, RG_WIDTH)
    xr, h_f, hn = _odd_fwd(xx, mods, norm_g[1:2], _Window(w_in, half, (0, 0)), o_conv_w[0, 0],
                           wide(o_conv_b[0, 0]), o_w_a[0, 0], o_w_x[0, 0],
                           wide(o_b_a[0, 0]), wide(o_b_x[0, 0]), wide(o_lam[0, 0]), nt)
    return _odd_bwd(xx, mods, hn, xr, h_f, _Window(w_in, half, (0, 1)), o_conv_w[0, 1],
                    wide(o_conv_b[0, 1]), o_w_a[0, 1], o_w_x[0, 1],
                    wide(o_b_a[0, 1]), wide(o_b_x[0, 1]), wide(o_lam[0, 1]), o_w_out[0],
                    final_g.reshape(1, D_MODEL), nt)
```

```python
import jax
import jax.numpy as jnp
from jax import lax
from jax.experimental import pallas as pl
from jax.experimental.pallas import tpu as pltpu

F32 = jnp.float32
BF16 = jnp.bfloat16

D_MODEL = 1024
EPS = 1e-6
GLA_HEADS = 4
GLA_DK = 128
GLA_DV = 256
GLA_QK = GLA_HEADS * GLA_DK
GLA_V = GLA_HEADS * GLA_DV
GLA_LOWRANK = 16
GLA_TAU = 16.0
GLA_CHUNK = 64
GRID_W = 64
SC_WIDTH = D_MODEL
RG_WIDTH = 2 * D_MODEL
RG_BLOCKS = 16
RG_BLOCK_W = 128
RG_C = 8.0
RG_CONV = 4

TM = 256
CHUNKS = TM // GLA_CHUNK
GLA_PAIR = 2
RG_PAIR = 4
EVEN_PAIR = 2
SUBLANES = 8
LANES = 128
VMEM_LIMIT = 56 * 1024 * 1024
SEG_LEN = TM // SUBLANES
SEG_PITCH = SEG_LEN + SUBLANES
HALO_ROWS = (RG_CONV - 1) * SUBLANES
RG_CHUNK_W = 256
ROW_DECAY, ROW_BIAS_A, ROW_BIAS_X, ROW_CONV_B, ROW_CONV_W = 0, 1, 2, 3, 4
RG_ROWS = ROW_CONV_W + RG_CONV
LOG2_E = 1.4426950408889634
F32_TINY = 1.1754943508222875e-38


def _silu(x):
    return x * jax.nn.sigmoid(x)


def _log_sigmoid(z):
    return jnp.minimum(z, 0.0) - jnp.log(1.0 + jnp.exp(-jnp.abs(z)))


def _softplus(z):
    return jnp.maximum(z, 0.0) + jnp.log1p(jnp.exp(-jnp.abs(z)))


def _dot(a, b):
    return jnp.dot(a, b, preferred_element_type=F32)


def _dot_nt(a, b):
    return lax.dot_general(a, b, (((1,), (1,)), ((), ())), preferred_element_type=F32)


def _dot_tn(a, b):
    return lax.dot_general(a, b, (((0,), (0,)), ((), ())), preferred_element_type=F32)


def _modulated_norm(x, g, shift, scale):
    ms = jnp.mean(x * x, axis=-1, keepdims=True)
    return (x * lax.rsqrt(ms + EPS) * g) * (1.0 + scale) + shift


def _params(n_axes):
    return pltpu.CompilerParams(
        dimension_semantics=("arbitrary",) * n_axes,
        vmem_limit_bytes=VMEM_LIMIT)


def _first_step():
    return (pl.program_id(0) == 0) & (pl.program_id(1) == 0)


class _Window:
    def __init__(self, array, block_shape, block_index):
        self.array, self.shape, self.index = array, tuple(block_shape), tuple(block_index)


def _operand(w):
    return w.array if isinstance(w, _Window) else w


def _const_spec(w):
    index = w.index if isinstance(w, _Window) else (0,) * len(w.shape)
    return pl.BlockSpec(w.shape, lambda *_: index, pipeline_mode=pl.Buffered(1))


def _mod_spec(layer, bsz, tile):
    return pl.BlockSpec((None, None, 3, D_MODEL),
                        lambda b, t: (layer, jnp.where(tile(t) == 0, bsz, b), 0, 0))


def _mods_kernel(c_ref, cctx_ref, w_ref, b_ref, o_ref, s_ref):
    bsz = c_ref.shape[0]
    s_ref[...] = jnp.zeros_like(s_ref)
    s_ref[0:bsz, :] = c_ref[...]
    s_ref[bsz:bsz + 1, :] = cctx_ref[...]
    s = _silu(s_ref[...])
    o_ref[0] = jnp.dot(s, w_ref[0], preferred_element_type=F32,
                       precision=lax.Precision.HIGHEST) + b_ref[0]


def _mods(c, c_ctx, w_mod, b_mod):
    depth = w_mod.shape[0]
    nb = 3
    return pl.pallas_call(
        _mods_kernel,
        grid=(depth, nb),
        in_specs=[
            pl.BlockSpec(c.shape, lambda l, j: (0, 0)),
            pl.BlockSpec((1, D_MODEL), lambda l, j: (0, 0)),
            pl.BlockSpec((1, D_MODEL, D_MODEL), lambda l, j: (l, 0, j)),
            pl.BlockSpec((1, 1, D_MODEL), lambda l, j: (l, 0, j)),
        ],
        out_specs=pl.BlockSpec((1, SUBLANES, D_MODEL), lambda l, j: (l, 0, j)),
        out_shape=jax.ShapeDtypeStruct((depth, SUBLANES, 3 * D_MODEL), F32),
        scratch_shapes=[pltpu.VMEM((SUBLANES, D_MODEL), F32)],
        compiler_params=_params(2),
        name="mods",
    )(c, c_ctx.reshape(1, D_MODEL), w_mod, b_mod.reshape(depth, 1, 3 * D_MODEL))


def _chunk_rows(x, offset):
    return jnp.concatenate(
        [jnp.broadcast_to(x[c * GLA_CHUNK + offset:c * GLA_CHUNK + offset + 1, :], (GLA_CHUNK, x.shape[1]))
         for c in range(x.shape[0] // GLA_CHUNK)], axis=0)


def _even_in_kernel(ctx_ref, x_ref, modc_ref, modl_ref, g_ref, b2_ref, cw_ref, wa2_ref,
                    wqk_ref, wv_ref, wga_ref, wa_ref, wcb_ref, wcc_ref, wcx_ref, wgb_ref,
                    of_ref, qib_ref, kib_ref, keb_ref, decb_ref, v_ref, sga_ref, y_ref,
                    lower_ref, upper_ref, w2_ref, qif_ref, kif_ref, kef_ref, decf_ref, st_ref):
    t = pl.program_id(1)
    samples = range(EVEN_PAIR)
    rows_of = lambda r: slice(r * TM, (r + 1) * TM)
    split = lambda val: val.reshape(EVEN_PAIR, TM, val.shape[-1])

    @pl.when(t == 0)
    def _():
        st_ref[...] = jnp.zeros_like(st_ref)

    @pl.when(_first_step())
    def _():
        r = lax.broadcasted_iota(jnp.int32, (TM, TM), 0)
        c = lax.broadcasted_iota(jnp.int32, (TM, TM), 1)
        same = (r // GLA_CHUNK) == (c // GLA_CHUNK)
        lower_ref[...] = jnp.where(same & (c <= r), 1.0, 0.0).astype(BF16)
        upper_ref[...] = jnp.where(same & (c >= r), 1.0, 0.0).astype(BF16)
        w2_ref[...] = jnp.zeros_like(w2_ref)
        w2_ref[0:GLA_LOWRANK, 0:GLA_QK] = wa2_ref[0].astype(BF16)
        w2_ref[GLA_LOWRANK:2 * GLA_LOWRANK, GLA_QK:] = wa2_ref[1].astype(BF16)

    normed = []
    for r in samples:
        xt = jnp.where(t == 0, ctx_ref[r], x_ref[r])
        shift = jnp.where(t == 0, modc_ref[0:1, :], modl_ref[r, 0:1, :])
        scale = jnp.where(t == 0, modc_ref[1:2, :], modl_ref[r, 1:2, :])
        normed.append(_modulated_norm(xt, g_ref[...], shift, scale).astype(BF16))
    hb = jnp.concatenate(normed, axis=0)

    a_lr = _dot(hb, wa_ref[...])
    z = _dot(a_lr.astype(BF16), w2_ref[...]) + b2_ref[...]
    qk = _dot(hb, wqk_ref[...])
    v_ref[...] = split(_dot(hb, wv_ref[...]).astype(BF16))

    lg = _log_sigmoid(z) * (1.0 / GLA_TAU)
    lgb = lg.astype(BF16)
    cs_f = jnp.concatenate([_dot(lower_ref[...], lgb[rows_of(r), :GLA_QK]) for r in samples], axis=0)
    cs_b = jnp.concatenate([_dot(upper_ref[...], lgb[rows_of(r), GLA_QK:]) for r in samples], axis=0)
    tot_f = _chunk_rows(cs_f, GLA_CHUNK - 1)
    tot_b = _chunk_rows(cs_b, 0)
    c_c = _dot(hb, wcc_ref[...])

    q = qk[:, :GLA_QK] * (GLA_DK ** -0.5)
    k = qk[:, GLA_QK:]
    qif_ref[...] = split((q * jnp.exp(cs_f)).astype(BF16))
    kif_ref[...] = split((k * jnp.exp(-cs_f)).astype(BF16))
    kef_ref[...] = split((k * jnp.exp(tot_f - cs_f)).astype(BF16))
    qib_ref[...] = split((q * jnp.exp(cs_b)).astype(BF16))
    kib_ref[...] = split((k * jnp.exp(-cs_b)).astype(BF16))
    keb_ref[...] = split((k * jnp.exp(tot_b - cs_b)).astype(BF16))
    for r in samples:
        for c in range(CHUNKS):
            row = r * TM + c * GLA_CHUNK
            decf_ref[r, c:c + 1, :] = jnp.exp(tot_f[row:row + 1, :])
            decb_ref[r, c:c + 1, :] = jnp.exp(tot_b[row:row + 1, :])

    def emit_to(r):
        def emit(rows, vcols, o):
            of_ref[r, rows, vcols] = o.astype(BF16)
        return emit

    mask = _gla_mask(reverse=False)
    open_head = lambda hd: [_gla_head_open(hd, qif_ref.at[r], kif_ref.at[r], kef_ref.at[r], v_ref.at[r], mask)
                            for r in samples]

    def close_head(hd, opened):
        for r in samples:
            _gla_head_close(hd, opened[r], qif_ref.at[r], v_ref.at[r], decf_ref.at[r], st_ref.at[r],
                            emit_to(r), reverse=False)

    opened = open_head(0)
    zz = c_c * _dot(hb, wcx_ref[...])
    close_head(0, opened)
    opened = open_head(1)
    c_b = _dot(hb, wcb_ref[...])
    close_head(1, opened)
    opened = open_head(2)
    g_b = _dot(hb, wgb_ref[...])
    close_head(2, opened)
    opened = open_head(3)
    sga_ref[...] = split(_silu(_dot(hb, wga_ref[...])).astype(BF16))
    close_head(3, opened)

    step_rows = EVEN_PAIR * TM
    pos = lax.broadcasted_iota(jnp.int32, (step_rows, SC_WIDTH), 0)
    row_len = jnp.where(t == 0, TM, GRID_W)
    in_row = pos & (row_len - 1)
    z_prev = jnp.where(in_row != 0, pltpu.roll(zz, 1, 0), 0.0)
    z_next = jnp.where(in_row != row_len - 1, pltpu.roll(zz, step_rows - 1, 0), 0.0)
    zc = cw_ref[0:1, :] * z_prev + cw_ref[1:2, :] * zz + cw_ref[2:3, :] * z_next
    y_ref[...] = split((c_b * zc * _silu(g_b)).astype(BF16))


def _even_in(ctx, x, mods, rows, mats, nt):
    bsz = x.shape[0]
    assert bsz % EVEN_PAIR == 0
    tok = lambda n: pl.BlockSpec((EVEN_PAIR, TM, n), lambda p, t: (p, t, 0))
    dec = pl.BlockSpec((EVEN_PAIR, None, CHUNKS, GLA_QK), lambda p, t: (p, t, 0, 0))
    tok_shape = lambda n: jax.ShapeDtypeStruct((bsz, nt * TM, n), BF16)
    dec_shape = jax.ShapeDtypeStruct((bsz, nt, CHUNKS, GLA_QK), F32)
    consts = tuple(rows) + tuple(mats)
    return pl.pallas_call(
        _even_in_kernel,
        grid=(bsz // EVEN_PAIR, nt),
        in_specs=[
            pl.BlockSpec((EVEN_PAIR, TM, D_MODEL), lambda p, t: (p, 0, 0)),
            pl.BlockSpec((EVEN_PAIR, TM, D_MODEL), lambda p, t: (p, jnp.maximum(t - 1, 0), 0)),
            pl.BlockSpec((None, None, 3, D_MODEL), lambda p, t: (0, bsz, 0, 0)),
            pl.BlockSpec((None, EVEN_PAIR, 3, D_MODEL), lambda p, t: (0, p, 0, 0)),
        ] + [_const_spec(w) for w in consts],
        out_specs=[tok(GLA_V), tok(GLA_QK), tok(GLA_QK), tok(GLA_QK), dec,
                   tok(GLA_V), tok(GLA_V), tok(SC_WIDTH)],
        out_shape=[tok_shape(GLA_V), tok_shape(GLA_QK), tok_shape(GLA_QK), tok_shape(GLA_QK), dec_shape,
                   tok_shape(GLA_V), tok_shape(GLA_V), tok_shape(SC_WIDTH)],
        scratch_shapes=[pltpu.VMEM((TM, TM), BF16)] * 2 + [pltpu.VMEM((LANES, 2 * GLA_QK), BF16)]
        + [pltpu.VMEM((EVEN_PAIR, TM, GLA_QK), BF16)] * 3
        + [pltpu.VMEM((EVEN_PAIR, CHUNKS, GLA_QK), F32),
           pltpu.VMEM((EVEN_PAIR, GLA_HEADS, GLA_DV, GLA_DK), F32)],
        compiler_params=_params(2),
        name="even_in",
    )(ctx, x, mods, mods, *[_operand(w) for w in consts])


def _chunk_slice(ch):
    return slice(ch * GLA_CHUNK, (ch + 1) * GLA_CHUNK)


def _gla_mask(reverse):
    r = lax.broadcasted_iota(jnp.int32, (TM, TM), 0)
    c = lax.broadcasted_iota(jnp.int32, (TM, TM), 1)
    same_chunk = (r // GLA_CHUNK) == (c // GLA_CHUNK)
    return same_chunk & ((c >= r) if reverse else (c <= r))


def _gla_head_open(hd, qi_ref, ki_ref, ke_ref, v_ref, mask):
    kcols = slice(hd * GLA_DK, (hd + 1) * GLA_DK)
    vcols = slice(hd * GLA_DV, (hd + 1) * GLA_DV)
    incr = [_dot_tn(v_ref[_chunk_slice(ch), vcols], ke_ref[_chunk_slice(ch), kcols]) for ch in range(CHUNKS)]
    scores = jnp.where(mask, _dot_nt(qi_ref[:, kcols], ki_ref[:, kcols]), 0.0).astype(BF16)
    return incr, scores


def _gla_head_close(hd, opened, qi_ref, v_ref, dec_ref, st_ref, emit, reverse):
    incr, scores = opened
    kcols = slice(hd * GLA_DK, (hd + 1) * GLA_DK)
    vcols = slice(hd * GLA_DV, (hd + 1) * GLA_DV)
    state = st_ref[hd]
    before = {}
    for ch in (range(CHUNKS - 1, -1, -1) if reverse else range(CHUNKS)):
        before[ch] = state.astype(BF16)
        state = dec_ref[ch:ch + 1, kcols] * state + incr[ch]
    st_ref[hd] = state
    intra = _dot(scores, v_ref[:, vcols])
    for ch in range(CHUNKS):
        rows = _chunk_slice(ch)
        emit(rows, vcols, intra[rows, :] + _dot_nt(qi_ref[rows, kcols], before[ch]))


def _gla_tile(qi_ref, ki_ref, ke_ref, v_ref, dec_ref, st_ref, emit, reverse):
    mask = _gla_mask(reverse)
    opened = [_gla_head_open(hd, qi_ref, ki_ref, ke_ref, v_ref, mask) for hd in range(GLA_HEADS)]
    for hd in range(GLA_HEADS):
        _gla_head_close(hd, opened[hd], qi_ref, v_ref, dec_ref, st_ref, emit, reverse)


def _gla_bwd_kernel(qi_ref, ki_ref, ke_ref, v_ref, dec_ref, of_ref, sga_ref, y_ref, ctx_ref, x_ref,
                    modc_ref, modl_ref, gg_ref, wo_in, wy_in, out_ref, st_ref, ob_ref, wo_ref, wy_ref, stage_ref):
    j = pl.program_id(1)
    samples = range(GLA_PAIR)

    @pl.when(_first_step())
    def _():
        wo_ref[...] = wo_in[...].astype(BF16)
        wy_ref[...] = wy_in[...].astype(BF16)

    @pl.when(j == 0)
    def _():
        st_ref[...] = jnp.zeros_like(st_ref)

    proj_y = _dot(y_ref[...].reshape(GLA_PAIR * TM, SC_WIDTH), wy_ref[...])

    def emit_to(r):
        def emit(rows, vcols, o):
            ob_ref[r, rows, vcols] = o
        return emit

    mask = _gla_mask(reverse=True)
    opened = [[_gla_head_open(hd, qi_ref.at[r], ki_ref.at[r], ke_ref.at[r], v_ref.at[r], mask)
               for hd in range(GLA_HEADS)] for r in samples]
    for hd in range(GLA_HEADS):
        for r in samples:
            _gla_head_close(hd, opened[r][hd], qi_ref.at[r], v_ref.at[r], dec_ref.at[r], st_ref.at[r],
                            emit_to(r), reverse=True)

    inner = []
    for r in samples:
        for hd in range(GLA_HEADS):
            vcols = slice(hd * GLA_DV, (hd + 1) * GLA_DV)
            o = of_ref[r, :, vcols].astype(F32) + ob_ref[r, :, vcols]
            ms = jnp.mean(o * o, axis=-1, keepdims=True)
            on = o * lax.rsqrt(ms + EPS) * gg_ref[...]
            inner.append((on * sga_ref[r, :, vcols].astype(F32)).astype(BF16))
    inner = jnp.concatenate([jnp.concatenate(inner[r * GLA_HEADS:(r + 1) * GLA_HEADS], axis=-1)
                             for r in samples], axis=0)
    proj = _dot(inner, wo_ref[...]) + proj_y
    for r in samples:
        gate = jnp.where(j == 0, modc_ref[2:3, :], modl_ref[r, 2:3, :])
        ob_ref[r] = jnp.where(j == 0, ctx_ref[r], x_ref[r]) + gate * proj[r * TM:(r + 1) * TM, :]
        out_ref[r] = _to_segment_major(ob_ref.at[r], stage_ref, D_MODEL)


def _bwd_tile(nt):
    return lambda j: jnp.where(j == 0, 0, nt - j)


def _gla_bwd(qi, ki, ke, v, dec, o_f, sga, y, ctx, x, mods, gg, wo, wy, nt):
    assert GLA_V == D_MODEL
    bsz = qi.shape[0]
    assert bsz % GLA_PAIR == 0
    tile = _bwd_tile(nt)
    tok = lambda n: pl.BlockSpec((GLA_PAIR, TM, n), lambda p, j: (p, tile(j), 0))
    latent_block = lambda p, j: (p, jnp.where(j == 0, nt - 2, tile(j) - 1), 0)
    return pl.pallas_call(
        _gla_bwd_kernel,
        grid=(bsz // GLA_PAIR, nt),
        in_specs=[tok(GLA_QK), tok(GLA_QK), tok(GLA_QK), tok(GLA_V),
                  pl.BlockSpec((GLA_PAIR, None, CHUNKS, GLA_QK), lambda p, j: (p, tile(j), 0, 0)),
                  tok(GLA_V), tok(GLA_V), tok(SC_WIDTH),
                  pl.BlockSpec((GLA_PAIR, TM, D_MODEL), lambda p, j: (p, 0, 0)),
                  pl.BlockSpec((GLA_PAIR, TM, D_MODEL), latent_block),
                  pl.BlockSpec((None, None, 3, D_MODEL), lambda p, j: (0, bsz, 0, 0)),
                  pl.BlockSpec((None, GLA_PAIR, 3, D_MODEL), lambda p, j: (0, p, 0, 0)),
                  _const_spec(gg), _const_spec(wo), _const_spec(wy)],
        out_specs=tok(D_MODEL),
        out_shape=jax.ShapeDtypeStruct((bsz, nt * TM, D_MODEL), F32),
        scratch_shapes=[pltpu.VMEM((GLA_PAIR, GLA_HEADS, GLA_DV, GLA_DK), F32),
                        pltpu.VMEM((GLA_PAIR, TM, GLA_V), F32),
                        pltpu.VMEM(wo.shape, BF16),
                        pltpu.VMEM(wy.shape, BF16),
                        pltpu.VMEM((D_MODEL // LANES, SUBLANES * SEG_PITCH, LANES), F32)],
        compiler_params=_params(2),
        name="gla_bwd",
    )(qi, ki, ke, v, dec, o_f, sga, y, ctx, x, mods, mods, gg, _operand(wo), _operand(wy))


def _to_segment_major(src_ref, stage_ref, width):
    slabs = width // LANES
    for n in range(slabs):
        for s in range(SUBLANES):
            stage_ref[n, s * SEG_PITCH:s * SEG_PITCH + SEG_LEN, :] = (
                src_ref[s * SEG_LEN:(s + 1) * SEG_LEN, n * LANES:(n + 1) * LANES])
    groups = [
        jnp.concatenate([stage_ref[n, pl.ds(j, SUBLANES, stride=SEG_PITCH), :] for n in range(slabs)], axis=1)
        for j in range(SEG_LEN)]
    return jnp.concatenate(groups, axis=0)


def _from_segment_major(val, stage_ref, dst_ref, width):
    slabs = width // LANES
    for j in range(SEG_LEN):
        for n in range(slabs):
            stage_ref[n, pl.ds(j, SUBLANES, stride=SEG_PITCH), :] = (
                val[j * SUBLANES:(j + 1) * SUBLANES, n * LANES:(n + 1) * LANES])
    for n in range(slabs):
        for s in range(SUBLANES):
            dst_ref[s * SEG_LEN:(s + 1) * SEG_LEN, n * LANES:(n + 1) * LANES] = (
                stage_ref[n, s * SEG_PITCH:s * SEG_PITCH + SEG_LEN, :])


def _rg_conv(xr, halo_ref, rows_ref, cols, reverse):
    row = lax.broadcasted_iota(jnp.int32, (SUBLANES, xr.shape[1]), 0)
    edge = []
    for g in range(RG_CONV - 1):
        grp = slice(g * SUBLANES, (g + 1) * SUBLANES)
        if reverse:
            edge.append(jnp.where(row == SUBLANES - 1, pltpu.roll(halo_ref[grp, cols], SUBLANES - 1, 0),
                                  pltpu.roll(xr[grp, :], SUBLANES - 1, 0)))
        else:
            cur = xr[TM - (RG_CONV - 1 - g) * SUBLANES:TM - (RG_CONV - 2 - g) * SUBLANES, :]
            edge.append(jnp.where(row == 0, pltpu.roll(halo_ref[grp, cols], 1, 0), pltpu.roll(cur, 1, 0)))
    tap_w = lambda j: rows_ref[ROW_CONV_W + j:ROW_CONV_W + j + 1, cols]
    xc = rows_ref[ROW_CONV_B:ROW_CONV_B + 1, cols] + tap_w(RG_CONV - 1) * xr
    for k in range(1, RG_CONV):
        if reverse:
            tap = jnp.concatenate([xr[k * SUBLANES:, :]] + edge[:k], axis=0)
        else:
            tap = jnp.concatenate(edge[RG_CONV - 1 - k:] + [xr[:TM - k * SUBLANES, :]], axis=0)
        xc = xc + tap_w(RG_CONV - 1 - k) * tap
    halo_ref[:, cols] = xr[:HALO_ROWS, :] if reverse else xr[TM - HALO_ROWS:, :]
    return xc


def _rg_setup(lam_ref, ba_ref, bx_ref, cb_ref, cw_ref, wa_ref, wx_ref, rows_ref, wg_s, copies):
    wg_s[:, :, :RG_BLOCK_W] = wa_ref[...].astype(BF16)
    wg_s[:, :, RG_BLOCK_W:] = wx_ref[...].astype(BF16)
    rows_ref[ROW_DECAY:ROW_DECAY + 1, :] = (-0.5 * RG_C * LOG2_E) * _softplus(-lam_ref[...])
    rows_ref[ROW_BIAS_A:ROW_BIAS_A + 1, :] = 0.5 * ba_ref[...]
    rows_ref[ROW_BIAS_X:ROW_BIAS_X + 1, :] = 0.5 * bx_ref[...]
    rows_ref[ROW_CONV_B:ROW_CONV_B + 1, :] = 0.5 * cb_ref[...]
    rows_ref[ROW_CONV_W:ROW_CONV_W + RG_CONV, :] = 0.5 * cw_ref[...]
    for src_ref, dst_ref in copies:
        dst_ref[...] = src_ref[...].astype(BF16)


def _rg_gates_scan(xc, first_block, wg_ref, rows_ref, carry_ref, reverse):
    row = lax.broadcasted_iota(jnp.int32, (SUBLANES, RG_BLOCK_W), 0)
    shift = SUBLANES - 1 if reverse else 1
    steps = range(SEG_LEN - 1, -1, -1) if reverse else range(SEG_LEN)
    out = []
    for i in range(xc.shape[1] // RG_BLOCK_W):
        n = first_block + i
        cols = slice(n * RG_BLOCK_W, (n + 1) * RG_BLOCK_W)
        xb = xc[:, i * RG_BLOCK_W:(i + 1) * RG_BLOCK_W]
        gates = _dot(xb.astype(BF16), wg_ref[n])
        th_r = jnp.tanh(gates[:, :RG_BLOCK_W] + rows_ref[ROW_BIAS_A:ROW_BIAS_A + 1, cols])
        th_i = jnp.tanh(gates[:, RG_BLOCK_W:] + rows_ref[ROW_BIAS_X:ROW_BIAS_X + 1, cols])
        half_decay = rows_ref[ROW_DECAY:ROW_DECAY + 1, cols]
        a = jnp.exp2(th_r * half_decay + half_decay)
        y = 1.0 - a * a
        root = y * lax.rsqrt(jnp.maximum(y, F32_TINY))
        u = root * ((th_i + 1.0) * xb)

        h = jnp.zeros((SUBLANES, RG_BLOCK_W), F32)
        prod = jnp.ones((SUBLANES, RG_BLOCK_W), F32)
        local, prods = {}, {}
        for j in steps:
            grp = slice(j * SUBLANES, (j + 1) * SUBLANES)
            h = a[grp, :] * h + u[grp, :]
            prod = prod * a[grp, :]
            local[j], prods[j] = h, prod

        entering = carry_ref[:, cols]
        for s in range(SUBLANES - 1):
            nxt = pltpu.roll(prod * entering + h, shift, 0)
            target = SUBLANES - 2 - s if reverse else s + 1
            entering = jnp.where(row == target, nxt, entering)
        carry_ref[:, cols] = pltpu.roll(prod * entering + h, shift, 0)

        out.append(jnp.concatenate([local[j] + prods[j] * entering for j in range(SEG_LEN)], axis=0))
    return jnp.concatenate(out, axis=1)


def _odd_fwd_kernel(xx_ref, modc_ref, modl_ref, g_ref, wx_ref, cw_ref, cb_ref, ga_ref, gx_ref, ba_ref, bx_ref,
                    lam_ref, xr_ref, hf_ref, hn_ref, halo_ref, carry_ref, rows_ref, wx_s, wg_s):
    t = pl.program_id(1)

    @pl.when(_first_step())
    def _():
        _rg_setup(lam_ref, ba_ref, bx_ref, cb_ref, cw_ref, ga_ref, gx_ref, rows_ref, wg_s, ((wx_ref, wx_s),))

    @pl.when(t == 0)
    def _():
        carry_ref[...] = jnp.zeros_like(carry_ref)

    @pl.when(t <= 1)
    def _():
        halo_ref[...] = jnp.zeros_like(halo_ref)

    samples = range(RG_PAIR)
    for r in samples:
        shift = jnp.where(t == 0, modc_ref[0:1, :], modl_ref[r, 0:1, :])
        scale = jnp.where(t == 0, modc_ref[1:2, :], modl_ref[r, 1:2, :])
        hn_ref[r] = _modulated_norm(xx_ref[r], g_ref[...], shift, scale).astype(BF16)
    hb = hn_ref[...].reshape(RG_PAIR * TM, D_MODEL)
    chunks = RG_WIDTH // RG_CHUNK_W
    project = lambda m: _dot(hb, wx_s[:, m * RG_CHUNK_W:(m + 1) * RG_CHUNK_W])
    xr_next = project(0)
    for m in range(chunks):
        cols = slice(m * RG_CHUNK_W, (m + 1) * RG_CHUNK_W)
        xr_pair = xr_next
        if m + 1 < chunks:
            xr_next = project(m + 1)
        for r in samples:
            xr = xr_pair[r * TM:(r + 1) * TM, :]
            xr_ref[r, :, cols] = xr.astype(BF16)
            xc = _rg_conv(xr, halo_ref.at[r], rows_ref, cols, reverse=False)
            hf_ref[r, :, cols] = _rg_gates_scan(xc, m * (RG_CHUNK_W // RG_BLOCK_W), wg_s, rows_ref,
                                                carry_ref.at[r], reverse=False).astype(BF16)


def _odd_fwd(xx, mods, g, wx, cw, cb, ga, gx, ba, bx, lam, nt):
    bsz = xx.shape[0]
    assert bsz % RG_PAIR == 0
    tok = lambda n: pl.BlockSpec((RG_PAIR, TM, n), lambda p, t: (p, t, 0))
    consts = (g, wx, cw, cb, ga, gx, ba, bx, lam)
    gate_scratch = pltpu.VMEM((RG_BLOCKS, RG_BLOCK_W, 2 * RG_BLOCK_W), BF16)
    wide = jax.ShapeDtypeStruct((bsz, nt * TM, RG_WIDTH), BF16)
    return pl.pallas_call(
        _odd_fwd_kernel,
        grid=(bsz // RG_PAIR, nt),
        in_specs=[tok(D_MODEL),
                  pl.BlockSpec((None, None, 3, D_MODEL), lambda p, t: (1, bsz, 0, 0)),
                  pl.BlockSpec((None, RG_PAIR, 3, D_MODEL), lambda p, t: (1, p, 0, 0))]
        + [_const_spec(w) for w in consts],
        out_specs=[tok(RG_WIDTH), tok(RG_WIDTH), tok(D_MODEL)],
        out_shape=[wide, wide, jax.ShapeDtypeStruct((bsz, nt * TM, D_MODEL), BF16)],
        scratch_shapes=[pltpu.VMEM((RG_PAIR, HALO_ROWS, RG_WIDTH), F32),
                        pltpu.VMEM((RG_PAIR, SUBLANES, RG_WIDTH), F32),
                        pltpu.VMEM((RG_ROWS, RG_WIDTH), F32),
                        pltpu.VMEM(wx.shape, BF16),
                        gate_scratch],
        compiler_params=_params(2),
        name="odd_fwd",
    )(xx, mods, mods, *[_operand(w) for w in consts])


def _odd_bwd_kernel(xx_ref, mod_ref, hn_ref, xr_ref, hf_ref, wgate_ref, cw_ref, cb_ref, ga_ref, gx_ref, ba_ref,
                    bx_ref, lam_ref, wout_ref, fg_ref, out_ref,
                    stage_ref, halo_ref, yy_ref, carry_ref, rows_ref, wgate_s, wg_s, wout_s):
    j = pl.program_id(1)

    @pl.when(_first_step())
    def _():
        _rg_setup(lam_ref, ba_ref, bx_ref, cb_ref, cw_ref, ga_ref, gx_ref, rows_ref, wg_s,
                  ((wgate_ref, wgate_s), (wout_ref, wout_s)))

    @pl.when(j == 0)
    def _():
        carry_ref[...] = jnp.zeros_like(carry_ref)

    @pl.when(j <= 1)
    def _():
        halo_ref[...] = jnp.zeros_like(halo_ref)

    def tile(with_output):
        hb = hn_ref[...]
        for m in range(RG_WIDTH // RG_CHUNK_W):
            cols = slice(m * RG_CHUNK_W, (m + 1) * RG_CHUNK_W)
            if with_output:
                gate = _dot(hb, wgate_s[:, cols])
            xc = _rg_conv(xr_ref[:, cols].astype(F32), halo_ref, rows_ref, cols, reverse=True)
            h_bwd = _rg_gates_scan(xc, m * (RG_CHUNK_W // RG_BLOCK_W), wg_s, rows_ref, carry_ref,
                                   reverse=True)
            if with_output:
                yy_ref[:, cols] = ((hf_ref[:, cols].astype(F32) + h_bwd) * _silu(gate)).astype(BF16)
        if with_output:
            xn = xx_ref[...] + mod_ref[2:3, :] * _dot(yy_ref[...], wout_s[...])
            ms = jnp.mean(xn * xn, axis=-1, keepdims=True)
            _from_segment_major(xn * lax.rsqrt(ms + EPS) * fg_ref[...], stage_ref, out_ref, D_MODEL)

    pl.when(j == 0)(lambda: tile(with_output=False))
    pl.when(j > 0)(lambda: tile(with_output=True))


def _odd_bwd(xx, mods, hn, xr, hf, wgate, cw, cb, ga, gx, ba, bx, lam, wout, fg, nt):
    bsz = xx.shape[0]
    tile = _bwd_tile(nt)
    tok = lambda n: pl.BlockSpec((None, TM, n), lambda b, j: (b, tile(j), 0))
    consts = (wgate, cw, cb, ga, gx, ba, bx, lam, wout, fg)
    gate_scratch = pltpu.VMEM((RG_BLOCKS, RG_BLOCK_W, 2 * RG_BLOCK_W), BF16)
    out_block = lambda b, j: (b, jnp.where(j == 0, nt - 2, nt - 1 - j), 0)
    return pl.pallas_call(
        _odd_bwd_kernel,
        grid=(bsz, nt),
        in_specs=[tok(D_MODEL), _mod_spec(1, bsz, tile), tok(D_MODEL), tok(RG_WIDTH), tok(RG_WIDTH)]
        + [_const_spec(w) for w in consts],
        out_specs=pl.BlockSpec((None, TM, D_MODEL), out_block),
        out_shape=jax.ShapeDtypeStruct((bsz, (nt - 1) * TM, D_MODEL), F32),
        scratch_shapes=[pltpu.VMEM((D_MODEL // LANES, SUBLANES * SEG_PITCH, LANES), F32),
                        pltpu.VMEM((HALO_ROWS, RG_WIDTH), F32),
                        pltpu.VMEM((TM, RG_WIDTH), BF16),
                        pltpu.VMEM((SUBLANES, RG_WIDTH), F32),
                        pltpu.VMEM((RG_ROWS, RG_WIDTH), F32),
                        pltpu.VMEM(wgate.shape, BF16),
                        gate_scratch,
                        pltpu.VMEM(wout.shape, BF16)],
        compiler_params=_params(2),
        name="odd_bwd",
    )(xx, mods, hn, xr, hf, *[_operand(w) for w in consts])


def kernel(x, c, ctx, c_ctx, norm_g, w_mod, b_mod, e_w_in, e_w_a2, e_b_a2, e_gla_g, e_conv_w, e_w_out,
           o_w_in, o_conv_w, o_conv_b, o_w_a, o_b_a, o_w_x, o_b_x, o_lam, o_w_out, final_g):
    bsz, seq, _ = x.shape
    assert ctx.shape[1] == TM and seq % TM == 0 and bsz < SUBLANES
    assert w_mod.shape[0] == 2
    nt = 1 + seq // TM

    mods = _mods(c, c_ctx, w_mod, b_mod).reshape(w_mod.shape[0], SUBLANES, 3, D_MODEL)

    w_in = e_w_in[0]
    w_in_bf16 = w_in.astype(BF16)
    square = (D_MODEL, D_MODEL)
    offs = {}
    off = 0
    for name, size in (("q", GLA_QK), ("k", GLA_QK), ("v", GLA_V), ("ga", GLA_V), ("af", GLA_LOWRANK),
                       ("ab", GLA_LOWRANK), ("cb", SC_WIDTH), ("cc", SC_WIDTH), ("cx", SC_WIDTH),
                       ("gb", SC_WIDTH)):
        offs[name] = (off, off + size)
        off += size
    cols = lambda lo, hi: w_in[:, lo:hi].astype(BF16)
    assert offs["q"][0] == 0 and offs["v"][0] == D_MODEL and offs["ga"] == (2 * D_MODEL, 3 * D_MODEL)
    wqk, wv, wga = (_Window(w_in_bf16, square, (0, i)) for i in range(3))
    wa = jnp.pad(w_in[:, offs["af"][0]:offs["ab"][1]], ((0, 0), (0, LANES - 2 * GLA_LOWRANK))).astype(BF16)
    b2 = e_b_a2[0].reshape(1, 2 * GLA_QK)
    mats = (wqk, wv, wga, wa, cols(*offs["cb"]), cols(*offs["cc"]), cols(*offs["cx"]), cols(*offs["gb"]))
    (o_f, qi_b, ki_b, ke_b, dec_b, v, sga, y) = _even_in(
        ctx, x, mods, (norm_g[0:1], b2, e_conv_w[0], e_w_a2[0]), mats, nt)
    w_out = e_w_out[0]
    xx = _gla_bwd(qi_b, ki_b, ke_b, v, dec_b, o_f, sga, y, ctx, x, mods, e_gla_g[0:1],
                  _Window(w_out, square, (0, 0)), _Window(w_out, square, (1, 0)), nt)

    w_in = o_w_in[0]
    half = (D_MODEL, RG_WIDTH)
    wide = lambda a: a.reshape(1, RG_WIDTH)
    xr, h_f, hn = _odd_fwd(xx, mods, norm_g[1:2], _Window(w_in, half, (0, 0)), o_conv_w[0, 0],
                           wide(o_conv_b[0, 0]), o_w_a[0, 0], o_w_x[0, 0],
                           wide(o_b_a[0, 0]), wide(o_b_x[0, 0]), wide(o_lam[0, 0]), nt)
    return _odd_bwd(xx, mods, hn, xr, h_f, _Window(w_in, half, (0, 1)), o_conv_w[0, 1],
                    wide(o_conv_b[0, 1]), o_w_a[0, 1], o_w_x[0, 1],
                    wide(o_b_a[0, 1]), wide(o_b_x[0, 1]), wide(o_lam[0, 1]), o_w_out[0],
                    final_g.reshape(1, D_MODEL), nt)
```

```python
import jax
import jax.numpy as jnp
from jax import lax
from jax.experimental import pallas as pl
from jax.experimental.pallas import tpu as pltpu

F32 = jnp.float32
BF16 = jnp.bfloat16

D_MODEL = 1024
EPS = 1e-6
GLA_HEADS = 4
GLA_DK = 128
GLA_DV = 256
GLA_QK = GLA_HEADS * GLA_DK
GLA_V = GLA_HEADS * GLA_DV
GLA_LOWRANK = 16
GLA_TAU = 16.0
GLA_CHUNK = 64
GRID_W = 64
SC_WIDTH = D_MODEL
RG_WIDTH = 2 * D_MODEL
RG_BLOCKS = 16
RG_BLOCK_W = 128
RG_C = 8.0
RG_CONV = 4

TM = 256
CHUNKS = TM // GLA_CHUNK
GLA_PAIR = 2
RG_PAIR = 4
EVEN_PAIR = 2
BWD_PAIR = 2
WEIGHT_LOAD_ROWS = 256
SUBLANES = 8
LANES = 128
VMEM_LIMIT = 56 * 1024 * 1024
SEG_LEN = TM // SUBLANES
SEG_PITCH = SEG_LEN + SUBLANES
HALO_ROWS = (RG_CONV - 1) * SUBLANES
RG_CHUNK_W = 256
ROW_DECAY, ROW_BIAS_A, ROW_BIAS_X, ROW_CONV_B, ROW_CONV_W = 0, 1, 2, 3, 4
RG_ROWS = ROW_CONV_W + RG_CONV
LOG2_E = 1.4426950408889634
F32_TINY = 1.1754943508222875e-38


def _silu(x):
    return x * jax.nn.sigmoid(x)


def _log_sigmoid(z):
    return jnp.minimum(z, 0.0) - jnp.log(1.0 + jnp.exp(-jnp.abs(z)))


def _softplus(z):
    return jnp.maximum(z, 0.0) + jnp.log1p(jnp.exp(-jnp.abs(z)))


def _dot(a, b):
    return jnp.dot(a, b, preferred_element_type=F32)


def _dot_nt(a, b):
    return lax.dot_general(a, b, (((1,), (1,)), ((), ())), preferred_element_type=F32)


def _dot_tn(a, b):
    return lax.dot_general(a, b, (((0,), (0,)), ((), ())), preferred_element_type=F32)


def _modulated_norm(x, g, shift, scale):
    ms = jnp.mean(x * x, axis=-1, keepdims=True)
    return (x * lax.rsqrt(ms + EPS) * g) * (1.0 + scale) + shift


def _params(n_axes):
    return pltpu.CompilerParams(
        dimension_semantics=("arbitrary",) * n_axes,
        vmem_limit_bytes=VMEM_LIMIT)


def _first_step():
    return (pl.program_id(0) == 0) & (pl.program_id(1) == 0)


class _Window:
    def __init__(self, array, block_shape, block_index):
        self.array, self.shape, self.index = array, tuple(block_shape), tuple(block_index)


def _operand(w):
    return w.array if isinstance(w, _Window) else w


def _const_spec(w):
    index = w.index if isinstance(w, _Window) else (0,) * len(w.shape)
    return pl.BlockSpec(w.shape, lambda *_: index, pipeline_mode=pl.Buffered(1))


def _mod_spec(layer, bsz, tile):
    return pl.BlockSpec((None, None, 3, D_MODEL),
                        lambda b, t: (layer, jnp.where(tile(t) == 0, bsz, b), 0, 0))


def _mods_kernel(c_ref, cctx_ref, w_ref, b_ref, o_ref, s_ref):
    bsz = c_ref.shape[0]
    s_ref[...] = jnp.zeros_like(s_ref)
    s_ref[0:bsz, :] = c_ref[...]
    s_ref[bsz:bsz + 1, :] = cctx_ref[...]
    s = _silu(s_ref[...])
    o_ref[0] = jnp.dot(s, w_ref[0], preferred_element_type=F32,
                       precision=lax.Precision.HIGHEST) + b_ref[0]


def _mods(c, c_ctx, w_mod, b_mod):
    depth = w_mod.shape[0]
    nb = 3
    return pl.pallas_call(
        _mods_kernel,
        grid=(depth, nb),
        in_specs=[
            pl.BlockSpec(c.shape, lambda l, j: (0, 0)),
            pl.BlockSpec((1, D_MODEL), lambda l, j: (0, 0)),
            pl.BlockSpec((1, D_MODEL, D_MODEL), lambda l, j: (l, 0, j)),
            pl.BlockSpec((1, 1, D_MODEL), lambda l, j: (l, 0, j)),
        ],
        out_specs=pl.BlockSpec((1, SUBLANES, D_MODEL), lambda l, j: (l, 0, j)),
        out_shape=jax.ShapeDtypeStruct((depth, SUBLANES, 3 * D_MODEL), F32),
        scratch_shapes=[pltpu.VMEM((SUBLANES, D_MODEL), F32)],
        compiler_params=_params(2),
        name="mods",
    )(c, c_ctx.reshape(1, D_MODEL), w_mod, b_mod.reshape(depth, 1, 3 * D_MODEL))


def _chunk_rows(x, offset):
    return jnp.concatenate(
        [jnp.broadcast_to(x[c * GLA_CHUNK + offset:c * GLA_CHUNK + offset + 1, :], (GLA_CHUNK, x.shape[1]))
         for c in range(x.shape[0] // GLA_CHUNK)], axis=0)


def _even_in_kernel(ctx_ref, x_ref, modc_ref, modl_ref, g_ref, b2_ref, cw_ref, wa2_ref,
                    wqk_ref, wv_ref, wga_ref, wa_ref, wcb_ref, wcc_ref, wcx_ref, wgb_ref,
                    of_ref, qib_ref, kib_ref, keb_ref, decb_ref, v_ref, sga_ref, y_ref,
                    lower_ref, upper_ref, w2_ref, qif_ref, kif_ref, kef_ref, decf_ref, st_ref):
    t = pl.program_id(1)
    samples = range(EVEN_PAIR)
    rows_of = lambda r: slice(r * TM, (r + 1) * TM)
    split = lambda val: val.reshape(EVEN_PAIR, TM, val.shape[-1])

    @pl.when(t == 0)
    def _():
        st_ref[...] = jnp.zeros_like(st_ref)

    @pl.when(_first_step())
    def _():
        r = lax.broadcasted_iota(jnp.int32, (TM, TM), 0)
        c = lax.broadcasted_iota(jnp.int32, (TM, TM), 1)
        same = (r // GLA_CHUNK) == (c // GLA_CHUNK)
        lower_ref[...] = jnp.where(same & (c <= r), 1.0, 0.0).astype(BF16)
        upper_ref[...] = jnp.where(same & (c >= r), 1.0, 0.0).astype(BF16)
        w2_ref[...] = jnp.zeros_like(w2_ref)
        w2_ref[0:GLA_LOWRANK, 0:GLA_QK] = wa2_ref[0].astype(BF16)
        w2_ref[GLA_LOWRANK:2 * GLA_LOWRANK, GLA_QK:] = wa2_ref[1].astype(BF16)

    normed = []
    for r in samples:
        xt = jnp.where(t == 0, ctx_ref[r], x_ref[r])
        shift = jnp.where(t == 0, modc_ref[0:1, :], modl_ref[r, 0:1, :])
        scale = jnp.where(t == 0, modc_ref[1:2, :], modl_ref[r, 1:2, :])
        normed.append(_modulated_norm(xt, g_ref[...], shift, scale).astype(BF16))
    hb = jnp.concatenate(normed, axis=0)

    a_lr = _dot(hb, wa_ref[...])
    z = _dot(a_lr.astype(BF16), w2_ref[...]) + b2_ref[...]
    qk = _dot(hb, wqk_ref[...])
    v_ref[...] = split(_dot(hb, wv_ref[...]).astype(BF16))

    lg = _log_sigmoid(z) * (1.0 / GLA_TAU)
    lgb = lg.astype(BF16)
    cs_f = jnp.concatenate([_dot(lower_ref[...], lgb[rows_of(r), :GLA_QK]) for r in samples], axis=0)
    cs_b = jnp.concatenate([_dot(upper_ref[...], lgb[rows_of(r), GLA_QK:]) for r in samples], axis=0)
    tot_f = _chunk_rows(cs_f, GLA_CHUNK - 1)
    tot_b = _chunk_rows(cs_b, 0)
    c_c = _dot(hb, wcc_ref[...])

    q = qk[:, :GLA_QK] * (GLA_DK ** -0.5)
    k = qk[:, GLA_QK:]
    qif_ref[...] = split((q * jnp.exp(cs_f)).astype(BF16))
    kif_ref[...] = split((k * jnp.exp(-cs_f)).astype(BF16))
    kef_ref[...] = split((k * jnp.exp(tot_f - cs_f)).astype(BF16))
    qib_ref[...] = split((q * jnp.exp(cs_b)).astype(BF16))
    kib_ref[...] = split((k * jnp.exp(-cs_b)).astype(BF16))
    keb_ref[...] = split((k * jnp.exp(tot_b - cs_b)).astype(BF16))
    for r in samples:
        for c in range(CHUNKS):
            row = r * TM + c * GLA_CHUNK
            decf_ref[r, c:c + 1, :] = jnp.exp(tot_f[row:row + 1, :])
            decb_ref[r, c:c + 1, :] = jnp.exp(tot_b[row:row + 1, :])

    def emit_to(r):
        def emit(rows, vcols, o):
            of_ref[r, rows, vcols] = o.astype(BF16)
        return emit

    mask = _gla_mask(reverse=False)
    open_head = lambda hd: [_gla_head_open(hd, qif_ref.at[r], kif_ref.at[r], kef_ref.at[r], v_ref.at[r], mask)
                            for r in samples]

    def close_head(hd, opened):
        for r in samples:
            _gla_head_close(hd, opened[r], qif_ref.at[r], v_ref.at[r], decf_ref.at[r], st_ref.at[r],
                            emit_to(r), reverse=False)

    opened = open_head(0)
    zz = c_c * _dot(hb, wcx_ref[...])
    close_head(0, opened)
    opened = open_head(1)
    c_b = _dot(hb, wcb_ref[...])
    close_head(1, opened)
    opened = open_head(2)
    g_b = _dot(hb, wgb_ref[...])
    close_head(2, opened)
    opened = open_head(3)
    sga_ref[...] = split(_silu(_dot(hb, wga_ref[...])).astype(BF16))
    close_head(3, opened)

    step_rows = EVEN_PAIR * TM
    pos = lax.broadcasted_iota(jnp.int32, (step_rows, SC_WIDTH), 0)
    row_len = jnp.where(t == 0, TM, GRID_W)
    in_row = pos & (row_len - 1)
    z_prev = jnp.where(in_row != 0, pltpu.roll(zz, 1, 0), 0.0)
    z_next = jnp.where(in_row != row_len - 1, pltpu.roll(zz, step_rows - 1, 0), 0.0)
    zc = cw_ref[0:1, :] * z_prev + cw_ref[1:2, :] * zz + cw_ref[2:3, :] * z_next
    y_ref[...] = split((c_b * zc * _silu(g_b)).astype(BF16))


def _even_in(ctx, x, mods, rows, mats, nt):
    bsz = x.shape[0]
    assert bsz % EVEN_PAIR == 0
    tok = lambda n: pl.BlockSpec((EVEN_PAIR, TM, n), lambda p, t: (p, t, 0))
    dec = pl.BlockSpec((EVEN_PAIR, None, CHUNKS, GLA_QK), lambda p, t: (p, t, 0, 0))
    tok_shape = lambda n: jax.ShapeDtypeStruct((bsz, nt * TM, n), BF16)
    dec_shape = jax.ShapeDtypeStruct((bsz, nt, CHUNKS, GLA_QK), F32)
    consts = tuple(rows) + tuple(mats)
    return pl.pallas_call(
        _even_in_kernel,
        grid=(bsz // EVEN_PAIR, nt),
        in_specs=[
            pl.BlockSpec((EVEN_PAIR, TM, D_MODEL), lambda p, t: (p, 0, 0)),
            pl.BlockSpec((EVEN_PAIR, TM, D_MODEL), lambda p, t: (p, jnp.maximum(t - 1, 0), 0)),
            pl.BlockSpec((None, None, 3, D_MODEL), lambda p, t: (0, bsz, 0, 0)),
            pl.BlockSpec((None, EVEN_PAIR, 3, D_MODEL), lambda p, t: (0, p, 0, 0)),
        ] + [_const_spec(w) for w in consts],
        out_specs=[tok(GLA_V), tok(GLA_QK), tok(GLA_QK), tok(GLA_QK), dec,
                   tok(GLA_V), tok(GLA_V), tok(SC_WIDTH)],
        out_shape=[tok_shape(GLA_V), tok_shape(GLA_QK), tok_shape(GLA_QK), tok_shape(GLA_QK), dec_shape,
                   tok_shape(GLA_V), tok_shape(GLA_V), tok_shape(SC_WIDTH)],
        scratch_shapes=[pltpu.VMEM((TM, TM), BF16)] * 2 + [pltpu.VMEM((LANES, 2 * GLA_QK), BF16)]
        + [pltpu.VMEM((EVEN_PAIR, TM, GLA_QK), BF16)] * 3
        + [pltpu.VMEM((EVEN_PAIR, CHUNKS, GLA_QK), F32),
           pltpu.VMEM((EVEN_PAIR, GLA_HEADS, GLA_DV, GLA_DK), F32)],
        compiler_params=_params(2),
        name="even_in",
    )(ctx, x, mods, mods, *[_operand(w) for w in consts])


def _chunk_slice(ch):
    return slice(ch * GLA_CHUNK, (ch + 1) * GLA_CHUNK)


def _gla_mask(reverse):
    r = lax.broadcasted_iota(jnp.int32, (TM, TM), 0)
    c = lax.broadcasted_iota(jnp.int32, (TM, TM), 1)
    same_chunk = (r // GLA_CHUNK) == (c // GLA_CHUNK)
    return same_chunk & ((c >= r) if reverse else (c <= r))


def _gla_head_open(hd, qi_ref, ki_ref, ke_ref, v_ref, mask):
    kcols = slice(hd * GLA_DK, (hd + 1) * GLA_DK)
    vcols = slice(hd * GLA_DV, (hd + 1) * GLA_DV)
    incr = [_dot_tn(v_ref[_chunk_slice(ch), vcols], ke_ref[_chunk_slice(ch), kcols]) for ch in range(CHUNKS)]
    scores = jnp.where(mask, _dot_nt(qi_ref[:, kcols], ki_ref[:, kcols]), 0.0).astype(BF16)
    return incr, scores


def _gla_head_close(hd, opened, qi_ref, v_ref, dec_ref, st_ref, emit, reverse):
    incr, scores = opened
    kcols = slice(hd * GLA_DK, (hd + 1) * GLA_DK)
    vcols = slice(hd * GLA_DV, (hd + 1) * GLA_DV)
    state = st_ref[hd]
    before = {}
    for ch in (range(CHUNKS - 1, -1, -1) if reverse else range(CHUNKS)):
        before[ch] = state.astype(BF16)
        state = dec_ref[ch:ch + 1, kcols] * state + incr[ch]
    st_ref[hd] = state
    intra = _dot(scores, v_ref[:, vcols])
    for ch in range(CHUNKS):
        rows = _chunk_slice(ch)
        emit(rows, vcols, intra[rows, :] + _dot_nt(qi_ref[rows, kcols], before[ch]))


def _gla_tile(qi_ref, ki_ref, ke_ref, v_ref, dec_ref, st_ref, emit, reverse):
    mask = _gla_mask(reverse)
    opened = [_gla_head_open(hd, qi_ref, ki_ref, ke_ref, v_ref, mask) for hd in range(GLA_HEADS)]
    for hd in range(GLA_HEADS):
        _gla_head_close(hd, opened[hd], qi_ref, v_ref, dec_ref, st_ref, emit, reverse)


def _gla_bwd_kernel(qi_ref, ki_ref, ke_ref, v_ref, dec_ref, of_ref, sga_ref, y_ref, ctx_ref, x_ref,
                    modc_ref, modl_ref, gg_ref, wo_in, wy_in, out_ref, st_ref, ob_ref, wo_ref, wy_ref, stage_ref):
    j = pl.program_id(1)
    samples = range(GLA_PAIR)

    @pl.when(_first_step())
    def _():
        wo_ref[...] = wo_in[...].astype(BF16)
        wy_ref[...] = wy_in[...].astype(BF16)

    @pl.when(j == 0)
    def _():
        st_ref[...] = jnp.zeros_like(st_ref)

    proj_y = _dot(y_ref[...].reshape(GLA_PAIR * TM, SC_WIDTH), wy_ref[...])

    def emit_to(r):
        def emit(rows, vcols, o):
            ob_ref[r, rows, vcols] = o
        return emit

    mask = _gla_mask(reverse=True)
    opened = [[_gla_head_open(hd, qi_ref.at[r], ki_ref.at[r], ke_ref.at[r], v_ref.at[r], mask)
               for hd in range(GLA_HEADS)] for r in samples]
    for hd in range(GLA_HEADS):
        for r in samples:
            _gla_head_close(hd, opened[r][hd], qi_ref.at[r], v_ref.at[r], dec_ref.at[r], st_ref.at[r],
                            emit_to(r), reverse=True)

    inner = []
    for r in samples:
        for hd in range(GLA_HEADS):
            vcols = slice(hd * GLA_DV, (hd + 1) * GLA_DV)
            o = of_ref[r, :, vcols].astype(F32) + ob_ref[r, :, vcols]
            ms = jnp.mean(o * o, axis=-1, keepdims=True)
            on = o * lax.rsqrt(ms + EPS) * gg_ref[...]
            inner.append((on * sga_ref[r, :, vcols].astype(F32)).astype(BF16))
    inner = jnp.concatenate([jnp.concatenate(inner[r * GLA_HEADS:(r + 1) * GLA_HEADS], axis=-1)
                             for r in samples], axis=0)
    proj = _dot(inner, wo_ref[...]) + proj_y
    for r in samples:
        gate = jnp.where(j == 0, modc_ref[2:3, :], modl_ref[r, 2:3, :])
        ob_ref[r] = jnp.where(j == 0, ctx_ref[r], x_ref[r]) + gate * proj[r * TM:(r + 1) * TM, :]
        out_ref[r] = _to_segment_major(ob_ref.at[r], stage_ref, D_MODEL)


def _bwd_tile(nt):
    return lambda j: jnp.where(j == 0, 0, nt - j)


def _gla_bwd(qi, ki, ke, v, dec, o_f, sga, y, ctx, x, mods, gg, wo, wy, nt):
    assert GLA_V == D_MODEL
    bsz = qi.shape[0]
    assert bsz % GLA_PAIR == 0
    tile = _bwd_tile(nt)
    tok = lambda n: pl.BlockSpec((GLA_PAIR, TM, n), lambda p, j: (p, tile(j), 0))
    latent_block = lambda p, j: (p, jnp.where(j == 0, nt - 2, tile(j) - 1), 0)
    return pl.pallas_call(
        _gla_bwd_kernel,
        grid=(bsz // GLA_PAIR, nt),
        in_specs=[tok(GLA_QK), tok(GLA_QK), tok(GLA_QK), tok(GLA_V),
                  pl.BlockSpec((GLA_PAIR, None, CHUNKS, GLA_QK), lambda p, j: (p, tile(j), 0, 0)),
                  tok(GLA_V), tok(GLA_V), tok(SC_WIDTH),
                  pl.BlockSpec((GLA_PAIR, TM, D_MODEL), lambda p, j: (p, 0, 0)),
                  pl.BlockSpec((GLA_PAIR, TM, D_MODEL), latent_block),
                  pl.BlockSpec((None, None, 3, D_MODEL), lambda p, j: (0, bsz, 0, 0)),
                  pl.BlockSpec((None, GLA_PAIR, 3, D_MODEL), lambda p, j: (0, p, 0, 0)),
                  _const_spec(gg), _const_spec(wo), _const_spec(wy)],
        out_specs=tok(D_MODEL),
        out_shape=jax.ShapeDtypeStruct((bsz, nt * TM, D_MODEL), F32),
        scratch_shapes=[pltpu.VMEM((GLA_PAIR, GLA_HEADS, GLA_DV, GLA_DK), F32),
                        pltpu.VMEM((GLA_PAIR, TM, GLA_V), F32),
                        pltpu.VMEM(wo.shape, BF16),
                        pltpu.VMEM(wy.shape, BF16),
                        pltpu.VMEM((D_MODEL // LANES, SUBLANES * SEG_PITCH, LANES), F32)],
        compiler_params=_params(2),
        name="gla_bwd",
    )(qi, ki, ke, v, dec, o_f, sga, y, ctx, x, mods, mods, gg, _operand(wo), _operand(wy))


def _to_segment_major(src_ref, stage_ref, width):
    slabs = width // LANES
    for n in range(slabs):
        for s in range(SUBLANES):
            stage_ref[n, s * SEG_PITCH:s * SEG_PITCH + SEG_LEN, :] = (
                src_ref[s * SEG_LEN:(s + 1) * SEG_LEN, n * LANES:(n + 1) * LANES])
    groups = [
        jnp.concatenate([stage_ref[n, pl.ds(j, SUBLANES, stride=SEG_PITCH), :] for n in range(slabs)], axis=1)
        for j in range(SEG_LEN)]
    return jnp.concatenate(groups, axis=0)


def _from_segment_major(val, stage_ref, dst_ref, width):
    slabs = width // LANES
    for j in range(SEG_LEN):
        for n in range(slabs):
            stage_ref[n, pl.ds(j, SUBLANES, stride=SEG_PITCH), :] = (
                val[j * SUBLANES:(j + 1) * SUBLANES, n * LANES:(n + 1) * LANES])
    for n in range(slabs):
        for s in range(SUBLANES):
            dst_ref[s * SEG_LEN:(s + 1) * SEG_LEN, n * LANES:(n + 1) * LANES] = (
                stage_ref[n, s * SEG_PITCH:s * SEG_PITCH + SEG_LEN, :])


def _rg_conv(xr, halo_ref, rows_ref, cols, reverse):
    row = lax.broadcasted_iota(jnp.int32, (SUBLANES, xr.shape[1]), 0)
    edge = []
    for g in range(RG_CONV - 1):
        grp = slice(g * SUBLANES, (g + 1) * SUBLANES)
        if reverse:
            edge.append(jnp.where(row == SUBLANES - 1, pltpu.roll(halo_ref[grp, cols], SUBLANES - 1, 0),
                                  pltpu.roll(xr[grp, :], SUBLANES - 1, 0)))
        else:
            cur = xr[TM - (RG_CONV - 1 - g) * SUBLANES:TM - (RG_CONV - 2 - g) * SUBLANES, :]
            edge.append(jnp.where(row == 0, pltpu.roll(halo_ref[grp, cols], 1, 0), pltpu.roll(cur, 1, 0)))
    tap_w = lambda j: rows_ref[ROW_CONV_W + j:ROW_CONV_W + j + 1, cols]
    xc = rows_ref[ROW_CONV_B:ROW_CONV_B + 1, cols] + tap_w(RG_CONV - 1) * xr
    for k in range(1, RG_CONV):
        if reverse:
            tap = jnp.concatenate([xr[k * SUBLANES:, :]] + edge[:k], axis=0)
        else:
            tap = jnp.concatenate(edge[RG_CONV - 1 - k:] + [xr[:TM - k * SUBLANES, :]], axis=0)
        xc = xc + tap_w(RG_CONV - 1 - k) * tap
    halo_ref[:, cols] = xr[:HALO_ROWS, :] if reverse else xr[TM - HALO_ROWS:, :]
    return xc


def _rg_setup(lam_ref, ba_ref, bx_ref, cb_ref, cw_ref, wa_ref, wx_ref, rows_ref, wg_s, copies):
    wg_s[:, :, :RG_BLOCK_W] = wa_ref[...].astype(BF16)
    wg_s[:, :, RG_BLOCK_W:] = wx_ref[...].astype(BF16)
    rows_ref[ROW_DECAY:ROW_DECAY + 1, :] = (-0.5 * RG_C * LOG2_E) * _softplus(-lam_ref[...])
    rows_ref[ROW_BIAS_A:ROW_BIAS_A + 1, :] = 0.5 * ba_ref[...]
    rows_ref[ROW_BIAS_X:ROW_BIAS_X + 1, :] = 0.5 * bx_ref[...]
    rows_ref[ROW_CONV_B:ROW_CONV_B + 1, :] = 0.5 * cb_ref[...]
    rows_ref[ROW_CONV_W:ROW_CONV_W + RG_CONV, :] = 0.5 * cw_ref[...]
    for src_ref, dst_ref in copies:
        dst_ref[...] = src_ref[...].astype(BF16)


def _rg_gates_scan(xc, first_block, wg_ref, rows_ref, carry_ref, reverse):
    row = lax.broadcasted_iota(jnp.int32, (SUBLANES, RG_BLOCK_W), 0)
    shift = SUBLANES - 1 if reverse else 1
    steps = range(SEG_LEN - 1, -1, -1) if reverse else range(SEG_LEN)
    out = []
    for i in range(xc.shape[1] // RG_BLOCK_W):
        n = first_block + i
        cols = slice(n * RG_BLOCK_W, (n + 1) * RG_BLOCK_W)
        xb = xc[:, i * RG_BLOCK_W:(i + 1) * RG_BLOCK_W]
        gates = _dot(xb.astype(BF16), wg_ref[n])
        th_r = jnp.tanh(gates[:, :RG_BLOCK_W] + rows_ref[ROW_BIAS_A:ROW_BIAS_A + 1, cols])
        th_i = jnp.tanh(gates[:, RG_BLOCK_W:] + rows_ref[ROW_BIAS_X:ROW_BIAS_X + 1, cols])
        half_decay = rows_ref[ROW_DECAY:ROW_DECAY + 1, cols]
        a = jnp.exp2(th_r * half_decay + half_decay)
        y = 1.0 - a * a
        root = y * lax.rsqrt(jnp.maximum(y, F32_TINY))
        u = root * ((th_i + 1.0) * xb)

        h = jnp.zeros((SUBLANES, RG_BLOCK_W), F32)
        prod = jnp.ones((SUBLANES, RG_BLOCK_W), F32)
        local, prods = {}, {}
        for j in steps:
            grp = slice(j * SUBLANES, (j + 1) * SUBLANES)
            h = a[grp, :] * h + u[grp, :]
            prod = prod * a[grp, :]
            local[j], prods[j] = h, prod

        entering = carry_ref[:, cols]
        for s in range(SUBLANES - 1):
            nxt = pltpu.roll(prod * entering + h, shift, 0)
            target = SUBLANES - 2 - s if reverse else s + 1
            entering = jnp.where(row == target, nxt, entering)
        carry_ref[:, cols] = pltpu.roll(prod * entering + h, shift, 0)

        out.append(jnp.concatenate([local[j] + prods[j] * entering for j in range(SEG_LEN)], axis=0))
    return jnp.concatenate(out, axis=1)


def _odd_fwd_kernel(xx_ref, modc_ref, modl_ref, g_ref, wx_ref, cw_ref, cb_ref, ga_ref, gx_ref, ba_ref, bx_ref,
                    lam_ref, xr_ref, hf_ref, hn_ref, halo_ref, carry_ref, rows_ref, wx_s, wg_s):
    t = pl.program_id(1)

    @pl.when(_first_step())
    def _():
        _rg_setup(lam_ref, ba_ref, bx_ref, cb_ref, cw_ref, ga_ref, gx_ref, rows_ref, wg_s, ((wx_ref, wx_s),))

    @pl.when(t == 0)
    def _():
        carry_ref[...] = jnp.zeros_like(carry_ref)

    @pl.when(t <= 1)
    def _():
        halo_ref[...] = jnp.zeros_like(halo_ref)

    samples = range(RG_PAIR)
    for r in samples:
        shift = jnp.where(t == 0, modc_ref[0:1, :], modl_ref[r, 0:1, :])
        scale = jnp.where(t == 0, modc_ref[1:2, :], modl_ref[r, 1:2, :])
        hn_ref[r] = _modulated_norm(xx_ref[r], g_ref[...], shift, scale).astype(BF16)
    hb = hn_ref[...].reshape(RG_PAIR * TM, D_MODEL)
    chunks = RG_WIDTH // RG_CHUNK_W
    project = lambda m: _dot(hb, wx_s[:, m * RG_CHUNK_W:(m + 1) * RG_CHUNK_W])
    xr_next = project(0)
    for m in range(chunks):
        cols = slice(m * RG_CHUNK_W, (m + 1) * RG_CHUNK_W)
        xr_pair = xr_next
        if m + 1 < chunks:
            xr_next = project(m + 1)
        for r in samples:
            xr = xr_pair[r * TM:(r + 1) * TM, :]
            xr_ref[r, :, cols] = xr.astype(BF16)
            xc = _rg_conv(xr, halo_ref.at[r], rows_ref, cols, reverse=False)
            hf_ref[r, :, cols] = _rg_gates_scan(xc, m * (RG_CHUNK_W // RG_BLOCK_W), wg_s, rows_ref,
                                                carry_ref.at[r], reverse=False).astype(BF16)


def _odd_fwd(xx, mods, g, wx, cw, cb, ga, gx, ba, bx, lam, nt):
    bsz = xx.shape[0]
    assert bsz % RG_PAIR == 0
    tok = lambda n: pl.BlockSpec((RG_PAIR, TM, n), lambda p, t: (p, t, 0))
    consts = (g, wx, cw, cb, ga, gx, ba, bx, lam)
    gate_scratch = pltpu.VMEM((RG_BLOCKS, RG_BLOCK_W, 2 * RG_BLOCK_W), BF16)
    wide = jax.ShapeDtypeStruct((bsz, nt * TM, RG_WIDTH), BF16)
    return pl.pallas_call(
        _odd_fwd_kernel,
        grid=(bsz // RG_PAIR, nt),
        in_specs=[tok(D_MODEL),
                  pl.BlockSpec((None, None, 3, D_MODEL), lambda p, t: (1, bsz, 0, 0)),
                  pl.BlockSpec((None, RG_PAIR, 3, D_MODEL), lambda p, t: (1, p, 0, 0))]
        + [_const_spec(w) for w in consts],
        out_specs=[tok(RG_WIDTH), tok(RG_WIDTH), tok(D_MODEL)],
        out_shape=[wide, wide, jax.ShapeDtypeStruct((bsz, nt * TM, D_MODEL), BF16)],
        scratch_shapes=[pltpu.VMEM((RG_PAIR, HALO_ROWS, RG_WIDTH), F32),
                        pltpu.VMEM((RG_PAIR, SUBLANES, RG_WIDTH), F32),
                        pltpu.VMEM((RG_ROWS, RG_WIDTH), F32),
                        pltpu.VMEM(wx.shape, BF16),
                        gate_scratch],
        compiler_params=_params(2),
        name="odd_fwd",
    )(xx, mods, mods, *[_operand(w) for w in consts])


def _odd_bwd_kernel(xx_ref, mod_ref, hn_ref, xr_ref, hf_ref, win_hbm, wout_hbm, cw_ref, cb_ref, ga_ref, gx_ref,
                    ba_ref, bx_ref, lam_ref, fg_ref, out_ref,
                    stage_ref, halo_ref, yy_ref, carry_ref, rows_ref, wgate_s, wg_s, wout_s, wload_ref):
    j = pl.program_id(1)
    samples = range(BWD_PAIR)

    @pl.when(_first_step())
    def _():
        _rg_setup(lam_ref, ba_ref, bx_ref, cb_ref, cw_ref, ga_ref, gx_ref, rows_ref, wg_s, ())
        rows_per_load = wload_ref.shape[0]
        for i in range(D_MODEL // rows_per_load):
            rows = slice(i * rows_per_load, (i + 1) * rows_per_load)
            pltpu.sync_copy(win_hbm.at[rows, pl.ds(RG_WIDTH, RG_WIDTH)], wload_ref)
            wgate_s[rows, :] = wload_ref[...].astype(BF16)
        for i in range(RG_WIDTH // rows_per_load):
            rows = slice(i * rows_per_load, (i + 1) * rows_per_load)
            pltpu.sync_copy(wout_hbm.at[rows, :], wload_ref.at[:, pl.ds(0, D_MODEL)])
            wout_s[rows, :] = wload_ref[:, 0:D_MODEL].astype(BF16)

    @pl.when(j == 0)
    def _():
        carry_ref[...] = jnp.zeros_like(carry_ref)

    @pl.when(j <= 1)
    def _():
        halo_ref[...] = jnp.zeros_like(halo_ref)

    def tile(with_output):
        hb = hn_ref[...].reshape(BWD_PAIR * TM, D_MODEL)
        for m in range(RG_WIDTH // RG_CHUNK_W):
            cols = slice(m * RG_CHUNK_W, (m + 1) * RG_CHUNK_W)
            if with_output:
                gate = _dot(hb, wgate_s[:, cols])
            for r in samples:
                xc = _rg_conv(xr_ref[r, :, cols].astype(F32), halo_ref.at[r], rows_ref, cols, reverse=True)
                h_bwd = _rg_gates_scan(xc, m * (RG_CHUNK_W // RG_BLOCK_W), wg_s, rows_ref, carry_ref.at[r],
                                       reverse=True)
                if with_output:
                    yy_ref[r, :, cols] = ((hf_ref[r, :, cols].astype(F32) + h_bwd)
                                          * _silu(gate[r * TM:(r + 1) * TM, :])).astype(BF16)
        if with_output:
            proj = _dot(yy_ref[...].reshape(BWD_PAIR * TM, RG_WIDTH), wout_s[...])
            for r in samples:
                xn = xx_ref[r] + mod_ref[r, 2:3, :] * proj[r * TM:(r + 1) * TM, :]
                ms = jnp.mean(xn * xn, axis=-1, keepdims=True)
                _from_segment_major(xn * lax.rsqrt(ms + EPS) * fg_ref[...], stage_ref, out_ref.at[r], D_MODEL)

    pl.when(j == 0)(lambda: tile(with_output=False))
    pl.when(j > 0)(lambda: tile(with_output=True))


def _odd_bwd(xx, mods, hn, xr, hf, w_in, w_out, cw, cb, ga, gx, ba, bx, lam, fg, nt):
    bsz = xx.shape[0]
    assert bsz % BWD_PAIR == 0 and w_in.shape == (D_MODEL, 2 * RG_WIDTH) and w_out.shape == (RG_WIDTH, D_MODEL)
    tile = _bwd_tile(nt)
    tok = lambda n: pl.BlockSpec((BWD_PAIR, TM, n), lambda p, j: (p, tile(j), 0))
    consts = (cw, cb, ga, gx, ba, bx, lam, fg)
    out_block = lambda p, j: (p, jnp.where(j == 0, nt - 2, nt - 1 - j), 0)
    mod_rows = pl.BlockSpec((None, BWD_PAIR, 3, D_MODEL), lambda p, j: (1, p, 0, 0))
    in_hbm = pl.BlockSpec(memory_space=pl.ANY)
    return pl.pallas_call(
        _odd_bwd_kernel,
        grid=(bsz // BWD_PAIR, nt),
        in_specs=[tok(D_MODEL), mod_rows, tok(D_MODEL), tok(RG_WIDTH), tok(RG_WIDTH), in_hbm, in_hbm]
        + [_const_spec(w) for w in consts],
        out_specs=pl.BlockSpec((BWD_PAIR, TM, D_MODEL), out_block),
        out_shape=jax.ShapeDtypeStruct((bsz, (nt - 1) * TM, D_MODEL), F32),
        scratch_shapes=[pltpu.VMEM((D_MODEL // LANES, SUBLANES * SEG_PITCH, LANES), F32),
                        pltpu.VMEM((BWD_PAIR, HALO_ROWS, RG_WIDTH), F32),
                        pltpu.VMEM((BWD_PAIR, TM, RG_WIDTH), BF16),
                        pltpu.VMEM((BWD_PAIR, SUBLANES, RG_WIDTH), F32),
                        pltpu.VMEM((RG_ROWS, RG_WIDTH), F32),
                        pltpu.VMEM((D_MODEL, RG_WIDTH), BF16),
                        pltpu.VMEM((RG_BLOCKS, RG_BLOCK_W, 2 * RG_BLOCK_W), BF16),
                        pltpu.VMEM((RG_WIDTH, D_MODEL), BF16),
                        pltpu.VMEM((WEIGHT_LOAD_ROWS, RG_WIDTH), F32)],
        compiler_params=_params(2),
        name="odd_bwd",
    )(xx, mods, hn, xr, hf, w_in, w_out, *consts)


def kernel(x, c, ctx, c_ctx, norm_g, w_mod, b_mod, e_w_in, e_w_a2, e_b_a2, e_gla_g, e_conv_w, e_w_out,
           o_w_in, o_conv_w, o_conv_b, o_w_a, o_b_a, o_w_x, o_b_x, o_lam, o_w_out, final_g):
    bsz, seq, _ = x.shape
    assert ctx.shape[1] == TM and seq % TM == 0 and bsz < SUBLANES
    assert w_mod.shape[0] == 2
    nt = 1 + seq // TM

    mods = _mods(c, c_ctx, w_mod, b_mod).reshape(w_mod.shape[0], SUBLANES, 3, D_MODEL)

    w_in = e_w_in[0]
    w_in_bf16 = w_in.astype(BF16)
    square = (D_MODEL, D_MODEL)
    offs = {}
    off = 0
    for name, size in (("q", GLA_QK), ("k", GLA_QK), ("v", GLA_V), ("ga", GLA_V), ("af", GLA_LOWRANK),
                       ("ab", GLA_LOWRANK), ("cb", SC_WIDTH), ("cc", SC_WIDTH), ("cx", SC_WIDTH),
                       ("gb", SC_WIDTH)):
        offs[name] = (off, off + size)
        off += size
    cols = lambda lo, hi: w_in[:, lo:hi].astype(BF16)
    assert offs["q"][0] == 0 and offs["v"][0] == D_MODEL and offs["ga"] == (2 * D_MODEL, 3 * D_MODEL)
    wqk, wv, wga = (_Window(w_in_bf16, square, (0, i)) for i in range(3))
    wa = jnp.pad(w_in[:, offs["af"][0]:offs["ab"][1]], ((0, 0), (0, LANES - 2 * GLA_LOWRANK))).astype(BF16)
    b2 = e_b_a2[0].reshape(1, 2 * GLA_QK)
    mats = (wqk, wv, wga, wa, cols(*offs["cb"]), cols(*offs["cc"]), cols(*offs["cx"]), cols(*offs["gb"]))
    (o_f, qi_b, ki_b, ke_b, dec_b, v, sga, y) = _even_in(
        ctx, x, mods, (norm_g[0:1], b2, e_conv_w[0], e_w_a2[0]), mats, nt)
    w_out = e_w_out[0]
    xx = _gla_bwd(qi_b, ki_b, ke_b, v, dec_b, o_f, sga, y, ctx, x, mods, e_gla_g[0:1],
                  _Window(w_out, square, (0, 0)), _Window(w_out, square, (1, 0)), nt)

    w_in = o_w_in[0]
    half = (D_MODEL, RG_WIDTH)
    wide = lambda a: a.reshape(1, RG_WIDTH)
    xr, h_f, hn = _odd_fwd(xx, mods, norm_g[1:2], _Window(w_in, half, (0, 0)), o_conv_w[0, 0],
                           wide(o_conv_b[0, 0]), o_w_a[0, 0], o_w_x[0, 0],
                           wide(o_b_a[0, 0]), wide(o_b_x[0, 0]), wide(o_lam[0, 0]), nt)
    return _odd_bwd(xx, mods, hn, xr, h_f, w_in, o_w_out[0], o_conv_w[0, 1],
                    wide(o_conv_b[0, 1]), o_w_a[0, 1], o_w_x[0, 1],
                    wide(o_b_a[0, 1]), wide(o_b_x[0, 1]), wide(o_lam[0, 1]),
                    final_g.reshape(1, D_MODEL), nt)
```

```python
import jax
import jax.numpy as jnp
from jax import lax
from jax.experimental import pallas as pl
from jax.experimental.pallas import tpu as pltpu

F32 = jnp.float32
BF16 = jnp.bfloat16

D_MODEL = 1024
EPS = 1e-6
GLA_HEADS = 4
GLA_DK = 128
GLA_DV = 256
GLA_QK = GLA_HEADS * GLA_DK
GLA_V = GLA_HEADS * GLA_DV
GLA_LOWRANK = 16
GLA_TAU = 16.0
GLA_CHUNK = 64
GRID_W = 64
SC_WIDTH = D_MODEL
RG_WIDTH = 2 * D_MODEL
RG_BLOCKS = 16
RG_BLOCK_W = 128
RG_C = 8.0
RG_CONV = 4

TM = 256
CHUNKS = TM // GLA_CHUNK
GLA_PAIR = 2
RG_PAIR = 4
EVEN_PAIR = 2
SUBLANES = 8
LANES = 128
VMEM_LIMIT = 56 * 1024 * 1024
SEG_LEN = TM // SUBLANES
SEG_PITCH = SEG_LEN + SUBLANES
HALO_ROWS = (RG_CONV - 1) * SUBLANES
RG_CHUNK_W = 256
ROW_DECAY, ROW_BIAS_A, ROW_BIAS_X, ROW_CONV_B, ROW_CONV_W = 0, 1, 2, 3, 4
RG_ROWS = ROW_CONV_W + RG_CONV
LOG2_E = 1.4426950408889634
F32_TINY = 1.1754943508222875e-38


def _silu(x):
    return x * jax.nn.sigmoid(x)


def _log_sigmoid(z):
    return jnp.minimum(z, 0.0) - jnp.log(1.0 + jnp.exp(-jnp.abs(z)))


def _softplus(z):
    return jnp.maximum(z, 0.0) + jnp.log1p(jnp.exp(-jnp.abs(z)))


def _dot(a, b):
    return jnp.dot(a, b, preferred_element_type=F32)


def _dot_nt(a, b):
    return lax.dot_general(a, b, (((1,), (1,)), ((), ())), preferred_element_type=F32)


def _dot_tn(a, b):
    return lax.dot_general(a, b, (((0,), (0,)), ((), ())), preferred_element_type=F32)


def _modulated_norm(x, g, shift, scale):
    ms = jnp.mean(x * x, axis=-1, keepdims=True)
    return (x * lax.rsqrt(ms + EPS) * g) * (1.0 + scale) + shift


def _params(n_axes):
    return pltpu.CompilerParams(
        dimension_semantics=("arbitrary",) * n_axes,
        vmem_limit_bytes=VMEM_LIMIT)


def _first_step():
    return (pl.program_id(0) == 0) & (pl.program_id(1) == 0)


class _Window:
    def __init__(self, array, block_shape, block_index):
        self.array, self.shape, self.index = array, tuple(block_shape), tuple(block_index)


def _operand(w):
    return w.array if isinstance(w, _Window) else w


def _const_spec(w):
    index = w.index if isinstance(w, _Window) else (0,) * len(w.shape)
    return pl.BlockSpec(w.shape, lambda *_: index, pipeline_mode=pl.Buffered(1))


def _mod_spec(layer, bsz, tile):
    return pl.BlockSpec((None, None, 3, D_MODEL),
                        lambda b, t: (layer, jnp.where(tile(t) == 0, bsz, b), 0, 0))


def _bf16_terms(val):
    hi = val.astype(BF16)
    return hi, (val - hi.astype(F32)).astype(BF16)


def _mods_kernel(c_ref, cctx_ref, w_ref, b_ref, o_ref, s_ref):
    bsz = c_ref.shape[0]
    s_ref[...] = jnp.zeros_like(s_ref)
    s_ref[0:bsz, :] = c_ref[...]
    s_ref[bsz:bsz + 1, :] = cctx_ref[...]
    s_terms = jnp.concatenate(_bf16_terms(_silu(s_ref[...])), axis=0)
    w_hi, w_lo = _bf16_terms(w_ref[0])
    by_hi = _dot(s_terms, w_hi)
    by_lo = _dot(s_terms, w_lo)
    o_ref[0] = (by_hi[:SUBLANES] + by_hi[SUBLANES:]) + (by_lo[:SUBLANES] + by_lo[SUBLANES:]) + b_ref[0]


def _mods(c, c_ctx, w_mod, b_mod):
    depth = w_mod.shape[0]
    nb = 3
    return pl.pallas_call(
        _mods_kernel,
        grid=(depth, nb),
        in_specs=[
            pl.BlockSpec(c.shape, lambda l, j: (0, 0)),
            pl.BlockSpec((1, D_MODEL), lambda l, j: (0, 0)),
            pl.BlockSpec((1, D_MODEL, D_MODEL), lambda l, j: (l, 0, j)),
            pl.BlockSpec((1, 1, D_MODEL), lambda l, j: (l, 0, j)),
        ],
        out_specs=pl.BlockSpec((1, SUBLANES, D_MODEL), lambda l, j: (l, 0, j)),
        out_shape=jax.ShapeDtypeStruct((depth, SUBLANES, 3 * D_MODEL), F32),
        scratch_shapes=[pltpu.VMEM((SUBLANES, D_MODEL), F32)],
        compiler_params=_params(2),
        name="mods",
    )(c, c_ctx.reshape(1, D_MODEL), w_mod, b_mod.reshape(depth, 1, 3 * D_MODEL))


def _chunk_rows(x, offset):
    return jnp.concatenate(
        [jnp.broadcast_to(x[c * GLA_CHUNK + offset:c * GLA_CHUNK + offset + 1, :], (GLA_CHUNK, x.shape[1]))
         for c in range(x.shape[0] // GLA_CHUNK)], axis=0)


def _even_in_kernel(ctx_ref, x_ref, modc_ref, modl_ref, g_ref, b2_ref, cw_ref, wa2_ref,
                    wqk_ref, wv_ref, wga_ref, wa_ref, wcb_ref, wcc_ref, wcx_ref, wgb_ref,
                    of_ref, qib_ref, kib_ref, keb_ref, decb_ref, v_ref, sga_ref, y_ref,
                    lower_ref, upper_ref, w2_ref, qif_ref, kif_ref, kef_ref, decf_ref, st_ref):
    t = pl.program_id(1)
    samples = range(EVEN_PAIR)
    rows_of = lambda r: slice(r * TM, (r + 1) * TM)
    split = lambda val: val.reshape(EVEN_PAIR, TM, val.shape[-1])

    @pl.when(t == 0)
    def _():
        st_ref[...] = jnp.zeros_like(st_ref)

    @pl.when(_first_step())
    def _():
        r = lax.broadcasted_iota(jnp.int32, (TM, TM), 0)
        c = lax.broadcasted_iota(jnp.int32, (TM, TM), 1)
        same = (r // GLA_CHUNK) == (c // GLA_CHUNK)
        lower_ref[...] = jnp.where(same & (c <= r), 1.0, 0.0).astype(BF16)
        upper_ref[...] = jnp.where(same & (c >= r), 1.0, 0.0).astype(BF16)
        w2_ref[...] = jnp.zeros_like(w2_ref)
        w2_ref[0:GLA_LOWRANK, 0:GLA_QK] = wa2_ref[0].astype(BF16)
        w2_ref[GLA_LOWRANK:2 * GLA_LOWRANK, GLA_QK:] = wa2_ref[1].astype(BF16)

    normed = []
    for r in samples:
        xt = jnp.where(t == 0, ctx_ref[r], x_ref[r])
        shift = jnp.where(t == 0, modc_ref[0:1, :], modl_ref[r, 0:1, :])
        scale = jnp.where(t == 0, modc_ref[1:2, :], modl_ref[r, 1:2, :])
        normed.append(_modulated_norm(xt, g_ref[...], shift, scale).astype(BF16))
    hb = jnp.concatenate(normed, axis=0)

    a_lr = _dot(hb, wa_ref[...])
    z = _dot(a_lr.astype(BF16), w2_ref[...]) + b2_ref[...]
    qk = _dot(hb, wqk_ref[...])
    v_ref[...] = split(_dot(hb, wv_ref[...]).astype(BF16))

    lg = _log_sigmoid(z) * (1.0 / GLA_TAU)
    lgb = lg.astype(BF16)
    cs_f = jnp.concatenate([_dot(lower_ref[...], lgb[rows_of(r), :GLA_QK]) for r in samples], axis=0)
    cs_b = jnp.concatenate([_dot(upper_ref[...], lgb[rows_of(r), GLA_QK:]) for r in samples], axis=0)
    tot_f = _chunk_rows(cs_f, GLA_CHUNK - 1)
    tot_b = _chunk_rows(cs_b, 0)
    c_c = _dot(hb, wcc_ref[...])

    q = qk[:, :GLA_QK] * (GLA_DK ** -0.5)
    k = qk[:, GLA_QK:]
    qif_ref[...] = split((q * jnp.exp(cs_f)).astype(BF16))
    kif_ref[...] = split((k * jnp.exp(-cs_f)).astype(BF16))
    kef_ref[...] = split((k * jnp.exp(tot_f - cs_f)).astype(BF16))
    qib_ref[...] = split((q * jnp.exp(cs_b)).astype(BF16))
    kib_ref[...] = split((k * jnp.exp(-cs_b)).astype(BF16))
    keb_ref[...] = split((k * jnp.exp(tot_b - cs_b)).astype(BF16))
    for r in samples:
        for c in range(CHUNKS):
            row = r * TM + c * GLA_CHUNK
            decf_ref[r, c:c + 1, :] = jnp.exp(tot_f[row:row + 1, :])
            decb_ref[r, c:c + 1, :] = jnp.exp(tot_b[row:row + 1, :])

    def emit_to(r):
        def emit(rows, vcols, o):
            of_ref[r, rows, vcols] = o.astype(BF16)
        return emit

    mask = _gla_mask(reverse=False)
    open_head = lambda hd: [_gla_head_open(hd, qif_ref.at[r], kif_ref.at[r], kef_ref.at[r], v_ref.at[r], mask)
                            for r in samples]

    def close_head(hd, opened):
        for r in samples:
            _gla_head_close(hd, opened[r], qif_ref.at[r], v_ref.at[r], decf_ref.at[r], st_ref.at[r],
                            emit_to(r), reverse=False)

    opened = open_head(0)
    zz = c_c * _dot(hb, wcx_ref[...])
    close_head(0, opened)
    opened = open_head(1)
    c_b = _dot(hb, wcb_ref[...])
    close_head(1, opened)
    opened = open_head(2)
    g_b = _dot(hb, wgb_ref[...])
    close_head(2, opened)
    opened = open_head(3)
    sga_ref[...] = split(_silu(_dot(hb, wga_ref[...])).astype(BF16))
    close_head(3, opened)

    step_rows = EVEN_PAIR * TM
    pos = lax.broadcasted_iota(jnp.int32, (step_rows, SC_WIDTH), 0)
    row_len = jnp.where(t == 0, TM, GRID_W)
    in_row = pos & (row_len - 1)
    z_prev = jnp.where(in_row != 0, pltpu.roll(zz, 1, 0), 0.0)
    z_next = jnp.where(in_row != row_len - 1, pltpu.roll(zz, step_rows - 1, 0), 0.0)
    zc = cw_ref[0:1, :] * z_prev + cw_ref[1:2, :] * zz + cw_ref[2:3, :] * z_next
    y_ref[...] = split((c_b * zc * _silu(g_b)).astype(BF16))


def _even_in(ctx, x, mods, rows, mats, nt):
    bsz = x.shape[0]
    assert bsz % EVEN_PAIR == 0
    tok = lambda n: pl.BlockSpec((EVEN_PAIR, TM, n), lambda p, t: (p, t, 0))
    dec = pl.BlockSpec((EVEN_PAIR, None, CHUNKS, GLA_QK), lambda p, t: (p, t, 0, 0))
    tok_shape = lambda n: jax.ShapeDtypeStruct((bsz, nt * TM, n), BF16)
    dec_shape = jax.ShapeDtypeStruct((bsz, nt, CHUNKS, GLA_QK), F32)
    consts = tuple(rows) + tuple(mats)
    return pl.pallas_call(
        _even_in_kernel,
        grid=(bsz // EVEN_PAIR, nt),
        in_specs=[
            pl.BlockSpec((EVEN_PAIR, TM, D_MODEL), lambda p, t: (p, 0, 0)),
            pl.BlockSpec((EVEN_PAIR, TM, D_MODEL), lambda p, t: (p, jnp.maximum(t - 1, 0), 0)),
            pl.BlockSpec((None, None, 3, D_MODEL), lambda p, t: (0, bsz, 0, 0)),
            pl.BlockSpec((None, EVEN_PAIR, 3, D_MODEL), lambda p, t: (0, p, 0, 0)),
        ] + [_const_spec(w) for w in consts],
        out_specs=[tok(GLA_V), tok(GLA_QK), tok(GLA_QK), tok(GLA_QK), dec,
                   tok(GLA_V), tok(GLA_V), tok(SC_WIDTH)],
        out_shape=[tok_shape(GLA_V), tok_shape(GLA_QK), tok_shape(GLA_QK), tok_shape(GLA_QK), dec_shape,
                   tok_shape(GLA_V), tok_shape(GLA_V), tok_shape(SC_WIDTH)],
        scratch_shapes=[pltpu.VMEM((TM, TM), BF16)] * 2 + [pltpu.VMEM((LANES, 2 * GLA_QK), BF16)]
        + [pltpu.VMEM((EVEN_PAIR, TM, GLA_QK), BF16)] * 3
        + [pltpu.VMEM((EVEN_PAIR, CHUNKS, GLA_QK), F32),
           pltpu.VMEM((EVEN_PAIR, GLA_HEADS, GLA_DV, GLA_DK), F32)],
        compiler_params=_params(2),
        name="even_in",
    )(ctx, x, mods, mods, *[_operand(w) for w in consts])


def _chunk_slice(ch):
    return slice(ch * GLA_CHUNK, (ch + 1) * GLA_CHUNK)


def _gla_mask(reverse):
    r = lax.broadcasted_iota(jnp.int32, (TM, TM), 0)
    c = lax.broadcasted_iota(jnp.int32, (TM, TM), 1)
    same_chunk = (r // GLA_CHUNK) == (c // GLA_CHUNK)
    return same_chunk & ((c >= r) if reverse else (c <= r))


def _gla_head_open(hd, qi_ref, ki_ref, ke_ref, v_ref, mask):
    kcols = slice(hd * GLA_DK, (hd + 1) * GLA_DK)
    vcols = slice(hd * GLA_DV, (hd + 1) * GLA_DV)
    incr = [_dot_tn(v_ref[_chunk_slice(ch), vcols], ke_ref[_chunk_slice(ch), kcols]) for ch in range(CHUNKS)]
    scores = jnp.where(mask, _dot_nt(qi_ref[:, kcols], ki_ref[:, kcols]), 0.0).astype(BF16)
    return incr, scores


def _gla_head_close(hd, opened, qi_ref, v_ref, dec_ref, st_ref, emit, reverse):
    incr, scores = opened
    kcols = slice(hd * GLA_DK, (hd + 1) * GLA_DK)
    vcols = slice(hd * GLA_DV, (hd + 1) * GLA_DV)
    state = st_ref[hd]
    before = {}
    for ch in (range(CHUNKS - 1, -1, -1) if reverse else range(CHUNKS)):
        before[ch] = state.astype(BF16)
        state = dec_ref[ch:ch + 1, kcols] * state + incr[ch]
    st_ref[hd] = state
    intra = _dot(scores, v_ref[:, vcols])
    for ch in range(CHUNKS):
        rows = _chunk_slice(ch)
        emit(rows, vcols, intra[rows, :] + _dot_nt(qi_ref[rows, kcols], before[ch]))


def _gla_tile(qi_ref, ki_ref, ke_ref, v_ref, dec_ref, st_ref, emit, reverse):
    mask = _gla_mask(reverse)
    opened = [_gla_head_open(hd, qi_ref, ki_ref, ke_ref, v_ref, mask) for hd in range(GLA_HEADS)]
    for hd in range(GLA_HEADS):
        _gla_head_close(hd, opened[hd], qi_ref, v_ref, dec_ref, st_ref, emit, reverse)


def _gla_bwd_kernel(qi_ref, ki_ref, ke_ref, v_ref, dec_ref, of_ref, sga_ref, y_ref, ctx_ref, x_ref,
                    modc_ref, modl_ref, gg_ref, wo_in, wy_in, out_ref, st_ref, ob_ref, wo_ref, wy_ref, stage_ref):
    j = pl.program_id(1)
    samples = range(GLA_PAIR)

    @pl.when(_first_step())
    def _():
        wo_ref[...] = wo_in[...].astype(BF16)
        wy_ref[...] = wy_in[...].astype(BF16)

    @pl.when(j == 0)
    def _():
        st_ref[...] = jnp.zeros_like(st_ref)

    proj_y = _dot(y_ref[...].reshape(GLA_PAIR * TM, SC_WIDTH), wy_ref[...])

    def emit_to(r):
        def emit(rows, vcols, o):
            ob_ref[r, rows, vcols] = o
        return emit

    mask = _gla_mask(reverse=True)
    opened = [[_gla_head_open(hd, qi_ref.at[r], ki_ref.at[r], ke_ref.at[r], v_ref.at[r], mask)
               for hd in range(GLA_HEADS)] for r in samples]
    for hd in range(GLA_HEADS):
        for r in samples:
            _gla_head_close(hd, opened[r][hd], qi_ref.at[r], v_ref.at[r], dec_ref.at[r], st_ref.at[r],
                            emit_to(r), reverse=True)

    inner = []
    for r in samples:
        for hd in range(GLA_HEADS):
            vcols = slice(hd * GLA_DV, (hd + 1) * GLA_DV)
            o = of_ref[r, :, vcols].astype(F32) + ob_ref[r, :, vcols]
            ms = jnp.mean(o * o, axis=-1, keepdims=True)
            on = o * lax.rsqrt(ms + EPS) * gg_ref[...]
            inner.append((on * sga_ref[r, :, vcols].astype(F32)).astype(BF16))
    inner = jnp.concatenate([jnp.concatenate(inner[r * GLA_HEADS:(r + 1) * GLA_HEADS], axis=-1)
                             for r in samples], axis=0)
    proj = _dot(inner, wo_ref[...]) + proj_y
    for r in samples:
        gate = jnp.where(j == 0, modc_ref[2:3, :], modl_ref[r, 2:3, :])
        ob_ref[r] = jnp.where(j == 0, ctx_ref[r], x_ref[r]) + gate * proj[r * TM:(r + 1) * TM, :]
        out_ref[r] = _to_segment_major(ob_ref.at[r], stage_ref, D_MODEL)


def _bwd_tile(nt):
    return lambda j: jnp.where(j == 0, 0, nt - j)


def _gla_bwd(qi, ki, ke, v, dec, o_f, sga, y, ctx, x, mods, gg, wo, wy, nt):
    assert GLA_V == D_MODEL
    bsz = qi.shape[0]
    assert bsz % GLA_PAIR == 0
    tile = _bwd_tile(nt)
    tok = lambda n: pl.BlockSpec((GLA_PAIR, TM, n), lambda p, j: (p, tile(j), 0))
    latent_block = lambda p, j: (p, jnp.where(j == 0, nt - 2, tile(j) - 1), 0)
    return pl.pallas_call(
        _gla_bwd_kernel,
        grid=(bsz // GLA_PAIR, nt),
        in_specs=[tok(GLA_QK), tok(GLA_QK), tok(GLA_QK), tok(GLA_V),
                  pl.BlockSpec((GLA_PAIR, None, CHUNKS, GLA_QK), lambda p, j: (p, tile(j), 0, 0)),
                  tok(GLA_V), tok(GLA_V), tok(SC_WIDTH),
                  pl.BlockSpec((GLA_PAIR, TM, D_MODEL), lambda p, j: (p, 0, 0)),
                  pl.BlockSpec((GLA_PAIR, TM, D_MODEL), latent_block),
                  pl.BlockSpec((None, None, 3, D_MODEL), lambda p, j: (0, bsz, 0, 0)),
                  pl.BlockSpec((None, GLA_PAIR, 3, D_MODEL), lambda p, j: (0, p, 0, 0)),
                  _const_spec(gg), _const_spec(wo), _const_spec(wy)],
        out_specs=tok(D_MODEL),
        out_shape=jax.ShapeDtypeStruct((bsz, nt * TM, D_MODEL), F32),
        scratch_shapes=[pltpu.VMEM((GLA_PAIR, GLA_HEADS, GLA_DV, GLA_DK), F32),
                        pltpu.VMEM((GLA_PAIR, TM, GLA_V), F32),
                        pltpu.VMEM(wo.shape, BF16),
                        pltpu.VMEM(wy.shape, BF16),
                        pltpu.VMEM((D_MODEL // LANES, SUBLANES * SEG_PITCH, LANES), F32)],
        compiler_params=_params(2),
        name="gla_bwd",
    )(qi, ki, ke, v, dec, o_f, sga, y, ctx, x, mods, mods, gg, _operand(wo), _operand(wy))


def _to_segment_major(src_ref, stage_ref, width):
    slabs = width // LANES
    for n in range(slabs):
        for s in range(SUBLANES):
            stage_ref[n, s * SEG_PITCH:s * SEG_PITCH + SEG_LEN, :] = (
                src_ref[s * SEG_LEN:(s + 1) * SEG_LEN, n * LANES:(n + 1) * LANES])
    groups = [
        jnp.concatenate([stage_ref[n, pl.ds(j, SUBLANES, stride=SEG_PITCH), :] for n in range(slabs)], axis=1)
        for j in range(SEG_LEN)]
    return jnp.concatenate(groups, axis=0)


def _from_segment_major(val, stage_ref, dst_ref, width):
    slabs = width // LANES
    for j in range(SEG_LEN):
        for n in range(slabs):
            stage_ref[n, pl.ds(j, SUBLANES, stride=SEG_PITCH), :] = (
                val[j * SUBLANES:(j + 1) * SUBLANES, n * LANES:(n + 1) * LANES])
    for n in range(slabs):
        for s in range(SUBLANES):
            dst_ref[s * SEG_LEN:(s + 1) * SEG_LEN, n * LANES:(n + 1) * LANES] = (
                stage_ref[n, s * SEG_PITCH:s * SEG_PITCH + SEG_LEN, :])


def _rg_conv(xr, halo_ref, rows_ref, cols, reverse):
    row = lax.broadcasted_iota(jnp.int32, (SUBLANES, xr.shape[1]), 0)
    edge = []
    for g in range(RG_CONV - 1):
        grp = slice(g * SUBLANES, (g + 1) * SUBLANES)
        if reverse:
            edge.append(jnp.where(row == SUBLANES - 1, pltpu.roll(halo_ref[grp, cols], SUBLANES - 1, 0),
                                  pltpu.roll(xr[grp, :], SUBLANES - 1, 0)))
        else:
            cur = xr[TM - (RG_CONV - 1 - g) * SUBLANES:TM - (RG_CONV - 2 - g) * SUBLANES, :]
            edge.append(jnp.where(row == 0, pltpu.roll(halo_ref[grp, cols], 1, 0), pltpu.roll(cur, 1, 0)))
    tap_w = lambda j: rows_ref[ROW_CONV_W + j:ROW_CONV_W + j + 1, cols]
    xc = rows_ref[ROW_CONV_B:ROW_CONV_B + 1, cols] + tap_w(RG_CONV - 1) * xr
    for k in range(1, RG_CONV):
        if reverse:
            tap = jnp.concatenate([xr[k * SUBLANES:, :]] + edge[:k], axis=0)
        else:
            tap = jnp.concatenate(edge[RG_CONV - 1 - k:] + [xr[:TM - k * SUBLANES, :]], axis=0)
        xc = xc + tap_w(RG_CONV - 1 - k) * tap
    halo_ref[:, cols] = xr[:HALO_ROWS, :] if reverse else xr[TM - HALO_ROWS:, :]
    return xc


def _rg_setup(lam_ref, ba_ref, bx_ref, cb_ref, cw_ref, wa_ref, wx_ref, rows_ref, wg_s, copies):
    wg_s[:, :, :RG_BLOCK_W] = wa_ref[...].astype(BF16)
    wg_s[:, :, RG_BLOCK_W:] = wx_ref[...].astype(BF16)
    rows_ref[ROW_DECAY:ROW_DECAY + 1, :] = (-0.5 * RG_C * LOG2_E) * _softplus(-lam_ref[...])
    rows_ref[ROW_BIAS_A:ROW_BIAS_A + 1, :] = 0.5 * ba_ref[...]
    rows_ref[ROW_BIAS_X:ROW_BIAS_X + 1, :] = 0.5 * bx_ref[...]
    rows_ref[ROW_CONV_B:ROW_CONV_B + 1, :] = 0.5 * cb_ref[...]
    rows_ref[ROW_CONV_W:ROW_CONV_W + RG_CONV, :] = 0.5 * cw_ref[...]
    for src_ref, dst_ref in copies:
        dst_ref[...] = src_ref[...].astype(BF16)


def _rg_gates_scan(xc, first_block, wg_ref, rows_ref, carry_ref, reverse):
    row = lax.broadcasted_iota(jnp.int32, (SUBLANES, RG_BLOCK_W), 0)
    shift = SUBLANES - 1 if reverse else 1
    steps = range(SEG_LEN - 1, -1, -1) if reverse else range(SEG_LEN)
    out = []
    for i in range(xc.shape[1] // RG_BLOCK_W):
        n = first_block + i
        cols = slice(n * RG_BLOCK_W, (n + 1) * RG_BLOCK_W)
        xb = xc[:, i * RG_BLOCK_W:(i + 1) * RG_BLOCK_W]
        gates = _dot(xb.astype(BF16), wg_ref[n])
        th_r = jnp.tanh(gates[:, :RG_BLOCK_W] + rows_ref[ROW_BIAS_A:ROW_BIAS_A + 1, cols])
        th_i = jnp.tanh(gates[:, RG_BLOCK_W:] + rows_ref[ROW_BIAS_X:ROW_BIAS_X + 1, cols])
        half_decay = rows_ref[ROW_DECAY:ROW_DECAY + 1, cols]
        a = jnp.exp2(th_r * half_decay + half_decay)
        y = 1.0 - a * a
        root = y * lax.rsqrt(jnp.maximum(y, F32_TINY))
        u = root * ((th_i + 1.0) * xb)

        h = jnp.zeros((SUBLANES, RG_BLOCK_W), F32)
        prod = jnp.ones((SUBLANES, RG_BLOCK_W), F32)
        local, prods = {}, {}
        for j in steps:
            grp = slice(j * SUBLANES, (j + 1) * SUBLANES)
            h = a[grp, :] * h + u[grp, :]
            prod = prod * a[grp, :]
            local[j], prods[j] = h, prod

        entering = carry_ref[:, cols]
        for s in range(SUBLANES - 1):
            nxt = pltpu.roll(prod * entering + h, shift, 0)
            target = SUBLANES - 2 - s if reverse else s + 1
            entering = jnp.where(row == target, nxt, entering)
        carry_ref[:, cols] = pltpu.roll(prod * entering + h, shift, 0)

        out.append(jnp.concatenate([local[j] + prods[j] * entering for j in range(SEG_LEN)], axis=0))
    return jnp.concatenate(out, axis=1)


def _odd_fwd_kernel(xx_ref, modc_ref, modl_ref, g_ref, wx_ref, cw_ref, cb_ref, ga_ref, gx_ref, ba_ref, bx_ref,
                    lam_ref, xr_ref, hf_ref, hn_ref, halo_ref, carry_ref, rows_ref, wx_s, wg_s):
    t = pl.program_id(1)

    @pl.when(_first_step())
    def _():
        _rg_setup(lam_ref, ba_ref, bx_ref, cb_ref, cw_ref, ga_ref, gx_ref, rows_ref, wg_s, ((wx_ref, wx_s),))

    @pl.when(t == 0)
    def _():
        carry_ref[...] = jnp.zeros_like(carry_ref)

    @pl.when(t <= 1)
    def _():
        halo_ref[...] = jnp.zeros_like(halo_ref)

    samples = range(RG_PAIR)
    for r in samples:
        shift = jnp.where(t == 0, modc_ref[0:1, :], modl_ref[r, 0:1, :])
        scale = jnp.where(t == 0, modc_ref[1:2, :], modl_ref[r, 1:2, :])
        hn_ref[r] = _modulated_norm(xx_ref[r], g_ref[...], shift, scale).astype(BF16)
    hb = hn_ref[...].reshape(RG_PAIR * TM, D_MODEL)
    chunks = RG_WIDTH // RG_CHUNK_W
    project = lambda m: _dot(hb, wx_s[:, m * RG_CHUNK_W:(m + 1) * RG_CHUNK_W])
    xr_next = project(0)
    for m in range(chunks):
        cols = slice(m * RG_CHUNK_W, (m + 1) * RG_CHUNK_W)
        xr_pair = xr_next
        if m + 1 < chunks:
            xr_next = project(m + 1)
        for r in samples:
            xr = xr_pair[r * TM:(r + 1) * TM, :]
            xr_ref[r, :, cols] = xr.astype(BF16)
            xc = _rg_conv(xr, halo_ref.at[r], rows_ref, cols, reverse=False)
            hf_ref[r, :, cols] = _rg_gates_scan(xc, m * (RG_CHUNK_W // RG_BLOCK_W), wg_s, rows_ref,
                                                carry_ref.at[r], reverse=False).astype(BF16)


def _odd_fwd(xx, mods, g, wx, cw, cb, ga, gx, ba, bx, lam, nt):
    bsz = xx.shape[0]
    assert bsz % RG_PAIR == 0
    tok = lambda n: pl.BlockSpec((RG_PAIR, TM, n), lambda p, t: (p, t, 0))
    consts = (g, wx, cw, cb, ga, gx, ba, bx, lam)
    gate_scratch = pltpu.VMEM((RG_BLOCKS, RG_BLOCK_W, 2 * RG_BLOCK_W), BF16)
    wide = jax.ShapeDtypeStruct((bsz, nt * TM, RG_WIDTH), BF16)
    return pl.pallas_call(
        _odd_fwd_kernel,
        grid=(bsz // RG_PAIR, nt),
        in_specs=[tok(D_MODEL),
                  pl.BlockSpec((None, None, 3, D_MODEL), lambda p, t: (1, bsz, 0, 0)),
                  pl.BlockSpec((None, RG_PAIR, 3, D_MODEL), lambda p, t: (1, p, 0, 0))]
        + [_const_spec(w) for w in consts],
        out_specs=[tok(RG_WIDTH), tok(RG_WIDTH), tok(D_MODEL)],
        out_shape=[wide, wide, jax.ShapeDtypeStruct((bsz, nt * TM, D_MODEL), BF16)],
        scratch_shapes=[pltpu.VMEM((RG_PAIR, HALO_ROWS, RG_WIDTH), F32),
                        pltpu.VMEM((RG_PAIR, SUBLANES, RG_WIDTH), F32),
                        pltpu.VMEM((RG_ROWS, RG_WIDTH), F32),
                        pltpu.VMEM(wx.shape, BF16),
                        gate_scratch],
        compiler_params=_params(2),
        name="odd_fwd",
    )(xx, mods, mods, *[_operand(w) for w in consts])


def _odd_bwd_kernel(xx_ref, mod_ref, hn_ref, xr_ref, hf_ref, wgate_ref, cw_ref, cb_ref, ga_ref, gx_ref, ba_ref,
                    bx_ref, lam_ref, wout_ref, fg_ref, out_ref,
                    stage_ref, halo_ref, yy_ref, carry_ref, rows_ref, wgate_s, wg_s, wout_s):
    j = pl.program_id(1)

    @pl.when(_first_step())
    def _():
        _rg_setup(lam_ref, ba_ref, bx_ref, cb_ref, cw_ref, ga_ref, gx_ref, rows_ref, wg_s,
                  ((wgate_ref, wgate_s), (wout_ref, wout_s)))

    @pl.when(j == 0)
    def _():
        carry_ref[...] = jnp.zeros_like(carry_ref)

    @pl.when(j <= 1)
    def _():
        halo_ref[...] = jnp.zeros_like(halo_ref)

    def tile(with_output):
        hb = hn_ref[...]
        for m in range(RG_WIDTH // RG_CHUNK_W):
            cols = slice(m * RG_CHUNK_W, (m + 1) * RG_CHUNK_W)
            if with_output:
                gate = _dot(hb, wgate_s[:, cols])
            xc = _rg_conv(xr_ref[:, cols].astype(F32), halo_ref, rows_ref, cols, reverse=True)
            h_bwd = _rg_gates_scan(xc, m * (RG_CHUNK_W // RG_BLOCK_W), wg_s, rows_ref, carry_ref,
                                   reverse=True)
            if with_output:
                yy_ref[:, cols] = ((hf_ref[:, cols].astype(F32) + h_bwd) * _silu(gate)).astype(BF16)
        if with_output:
            xn = xx_ref[...] + mod_ref[2:3, :] * _dot(yy_ref[...], wout_s[...])
            ms = jnp.mean(xn * xn, axis=-1, keepdims=True)
            _from_segment_major(xn * lax.rsqrt(ms + EPS) * fg_ref[...], stage_ref, out_ref, D_MODEL)

    pl.when(j == 0)(lambda: tile(with_output=False))
    pl.when(j > 0)(lambda: tile(with_output=True))


def _odd_bwd(xx, mods, hn, xr, hf, wgate, cw, cb, ga, gx, ba, bx, lam, wout, fg, nt):
    bsz = xx.shape[0]
    tile = _bwd_tile(nt)
    tok = lambda n: pl.BlockSpec((None, TM, n), lambda b, j: (b, tile(j), 0))
    consts = (wgate, cw, cb, ga, gx, ba, bx, lam, wout, fg)
    gate_scratch = pltpu.VMEM((RG_BLOCKS, RG_BLOCK_W, 2 * RG_BLOCK_W), BF16)
    out_block = lambda b, j: (b, jnp.where(j == 0, nt - 2, nt - 1 - j), 0)
    return pl.pallas_call(
        _odd_bwd_kernel,
        grid=(bsz, nt),
        in_specs=[tok(D_MODEL), _mod_spec(1, bsz, tile), tok(D_MODEL), tok(RG_WIDTH), tok(RG_WIDTH)]
        + [_const_spec(w) for w in consts],
        out_specs=pl.BlockSpec((None, TM, D_MODEL), out_block),
        out_shape=jax.ShapeDtypeStruct((bsz, (nt - 1) * TM, D_MODEL), F32),
        scratch_shapes=[pltpu.VMEM((D_MODEL // LANES, SUBLANES * SEG_PITCH, LANES), F32),
                        pltpu.VMEM((HALO_ROWS, RG_WIDTH), F32),
                        pltpu.VMEM((TM, RG_WIDTH), BF16),
                        pltpu.VMEM((SUBLANES, RG_WIDTH), F32),
                        pltpu.VMEM((RG_ROWS, RG_WIDTH), F32),
                        pltpu.VMEM(wgate.shape, BF16),
                        gate_scratch,
                        pltpu.VMEM(wout.shape, BF16)],
        compiler_params=_params(2),
        name="odd_bwd",
    )(xx, mods, hn, xr, hf, *[_operand(w) for w in consts])


def kernel(x, c, ctx, c_ctx, norm_g, w_mod, b_mod, e_w_in, e_w_a2, e_b_a2, e_gla_g, e_conv_w, e_w_out,
           o_w_in, o_conv_w, o_conv_b, o_w_a, o_b_a, o_w_x, o_b_x, o_lam, o_w_out, final_g):
    bsz, seq, _ = x.shape
    assert ctx.shape[1] == TM and seq % TM == 0 and bsz < SUBLANES
    assert w_mod.shape[0] == 2
    nt = 1 + seq // TM

    mods = _mods(c, c_ctx, w_mod, b_mod).reshape(w_mod.shape[0], SUBLANES, 3, D_MODEL)

    w_in = e_w_in[0]
    w_in_bf16 = w_in.astype(BF16)
    square = (D_MODEL, D_MODEL)
    offs = {}
    off = 0
    for name, size in (("q", GLA_QK), ("k", GLA_QK), ("v", GLA_V), ("ga", GLA_V), ("af", GLA_LOWRANK),
                       ("ab", GLA_LOWRANK), ("cb", SC_WIDTH), ("cc", SC_WIDTH), ("cx", SC_WIDTH),
                       ("gb", SC_WIDTH)):
        offs[name] = (off, off + size)
        off += size
    cols = lambda lo, hi: w_in[:, lo:hi].astype(BF16)
    assert offs["q"][0] == 0 and offs["v"][0] == D_MODEL and offs["ga"] == (2 * D_MODEL, 3 * D_MODEL)
    wqk, wv, wga = (_Window(w_in_bf16, square, (0, i)) for i in range(3))
    wa = jnp.pad(w_in[:, offs["af"][0]:offs["ab"][1]], ((0, 0), (0, LANES - 2 * GLA_LOWRANK))).astype(BF16)
    b2 = e_b_a2[0].reshape(1, 2 * GLA_QK)
    mats = (wqk, wv, wga, wa, cols(*offs["cb"]), cols(*offs["cc"]), cols(*offs["cx"]), cols(*offs["gb"]))
    (o_f, qi_b, ki_b, ke_b, dec_b, v, sga, y) = _even_in(
        ctx, x, mods, (norm_g[0:1], b2, e_conv_w[0], e_w_a2[0]), mats, nt)
    w_out = e_w_out[0]
    xx = _gla_bwd(qi_b, ki_b, ke_b, v, dec_b, o_f, sga, y, ctx, x, mods, e_gla_g[0:1],
                  _Window(w_out, square, (0, 0)), _Window(w_out, square, (1, 0)), nt)

    w_in = o_w_in[0]
    half = (D_MODEL, RG_WIDTH)
    wide = lambda a: a.reshape(1, RG_WIDTH)
    xr, h_f, hn = _odd_fwd(xx, mods, norm_g[1:2], _Window(w_in, half, (0, 0)), o_conv_w[0, 0],
                           wide(o_conv_b[0, 0]), o_w_a[0, 0], o_w_x[0, 0],
                           wide(o_b_a[0, 0]), wide(o_b_x[0, 0]), wide(o_lam[0, 0]), nt)
    return _odd_bwd(xx, mods, hn, xr, h_f, _Window(w_in, half, (0, 1)), o_conv_w[0, 1],
                    wide(o_conv_b[0, 1]), o_w_a[0, 1], o_w_x[0, 1],
                    wide(o_b_a[0, 1]), wide(o_b_x[0, 1]), wide(o_lam[0, 1]), o_w_out[0],
                    final_g.reshape(1, D_MODEL), nt)
```

```python
import jax
import jax.numpy as jnp
from jax import lax
from jax.experimental import pallas as pl
from jax.experimental.pallas import tpu as pltpu

F32 = jnp.float32
BF16 = jnp.bfloat16

D_MODEL = 1024
EPS = 1e-6
GLA_HEADS = 4
GLA_DK = 128
GLA_DV = 256
GLA_QK = GLA_HEADS * GLA_DK
GLA_V = GLA_HEADS * GLA_DV
GLA_LOWRANK = 16
GLA_TAU = 16.0
GLA_CHUNK = 64
GRID_W = 64
SC_WIDTH = D_MODEL
RG_WIDTH = 2 * D_MODEL
RG_BLOCKS = 16
RG_BLOCK_W = 128
RG_C = 8.0
RG_CONV = 4

TM = 256
CHUNKS = TM // GLA_CHUNK
GLA_PAIR = 2
RG_PAIR = 4
EVEN_PAIR = 2
MOD_CHUNK_ROWS = 256
MOD_BUFFERS = 3
SUBLANES = 8
LANES = 128
VMEM_LIMIT = 56 * 1024 * 1024
SEG_LEN = TM // SUBLANES
SEG_PITCH = SEG_LEN + SUBLANES
HALO_ROWS = (RG_CONV - 1) * SUBLANES
RG_CHUNK_W = 256
ROW_DECAY, ROW_BIAS_A, ROW_BIAS_X, ROW_CONV_B, ROW_CONV_W = 0, 1, 2, 3, 4
RG_ROWS = ROW_CONV_W + RG_CONV
LOG2_E = 1.4426950408889634
F32_TINY = 1.1754943508222875e-38


def _silu(x):
    return x * jax.nn.sigmoid(x)


def _log_sigmoid(z):
    return jnp.minimum(z, 0.0) - jnp.log(1.0 + jnp.exp(-jnp.abs(z)))


def _softplus(z):
    return jnp.maximum(z, 0.0) + jnp.log1p(jnp.exp(-jnp.abs(z)))


def _dot(a, b):
    return jnp.dot(a, b, preferred_element_type=F32)


def _dot_nt(a, b):
    return lax.dot_general(a, b, (((1,), (1,)), ((), ())), preferred_element_type=F32)


def _dot_tn(a, b):
    return lax.dot_general(a, b, (((0,), (0,)), ((), ())), preferred_element_type=F32)


def _modulated_norm(x, g, shift, scale):
    ms = jnp.mean(x * x, axis=-1, keepdims=True)
    return (x * lax.rsqrt(ms + EPS) * g) * (1.0 + scale) + shift


def _params(n_axes):
    return pltpu.CompilerParams(
        dimension_semantics=("arbitrary",) * n_axes,
        vmem_limit_bytes=VMEM_LIMIT)


def _first_step():
    return (pl.program_id(0) == 0) & (pl.program_id(1) == 0)


class _Window:
    def __init__(self, array, block_shape, block_index):
        self.array, self.shape, self.index = array, tuple(block_shape), tuple(block_index)


def _operand(w):
    return w.array if isinstance(w, _Window) else w


def _const_spec(w):
    index = w.index if isinstance(w, _Window) else (0,) * len(w.shape)
    return pl.BlockSpec(w.shape, lambda *_: index, pipeline_mode=pl.Buffered(1))


def _mod_spec(layer, bsz, tile):
    return pl.BlockSpec((None, None, 3, D_MODEL),
                        lambda b, t: (layer, jnp.where(tile(t) == 0, bsz, b), 0, 0))


def _bf16_terms(val):
    hi = val.astype(BF16)
    return hi, (val - hi.astype(F32)).astype(BF16)


def _mods_kernel(c_ref, cctx_ref, w_hbm, b_ref, o_ref, s_ref, wbuf_ref, sem):
    bsz = c_ref.shape[0]
    per_layer = D_MODEL // MOD_CHUNK_ROWS
    chunks = [(l, k) for l in range(w_hbm.shape[0]) for k in range(per_layer)]

    def copy(i):
        l, k = chunks[i]
        slot = i % MOD_BUFFERS
        return pltpu.make_async_copy(w_hbm.at[l, pl.ds(k * MOD_CHUNK_ROWS, MOD_CHUNK_ROWS), :],
                                     wbuf_ref.at[slot], sem.at[slot])

    for i in range(min(MOD_BUFFERS, len(chunks))):
        copy(i).start()
    s_ref[...] = jnp.zeros_like(s_ref)
    s_ref[0:bsz, :] = c_ref[...]
    s_ref[bsz:bsz + 1, :] = cctx_ref[...]
    s_terms = jnp.concatenate(_bf16_terms(_silu(s_ref[...])), axis=0)
    acc = None
    for i, (l, k) in enumerate(chunks):
        copy(i).wait()
        w_hi, w_lo = _bf16_terms(wbuf_ref[i % MOD_BUFFERS])
        s_cols = s_terms[:, k * MOD_CHUNK_ROWS:(k + 1) * MOD_CHUNK_ROWS]
        part = _dot(s_cols, w_hi) + _dot(s_cols, w_lo)
        acc = part if k == 0 else acc + part
        if k == per_layer - 1:
            o_ref[l] = acc[:SUBLANES] + acc[SUBLANES:] + b_ref[l]
        if i + MOD_BUFFERS < len(chunks):
            copy(i + MOD_BUFFERS).start()


def _mods(c, c_ctx, w_mod, b_mod):
    depth, _, width = w_mod.shape
    whole = lambda a: pl.BlockSpec(a.shape, lambda i: (0,) * a.ndim)
    c_ctx = c_ctx.reshape(1, D_MODEL)
    b_mod = b_mod.reshape(depth, 1, width)
    return pl.pallas_call(
        _mods_kernel,
        grid=(1,),
        in_specs=[whole(c), whole(c_ctx), pl.BlockSpec(memory_space=pl.ANY), whole(b_mod)],
        out_specs=pl.BlockSpec((depth, SUBLANES, width), lambda i: (0, 0, 0)),
        out_shape=jax.ShapeDtypeStruct((depth, SUBLANES, width), F32),
        scratch_shapes=[pltpu.VMEM((SUBLANES, D_MODEL), F32),
                        pltpu.VMEM((MOD_BUFFERS, MOD_CHUNK_ROWS, width), F32),
                        pltpu.SemaphoreType.DMA((MOD_BUFFERS,))],
        compiler_params=_params(1),
        name="mods",
    )(c, c_ctx, w_mod, b_mod)


def _chunk_rows(x, offset):
    return jnp.concatenate(
        [jnp.broadcast_to(x[c * GLA_CHUNK + offset:c * GLA_CHUNK + offset + 1, :], (GLA_CHUNK, x.shape[1]))
         for c in range(x.shape[0] // GLA_CHUNK)], axis=0)


def _even_in_kernel(ctx_ref, x_ref, modc_ref, modl_ref, g_ref, b2_ref, cw_ref, wa2_ref,
                    wqk_ref, wv_ref, wga_ref, wa_ref, wcb_ref, wcc_ref, wcx_ref, wgb_ref,
                    of_ref, qib_ref, kib_ref, keb_ref, decb_ref, v_ref, sga_ref, y_ref,
                    lower_ref, upper_ref, w2_ref, qif_ref, kif_ref, kef_ref, decf_ref, st_ref):
    t = pl.program_id(1)
    samples = range(EVEN_PAIR)
    rows_of = lambda r: slice(r * TM, (r + 1) * TM)
    split = lambda val: val.reshape(EVEN_PAIR, TM, val.shape[-1])

    @pl.when(t == 0)
    def _():
        st_ref[...] = jnp.zeros_like(st_ref)

    @pl.when(_first_step())
    def _():
        r = lax.broadcasted_iota(jnp.int32, (TM, TM), 0)
        c = lax.broadcasted_iota(jnp.int32, (TM, TM), 1)
        same = (r // GLA_CHUNK) == (c // GLA_CHUNK)
        lower_ref[...] = jnp.where(same & (c <= r), 1.0, 0.0).astype(BF16)
        upper_ref[...] = jnp.where(same & (c >= r), 1.0, 0.0).astype(BF16)
        w2_ref[...] = jnp.zeros_like(w2_ref)
        w2_ref[0:GLA_LOWRANK, 0:GLA_QK] = wa2_ref[0].astype(BF16)
        w2_ref[GLA_LOWRANK:2 * GLA_LOWRANK, GLA_QK:] = wa2_ref[1].astype(BF16)

    normed = []
    for r in samples:
        xt = jnp.where(t == 0, ctx_ref[r], x_ref[r])
        shift = jnp.where(t == 0, modc_ref[0:1, :], modl_ref[r, 0:1, :])
        scale = jnp.where(t == 0, modc_ref[1:2, :], modl_ref[r, 1:2, :])
        normed.append(_modulated_norm(xt, g_ref[...], shift, scale).astype(BF16))
    hb = jnp.concatenate(normed, axis=0)

    a_lr = _dot(hb, wa_ref[...])
    z = _dot(a_lr.astype(BF16), w2_ref[...]) + b2_ref[...]
    qk = _dot(hb, wqk_ref[...])
    v_ref[...] = split(_dot(hb, wv_ref[...]).astype(BF16))

    lg = _log_sigmoid(z) * (1.0 / GLA_TAU)
    lgb = lg.astype(BF16)
    cs_f = jnp.concatenate([_dot(lower_ref[...], lgb[rows_of(r), :GLA_QK]) for r in samples], axis=0)
    cs_b = jnp.concatenate([_dot(upper_ref[...], lgb[rows_of(r), GLA_QK:]) for r in samples], axis=0)
    tot_f = _chunk_rows(cs_f, GLA_CHUNK - 1)
    tot_b = _chunk_rows(cs_b, 0)
    c_c = _dot(hb, wcc_ref[...])

    q = qk[:, :GLA_QK] * (GLA_DK ** -0.5)
    k = qk[:, GLA_QK:]
    qif_ref[...] = split((q * jnp.exp(cs_f)).astype(BF16))
    kif_ref[...] = split((k * jnp.exp(-cs_f)).astype(BF16))
    kef_ref[...] = split((k * jnp.exp(tot_f - cs_f)).astype(BF16))
    qib_ref[...] = split((q * jnp.exp(cs_b)).astype(BF16))
    kib_ref[...] = split((k * jnp.exp(-cs_b)).astype(BF16))
    keb_ref[...] = split((k * jnp.exp(tot_b - cs_b)).astype(BF16))
    for r in samples:
        for c in range(CHUNKS):
            row = r * TM + c * GLA_CHUNK
            decf_ref[r, c:c + 1, :] = jnp.exp(tot_f[row:row + 1, :])
            decb_ref[r, c:c + 1, :] = jnp.exp(tot_b[row:row + 1, :])

    def emit_to(r):
        def emit(rows, vcols, o):
            of_ref[r, rows, vcols] = o.astype(BF16)
        return emit

    mask = _gla_mask(reverse=False)
    open_head = lambda hd: [_gla_head_open(hd, qif_ref.at[r], kif_ref.at[r], kef_ref.at[r], v_ref.at[r], mask)
                            for r in samples]

    def close_head(hd, opened):
        for r in samples:
            _gla_head_close(hd, opened[r], qif_ref.at[r], v_ref.at[r], decf_ref.at[r], st_ref.at[r],
                            emit_to(r), reverse=False)

    opened = open_head(0)
    zz = c_c * _dot(hb, wcx_ref[...])
    close_head(0, opened)
    opened = open_head(1)
    c_b = _dot(hb, wcb_ref[...])
    close_head(1, opened)
    opened = open_head(2)
    g_b = _dot(hb, wgb_ref[...])
    close_head(2, opened)
    opened = open_head(3)
    sga_ref[...] = split(_silu(_dot(hb, wga_ref[...])).astype(BF16))
    close_head(3, opened)

    step_rows = EVEN_PAIR * TM
    pos = lax.broadcasted_iota(jnp.int32, (step_rows, SC_WIDTH), 0)
    row_len = jnp.where(t == 0, TM, GRID_W)
    in_row = pos & (row_len - 1)
    z_prev = jnp.where(in_row != 0, pltpu.roll(zz, 1, 0), 0.0)
    z_next = jnp.where(in_row != row_len - 1, pltpu.roll(zz, step_rows - 1, 0), 0.0)
    zc = cw_ref[0:1, :] * z_prev + cw_ref[1:2, :] * zz + cw_ref[2:3, :] * z_next
    y_ref[...] = split((c_b * zc * _silu(g_b)).astype(BF16))


def _even_in(ctx, x, mods, rows, mats, nt):
    bsz = x.shape[0]
    assert bsz % EVEN_PAIR == 0
    tok = lambda n: pl.BlockSpec((EVEN_PAIR, TM, n), lambda p, t: (p, t, 0))
    dec = pl.BlockSpec((EVEN_PAIR, None, CHUNKS, GLA_QK), lambda p, t: (p, t, 0, 0))
    tok_shape = lambda n: jax.ShapeDtypeStruct((bsz, nt * TM, n), BF16)
    dec_shape = jax.ShapeDtypeStruct((bsz, nt, CHUNKS, GLA_QK), F32)
    consts = tuple(rows) + tuple(mats)
    return pl.pallas_call(
        _even_in_kernel,
        grid=(bsz // EVEN_PAIR, nt),
        in_specs=[
            pl.BlockSpec((EVEN_PAIR, TM, D_MODEL), lambda p, t: (p, 0, 0)),
            pl.BlockSpec((EVEN_PAIR, TM, D_MODEL), lambda p, t: (p, jnp.maximum(t - 1, 0), 0)),
            pl.BlockSpec((None, None, 3, D_MODEL), lambda p, t: (0, bsz, 0, 0)),
            pl.BlockSpec((None, EVEN_PAIR, 3, D_MODEL), lambda p, t: (0, p, 0, 0)),
        ] + [_const_spec(w) for w in consts],
        out_specs=[tok(GLA_V), tok(GLA_QK), tok(GLA_QK), tok(GLA_QK), dec,
                   tok(GLA_V), tok(GLA_V), tok(SC_WIDTH)],
        out_shape=[tok_shape(GLA_V), tok_shape(GLA_QK), tok_shape(GLA_QK), tok_shape(GLA_QK), dec_shape,
                   tok_shape(GLA_V), tok_shape(GLA_V), tok_shape(SC_WIDTH)],
        scratch_shapes=[pltpu.VMEM((TM, TM), BF16)] * 2 + [pltpu.VMEM((LANES, 2 * GLA_QK), BF16)]
        + [pltpu.VMEM((EVEN_PAIR, TM, GLA_QK), BF16)] * 3
        + [pltpu.VMEM((EVEN_PAIR, CHUNKS, GLA_QK), F32),
           pltpu.VMEM((EVEN_PAIR, GLA_HEADS, GLA_DV, GLA_DK), F32)],
        compiler_params=_params(2),
        name="even_in",
    )(ctx, x, mods, mods, *[_operand(w) for w in consts])


def _chunk_slice(ch):
    return slice(ch * GLA_CHUNK, (ch + 1) * GLA_CHUNK)


def _gla_mask(reverse):
    r = lax.broadcasted_iota(jnp.int32, (TM, TM), 0)
    c = lax.broadcasted_iota(jnp.int32, (TM, TM), 1)
    same_chunk = (r // GLA_CHUNK) == (c // GLA_CHUNK)
    return same_chunk & ((c >= r) if reverse else (c <= r))


def _gla_head_open(hd, qi_ref, ki_ref, ke_ref, v_ref, mask):
    kcols = slice(hd * GLA_DK, (hd + 1) * GLA_DK)
    vcols = slice(hd * GLA_DV, (hd + 1) * GLA_DV)
    incr = [_dot_tn(v_ref[_chunk_slice(ch), vcols], ke_ref[_chunk_slice(ch), kcols]) for ch in range(CHUNKS)]
    scores = jnp.where(mask, _dot_nt(qi_ref[:, kcols], ki_ref[:, kcols]), 0.0).astype(BF16)
    return incr, scores


def _gla_head_close(hd, opened, qi_ref, v_ref, dec_ref, st_ref, emit, reverse):
    incr, scores = opened
    kcols = slice(hd * GLA_DK, (hd + 1) * GLA_DK)
    vcols = slice(hd * GLA_DV, (hd + 1) * GLA_DV)
    state = st_ref[hd]
    before = {}
    for ch in (range(CHUNKS - 1, -1, -1) if reverse else range(CHUNKS)):
        before[ch] = state.astype(BF16)
        state = dec_ref[ch:ch + 1, kcols] * state + incr[ch]
    st_ref[hd] = state
    intra = _dot(scores, v_ref[:, vcols])
    for ch in range(CHUNKS):
        rows = _chunk_slice(ch)
        emit(rows, vcols, intra[rows, :] + _dot_nt(qi_ref[rows, kcols], before[ch]))


def _gla_tile(qi_ref, ki_ref, ke_ref, v_ref, dec_ref, st_ref, emit, reverse):
    mask = _gla_mask(reverse)
    opened = [_gla_head_open(hd, qi_ref, ki_ref, ke_ref, v_ref, mask) for hd in range(GLA_HEADS)]
    for hd in range(GLA_HEADS):
        _gla_head_close(hd, opened[hd], qi_ref, v_ref, dec_ref, st_ref, emit, reverse)


def _gla_bwd_kernel(qi_ref, ki_ref, ke_ref, v_ref, dec_ref, of_ref, sga_ref, y_ref, ctx_ref, x_ref,
                    modc_ref, modl_ref, gg_ref, wo_in, wy_in, out_ref, st_ref, ob_ref, wo_ref, wy_ref, stage_ref):
    j = pl.program_id(1)
    samples = range(GLA_PAIR)

    @pl.when(_first_step())
    def _():
        wo_ref[...] = wo_in[...].astype(BF16)
        wy_ref[...] = wy_in[...].astype(BF16)

    @pl.when(j == 0)
    def _():
        st_ref[...] = jnp.zeros_like(st_ref)

    proj_y = _dot(y_ref[...].reshape(GLA_PAIR * TM, SC_WIDTH), wy_ref[...])

    def emit_to(r):
        def emit(rows, vcols, o):
            ob_ref[r, rows, vcols] = o
        return emit

    mask = _gla_mask(reverse=True)
    opened = [[_gla_head_open(hd, qi_ref.at[r], ki_ref.at[r], ke_ref.at[r], v_ref.at[r], mask)
               for hd in range(GLA_HEADS)] for r in samples]
    for hd in range(GLA_HEADS):
        for r in samples:
            _gla_head_close(hd, opened[r][hd], qi_ref.at[r], v_ref.at[r], dec_ref.at[r], st_ref.at[r],
                            emit_to(r), reverse=True)

    inner = []
    for r in samples:
        for hd in range(GLA_HEADS):
            vcols = slice(hd * GLA_DV, (hd + 1) * GLA_DV)
            o = of_ref[r, :, vcols].astype(F32) + ob_ref[r, :, vcols]
            ms = jnp.mean(o * o, axis=-1, keepdims=True)
            on = o * lax.rsqrt(ms + EPS) * gg_ref[...]
            inner.append((on * sga_ref[r, :, vcols].astype(F32)).astype(BF16))
    inner = jnp.concatenate([jnp.concatenate(inner[r * GLA_HEADS:(r + 1) * GLA_HEADS], axis=-1)
                             for r in samples], axis=0)
    proj = _dot(inner, wo_ref[...]) + proj_y
    for r in samples:
        gate = jnp.where(j == 0, modc_ref[2:3, :], modl_ref[r, 2:3, :])
        ob_ref[r] = jnp.where(j == 0, ctx_ref[r], x_ref[r]) + gate * proj[r * TM:(r + 1) * TM, :]
        out_ref[r] = _to_segment_major(ob_ref.at[r], stage_ref, D_MODEL)


def _bwd_tile(nt):
    return lambda j: jnp.where(j == 0, 0, nt - j)


def _gla_bwd(qi, ki, ke, v, dec, o_f, sga, y, ctx, x, mods, gg, wo, wy, nt):
    assert GLA_V == D_MODEL
    bsz = qi.shape[0]
    assert bsz % GLA_PAIR == 0
    tile = _bwd_tile(nt)
    tok = lambda n: pl.BlockSpec((GLA_PAIR, TM, n), lambda p, j: (p, tile(j), 0))
    latent_block = lambda p, j: (p, jnp.where(j == 0, nt - 2, tile(j) - 1), 0)
    return pl.pallas_call(
        _gla_bwd_kernel,
        grid=(bsz // GLA_PAIR, nt),
        in_specs=[tok(GLA_QK), tok(GLA_QK), tok(GLA_QK), tok(GLA_V),
                  pl.BlockSpec((GLA_PAIR, None, CHUNKS, GLA_QK), lambda p, j: (p, tile(j), 0, 0)),
                  tok(GLA_V), tok(GLA_V), tok(SC_WIDTH),
                  pl.BlockSpec((GLA_PAIR, TM, D_MODEL), lambda p, j: (p, 0, 0)),
                  pl.BlockSpec((GLA_PAIR, TM, D_MODEL), latent_block),
                  pl.BlockSpec((None, None, 3, D_MODEL), lambda p, j: (0, bsz, 0, 0)),
                  pl.BlockSpec((None, GLA_PAIR, 3, D_MODEL), lambda p, j: (0, p, 0, 0)),
                  _const_spec(gg), _const_spec(wo), _const_spec(wy)],
        out_specs=tok(D_MODEL),
        out_shape=jax.ShapeDtypeStruct((bsz, nt * TM, D_MODEL), F32),
        scratch_shapes=[pltpu.VMEM((GLA_PAIR, GLA_HEADS, GLA_DV, GLA_DK), F32),
                        pltpu.VMEM((GLA_PAIR, TM, GLA_V), F32),
                        pltpu.VMEM(wo.shape, BF16),
                        pltpu.VMEM(wy.shape, BF16),
                        pltpu.VMEM((D_MODEL // LANES, SUBLANES * SEG_PITCH, LANES), F32)],
        compiler_params=_params(2),
        name="gla_bwd",
    )(qi, ki, ke, v, dec, o_f, sga, y, ctx, x, mods, mods, gg, _operand(wo), _operand(wy))


def _to_segment_major(src_ref, stage_ref, width):
    slabs = width // LANES
    for n in range(slabs):
        for s in range(SUBLANES):
            stage_ref[n, s * SEG_PITCH:s * SEG_PITCH + SEG_LEN, :] = (
                src_ref[s * SEG_LEN:(s + 1) * SEG_LEN, n * LANES:(n + 1) * LANES])
    groups = [
        jnp.concatenate([stage_ref[n, pl.ds(j, SUBLANES, stride=SEG_PITCH), :] for n in range(slabs)], axis=1)
        for j in range(SEG_LEN)]
    return jnp.concatenate(groups, axis=0)


def _from_segment_major(val, stage_ref, dst_ref, width):
    slabs = width // LANES
    for j in range(SEG_LEN):
        for n in range(slabs):
            stage_ref[n, pl.ds(j, SUBLANES, stride=SEG_PITCH), :] = (
                val[j * SUBLANES:(j + 1) * SUBLANES, n * LANES:(n + 1) * LANES])
    for n in range(slabs):
        for s in range(SUBLANES):
            dst_ref[s * SEG_LEN:(s + 1) * SEG_LEN, n * LANES:(n + 1) * LANES] = (
                stage_ref[n, s * SEG_PITCH:s * SEG_PITCH + SEG_LEN, :])


def _rg_conv(xr, halo_ref, rows_ref, cols, reverse):
    row = lax.broadcasted_iota(jnp.int32, (SUBLANES, xr.shape[1]), 0)
    edge = []
    for g in range(RG_CONV - 1):
        grp = slice(g * SUBLANES, (g + 1) * SUBLANES)
        if reverse:
            edge.append(jnp.where(row == SUBLANES - 1, pltpu.roll(halo_ref[grp, cols], SUBLANES - 1, 0),
                                  pltpu.roll(xr[grp, :], SUBLANES - 1, 0)))
        else:
            cur = xr[TM - (RG_CONV - 1 - g) * SUBLANES:TM - (RG_CONV - 2 - g) * SUBLANES, :]
            edge.append(jnp.where(row == 0, pltpu.roll(halo_ref[grp, cols], 1, 0), pltpu.roll(cur, 1, 0)))
    tap_w = lambda j: rows_ref[ROW_CONV_W + j:ROW_CONV_W + j + 1, cols]
    xc = rows_ref[ROW_CONV_B:ROW_CONV_B + 1, cols] + tap_w(RG_CONV - 1) * xr
    for k in range(1, RG_CONV):
        if reverse:
            tap = jnp.concatenate([xr[k * SUBLANES:, :]] + edge[:k], axis=0)
        else:
            tap = jnp.concatenate(edge[RG_CONV - 1 - k:] + [xr[:TM - k * SUBLANES, :]], axis=0)
        xc = xc + tap_w(RG_CONV - 1 - k) * tap
    halo_ref[:, cols] = xr[:HALO_ROWS, :] if reverse else xr[TM - HALO_ROWS:, :]
    return xc


def _rg_setup(lam_ref, ba_ref, bx_ref, cb_ref, cw_ref, wa_ref, wx_ref, rows_ref, wg_s, copies):
    wg_s[:, :, :RG_BLOCK_W] = wa_ref[...].astype(BF16)
    wg_s[:, :, RG_BLOCK_W:] = wx_ref[...].astype(BF16)
    rows_ref[ROW_DECAY:ROW_DECAY + 1, :] = (-0.5 * RG_C * LOG2_E) * _softplus(-lam_ref[...])
    rows_ref[ROW_BIAS_A:ROW_BIAS_A + 1, :] = 0.5 * ba_ref[...]
    rows_ref[ROW_BIAS_X:ROW_BIAS_X + 1, :] = 0.5 * bx_ref[...]
    rows_ref[ROW_CONV_B:ROW_CONV_B + 1, :] = 0.5 * cb_ref[...]
    rows_ref[ROW_CONV_W:ROW_CONV_W + RG_CONV, :] = 0.5 * cw_ref[...]
    for src_ref, dst_ref in copies:
        dst_ref[...] = src_ref[...].astype(BF16)


def _rg_gates_scan(xc, first_block, wg_ref, rows_ref, carry_ref, reverse):
    row = lax.broadcasted_iota(jnp.int32, (SUBLANES, RG_BLOCK_W), 0)
    shift = SUBLANES - 1 if reverse else 1
    steps = range(SEG_LEN - 1, -1, -1) if reverse else range(SEG_LEN)
    out = []
    for i in range(xc.shape[1] // RG_BLOCK_W):
        n = first_block + i
        cols = slice(n * RG_BLOCK_W, (n + 1) * RG_BLOCK_W)
        xb = xc[:, i * RG_BLOCK_W:(i + 1) * RG_BLOCK_W]
        gates = _dot(xb.astype(BF16), wg_ref[n])
        th_r = jnp.tanh(gates[:, :RG_BLOCK_W] + rows_ref[ROW_BIAS_A:ROW_BIAS_A + 1, cols])
        th_i = jnp.tanh(gates[:, RG_BLOCK_W:] + rows_ref[ROW_BIAS_X:ROW_BIAS_X + 1, cols])
        half_decay = rows_ref[ROW_DECAY:ROW_DECAY + 1, cols]
        a = jnp.exp2(th_r * half_decay + half_decay)
        y = 1.0 - a * a
        root = y * lax.rsqrt(jnp.maximum(y, F32_TINY))
        u = root * ((th_i + 1.0) * xb)

        h = jnp.zeros((SUBLANES, RG_BLOCK_W), F32)
        prod = jnp.ones((SUBLANES, RG_BLOCK_W), F32)
        local, prods = {}, {}
        for j in steps:
            grp = slice(j * SUBLANES, (j + 1) * SUBLANES)
            h = a[grp, :] * h + u[grp, :]
            prod = prod * a[grp, :]
            local[j], prods[j] = h, prod

        entering = carry_ref[:, cols]
        for s in range(SUBLANES - 1):
            nxt = pltpu.roll(prod * entering + h, shift, 0)
            target = SUBLANES - 2 - s if reverse else s + 1
            entering = jnp.where(row == target, nxt, entering)
        carry_ref[:, cols] = pltpu.roll(prod * entering + h, shift, 0)

        out.append(jnp.concatenate([local[j] + prods[j] * entering for j in range(SEG_LEN)], axis=0))
    return jnp.concatenate(out, axis=1)


def _odd_fwd_kernel(xx_ref, modc_ref, modl_ref, g_ref, wx_ref, cw_ref, cb_ref, ga_ref, gx_ref, ba_ref, bx_ref,
                    lam_ref, xr_ref, hf_ref, hn_ref, halo_ref, carry_ref, rows_ref, wx_s, wg_s):
    t = pl.program_id(1)

    @pl.when(_first_step())
    def _():
        _rg_setup(lam_ref, ba_ref, bx_ref, cb_ref, cw_ref, ga_ref, gx_ref, rows_ref, wg_s, ((wx_ref, wx_s),))

    @pl.when(t == 0)
    def _():
        carry_ref[...] = jnp.zeros_like(carry_ref)

    @pl.when(t <= 1)
    def _():
        halo_ref[...] = jnp.zeros_like(halo_ref)

    samples = range(RG_PAIR)
    for r in samples:
        shift = jnp.where(t == 0, modc_ref[0:1, :], modl_ref[r, 0:1, :])
        scale = jnp.where(t == 0, modc_ref[1:2, :], modl_ref[r, 1:2, :])
        hn_ref[r] = _modulated_norm(xx_ref[r], g_ref[...], shift, scale).astype(BF16)
    hb = hn_ref[...].reshape(RG_PAIR * TM, D_MODEL)
    chunks = RG_WIDTH // RG_CHUNK_W
    project = lambda m: _dot(hb, wx_s[:, m * RG_CHUNK_W:(m + 1) * RG_CHUNK_W])
    xr_next = project(0)
    for m in range(chunks):
        cols = slice(m * RG_CHUNK_W, (m + 1) * RG_CHUNK_W)
        xr_pair = xr_next
        if m + 1 < chunks:
            xr_next = project(m + 1)
        for r in samples:
            xr = xr_pair[r * TM:(r + 1) * TM, :]
            xr_ref[r, :, cols] = xr.astype(BF16)
            xc = _rg_conv(xr, halo_ref.at[r], rows_ref, cols, reverse=False)
            hf_ref[r, :, cols] = _rg_gates_scan(xc, m * (RG_CHUNK_W // RG_BLOCK_W), wg_s, rows_ref,
                                                carry_ref.at[r], reverse=False).astype(BF16)


def _odd_fwd(xx, mods, g, wx, cw, cb, ga, gx, ba, bx, lam, nt):
    bsz = xx.shape[0]
    assert bsz % RG_PAIR == 0
    tok = lambda n: pl.BlockSpec((RG_PAIR, TM, n), lambda p, t: (p, t, 0))
    consts = (g, wx, cw, cb, ga, gx, ba, bx, lam)
    gate_scratch = pltpu.VMEM((RG_BLOCKS, RG_BLOCK_W, 2 * RG_BLOCK_W), BF16)
    wide = jax.ShapeDtypeStruct((bsz, nt * TM, RG_WIDTH), BF16)
    return pl.pallas_call(
        _odd_fwd_kernel,
        grid=(bsz // RG_PAIR, nt),
        in_specs=[tok(D_MODEL),
                  pl.BlockSpec((None, None, 3, D_MODEL), lambda p, t: (1, bsz, 0, 0)),
                  pl.BlockSpec((None, RG_PAIR, 3, D_MODEL), lambda p, t: (1, p, 0, 0))]
        + [_const_spec(w) for w in consts],
        out_specs=[tok(RG_WIDTH), tok(RG_WIDTH), tok(D_MODEL)],
        out_shape=[wide, wide, jax.ShapeDtypeStruct((bsz, nt * TM, D_MODEL), BF16)],
        scratch_shapes=[pltpu.VMEM((RG_PAIR, HALO_ROWS, RG_WIDTH), F32),
                        pltpu.VMEM((RG_PAIR, SUBLANES, RG_WIDTH), F32),
                        pltpu.VMEM((RG_ROWS, RG_WIDTH), F32),
                        pltpu.VMEM(wx.shape, BF16),
                        gate_scratch],
        compiler_params=_params(2),
        name="odd_fwd",
    )(xx, mods, mods, *[_operand(w) for w in consts])


def _odd_bwd_kernel(xx_ref, mod_ref, hn_ref, xr_ref, hf_ref, wgate_ref, cw_ref, cb_ref, ga_ref, gx_ref, ba_ref,
                    bx_ref, lam_ref, wout_ref, fg_ref, out_ref,
                    stage_ref, halo_ref, yy_ref, carry_ref, rows_ref, wgate_s, wg_s, wout_s):
    j = pl.program_id(1)

    @pl.when(_first_step())
    def _():
        _rg_setup(lam_ref, ba_ref, bx_ref, cb_ref, cw_ref, ga_ref, gx_ref, rows_ref, wg_s,
                  ((wgate_ref, wgate_s), (wout_ref, wout_s)))

    @pl.when(j == 0)
    def _():
        carry_ref[...] = jnp.zeros_like(carry_ref)

    @pl.when(j <= 1)
    def _():
        halo_ref[...] = jnp.zeros_like(halo_ref)

    def tile(with_output):
        hb = hn_ref[...]
        for m in range(RG_WIDTH // RG_CHUNK_W):
            cols = slice(m * RG_CHUNK_W, (m + 1) * RG_CHUNK_W)
            if with_output:
                gate = _dot(hb, wgate_s[:, cols])
            xc = _rg_conv(xr_ref[:, cols].astype(F32), halo_ref, rows_ref, cols, reverse=True)
            h_bwd = _rg_gates_scan(xc, m * (RG_CHUNK_W // RG_BLOCK_W), wg_s, rows_ref, carry_ref,
                                   reverse=True)
            if with_output:
                yy_ref[:, cols] = ((hf_ref[:, cols].astype(F32) + h_bwd) * _silu(gate)).astype(BF16)
        if with_output:
            xn = xx_ref[...] + mod_ref[2:3, :] * _dot(yy_ref[...], wout_s[...])
            ms = jnp.mean(xn * xn, axis=-1, keepdims=True)
            _from_segment_major(xn * lax.rsqrt(ms + EPS) * fg_ref[...], stage_ref, out_ref, D_MODEL)

    pl.when(j == 0)(lambda: tile(with_output=False))
    pl.when(j > 0)(lambda: tile(with_output=True))


def _odd_bwd(xx, mods, hn, xr, hf, wgate, cw, cb, ga, gx, ba, bx, lam, wout, fg, nt):
    bsz = xx.shape[0]
    tile = _bwd_tile(nt)
    tok = lambda n: pl.BlockSpec((None, TM, n), lambda b, j: (b, tile(j), 0))
    consts = (wgate, cw, cb, ga, gx, ba, bx, lam, wout, fg)
    gate_scratch = pltpu.VMEM((RG_BLOCKS, RG_BLOCK_W, 2 * RG_BLOCK_W), BF16)
    out_block = lambda b, j: (b, jnp.where(j == 0, nt - 2, nt - 1 - j), 0)
    return pl.pallas_call(
        _odd_bwd_kernel,
        grid=(bsz, nt),
        in_specs=[tok(D_MODEL), _mod_spec(1, bsz, tile), tok(D_MODEL), tok(RG_WIDTH), tok(RG_WIDTH)]
        + [_const_spec(w) for w in consts],
        out_specs=pl.BlockSpec((None, TM, D_MODEL), out_block),
        out_shape=jax.ShapeDtypeStruct((bsz, (nt - 1) * TM, D_MODEL), F32),
        scratch_shapes=[pltpu.VMEM((D_MODEL // LANES, SUBLANES * SEG_PITCH, LANES), F32),
                        pltpu.VMEM((HALO_ROWS, RG_WIDTH), F32),
                        pltpu.VMEM((TM, RG_WIDTH), BF16),
                        pltpu.VMEM((SUBLANES, RG_WIDTH), F32),
                        pltpu.VMEM((RG_ROWS, RG_WIDTH), F32),
                        pltpu.VMEM(wgate.shape, BF16),
                        gate_scratch,
                        pltpu.VMEM(wout.shape, BF16)],
        compiler_params=_params(2),
        name="odd_bwd",
    )(xx, mods, hn, xr, hf, *[_operand(w) for w in consts])


def kernel(x, c, ctx, c_ctx, norm_g, w_mod, b_mod, e_w_in, e_w_a2, e_b_a2, e_gla_g, e_conv_w, e_w_out,
           o_w_in, o_conv_w, o_conv_b, o_w_a, o_b_a, o_w_x, o_b_x, o_lam, o_w_out, final_g):
    bsz, seq, _ = x.shape
    assert ctx.shape[1] == TM and seq % TM == 0 and bsz < SUBLANES
    assert w_mod.shape[0] == 2
    nt = 1 + seq // TM

    mods = _mods(c, c_ctx, w_mod, b_mod).reshape(w_mod.shape[0], SUBLANES, 3, D_MODEL)

    w_in = e_w_in[0]
    w_in_bf16 = w_in.astype(BF16)
    square = (D_MODEL, D_MODEL)
    offs = {}
    off = 0
    for name, size in (("q", GLA_QK), ("k", GLA_QK), ("v", GLA_V), ("ga", GLA_V), ("af", GLA_LOWRANK),
                       ("ab", GLA_LOWRANK), ("cb", SC_WIDTH), ("cc", SC_WIDTH), ("cx", SC_WIDTH),
                       ("gb", SC_WIDTH)):
        offs[name] = (off, off + size)
        off += size
    cols = lambda lo, hi: w_in[:, lo:hi].astype(BF16)
    assert offs["q"][0] == 0 and offs["v"][0] == D_MODEL and offs["ga"] == (2 * D_MODEL, 3 * D_MODEL)
    wqk, wv, wga = (_Window(w_in_bf16, square, (0, i)) for i in range(3))
    wa = jnp.pad(w_in[:, offs["af"][0]:offs["ab"][1]], ((0, 0), (0, LANES - 2 * GLA_LOWRANK))).astype(BF16)
    b2 = e_b_a2[0].reshape(1, 2 * GLA_QK)
    mats = (wqk, wv, wga, wa, cols(*offs["cb"]), cols(*offs["cc"]), cols(*offs["cx"]), cols(*offs["gb"]))
    (o_f, qi_b, ki_b, ke_b, dec_b, v, sga, y) = _even_in(
        ctx, x, mods, (norm_g[0:1], b2, e_conv_w[0], e_w_a2[0]), mats, nt)
    w_out = e_w_out[0]
    xx = _gla_bwd(qi_b, ki_b, ke_b, v, dec_b, o_f, sga, y, ctx, x, mods, e_gla_g[0:1],
                  _Window(w_out, square, (0, 0)), _Window(w_out, square, (1, 0)), nt)

    w_in = o_w_in[0]
    half = (D_MODEL, RG_WIDTH)
    wide = lambda a: a.reshape(1, RG_WIDTH)
    xr, h_f, hn = _odd_fwd(xx, mods, norm_g[1:2], _Window(w_in, half, (0, 0)), o_conv_w[0, 0],
                           wide(o_conv_b[0, 0]), o_w_a[0, 0], o_w_x[0, 0],
                           wide(o_b_a[0, 0]), wide(o_b_x[0, 0]), wide(o_lam[0, 0]), nt)
    return _odd_bwd(xx, mods, hn, xr, h_f, _Window(w_in, half, (0, 1)), o_conv_w[0, 1],
                    wide(o_conv_b[0, 1]), o_w_a[0, 1], o_w_x[0, 1],
                    wide(o_b_a[0, 1]), wide(o_b_x[0, 1]), wide(o_lam[0, 1]), o_w_out[0],
                    final_g.reshape(1, D_MODEL), nt)
```
